```python
import math
import jax, jax.numpy as jnp
from jax import lax
import numpy as np

D_MODEL = 1024
BATCH = 16
SEQ = 256
DEPTH = 4
DEC_BATCH = 4
DEC_SEQ = 1024
PAST_LEN = 256

GRID_W = 64
Q_BLOCK = 128
HA = 4
DA = 64
HB = 8
KVB = 2
DB = 64
HC = 8
DC = 64
NA_ROWS = 8
NA_COLS = 16
D_FF = 4 * D_MODEL
N_BRANCH = 3
ROPE_BASE = 10000.0
EPS = 1e-6
NEG = -1e30

QA = 2 * HA * DA
KA = 2 * HA * DA
VA = HA * 2 * DA
QB = HB * DB
KB = KVB * DB
VB = KVB * DB
QC = HC * DC
KC = HC * DC
VC = HC * DC
D_IN = QA + KA + VA + QB + KB + VB + QC + KC + VC + N_BRANCH * D_MODEL
SPLIT_IDX = (QA, QA + KA, QA + KA + VA, QA + KA + VA + QB, QA + KA + VA + QB + KB,
             QA + KA + VA + QB + KB + VB, QA + KA + VA + QB + KB + VB + QC,
             QA + KA + VA + QB + KB + VB + QC + KC, QA + KA + VA + QB + KB + VB + QC + KC + VC)
OUT_A = HA * 2 * DA
OUT_B = HB * DB
OUT_C = HC * DC

kernel_name = "hybrid_diffusion_parallel_mixer_step"


def rmsnorm(x, g):
    xf = x.astype(jnp.float32)
    y = xf * lax.rsqrt(jnp.mean(xf * xf, axis=-1, keepdims=True) + EPS)
    return (y * g.astype(jnp.float32)).astype(x.dtype)


def axial_rope_tables(S, head_dim):
    nf = head_dim // 4
    t = jnp.arange(S)
    row = (t // GRID_W).astype(jnp.float32)
    col = (t % GRID_W).astype(jnp.float32)
    inv = ROPE_BASE ** (-jnp.arange(nf, dtype=jnp.float32) / nf)
    ar = row[:, None] * inv[None, :]
    ac = col[:, None] * inv[None, :]
    return jnp.cos(ar), jnp.sin(ar), jnp.cos(ac), jnp.sin(ac)


def _rope_half(x, cos, sin):
    x1, x2 = jnp.split(x, 2, axis=-1)
    c = cos[:, None, :]
    s = sin[:, None, :]
    return jnp.concatenate([x1 * c - x2 * s, x2 * c + x1 * s], axis=-1)


def apply_axial_rope(x, rope):
    cr, sr, cc, sc = rope
    xr, xc = jnp.split(x, 2, axis=-1)
    return jnp.concatenate([_rope_half(xr, cr, sr), _rope_half(xc, cc, sc)], axis=-1).astype(x.dtype)


def blocked_attention(q, k, v):
    B, Sq, Hkv, G, D = q.shape
    nb = Sq // Q_BLOCK
    scale = D ** -0.5
    qb = jnp.moveaxis(q.reshape(B, nb, Q_BLOCK, Hkv, G, D), 1, 0)

    def one_block(qblk):
        s = jnp.einsum('bqhgd,bkhd->bhgqk', qblk, k, preferred_element_type=jnp.float32) * scale
        p = jax.nn.softmax(s, axis=-1).astype(v.dtype)
        return jnp.einsum('bhgqk,bkhe->bqhge', p, v)

    out = lax.map(one_block, qb)
    return jnp.moveaxis(out, 0, 1).reshape(B, Sq, Hkv, G, v.shape[-1])


def diff_lambda(lambda_qk_l, lam_init):
    lq = lambda_qk_l.astype(jnp.float32)
    return jnp.exp(jnp.sum(lq[0] * lq[1])) - jnp.exp(jnp.sum(lq[2] * lq[3])) + lam_init


def diff_attention(q, k, v, lam, lam_init, g_subln):
    B, Sq = q.shape[:2]
    o1 = blocked_attention(q[:, :, :HA, None], k[:, :, :HA], v)[:, :, :, 0]
    o2 = blocked_attention(q[:, :, HA:, None], k[:, :, HA:], v)[:, :, :, 0]
    o = o1 - lam.astype(o1.dtype) * o2
    o = rmsnorm(o, g_subln) * (1.0 - lam_init)
    return o.reshape(B, Sq, OUT_A)


def gqa_attention(q, k, v):
    B, Sq = q.shape[:2]
    o = blocked_attention(q.reshape(B, Sq, KVB, HB // KVB, DB), k, v)
    return o.reshape(B, Sq, OUT_B)


def neighbourhood_attention(q, k, v, k_ctx, v_ctx, rel_bias):
    B, S, H, D = q.shape
    rows = S // GRID_W
    wr = min(NA_ROWS, rows)
    scale = D ** -0.5
    r = jnp.arange(rows)
    r0 = jnp.clip(r - wr // 2, 0, rows - wr)
    key_rows = r0[:, None] + jnp.arange(wr)[None, :]
    kg = k.reshape(B, rows, GRID_W, H, D)[:, key_rows].reshape(B, rows, wr * GRID_W, H, D)
    vg = v.reshape(B, rows, GRID_W, H, D)[:, key_rows].reshape(B, rows, wr * GRID_W, H, D)
    qg = q.reshape(B, rows, GRID_W, H, D)
    cq = jnp.arange(GRID_W)
    c0 = jnp.clip(cq - NA_COLS // 2, 0, GRID_W - NA_COLS)
    ck = jnp.arange(GRID_W)
    in_win = (ck[None, :] >= c0[:, None]) & (ck[None, :] < c0[:, None] + NA_COLS)
    dr_idx = key_rows - r[:, None] + (NA_ROWS - 1)
    dc_idx = jnp.clip(ck[None, :] - cq[:, None] + (NA_COLS - 1), 0, 2 * NA_COLS - 2)
    bias = rel_bias[:, dr_idx[:, None, :, None], dc_idx[None, :, None, :]]
    bias = jnp.where(in_win[None, None, :, None, :], bias.astype(jnp.float32), NEG)
    bias = bias.reshape(H, rows, GRID_W, wr * GRID_W)
    s_lat = jnp.einsum('brqhd,brkhd->bhrqk', qg, kg, preferred_element_type=jnp.float32) * scale + bias[None]
    s_ctx = jnp.einsum('brqhd,bkhd->bhrqk', qg, k_ctx, preferred_element_type=jnp.float32) * scale
    p = jax.nn.softmax(jnp.concatenate([s_lat, s_ctx], axis=-1), axis=-1).astype(v.dtype)
    p_lat, p_ctx = p[..., :wr * GRID_W], p[..., wr * GRID_W:]
    o = jnp.einsum('bhrqk,brkhd->brqhd', p_lat, vg) + jnp.einsum('bhrqk,bkhd->brqhd', p_ctx, v_ctx)
    return o.reshape(B, S, OUT_C)


def modulation(cvec, w_mod_l, b_mod_l):
    m = jax.nn.silu(cvec) @ w_mod_l + b_mod_l
    return [t[:, None, :] for t in jnp.split(m, 6, axis=-1)]


def project(h, w_in_l, b_gate_l):
    B, S = h.shape[:2]
    z = h @ w_in_l
    qa, ka, va, qb, kb, vb, qc, kc, vc, g = jnp.split(z, SPLIT_IDX, axis=-1)
    gates = jax.nn.sigmoid((g + b_gate_l).astype(jnp.float32)).astype(h.dtype).reshape(B, S, N_BRANCH, D_MODEL)
    return (qa.reshape(B, S, 2 * HA, DA), ka.reshape(B, S, 2 * HA, DA), va.reshape(B, S, HA, 2 * DA),
            qb.reshape(B, S, HB, DB), kb.reshape(B, S, KVB, DB), vb.reshape(B, S, KVB, DB),
            qc.reshape(B, S, HC, DC), kc.reshape(B, S, HC, DC), vc.reshape(B, S, HC, DC), gates)


def merge(oa, ob, oc, gates, w_ba, w_bb, w_bc, w_o):
    y = gates[:, :, 0] * (oa @ w_ba) + gates[:, :, 1] * (ob @ w_bb) + gates[:, :, 2] * (oc @ w_bc)
    return y @ w_o


def sq_relu_mlp(h, w1, w2):
    return jnp.square(jax.nn.relu(h @ w1)) @ w2


def setup_inputs(seed: int = 0) -> dict:
    key = jax.random.key(seed)
    ks = jax.random.split(key, 32)
    f32 = jnp.float32

    def nrm(k, shape, scale):
        return jax.random.normal(k, shape, f32) * scale

    L = PAST_LEN
    return {
        "x_prompt": nrm(ks[0], (BATCH, SEQ, D_MODEL), 1.0),
        "x_sample": nrm(ks[1], (DEC_BATCH, DEC_SEQ, D_MODEL), 1.0),
        "c": nrm(ks[2], (DEC_BATCH, D_MODEL), 1.0),
        "cache_a_k": nrm(ks[3], (DEC_BATCH, DEPTH, L, 2 * HA, DA), 1.0),
        "cache_a_v": nrm(ks[4], (DEC_BATCH, DEPTH, L, HA, 2 * DA), 1.0),
        "cache_b_k": nrm(ks[5], (DEC_BATCH, DEPTH, L, KVB, DB), 1.0),
        "cache_b_v": nrm(ks[6], (DEC_BATCH, DEPTH, L, KVB, DB), 1.0),
        "cache_c_k": nrm(ks[7], (DEC_BATCH, DEPTH, L, HC, DC), 1.0),
        "cache_c_v": nrm(ks[8], (DEC_BATCH, DEPTH, L, HC, DC), 1.0),
        "c_ctx": nrm(ks[9], (D_MODEL,), 1.0),
        "w_mod": nrm(ks[10], (DEPTH, D_MODEL, 6 * D_MODEL), D_MODEL ** -0.5),
        "b_mod": nrm(ks[11], (DEPTH, 6 * D_MODEL), 0.02),
        "g_norm1": 1.0 + nrm(ks[12], (DEPTH, D_MODEL), 0.1),
        "g_norm2": 1.0 + nrm(ks[13], (DEPTH, D_MODEL), 0.1),
        "w_in": nrm(ks[14], (DEPTH, D_MODEL, D_IN), D_MODEL ** -0.5),
        "b_gate": nrm(ks[15], (DEPTH, N_BRANCH * D_MODEL), 0.02),
        "lambda_qk": nrm(ks[16], (DEPTH, 4, DA), 0.1),
        "g_subln": 1.0 + nrm(ks[17], (DEPTH, 2 * DA), 0.1),
        "g_qnorm": 1.0 + nrm(ks[18], (DEPTH, DB), 0.1),
        "g_knorm": 1.0 + nrm(ks[19], (DEPTH, DB), 0.1),
        "rel_bias": nrm(ks[20], (DEPTH, HC, 2 * NA_ROWS - 1, 2 * NA_COLS - 1), 0.1),
        "w_branch_a": nrm(ks[21], (DEPTH, OUT_A, D_MODEL), OUT_A ** -0.5),
        "w_branch_b": nrm(ks[22], (DEPTH, OUT_B, D_MODEL), OUT_B ** -0.5),
        "w_branch_c": nrm(ks[23], (DEPTH, OUT_C, D_MODEL), OUT_C ** -0.5),
        "w_out": nrm(ks[24], (DEPTH, D_MODEL, D_MODEL), D_MODEL ** -0.5),
        "w_ff1": nrm(ks[25], (DEPTH, D_MODEL, D_FF), D_MODEL ** -0.5),
        "w_ff2": nrm(ks[26], (DEPTH, D_FF, D_MODEL), D_FF ** -0.5),
        "g_final": 1.0 + nrm(ks[27], (D_MODEL,), 0.1),
    }


def reference(x_prompt, x_sample, c, cache_a_k, cache_a_v, cache_b_k, cache_b_v, cache_c_k, cache_c_v,
              c_ctx, w_mod, b_mod, g_norm1, g_norm2, w_in, b_gate, lambda_qk, g_subln, g_qnorm, g_knorm,
              rel_bias, w_branch_a, w_branch_b, w_branch_c, w_out, w_ff1, w_ff2, g_final):
    xp = x_prompt
    xs = x_sample
    S = xs.shape[1]
    rope = axial_rope_tables(S, DA)
    new_ak, new_av, new_bk, new_bv, new_ck, new_cv = [], [], [], [], [], []
    for l in range(DEPTH):
        lam_init = 0.8 - 0.6 * math.exp(-0.3 * l)
        lam = diff_lambda(lambda_qk[l], lam_init)
        mp = modulation(c_ctx[None, :], w_mod[l], b_mod[l])
        ms = modulation(c, w_mod[l], b_mod[l])

        h = rmsnorm(xp, g_norm1[l]) * (1.0 + mp[1]) + mp[0]
        qa, ka, va, qb, kb, vb, qc, kc, vc, gates = project(h, w_in[l], b_gate[l])
        qb = rmsnorm(qb, g_qnorm[l])
        kb = rmsnorm(kb, g_knorm[l])
        oa = diff_attention(qa, ka, va, lam, lam_init, g_subln[l])
        ob = gqa_attention(qb, kb, vb)
        oc = blocked_attention(qc[:, :, :, None], kc, vc).reshape(xp.shape[0], xp.shape[1], OUT_C)
        xp = xp + mp[2] * merge(oa, ob, oc, gates, w_branch_a[l], w_branch_b[l], w_branch_c[l], w_out[l])
        h2 = rmsnorm(xp, g_norm2[l]) * (1.0 + mp[4]) + mp[3]
        xp = xp + mp[5] * sq_relu_mlp(h2, w_ff1[l], w_ff2[l])
        new_ak.append(ka)
        new_av.append(va)
        new_bk.append(kb)
        new_bv.append(vb)
        new_ck.append(kc)
        new_cv.append(vc)

        h = rmsnorm(xs, g_norm1[l]) * (1.0 + ms[1]) + ms[0]
        qa, ka, va, qb, kb, vb, qc, kc, vc, gates = project(h, w_in[l], b_gate[l])
        qa = apply_axial_rope(qa, rope)
        ka = apply_axial_rope(ka, rope)
        qb = apply_axial_rope(rmsnorm(qb, g_qnorm[l]), rope)
        kb = apply_axial_rope(rmsnorm(kb, g_knorm[l]), rope)
        oa = diff_attention(qa, jnp.concatenate([ka, cache_a_k[:, l]], axis=1),
                            jnp.concatenate([va, cache_a_v[:, l]], axis=1), lam, lam_init, g_subln[l])
        ob = gqa_attention(qb, jnp.concatenate([kb, cache_b_k[:, l]], axis=1),
                           jnp.concatenate([vb, cache_b_v[:, l]], axis=1))
        oc = neighbourhood_attention(qc, kc, vc, cache_c_k[:, l], cache_c_v[:, l], rel_bias[l])
        xs = xs + ms[2] * merge(oa, ob, oc, gates, w_branch_a[l], w_branch_b[l], w_branch_c[l], w_out[l])
        h2 = rmsnorm(xs, g_norm2[l]) * (1.0 + ms[4]) + ms[3]
        xs = xs + ms[5] * sq_relu_mlp(h2, w_ff1[l], w_ff2[l])

    y_prompt = rmsnorm(xp, g_final)
    y_sample = rmsnorm(xs, g_final)
    new_a_k = jnp.stack(new_ak, axis=1)
    new_a_v = jnp.stack(new_av, axis=1)
    new_b_k = jnp.stack(new_bk, axis=1)
    new_b_v = jnp.stack(new_bv, axis=1)
    new_c_k = jnp.stack(new_ck, axis=1)
    new_c_v = jnp.stack(new_cv, axis=1)
    return (y_prompt, y_sample, new_a_k, new_a_v, new_b_k, new_b_v, new_c_k, new_c_v)
```

```python
import functools
import math

import jax
import jax.numpy as jnp
from jax import lax
from jax.experimental import pallas as pl
from jax.experimental.pallas import tpu as pltpu

D_MODEL = 1024
BATCH = 16
SEQ = 256
DEPTH = 4
DEC_BATCH = 4
DEC_SEQ = 1024
PAST_LEN = 256
GRID_W = 64
HA, DA = 4, 64
HB, KVB, DB = 8, 2, 64
HC, DC = 8, 64
NA_ROWS, NA_COLS = 8, 16
D_FF = 4 * D_MODEL
ROPE_BASE = 10000.0
EPS = 1e-6
NEG = -1e30
HEAD_DIM = 64
SCALE = HEAD_DIM ** -0.5

F32 = jnp.float32
BF16 = jnp.bfloat16

LANES = 128
N_QKV_GROUPS = 30
N_GATE_GROUPS = 24
COL_TILE = 768
GROUPS_PER_TILE = COL_TILE // LANES
N_QKV_TILES = N_QKV_GROUPS // GROUPS_PER_TILE
N_COL_TILES = N_QKV_TILES + N_GATE_GROUPS // GROUPS_PER_TILE
ROW_TILE = 1024
ROW_CHUNK = 256
N_TOK = BATCH * SEQ
VMEM_LIMIT_V7X = 58 * 1024 * 1024

G_QA, G_KA, G_VA, G_QB, G_KB, G_VB, G_QC, G_KC, G_VC = 0, 4, 8, 12, 16, 17, 18, 22, 26

_NT = (((1,), (1,)), ((), ()))


def _params(sem, vmem=VMEM_LIMIT_V7X):
    return pltpu.CompilerParams(dimension_semantics=sem, vmem_limit_bytes=vmem)


def _lane_head(shape):
    return lax.shift_right_logical(lax.broadcasted_iota(jnp.int32, shape, len(shape) - 1), 6)


def _mod_kernel(c_ref, w_ref, b_ref, o_ref):
    c = c_ref[...]
    s = (c * jax.nn.sigmoid(c)).astype(BF16)
    o_ref[...] = jnp.dot(s, w_ref[...].astype(BF16), preferred_element_type=F32) + b_ref[...]


def _modulation(cvec, w_mod, b_mod):
    tn = 1536
    n6 = 6 * D_MODEL
    return pl.pallas_call(
        _mod_kernel,
        grid=(DEPTH, n6 // tn),
        in_specs=[pl.BlockSpec((8, D_MODEL), lambda l, n: (0, 0)),
                  pl.BlockSpec((None, D_MODEL, tn), lambda l, n: (l, 0, n)),
                  pl.BlockSpec((None, 1, tn), lambda l, n: (l, 0, n))],
        out_specs=pl.BlockSpec((None, 8, tn), lambda l, n: (l, 0, n)),
        out_shape=jax.ShapeDtypeStruct((DEPTH, 8, n6), F32),
        compiler_params=_params(("arbitrary", "arbitrary")),
        name="modulation",
    )(cvec, w_mod, b_mod.reshape(DEPTH, 1, n6))


def _rope(v, cos, sin):
    first = (lax.broadcasted_iota(jnp.int32, v.shape, 1) & 16) == 0
    partner = jnp.where(first, pltpu.roll(v, LANES - 16, 1), pltpu.roll(v, 16, 1))
    return v * cos + partner * sin


def _head_rmsnorm(v, bd, g):
    msq = jnp.dot((v * v).astype(BF16), bd, preferred_element_type=F32)
    return v * lax.rsqrt(msq + EPS) * g


def _make_inproj_kernel(latent):
    def kern(*refs):
        if latent:
            (x_ref, mod_ref, g1_ref, w_ref, bg_ref, gq_ref, gk_ref, bd_ref, cos_ref, sin_ref,
             z_ref, gate_ref, h_scr, acc_scr) = refs
        else:
            (x_ref, mod_ref, g1_ref, w_ref, bg_ref, gq_ref, gk_ref, bd_ref,
             _, _, _, _, _, _,
             z_ref, gate_ref, ak_ref, av_ref, bk_ref, bv_ref, ck_ref, cv_ref, h_scr, acc_scr) = refs
        j = pl.program_id(1)

        @pl.when(j == 0)
        def _():
            x = x_ref[...]
            ms = jnp.mean(x * x, axis=-1, keepdims=True)
            y = x * lax.rsqrt(ms + EPS) * g1_ref[...]
            m = mod_ref[0]
            h_scr[...] = (y * (1.0 + m[:, D_MODEL:2 * D_MODEL]) + m[:, 0:D_MODEL]).astype(BF16)

        acc_scr[...] = jnp.dot(h_scr[...], w_ref[...].astype(BF16), preferred_element_type=F32)

        def chunks():
            for rc in range(ROW_TILE // ROW_CHUNK):
                yield rc, pl.ds(rc * ROW_CHUNK, ROW_CHUNK)

        def grp(a, c, n=1):
            return a[:, c * LANES:(c + n) * LANES]

        def rot(v, rows):
            return _rope(v, cos_ref[rows, :], sin_ref[rows, :]) if latent else v

        @pl.when(j == 0)
        def _():
            for rc, rows in chunks():
                a = acc_scr[rows, :]
                for c in range(GROUPS_PER_TILE):
                    z_ref[c, rows, :] = rot(grp(a, c), rows).astype(BF16)
                if not latent:
                    ak_ref[rc, :, 0:256] = grp(a, 4, 2)

        @pl.when(j == 1)
        def _():
            for rc, rows in chunks():
                a = acc_scr[rows, :]
                for c in range(2):
                    z_ref[c, rows, :] = rot(grp(a, c), rows).astype(BF16)
                for c in range(2, GROUPS_PER_TILE):
                    z_ref[c, rows, :] = grp(a, c).astype(BF16)
                if not latent:
                    ak_ref[rc, :, 256:512] = grp(a, 0, 2)
                    av_ref[rc] = grp(a, 2, 4)

        @pl.when(j == 2)
        def _():
            bd = bd_ref[...]
            for rc, rows in chunks():
                a = acc_scr[rows, :]
                for half in range(2):
                    qn = _head_rmsnorm(grp(a, 2 * half, 2), bd, gq_ref[...])
                    for c in range(2):
                        z_ref[2 * half + c, rows, :] = rot(grp(qn, c), rows).astype(BF16)
                kn = _head_rmsnorm(grp(a, 4), bd[0:LANES, 0:LANES], gk_ref[...])
                z_ref[4, rows, :] = rot(kn, rows).astype(BF16)
                z_ref[5, rows, :] = grp(a, 5).astype(BF16)
                if not latent:
                    bk_ref[rc] = kn
                    bv_ref[rc] = grp(a, 5)

        @pl.when(j == 3)
        def _():
            for rc, rows in chunks():
                a = acc_scr[rows, :]
                for c in range(GROUPS_PER_TILE):
                    z_ref[c, rows, :] = grp(a, c).astype(BF16)
                if not latent:
                    ck_ref[rc, :, 0:256] = grp(a, 4, 2)

        @pl.when(j == 4)
        def _():
            for rc, rows in chunks():
                a = acc_scr[rows, :]
                for c in range(GROUPS_PER_TILE):
                    z_ref[c, rows, :] = grp(a, c).astype(BF16)
                if not latent:
                    ck_ref[rc, :, 256:512] = grp(a, 0, 2)
                    cv_ref[rc] = grp(a, 2, 4)

        @pl.when(j >= N_QKV_TILES)
        def _():
            for rc, rows in chunks():
                a = acc_scr[rows, :] + bg_ref[...]
                for c in range(GROUPS_PER_TILE):
                    gate_ref[c, rows, :] = jax.nn.sigmoid(grp(a, c)).astype(BF16)

    return kern


def _inproj(latent, l, x, mod, g_norm1, w_in, b_gate, gq, gk, bd, rope_cos, rope_sin, caches):
    n_row = N_TOK // ROW_TILE
    mod_idx = (lambda i, j: (1 + i, 0, 0)) if latent else (lambda i, j: (0, 0, 0))
    in_specs = [
        pl.BlockSpec((ROW_TILE, D_MODEL), lambda i, j: (i, 0)),
        pl.BlockSpec((1, 1, 6 * D_MODEL), mod_idx),
        pl.BlockSpec((None, 1, D_MODEL), lambda i, j: (l, 0, 0)),
        pl.BlockSpec((None, D_MODEL, COL_TILE), lambda i, j: (l, 0, j)),
        pl.BlockSpec((None, 1, COL_TILE), lambda i, j: (l, 0, jnp.maximum(j - N_QKV_TILES, 0))),
        pl.BlockSpec((None, 1, 2 * LANES), lambda i, j: (l, 0, 0)),
        pl.BlockSpec((None, 1, LANES), lambda i, j: (l, 0, 0)),
        pl.BlockSpec((2 * LANES, 2 * LANES), lambda i, j: (0, 0)),
    ]
    args = [x, mod, g_norm1, w_in, b_gate, gq, gk, bd]
    out_specs = [
        pl.BlockSpec((GROUPS_PER_TILE, ROW_TILE, LANES), lambda i, j: (jnp.minimum(j, N_QKV_TILES - 1), i, 0)),
        pl.BlockSpec((GROUPS_PER_TILE, ROW_TILE, LANES), lambda i, j: (jnp.maximum(j - N_QKV_TILES, 0), i, 0)),
    ]
    out_shape = [jax.ShapeDtypeStruct((N_QKV_GROUPS, N_TOK, LANES), BF16),
                 jax.ShapeDtypeStruct((N_GATE_GROUPS, N_TOK, LANES), BF16)]
    aliases = {}
    if latent:
        in_specs += [pl.BlockSpec((DEC_SEQ, LANES), lambda i, j: (0, 0))] * 2
        args += [rope_cos, rope_sin]
    else:
        nb = ROW_TILE // SEQ
        for k, cache in enumerate(caches):
            in_specs.append(pl.BlockSpec(memory_space=pl.ANY))
            args.append(cache)
            w = cache.shape[-1]
            out_specs.append(pl.BlockSpec((nb, None, SEQ, w), lambda i, j: (i, l, 0, 0)))
            out_shape.append(jax.ShapeDtypeStruct(cache.shape, F32))
            aliases[8 + k] = 2 + k
    return pl.pallas_call(
        _make_inproj_kernel(latent),
        grid=(n_row, N_COL_TILES),
        in_specs=in_specs,
        out_specs=out_specs,
        out_shape=out_shape,
        scratch_shapes=[pltpu.VMEM((ROW_TILE, D_MODEL), BF16), pltpu.VMEM((ROW_TILE, COL_TILE), F32)],
        input_output_aliases=aliases,
        compiler_params=_params(("arbitrary", "arbitrary")),
        name="inproj_lat" if latent else "inproj_ctx",
    )(*args)


def _mask_head(q, head):
    qf = q.astype(F32)
    keep = _lane_head(qf.shape) == head
    return (jnp.where(keep, qf, 0.0) * SCALE).astype(BF16)


def _stack_heads(q):
    return jnp.concatenate([_mask_head(q, 0), _mask_head(q, 1)], axis=0)


def _unstack_heads(o, rows):
    return jnp.where(_lane_head((rows, LANES)) == 1, o[rows:2 * rows], o[0:rows])


def _dup_head(kv, head):
    f = kv.astype(F32)
    r = pltpu.roll(f, HEAD_DIM, 1)
    return jnp.where(_lane_head(f.shape) == head, f, r).astype(BF16)


def _attend(qm, ks, vs, biases=None):
    ss = [lax.dot_general(qm, k, _NT, preferred_element_type=F32) for k in ks]
    if biases is not None:
        ss = [s if b is None else s + b for s, b in zip(ss, biases)]
    m = functools.reduce(jnp.maximum, [jnp.max(s, axis=-1, keepdims=True) for s in ss])
    ps = [jnp.exp(s - m) for s in ss]
    den = functools.reduce(jnp.add, [jnp.sum(p, axis=-1, keepdims=True) for p in ps])
    o = functools.reduce(jnp.add, [jnp.dot(p.astype(BF16), v, preferred_element_type=F32)
                                    for p, v in zip(ps, vs)])
    return o * (1.0 / den)


def _diff_lambda(lam_ref, lam_init):
    lq = lam_ref[...]
    a = jnp.sum(lq[0:1] * lq[1:2], axis=-1, keepdims=True)
    b = jnp.sum(lq[2:3] * lq[3:4], axis=-1, keepdims=True)
    return jnp.exp(a) - jnp.exp(b) + lam_init


def _diff_combine(o1, o2, lam, gsub, lam_init):
    o = o1 - lam * o2
    ms = jnp.mean(o * o, axis=-1, keepdims=True)
    return (o * lax.rsqrt(ms + EPS) * gsub) * (1.0 - lam_init)


def _make_attn_ctx_kernel(lam_init):
    def kern(z_ref, lam_ref, gsub_ref, oa_ref, ob_ref, oc_ref):
        lam = _diff_lambda(lam_ref, lam_init)
        gsub = gsub_ref[...]
        for vh in range(HA):
            hi = vh % 2
            v = z_ref[G_VA + vh]
            o1 = _attend(_mask_head(z_ref[G_QA + vh // 2], hi), [z_ref[G_KA + vh // 2]], [v])
            o2 = _attend(_mask_head(z_ref[G_QA + 2 + vh // 2], hi), [z_ref[G_KA + 2 + vh // 2]], [v])
            oa_ref[vh] = _diff_combine(o1, o2, lam, gsub, lam_init).astype(BF16)
        for g in range(KVB):
            kd = _dup_head(z_ref[G_KB], g)
            vd = _dup_head(z_ref[G_VB], g)
            for c in range(2 * g, 2 * g + 2):
                o = _attend(_stack_heads(z_ref[G_QB + c]), [kd], [vd])
                ob_ref[c] = _unstack_heads(o, SEQ).astype(BF16)
        for c in range(HC // 2):
            o = _attend(_stack_heads(z_ref[G_QC + c]), [z_ref[G_KC + c]], [z_ref[G_VC + c]])
            oc_ref[c] = _unstack_heads(o, SEQ).astype(BF16)
    return kern


def _attn_ctx(l, z, lambda_qk, g_subln):
    lam_init = 0.8 - 0.6 * math.exp(-0.3 * l)
    o_spec = pl.BlockSpec((4, SEQ, LANES), lambda b: (0, b, 0))
    o_shape = jax.ShapeDtypeStruct((4, N_TOK, LANES), BF16)
    return pl.pallas_call(
        _make_attn_ctx_kernel(lam_init),
        grid=(BATCH,),
        in_specs=[pl.BlockSpec((N_QKV_GROUPS, SEQ, LANES), lambda b: (0, b, 0)),
                  pl.BlockSpec((None, 4, DA), lambda b: (l, 0, 0)),
                  pl.BlockSpec((None, 1, 2 * DA), lambda b: (l, 0, 0))],
        out_specs=[o_spec, o_spec, o_spec],
        out_shape=[o_shape, o_shape, o_shape],
        compiler_params=_params(("arbitrary",)),
        name="attn_ctx",
    )(z, lambda_qk, g_subln)


Q_BLK = 256


def _make_attn_lat_a_kernel(lam_init):
    def kern(q1_ref, q2_ref, k1_ref, k2_ref, v_ref, ck1_ref, ck2_ref, cv_ref, lam_ref, gsub_ref, o_ref):
        hi = jnp.bitwise_and(pl.program_id(1), 1)
        lam = _diff_lambda(lam_ref, lam_init)
        gsub = gsub_ref[...]
        ck1 = ck1_ref[...].astype(BF16)
        ck2 = ck2_ref[...].astype(BF16)
        cv = cv_ref[...].astype(BF16)
        k1, k2, v = k1_ref[0], k2_ref[0], v_ref[0]
        for qb in range(DEC_SEQ // Q_BLK):
            rows = pl.ds(qb * Q_BLK, Q_BLK)
            o1 = _attend(_mask_head(q1_ref[0, rows, :], hi), [k1, ck1], [v, cv])
            o2 = _attend(_mask_head(q2_ref[0, rows, :], hi), [k2, ck2], [v, cv])
            o_ref[0, rows, :] = _diff_combine(o1, o2, lam, gsub, lam_init).astype(BF16)
    return kern


def _attn_lat_a(l, z, cache_k, cache_v, lambda_qk, g_subln):
    lam_init = 0.8 - 0.6 * math.exp(-0.3 * l)

    def zspec(fn):
        return pl.BlockSpec((1, DEC_SEQ, LANES), lambda b, h: (fn(h), b, 0))

    def cspec(fn):
        return pl.BlockSpec((None, None, PAST_LEN, LANES), lambda b, h: (b, l, 0, fn(h)))

    return pl.pallas_call(
        _make_attn_lat_a_kernel(lam_init),
        grid=(DEC_BATCH, HA),
        in_specs=[zspec(lambda h: G_QA + h // 2), zspec(lambda h: G_QA + 2 + h // 2),
                  zspec(lambda h: G_KA + h // 2), zspec(lambda h: G_KA + 2 + h // 2),
                  zspec(lambda h: G_VA + h),
                  cspec(lambda h: h // 2), cspec(lambda h: 2 + h // 2), cspec(lambda h: h),
                  pl.BlockSpec((None, 4, DA), lambda b, h: (l, 0, 0)),
                  pl.BlockSpec((None, 1, 2 * DA), lambda b, h: (l, 0, 0))],
        out_specs=pl.BlockSpec((1, DEC_SEQ, LANES), lambda b, h: (h, b, 0)),
        out_shape=jax.ShapeDtypeStruct((4, N_TOK, LANES), BF16),
        compiler_params=_params(("arbitrary", "arbitrary")),
        name="attn_lat_a",
    )(z, z, z, z, z, cache_k, cache_k, cache_v, lambda_qk, g_subln)


def _attn_lat_b_kernel(q_ref, k_ref, v_ref, ck_ref, cv_ref, o_ref):
    hi = lax.shift_right_logical(pl.program_id(1), 1)
    kd = _dup_head(k_ref[0], hi)
    vd = _dup_head(v_ref[0], hi)
    ckd = _dup_head(ck_ref[...].astype(BF16), hi)
    cvd = _dup_head(cv_ref[...].astype(BF16), hi)
    for qb in range(DEC_SEQ // Q_BLK):
        rows = pl.ds(qb * Q_BLK, Q_BLK)
        o = _attend(_stack_heads(q_ref[0, rows, :]), [kd, ckd], [vd, cvd])
        o_ref[0, rows, :] = _unstack_heads(o, Q_BLK).astype(BF16)


def _attn_lat_b(l, z, cache_k, cache_v):
    cspec = pl.BlockSpec((None, None, PAST_LEN, LANES), lambda b, c: (b, l, 0, 0))
    return pl.pallas_call(
        _attn_lat_b_kernel,
        grid=(DEC_BATCH, HB // 2),
        in_specs=[pl.BlockSpec((1, DEC_SEQ, LANES), lambda b, c: (G_QB + c, b, 0)),
                  pl.BlockSpec((1, DEC_SEQ, LANES), lambda b, c: (G_KB, b, 0)),
                  pl.BlockSpec((1, DEC_SEQ, LANES), lambda b, c: (G_VB, b, 0)),
                  cspec, cspec],
        out_specs=pl.BlockSpec((1, DEC_SEQ, LANES), lambda b, c: (c, b, 0)),
        out_shape=jax.ShapeDtypeStruct((4, N_TOK, LANES), BF16),
        compiler_params=_params(("arbitrary", "arbitrary")),
        name="attn_lat_b",
    )(z, z, z, cache_k, cache_v)


N_GRID_ROWS = DEC_SEQ // GRID_W
NA_KEYS = NA_ROWS * GRID_W


def _attn_lat_c_kernel(q_ref, k_ref, v_ref, ck_ref, cv_ref, bias_ref, o_ref):
    ck = ck_ref[...].astype(BF16)
    cv = cv_ref[...].astype(BF16)
    for r in range(N_GRID_ROWS):
        r0 = min(max(r - NA_ROWS // 2, 0), N_GRID_ROWS - NA_ROWS)
        rows = pl.ds(r * GRID_W, GRID_W)
        keys = pl.ds(r0 * GRID_W, NA_KEYS)
        bias = jnp.concatenate([bias_ref[0, r], bias_ref[1, r]], axis=0)
        o = _attend(_stack_heads(q_ref[0, rows, :]), [k_ref[0, keys, :], ck], [v_ref[0, keys, :], cv],
                    biases=[bias, None])
        o_ref[0, rows, :] = _unstack_heads(o, GRID_W).astype(BF16)


def _attn_lat_c(l, z, cache_k, cache_v, bias):
    def zspec(g0):
        return pl.BlockSpec((1, DEC_SEQ, LANES), lambda c, b: (g0 + c, b, 0))

    cspec = pl.BlockSpec((None, None, PAST_LEN, LANES), lambda c, b: (b, l, 0, c))
    return pl.pallas_call(
        _attn_lat_c_kernel,
        grid=(HC // 2, DEC_BATCH),
        in_specs=[zspec(G_QC), zspec(G_KC), zspec(G_VC), cspec, cspec,
                  pl.BlockSpec((None, 2, N_GRID_ROWS, GRID_W, NA_KEYS), lambda c, b: (l, c, 0, 0, 0))],
        out_specs=pl.BlockSpec((1, DEC_SEQ, LANES), lambda c, b: (c, b, 0)),
        out_shape=jax.ShapeDtypeStruct((4, N_TOK, LANES), BF16),
        compiler_params=_params(("arbitrary", "arbitrary")),
        name="attn_lat_c",
    )(z, z, z, cache_k, cache_v, bias)


MERGE_TILE = 512


def _merge_kernel(x_ref, oa_ref, ob_ref, oc_ref, gate_ref, mod_ref, g2_ref, wa_ref, wb_ref, wc_ref, wo_ref,
                  xo_ref, h2_ref, wbr_scr, wo_scr):
    @pl.when(pl.program_id(0) == 0)
    def _():
        wbr_scr[0] = wa_ref[...].astype(BF16)
        wbr_scr[1] = wb_ref[...].astype(BF16)
        wbr_scr[2] = wc_ref[...].astype(BF16)
        wo_scr[...] = wo_ref[...].astype(BF16)

    y = None
    for k, o_ref in enumerate((oa_ref, ob_ref, oc_ref)):
        o = jnp.concatenate([o_ref[c] for c in range(4)], axis=-1)
        p = jnp.dot(o, wbr_scr[k], preferred_element_type=F32)
        g = jnp.concatenate([gate_ref[8 * k + c] for c in range(8)], axis=-1).astype(F32)
        y = g * p if y is None else y + g * p
    out = jnp.dot(y.astype(BF16), wo_scr[...], preferred_element_type=F32)
    m = mod_ref[0]
    x = x_ref[...] + m[:, 2 * D_MODEL:3 * D_MODEL] * out
    xo_ref[...] = x
    ms = jnp.mean(x * x, axis=-1, keepdims=True)
    h2 = x * lax.rsqrt(ms + EPS) * g2_ref[...]
    h2_ref[...] = (h2 * (1.0 + m[:, 4 * D_MODEL:5 * D_MODEL]) + m[:, 3 * D_MODEL:4 * D_MODEL]).astype(BF16)


def _merge(latent, l, x, oa, ob, oc, gates, mod, g_norm2, w_a, w_b, w_c, w_o):
    tm = MERGE_TILE
    if latent:
        mod_idx = lambda i: (1 + (i * tm) // DEC_SEQ, 0, 0)
    else:
        mod_idx = lambda i: (0, 0, 0)
    o_spec = pl.BlockSpec((4, tm, LANES), lambda i: (0, i, 0))
    wbr_spec = pl.BlockSpec((None, 4 * LANES, D_MODEL), lambda i: (l, 0, 0))
    return pl.pallas_call(
        _merge_kernel,
        grid=(N_TOK // tm,),
        in_specs=[pl.BlockSpec((tm, D_MODEL), lambda i: (i, 0)),
                  o_spec, o_spec, o_spec,
                  pl.BlockSpec((N_GATE_GROUPS, tm, LANES), lambda i: (0, i, 0)),
                  pl.BlockSpec((1, 1, 6 * D_MODEL), mod_idx),
                  pl.BlockSpec((None, 1, D_MODEL), lambda i: (l, 0, 0)),
                  wbr_spec, wbr_spec, wbr_spec,
                  pl.BlockSpec((None, D_MODEL, D_MODEL), lambda i: (l, 0, 0))],
        out_specs=[pl.BlockSpec((tm, D_MODEL), lambda i: (i, 0)),
                   pl.BlockSpec((tm, D_MODEL), lambda i: (i, 0))],
        out_shape=[jax.ShapeDtypeStruct((N_TOK, D_MODEL), F32),
                   jax.ShapeDtypeStruct((N_TOK, D_MODEL), BF16)],
        scratch_shapes=[pltpu.VMEM((3, 4 * LANES, D_MODEL), BF16), pltpu.VMEM((D_MODEL, D_MODEL), BF16)],
        input_output_aliases={0: 0},
        compiler_params=_params(("arbitrary",)),
        name="merge_lat" if latent else "merge_ctx",
    )(x, oa, ob, oc, gates, mod, g_norm2, w_a, w_b, w_c, w_o)


FF_TILE = 512


def _ffn_kernel(x_ref, h2_ref, mod_ref, w1_ref, w2_ref, xo_ref, acc_scr):
    f = pl.program_id(1)
    u = jnp.dot(h2_ref[...], w1_ref[...].astype(BF16), preferred_element_type=F32)
    u = jnp.square(jnp.maximum(u, 0.0)).astype(BF16)
    d = jnp.dot(u, w2_ref[...].astype(BF16), preferred_element_type=F32)

    @pl.when(f == 0)
    def _():
        acc_scr[...] = d

    @pl.when(f > 0)
    def _():
        acc_scr[...] += d

    @pl.when(f == D_FF // FF_TILE - 1)
    def _():
        xo_ref[...] = x_ref[...] + mod_ref[0][:, 5 * D_MODEL:6 * D_MODEL] * acc_scr[...]


def _ffn(latent, l, x, h2, mod, w1, w2):
    tm = ROW_TILE
    mod_idx = (lambda i, f: (1 + i, 0, 0)) if latent else (lambda i, f: (0, 0, 0))
    return pl.pallas_call(
        _ffn_kernel,
        grid=(N_TOK // tm, D_FF // FF_TILE),
        in_specs=[pl.BlockSpec((tm, D_MODEL), lambda i, f: (i, 0)),
                  pl.BlockSpec((tm, D_MODEL), lambda i, f: (i, 0)),
                  pl.BlockSpec((1, 1, 6 * D_MODEL), mod_idx),
                  pl.BlockSpec((None, D_MODEL, FF_TILE), lambda i, f: (l, 0, f)),
                  pl.BlockSpec((None, FF_TILE, D_MODEL), lambda i, f: (l, f, 0))],
        out_specs=pl.BlockSpec((tm, D_MODEL), lambda i, f: (i, 0)),
        out_shape=jax.ShapeDtypeStruct((N_TOK, D_MODEL), F32),
        scratch_shapes=[pltpu.VMEM((tm, D_MODEL), F32)],
        input_output_aliases={0: 0},
        compiler_params=_params(("arbitrary", "arbitrary")),
        name="ffn_lat" if latent else "ffn_ctx",
    )(x, h2, mod, w1, w2)


def _final_norm_kernel(x_ref, g_ref, o_ref):
    x = x_ref[...]
    ms = jnp.mean(x * x, axis=-1, keepdims=True)
    o_ref[...] = x * lax.rsqrt(ms + EPS) * g_ref[...]


def _final_norm(x, g):
    tm = ROW_TILE
    return pl.pallas_call(
        _final_norm_kernel,
        grid=(N_TOK // tm,),
        in_specs=[pl.BlockSpec((tm, D_MODEL), lambda i: (i, 0)),
                  pl.BlockSpec((1, D_MODEL), lambda i: (0, 0))],
        out_specs=pl.BlockSpec((tm, D_MODEL), lambda i: (i, 0)),
        out_shape=jax.ShapeDtypeStruct((N_TOK, D_MODEL), F32),
        compiler_params=_params(("arbitrary",)),
        name="final_norm",
    )(x, g)


def _rope_tables():
    nf = HEAD_DIM // 4
    t = jnp.arange(DEC_SEQ)
    row = (t // GRID_W).astype(F32)
    col = (t % GRID_W).astype(F32)
    inv = ROPE_BASE ** (-jnp.arange(nf, dtype=F32) / nf)
    ar = row[:, None] * inv[None, :]
    ac = col[:, None] * inv[None, :]
    cos = jnp.concatenate([jnp.cos(ar), jnp.cos(ar), jnp.cos(ac), jnp.cos(ac)], axis=-1)
    sin = jnp.concatenate([-jnp.sin(ar), jnp.sin(ar), -jnp.sin(ac), jnp.sin(ac)], axis=-1)
    return jnp.tile(cos, (1, 2)), jnp.tile(sin, (1, 2))


def _neighbourhood_bias(rel_bias):
    r = jnp.arange(N_GRID_ROWS)
    r0 = jnp.clip(r - NA_ROWS // 2, 0, N_GRID_ROWS - NA_ROWS)
    key_rows = r0[:, None] + jnp.arange(NA_ROWS)[None, :]
    cq = jnp.arange(GRID_W)
    c0 = jnp.clip(cq - NA_COLS // 2, 0, GRID_W - NA_COLS)
    ck = jnp.arange(GRID_W)
    in_win = (ck[None, :] >= c0[:, None]) & (ck[None, :] < c0[:, None] + NA_COLS)
    dr_idx = key_rows - r[:, None] + (NA_ROWS - 1)
    dc_idx = jnp.clip(ck[None, :] - cq[:, None] + (NA_COLS - 1), 0, 2 * NA_COLS - 2)
    bias = rel_bias[:, :, dr_idx[:, None, :, None], dc_idx[None, :, None, :]]
    bias = jnp.where(in_win[None, None, None, :, None, :], bias.astype(F32), NEG)
    return bias.reshape(DEPTH, HC, N_GRID_ROWS, GRID_W, NA_KEYS)


def kernel(x_prompt, x_sample, c, cache_a_k, cache_a_v, cache_b_k, cache_b_v, cache_c_k, cache_c_v, c_ctx, w_mod, b_mod, g_norm1, g_norm2, w_in, b_gate, lambda_qk, g_subln, g_qnorm, g_knorm, rel_bias, w_branch_a, w_branch_b, w_branch_c, w_out, w_ff1, w_ff2, g_final):
    xp = x_prompt.reshape(N_TOK, D_MODEL)
    xs = x_sample.reshape(N_TOK, D_MODEL)

    cvec = jnp.concatenate([c_ctx[None, :], c, jnp.zeros((3, D_MODEL), F32)], axis=0)
    mods = _modulation(cvec, w_mod, b_mod)

    rope_cos, rope_sin = _rope_tables()
    na_bias = _neighbourhood_bias(rel_bias)
    bd = jnp.kron(jnp.eye(2 * LANES // HEAD_DIM, dtype=F32),
                  jnp.full((HEAD_DIM, HEAD_DIM), 1.0 / HEAD_DIM, F32)).astype(BF16)
    gq = jnp.tile(g_qnorm, (1, 2 * LANES // DB)).reshape(DEPTH, 1, 2 * LANES)
    gk = jnp.tile(g_knorm, (1, LANES // DB)).reshape(DEPTH, 1, LANES)
    g1 = g_norm1.reshape(DEPTH, 1, D_MODEL)
    g2 = g_norm2.reshape(DEPTH, 1, D_MODEL)
    bg = b_gate.reshape(DEPTH, 1, 3 * D_MODEL)
    gsub = g_subln.reshape(DEPTH, 1, 2 * DA)

    lat_ak = cache_a_k.reshape(DEC_BATCH, DEPTH, PAST_LEN, 2 * HA * DA)
    lat_av = cache_a_v.reshape(DEC_BATCH, DEPTH, PAST_LEN, HA * 2 * DA)
    lat_bk = cache_b_k.reshape(DEC_BATCH, DEPTH, PAST_LEN, KVB * DB)
    lat_bv = cache_b_v.reshape(DEC_BATCH, DEPTH, PAST_LEN, KVB * DB)
    lat_ck = cache_c_k.reshape(DEC_BATCH, DEPTH, PAST_LEN, HC * DC)
    lat_cv = cache_c_v.reshape(DEC_BATCH, DEPTH, PAST_LEN, HC * DC)

    new_caches = [jnp.zeros((BATCH, DEPTH, SEQ, w), F32)
                  for w in (2 * HA * DA, HA * 2 * DA, KVB * DB, KVB * DB, HC * DC, HC * DC)]

    for l in range(DEPTH):
        mod = mods[l].reshape(8, 1, 6 * D_MODEL)

        outs = _inproj(False, l, xp, mod, g1, w_in, bg, gq, gk, bd, None, None, new_caches)
        z, gates, new_caches = outs[0], outs[1], list(outs[2:])
        oa, ob, oc = _attn_ctx(l, z, lambda_qk, gsub)
        xp, h2 = _merge(False, l, xp, oa, ob, oc, gates, mod, g2, w_branch_a, w_branch_b, w_branch_c, w_out)
        xp = _ffn(False, l, xp, h2, mod, w_ff1, w_ff2)

        z, gates = _inproj(True, l, xs, mod, g1, w_in, bg, gq, gk, bd, rope_cos, rope_sin, None)
        oa = _attn_lat_a(l, z, lat_ak, lat_av, lambda_qk, gsub)
        ob = _attn_lat_b(l, z, lat_bk, lat_bv)
        oc = _attn_lat_c(l, z, lat_ck, lat_cv, na_bias)
        xs, h2 = _merge(True, l, xs, oa, ob, oc, gates, mod, g2, w_branch_a, w_branch_b, w_branch_c, w_out)
        xs = _ffn(True, l, xs, h2, mod, w_ff1, w_ff2)

    gf = g_final.reshape(1, D_MODEL)
    y_prompt = _final_norm(xp, gf).reshape(BATCH, SEQ, D_MODEL)
    y_sample = _final_norm(xs, gf).reshape(DEC_BATCH, DEC_SEQ, D_MODEL)
    ak, av, bk, bv, ck, cv = new_caches
    return (y_prompt, y_sample,
            ak.reshape(BATCH, DEPTH, SEQ, 2 * HA, DA), av.reshape(BATCH, DEPTH, SEQ, HA, 2 * DA),
            bk.reshape(BATCH, DEPTH, SEQ, KVB, DB), bv.reshape(BATCH, DEPTH, SEQ, KVB, DB),
            ck.reshape(BATCH, DEPTH, SEQ, HC, DC), cv.reshape(BATCH, DEPTH, SEQ, HC, DC))
```

```python
import functools
import math

import jax
import jax.numpy as jnp
from jax import lax
from jax.experimental import pallas as pl
from jax.experimental.pallas import tpu as pltpu

D_MODEL = 1024
BATCH = 16
SEQ = 256
DEPTH = 4
DEC_BATCH = 4
DEC_SEQ = 1024
PAST_LEN = 256
GRID_W = 64
HA, DA = 4, 64
HB, KVB, DB = 8, 2, 64
HC, DC = 8, 64
NA_ROWS, NA_COLS = 8, 16
D_FF = 4 * D_MODEL
ROPE_BASE = 10000.0
EPS = 1e-6
NEG = -1e30
HEAD_DIM = 64
SCALE = HEAD_DIM ** -0.5

F32 = jnp.float32
BF16 = jnp.bfloat16

LANES = 128
N_QKV_GROUPS = 30
N_GATE_GROUPS = 24
COL_TILE = 768
GROUPS_PER_TILE = COL_TILE // LANES
N_QKV_TILES = N_QKV_GROUPS // GROUPS_PER_TILE
N_COL_TILES = N_QKV_TILES + N_GATE_GROUPS // GROUPS_PER_TILE
ROW_TILE = 1024
ROW_CHUNK = 256
N_TOK = BATCH * SEQ
VMEM_LIMIT_V7X = 58 * 1024 * 1024

G_QA, G_KA, G_VA, G_QB, G_KB, G_VB, G_QC, G_KC, G_VC = 0, 4, 8, 12, 16, 17, 18, 22, 26
CACHE_WIDTHS = (2 * HA * DA, HA * 2 * DA, KVB * DB, KVB * DB, HC * DC, HC * DC)

_NT = (((1,), (1,)), ((), ()))


def _params(sem, vmem=VMEM_LIMIT_V7X):
    return pltpu.CompilerParams(dimension_semantics=sem, vmem_limit_bytes=vmem)


def _lane_head(shape):
    return lax.shift_right_logical(lax.broadcasted_iota(jnp.int32, shape, len(shape) - 1), 6)


def _mod_kernel(c_ref, w_ref, b_ref, o_ref):
    c = c_ref[...]
    s = (c * jax.nn.sigmoid(c)).astype(BF16)
    o_ref[...] = jnp.dot(s, w_ref[...].astype(BF16), preferred_element_type=F32) + b_ref[...]


def _modulation(cvec, w_mod, b_mod):
    tn = 1536
    n6 = 6 * D_MODEL
    return pl.pallas_call(
        _mod_kernel,
        grid=(DEPTH, n6 // tn),
        in_specs=[pl.BlockSpec((8, D_MODEL), lambda l, n: (0, 0)),
                  pl.BlockSpec((None, D_MODEL, tn), lambda l, n: (l, 0, n)),
                  pl.BlockSpec((None, 1, tn), lambda l, n: (l, 0, n))],
        out_specs=pl.BlockSpec((None, 8, tn), lambda l, n: (l, 0, n)),
        out_shape=jax.ShapeDtypeStruct((DEPTH, 8, n6), F32),
        compiler_params=_params(("arbitrary", "arbitrary")),
        name="modulation",
    )(cvec, w_mod, b_mod.reshape(DEPTH, 1, n6))


def _rope(v, cos, sin):
    first = (lax.broadcasted_iota(jnp.int32, v.shape, 1) & 16) == 0
    partner = jnp.where(first, pltpu.roll(v, LANES - 16, 1), pltpu.roll(v, 16, 1))
    return v * cos + partner * sin


def _head_rmsnorm(v, bd, g):
    msq = jnp.dot((v * v).astype(BF16), bd, preferred_element_type=F32)
    return v * lax.rsqrt(msq + EPS) * g


def _make_inproj_kernel(latent, n_aliased):
    def kern(*refs):
        if latent:
            (x_ref, mod_ref, g1_ref, w_ref, bg_ref, gq_ref, gk_ref, bd_ref, cos_ref, sin_ref,
             z_ref, gate_ref, h_scr, acc_scr) = refs
        else:
            (x_ref, mod_ref, g1_ref, w_ref, bg_ref, gq_ref, gk_ref, bd_ref) = refs[:8]
            (z_ref, gate_ref, ak_ref, av_ref, bk_ref, bv_ref, ck_ref, cv_ref,
             h_scr, acc_scr) = refs[8 + n_aliased:]
        j = pl.program_id(1)

        @pl.when(j == 0)
        def _():
            x = x_ref[...]
            ms = jnp.mean(x * x, axis=-1, keepdims=True)
            y = x * lax.rsqrt(ms + EPS) * g1_ref[...]
            m = mod_ref[0]
            h_scr[...] = (y * (1.0 + m[:, D_MODEL:2 * D_MODEL]) + m[:, 0:D_MODEL]).astype(BF16)

        acc_scr[...] = jnp.dot(h_scr[...], w_ref[...].astype(BF16), preferred_element_type=F32)

        def chunks():
            for rc in range(ROW_TILE // ROW_CHUNK):
                yield rc, pl.ds(rc * ROW_CHUNK, ROW_CHUNK)

        def grp(a, c, n=1):
            return a[:, c * LANES:(c + n) * LANES]

        def rot(v, rows):
            return _rope(v, cos_ref[rows, :], sin_ref[rows, :]) if latent else v

        @pl.when(j == 0)
        def _():
            for rc, rows in chunks():
                a = acc_scr[rows, :]
                for c in range(GROUPS_PER_TILE):
                    z_ref[c, rows, :] = rot(grp(a, c), rows).astype(BF16)
                if not latent:
                    ak_ref[rc, :, 0:256] = grp(a, 4, 2)

        @pl.when(j == 1)
        def _():
            for rc, rows in chunks():
                a = acc_scr[rows, :]
                for c in range(2):
                    z_ref[c, rows, :] = rot(grp(a, c), rows).astype(BF16)
                for c in range(2, GROUPS_PER_TILE):
                    z_ref[c, rows, :] = grp(a, c).astype(BF16)
                if not latent:
                    ak_ref[rc, :, 256:512] = grp(a, 0, 2)
                    av_ref[rc] = grp(a, 2, 4)

        @pl.when(j == 2)
        def _():
            bd = bd_ref[...]
            for rc, rows in chunks():
                a = acc_scr[rows, :]
                for half in range(2):
                    qn = _head_rmsnorm(grp(a, 2 * half, 2), bd, gq_ref[...])
                    for c in range(2):
                        z_ref[2 * half + c, rows, :] = rot(grp(qn, c), rows).astype(BF16)
                kn = _head_rmsnorm(grp(a, 4), bd[0:LANES, 0:LANES], gk_ref[...])
                z_ref[4, rows, :] = rot(kn, rows).astype(BF16)
                z_ref[5, rows, :] = grp(a, 5).astype(BF16)
                if not latent:
                    bk_ref[rc] = kn
                    bv_ref[rc] = grp(a, 5)

        @pl.when(j == 3)
        def _():
            for rc, rows in chunks():
                a = acc_scr[rows, :]
                for c in range(GROUPS_PER_TILE):
                    z_ref[c, rows, :] = grp(a, c).astype(BF16)
                if not latent:
                    ck_ref[rc, :, 0:256] = grp(a, 4, 2)

        @pl.when(j == 4)
        def _():
            for rc, rows in chunks():
                a = acc_scr[rows, :]
                for c in range(GROUPS_PER_TILE):
                    z_ref[c, rows, :] = grp(a, c).astype(BF16)
                if not latent:
                    ck_ref[rc, :, 256:512] = grp(a, 0, 2)
                    cv_ref[rc] = grp(a, 2, 4)

        @pl.when(j >= N_QKV_TILES)
        def _():
            for rc, rows in chunks():
                a = acc_scr[rows, :] + bg_ref[...]
                for c in range(GROUPS_PER_TILE):
                    gate_ref[c, rows, :] = jax.nn.sigmoid(grp(a, c)).astype(BF16)

    return kern


def _inproj(latent, l, x, mod, g_norm1, w_in, b_gate, gq, gk, bd, rope_cos, rope_sin, caches):
    n_row = N_TOK // ROW_TILE
    mod_idx = (lambda i, j: (1 + i, 0, 0)) if latent else (lambda i, j: (0, 0, 0))
    in_specs = [
        pl.BlockSpec((ROW_TILE, D_MODEL), lambda i, j: (i, 0)),
        pl.BlockSpec((1, 1, 6 * D_MODEL), mod_idx),
        pl.BlockSpec((None, 1, D_MODEL), lambda i, j: (l, 0, 0)),
        pl.BlockSpec((None, D_MODEL, COL_TILE), lambda i, j: (l, 0, j)),
        pl.BlockSpec((None, 1, COL_TILE), lambda i, j: (l, 0, jnp.maximum(j - N_QKV_TILES, 0))),
        pl.BlockSpec((None, 1, 2 * LANES), lambda i, j: (l, 0, 0)),
        pl.BlockSpec((None, 1, LANES), lambda i, j: (l, 0, 0)),
        pl.BlockSpec((2 * LANES, 2 * LANES), lambda i, j: (0, 0)),
    ]
    args = [x, mod, g_norm1, w_in, b_gate, gq, gk, bd]
    out_specs = [
        pl.BlockSpec((GROUPS_PER_TILE, ROW_TILE, LANES), lambda i, j: (jnp.minimum(j, N_QKV_TILES - 1), i, 0)),
        pl.BlockSpec((GROUPS_PER_TILE, ROW_TILE, LANES), lambda i, j: (jnp.maximum(j - N_QKV_TILES, 0), i, 0)),
    ]
    out_shape = [jax.ShapeDtypeStruct((N_QKV_GROUPS, N_TOK, LANES), BF16),
                 jax.ShapeDtypeStruct((N_GATE_GROUPS, N_TOK, LANES), BF16)]
    aliases = {}
    if latent:
        in_specs += [pl.BlockSpec((DEC_SEQ, LANES), lambda i, j: (0, 0))] * 2
        args += [rope_cos, rope_sin]
    else:
        nb = ROW_TILE // SEQ
        for k, w in enumerate(CACHE_WIDTHS):
            if caches is not None:
                in_specs.append(pl.BlockSpec(memory_space=pl.ANY))
                args.append(caches[k])
                aliases[8 + k] = 2 + k
            out_specs.append(pl.BlockSpec((nb, None, SEQ, w), lambda i, j: (i, l, 0, 0)))
            out_shape.append(jax.ShapeDtypeStruct((BATCH, DEPTH, SEQ, w), F32))
    return pl.pallas_call(
        _make_inproj_kernel(latent, len(aliases)),
        grid=(n_row, N_COL_TILES),
        in_specs=in_specs,
        out_specs=out_specs,
        out_shape=out_shape,
        scratch_shapes=[pltpu.VMEM((ROW_TILE, D_MODEL), BF16), pltpu.VMEM((ROW_TILE, COL_TILE), F32)],
        input_output_aliases=aliases,
        compiler_params=_params(("arbitrary", "arbitrary")),
        name="inproj_lat" if latent else "inproj_ctx",
    )(*args)


def _mask_head(q, head):
    qf = q.astype(F32)
    keep = _lane_head(qf.shape) == head
    return (jnp.where(keep, qf, 0.0) * SCALE).astype(BF16)


def _stack_heads(q):
    return jnp.concatenate([_mask_head(q, 0), _mask_head(q, 1)], axis=0)


def _unstack_heads(o, rows):
    return jnp.where(_lane_head((rows, LANES)) == 1, o[rows:2 * rows], o[0:rows])


def _dup_head(kv, head):
    f = kv.astype(F32)
    r = pltpu.roll(f, HEAD_DIM, 1)
    return jnp.where(_lane_head(f.shape) == head, f, r).astype(BF16)


def _attend(qm, ks, vs, biases=None):
    ss = [lax.dot_general(qm, k, _NT, preferred_element_type=F32) for k in ks]
    if biases is not None:
        ss = [s if b is None else s + b for s, b in zip(ss, biases)]
    m = functools.reduce(jnp.maximum, [jnp.max(s, axis=-1, keepdims=True) for s in ss])
    ps = [jnp.exp(s - m) for s in ss]
    den = functools.reduce(jnp.add, [jnp.sum(p, axis=-1, keepdims=True) for p in ps])
    o = functools.reduce(jnp.add, [jnp.dot(p.astype(BF16), v, preferred_element_type=F32)
                                    for p, v in zip(ps, vs)])
    return o * (1.0 / den)


def _diff_lambda(lam_ref, lam_init):
    lq = lam_ref[...]
    a = jnp.sum(lq[0:1] * lq[1:2], axis=-1, keepdims=True)
    b = jnp.sum(lq[2:3] * lq[3:4], axis=-1, keepdims=True)
    return jnp.exp(a) - jnp.exp(b) + lam_init


def _diff_combine(o1, o2, lam, gsub, lam_init):
    o = o1 - lam * o2
    ms = jnp.mean(o * o, axis=-1, keepdims=True)
    return (o * lax.rsqrt(ms + EPS) * gsub) * (1.0 - lam_init)


def _make_attn_ctx_kernel(lam_init):
    def kern(z_ref, lam_ref, gsub_ref, oa_ref, ob_ref, oc_ref):
        lam = _diff_lambda(lam_ref, lam_init)
        gsub = gsub_ref[...]
        for vh in range(HA):
            hi = vh % 2
            v = z_ref[G_VA + vh]
            o1 = _attend(_mask_head(z_ref[G_QA + vh // 2], hi), [z_ref[G_KA + vh // 2]], [v])
            o2 = _attend(_mask_head(z_ref[G_QA + 2 + vh // 2], hi), [z_ref[G_KA + 2 + vh // 2]], [v])
            oa_ref[vh] = _diff_combine(o1, o2, lam, gsub, lam_init).astype(BF16)
        for g in range(KVB):
            kd = _dup_head(z_ref[G_KB], g)
            vd = _dup_head(z_ref[G_VB], g)
            for c in range(2 * g, 2 * g + 2):
                o = _attend(_stack_heads(z_ref[G_QB + c]), [kd], [vd])
                ob_ref[c] = _unstack_heads(o, SEQ).astype(BF16)
        for c in range(HC // 2):
            o = _attend(_stack_heads(z_ref[G_QC + c]), [z_ref[G_KC + c]], [z_ref[G_VC + c]])
            oc_ref[c] = _unstack_heads(o, SEQ).astype(BF16)
    return kern


def _attn_ctx(l, z, lambda_qk, g_subln):
    lam_init = 0.8 - 0.6 * math.exp(-0.3 * l)
    o_spec = pl.BlockSpec((4, SEQ, LANES), lambda b: (0, b, 0))
    o_shape = jax.ShapeDtypeStruct((4, N_TOK, LANES), BF16)
    return pl.pallas_call(
        _make_attn_ctx_kernel(lam_init),
        grid=(BATCH,),
        in_specs=[pl.BlockSpec((N_QKV_GROUPS, SEQ, LANES), lambda b: (0, b, 0)),
                  pl.BlockSpec((None, 4, DA), lambda b: (l, 0, 0)),
                  pl.BlockSpec((None, 1, 2 * DA), lambda b: (l, 0, 0))],
        out_specs=[o_spec, o_spec, o_spec],
        out_shape=[o_shape, o_shape, o_shape],
        compiler_params=_params(("arbitrary",)),
        name="attn_ctx",
    )(z, lambda_qk, g_subln)


Q_BLK = 256


def _make_attn_lat_a_kernel(lam_init):
    def kern(q1_ref, q2_ref, k1_ref, k2_ref, v_ref, ck1_ref, ck2_ref, cv_ref, lam_ref, gsub_ref, o_ref):
        hi = jnp.bitwise_and(pl.program_id(1), 1)
        lam = _diff_lambda(lam_ref, lam_init)
        gsub = gsub_ref[...]
        ck1 = ck1_ref[...].astype(BF16)
        ck2 = ck2_ref[...].astype(BF16)
        cv = cv_ref[...].astype(BF16)
        k1, k2, v = k1_ref[0], k2_ref[0], v_ref[0]
        for qb in range(DEC_SEQ // Q_BLK):
            rows = pl.ds(qb * Q_BLK, Q_BLK)
            o1 = _attend(_mask_head(q1_ref[0, rows, :], hi), [k1, ck1], [v, cv])
            o2 = _attend(_mask_head(q2_ref[0, rows, :], hi), [k2, ck2], [v, cv])
            o_ref[0, rows, :] = _diff_combine(o1, o2, lam, gsub, lam_init).astype(BF16)
    return kern


def _attn_lat_a(l, z, cache_k, cache_v, lambda_qk, g_subln):
    lam_init = 0.8 - 0.6 * math.exp(-0.3 * l)

    def zspec(fn):
        return pl.BlockSpec((1, DEC_SEQ, LANES), lambda b, h: (fn(h), b, 0))

    def cspec(fn):
        return pl.BlockSpec((None, None, PAST_LEN, LANES), lambda b, h: (b, l, 0, fn(h)))

    return pl.pallas_call(
        _make_attn_lat_a_kernel(lam_init),
        grid=(DEC_BATCH, HA),
        in_specs=[zspec(lambda h: G_QA + h // 2), zspec(lambda h: G_QA + 2 + h // 2),
                  zspec(lambda h: G_KA + h // 2), zspec(lambda h: G_KA + 2 + h // 2),
                  zspec(lambda h: G_VA + h),
                  cspec(lambda h: h // 2), cspec(lambda h: 2 + h // 2), cspec(lambda h: h),
                  pl.BlockSpec((None, 4, DA), lambda b, h: (l, 0, 0)),
                  pl.BlockSpec((None, 1, 2 * DA), lambda b, h: (l, 0, 0))],
        out_specs=pl.BlockSpec((1, DEC_SEQ, LANES), lambda b, h: (h, b, 0)),
        out_shape=jax.ShapeDtypeStruct((4, N_TOK, LANES), BF16),
        compiler_params=_params(("arbitrary", "arbitrary")),
        name="attn_lat_a",
    )(z, z, z, z, z, cache_k, cache_k, cache_v, lambda_qk, g_subln)


def _attn_lat_b_kernel(q_ref, k_ref, v_ref, ck_ref, cv_ref, o_ref):
    hi = lax.shift_right_logical(pl.program_id(1), 1)
    kd = _dup_head(k_ref[0], hi)
    vd = _dup_head(v_ref[0], hi)
    ckd = _dup_head(ck_ref[...].astype(BF16), hi)
    cvd = _dup_head(cv_ref[...].astype(BF16), hi)
    for qb in range(DEC_SEQ // Q_BLK):
        rows = pl.ds(qb * Q_BLK, Q_BLK)
        o = _attend(_stack_heads(q_ref[0, rows, :]), [kd, ckd], [vd, cvd])
        o_ref[0, rows, :] = _unstack_heads(o, Q_BLK).astype(BF16)


def _attn_lat_b(l, z, cache_k, cache_v):
    cspec = pl.BlockSpec((None, None, PAST_LEN, LANES), lambda b, c: (b, l, 0, 0))
    return pl.pallas_call(
        _attn_lat_b_kernel,
        grid=(DEC_BATCH, HB // 2),
        in_specs=[pl.BlockSpec((1, DEC_SEQ, LANES), lambda b, c: (G_QB + c, b, 0)),
                  pl.BlockSpec((1, DEC_SEQ, LANES), lambda b, c: (G_KB, b, 0)),
                  pl.BlockSpec((1, DEC_SEQ, LANES), lambda b, c: (G_VB, b, 0)),
                  cspec, cspec],
        out_specs=pl.BlockSpec((1, DEC_SEQ, LANES), lambda b, c: (c, b, 0)),
        out_shape=jax.ShapeDtypeStruct((4, N_TOK, LANES), BF16),
        compiler_params=_params(("arbitrary", "arbitrary")),
        name="attn_lat_b",
    )(z, z, z, cache_k, cache_v)


N_GRID_ROWS = DEC_SEQ // GRID_W
NA_KEYS = NA_ROWS * GRID_W


def _attn_lat_c_kernel(q_ref, k_ref, v_ref, ck_ref, cv_ref, bias_ref, o_ref):
    ck = ck_ref[...].astype(BF16)
    cv = cv_ref[...].astype(BF16)
    for r in range(N_GRID_ROWS):
        r0 = min(max(r - NA_ROWS // 2, 0), N_GRID_ROWS - NA_ROWS)
        rows = pl.ds(r * GRID_W, GRID_W)
        keys = pl.ds(r0 * GRID_W, NA_KEYS)
        cls = _row_class(r)
        bias = jnp.concatenate([bias_ref[0, cls], bias_ref[1, cls]], axis=0)
        o = _attend(_stack_heads(q_ref[0, rows, :]), [k_ref[0, keys, :], ck], [v_ref[0, keys, :], cv],
                    biases=[bias, None])
        o_ref[0, rows, :] = _unstack_heads(o, GRID_W).astype(BF16)


def _attn_lat_c(l, z, cache_k, cache_v, bias):
    def zspec(g0):
        return pl.BlockSpec((1, DEC_SEQ, LANES), lambda c, b: (g0 + c, b, 0))

    cspec = pl.BlockSpec((None, None, PAST_LEN, LANES), lambda c, b: (b, l, 0, c))
    return pl.pallas_call(
        _attn_lat_c_kernel,
        grid=(HC // 2, DEC_BATCH),
        in_specs=[zspec(G_QC), zspec(G_KC), zspec(G_VC), cspec, cspec,
                  pl.BlockSpec((None, 2, NA_ROWS, GRID_W, NA_KEYS), lambda c, b: (l, c, 0, 0, 0))],
        out_specs=pl.BlockSpec((1, DEC_SEQ, LANES), lambda c, b: (c, b, 0)),
        out_shape=jax.ShapeDtypeStruct((4, N_TOK, LANES), BF16),
        compiler_params=_params(("arbitrary", "arbitrary")),
        name="attn_lat_c",
    )(z, z, z, cache_k, cache_v, bias)


MERGE_TILE = 512


def _merge_kernel(x_ref, oa_ref, ob_ref, oc_ref, gate_ref, mod_ref, g2_ref, wa_ref, wb_ref, wc_ref, wo_ref,
                  xo_ref, h2_ref, wbr_scr, wo_scr):
    @pl.when(pl.program_id(0) == 0)
    def _():
        wbr_scr[0] = wa_ref[...].astype(BF16)
        wbr_scr[1] = wb_ref[...].astype(BF16)
        wbr_scr[2] = wc_ref[...].astype(BF16)
        wo_scr[...] = wo_ref[...].astype(BF16)

    y = None
    for k, o_ref in enumerate((oa_ref, ob_ref, oc_ref)):
        o = jnp.concatenate([o_ref[c] for c in range(4)], axis=-1)
        p = jnp.dot(o, wbr_scr[k], preferred_element_type=F32)
        g = jnp.concatenate([gate_ref[8 * k + c] for c in range(8)], axis=-1).astype(F32)
        y = g * p if y is None else y + g * p
    out = jnp.dot(y.astype(BF16), wo_scr[...], preferred_element_type=F32)
    m = mod_ref[0]
    x = x_ref[...] + m[:, 2 * D_MODEL:3 * D_MODEL] * out
    xo_ref[...] = x
    ms = jnp.mean(x * x, axis=-1, keepdims=True)
    h2 = x * lax.rsqrt(ms + EPS) * g2_ref[...]
    h2_ref[...] = (h2 * (1.0 + m[:, 4 * D_MODEL:5 * D_MODEL]) + m[:, 3 * D_MODEL:4 * D_MODEL]).astype(BF16)


def _merge(latent, l, x, oa, ob, oc, gates, mod, g_norm2, w_a, w_b, w_c, w_o):
    tm = MERGE_TILE
    if latent:
        mod_idx = lambda i: (1 + (i * tm) // DEC_SEQ, 0, 0)
    else:
        mod_idx = lambda i: (0, 0, 0)
    o_spec = pl.BlockSpec((4, tm, LANES), lambda i: (0, i, 0))
    wbr_spec = pl.BlockSpec((None, 4 * LANES, D_MODEL), lambda i: (l, 0, 0))
    return pl.pallas_call(
        _merge_kernel,
        grid=(N_TOK // tm,),
        in_specs=[pl.BlockSpec((tm, D_MODEL), lambda i: (i, 0)),
                  o_spec, o_spec, o_spec,
                  pl.BlockSpec((N_GATE_GROUPS, tm, LANES), lambda i: (0, i, 0)),
                  pl.BlockSpec((1, 1, 6 * D_MODEL), mod_idx),
                  pl.BlockSpec((None, 1, D_MODEL), lambda i: (l, 0, 0)),
                  wbr_spec, wbr_spec, wbr_spec,
                  pl.BlockSpec((None, D_MODEL, D_MODEL), lambda i: (l, 0, 0))],
        out_specs=[pl.BlockSpec((tm, D_MODEL), lambda i: (i, 0)),
                   pl.BlockSpec((tm, D_MODEL), lambda i: (i, 0))],
        out_shape=[jax.ShapeDtypeStruct((N_TOK, D_MODEL), F32),
                   jax.ShapeDtypeStruct((N_TOK, D_MODEL), BF16)],
        scratch_shapes=[pltpu.VMEM((3, 4 * LANES, D_MODEL), BF16), pltpu.VMEM((D_MODEL, D_MODEL), BF16)],
        compiler_params=_params(("arbitrary",)),
        name="merge_lat" if latent else "merge_ctx",
    )(x, oa, ob, oc, gates, mod, g_norm2, w_a, w_b, w_c, w_o)


FF_TILE = 512


def _ffn_kernel(x_ref, h2_ref, mod_ref, w1_ref, w2_ref, xo_ref, acc_scr):
    f = pl.program_id(1)
    u = jnp.dot(h2_ref[...], w1_ref[...].astype(BF16), preferred_element_type=F32)
    u = jnp.square(jnp.maximum(u, 0.0)).astype(BF16)
    d = jnp.dot(u, w2_ref[...].astype(BF16), preferred_element_type=F32)

    @pl.when(f == 0)
    def _():
        acc_scr[...] = d

    @pl.when(f > 0)
    def _():
        acc_scr[...] += d

    @pl.when(f == D_FF // FF_TILE - 1)
    def _():
        xo_ref[...] = x_ref[...] + mod_ref[0][:, 5 * D_MODEL:6 * D_MODEL] * acc_scr[...]


def _ffn(latent, l, x, h2, mod, w1, w2):
    tm = ROW_TILE
    mod_idx = (lambda i, f: (1 + i, 0, 0)) if latent else (lambda i, f: (0, 0, 0))
    return pl.pallas_call(
        _ffn_kernel,
        grid=(N_TOK // tm, D_FF // FF_TILE),
        in_specs=[pl.BlockSpec((tm, D_MODEL), lambda i, f: (i, 0)),
                  pl.BlockSpec((tm, D_MODEL), lambda i, f: (i, 0)),
                  pl.BlockSpec((1, 1, 6 * D_MODEL), mod_idx),
                  pl.BlockSpec((None, D_MODEL, FF_TILE), lambda i, f: (l, 0, f)),
                  pl.BlockSpec((None, FF_TILE, D_MODEL), lambda i, f: (l, f, 0))],
        out_specs=pl.BlockSpec((tm, D_MODEL), lambda i, f: (i, 0)),
        out_shape=jax.ShapeDtypeStruct((N_TOK, D_MODEL), F32),
        scratch_shapes=[pltpu.VMEM((tm, D_MODEL), F32)],
        compiler_params=_params(("arbitrary", "arbitrary")),
        name="ffn_lat" if latent else "ffn_ctx",
    )(x, h2, mod, w1, w2)


def _final_norm_kernel(x_ref, g_ref, o_ref):
    x = x_ref[...]
    ms = jnp.mean(x * x, axis=-1, keepdims=True)
    o_ref[...] = x * lax.rsqrt(ms + EPS) * g_ref[...]


def _final_norm(x, g):
    tm = ROW_TILE
    return pl.pallas_call(
        _final_norm_kernel,
        grid=(N_TOK // tm,),
        in_specs=[pl.BlockSpec((tm, D_MODEL), lambda i: (i, 0)),
                  pl.BlockSpec((1, D_MODEL), lambda i: (0, 0))],
        out_specs=pl.BlockSpec((tm, D_MODEL), lambda i: (i, 0)),
        out_shape=jax.ShapeDtypeStruct((N_TOK, D_MODEL), F32),
        compiler_params=_params(("arbitrary",)),
        name="final_norm",
    )(x, g)


def _rope_tables():
    nf = HEAD_DIM // 4
    t = jnp.arange(DEC_SEQ)
    row = (t // GRID_W).astype(F32)
    col = (t % GRID_W).astype(F32)
    inv = ROPE_BASE ** (-jnp.arange(nf, dtype=F32) / nf)
    ar = row[:, None] * inv[None, :]
    ac = col[:, None] * inv[None, :]
    cos = jnp.concatenate([jnp.cos(ar), jnp.cos(ar), jnp.cos(ac), jnp.cos(ac)], axis=-1)
    sin = jnp.concatenate([-jnp.sin(ar), jnp.sin(ar), -jnp.sin(ac), jnp.sin(ac)], axis=-1)
    return jnp.tile(cos, (1, 2)), jnp.tile(sin, (1, 2))


def _row_class(r):
    return r - min(max(r - NA_ROWS // 2, 0), N_GRID_ROWS - NA_ROWS)


def _neighbourhood_bias(rel_bias):
    w = GRID_W
    span = 2 * w - 1
    lo = w - NA_COLS
    v = jnp.pad(rel_bias, ((0, 0), (0, 0), (0, 0), (lo, span + 1 - lo - (2 * NA_COLS - 1))))
    toep = jnp.tile(v, (1, 1, 1, w))[..., :w * span].reshape(DEPTH, HC, 2 * NA_ROWS - 1, w, span)[..., w - 1:]
    cq = jnp.arange(w)
    c0 = jnp.clip(cq - NA_COLS // 2, 0, w - NA_COLS)
    in_win = (cq[None, :] >= c0[:, None]) & (cq[None, :] < c0[:, None] + NA_COLS)
    t = jnp.where(in_win[None, None, None], toep, NEG)
    slabs = []
    for cls in range(NA_ROWS):
        s = t[:, :, NA_ROWS - 1 - cls:2 * NA_ROWS - 1 - cls]
        slabs.append(jnp.swapaxes(s, 2, 3).reshape(DEPTH, HC, w, NA_KEYS))
    return jnp.stack(slabs, axis=2)


def kernel(x_prompt, x_sample, c, cache_a_k, cache_a_v, cache_b_k, cache_b_v, cache_c_k, cache_c_v, c_ctx, w_mod, b_mod, g_norm1, g_norm2, w_in, b_gate, lambda_qk, g_subln, g_qnorm, g_knorm, rel_bias, w_branch_a, w_branch_b, w_branch_c, w_out, w_ff1, w_ff2, g_final):
    xp = x_prompt.reshape(N_TOK, D_MODEL)
    xs = x_sample.reshape(N_TOK, D_MODEL)

    cvec = jnp.concatenate([c_ctx[None, :], c, jnp.zeros((3, D_MODEL), F32)], axis=0)
    mods = _modulation(cvec, w_mod, b_mod)

    rope_cos, rope_sin = _rope_tables()
    na_bias = _neighbourhood_bias(rel_bias)
    bd = jnp.kron(jnp.eye(2 * LANES // HEAD_DIM, dtype=F32),
                  jnp.full((HEAD_DIM, HEAD_DIM), 1.0 / HEAD_DIM, F32)).astype(BF16)
    gq = jnp.tile(g_qnorm, (1, 2 * LANES // DB)).reshape(DEPTH, 1, 2 * LANES)
    gk = jnp.tile(g_knorm, (1, LANES // DB)).reshape(DEPTH, 1, LANES)
    g1 = g_norm1.reshape(DEPTH, 1, D_MODEL)
    g2 = g_norm2.reshape(DEPTH, 1, D_MODEL)
    bg = b_gate.reshape(DEPTH, 1, 3 * D_MODEL)
    gsub = g_subln.reshape(DEPTH, 1, 2 * DA)

    lat_ak = cache_a_k.reshape(DEC_BATCH, DEPTH, PAST_LEN, 2 * HA * DA)
    lat_av = cache_a_v.reshape(DEC_BATCH, DEPTH, PAST_LEN, HA * 2 * DA)
    lat_bk = cache_b_k.reshape(DEC_BATCH, DEPTH, PAST_LEN, KVB * DB)
    lat_bv = cache_b_v.reshape(DEC_BATCH, DEPTH, PAST_LEN, KVB * DB)
    lat_ck = cache_c_k.reshape(DEC_BATCH, DEPTH, PAST_LEN, HC * DC)
    lat_cv = cache_c_v.reshape(DEC_BATCH, DEPTH, PAST_LEN, HC * DC)

    new_caches = None

    for l in range(DEPTH):
        mod = mods[l].reshape(8, 1, 6 * D_MODEL)

        outs = _inproj(False, l, xp, mod, g1, w_in, bg, gq, gk, bd, None, None, new_caches)
        z, gates, new_caches = outs[0], outs[1], list(outs[2:])
        oa, ob, oc = _attn_ctx(l, z, lambda_qk, gsub)
        xp, h2 = _merge(False, l, xp, oa, ob, oc, gates, mod, g2, w_branch_a, w_branch_b, w_branch_c, w_out)
        xp = _ffn(False, l, xp, h2, mod, w_ff1, w_ff2)

        z, gates = _inproj(True, l, xs, mod, g1, w_in, bg, gq, gk, bd, rope_cos, rope_sin, None)
        oa = _attn_lat_a(l, z, lat_ak, lat_av, lambda_qk, gsub)
        ob = _attn_lat_b(l, z, lat_bk, lat_bv)
        oc = _attn_lat_c(l, z, lat_ck, lat_cv, na_bias)
        xs, h2 = _merge(True, l, xs, oa, ob, oc, gates, mod, g2, w_branch_a, w_branch_b, w_branch_c, w_out)
        xs = _ffn(True, l, xs, h2, mod, w_ff1, w_ff2)

    gf = g_final.reshape(1, D_MODEL)
    y_prompt = _final_norm(xp, gf).reshape(BATCH, SEQ, D_MODEL)
    y_sample = _final_norm(xs, gf).reshape(DEC_BATCH, DEC_SEQ, D_MODEL)
    ak, av, bk, bv, ck, cv = new_caches
    return (y_prompt, y_sample,
            ak.reshape(BATCH, DEPTH, SEQ, 2 * HA, DA), av.reshape(BATCH, DEPTH, SEQ, HA, 2 * DA),
            bk.reshape(BATCH, DEPTH, SEQ, KVB, DB), bv.reshape(BATCH, DEPTH, SEQ, KVB, DB),
            ck.reshape(BATCH, DEPTH, SEQ, HC, DC), cv.reshape(BATCH, DEPTH, SEQ, HC, DC))
```

```python
import functools
import math

import jax
import jax.numpy as jnp
from jax import lax
from jax.experimental import pallas as pl
from jax.experimental.pallas import tpu as pltpu

D_MODEL = 1024
BATCH = 16
SEQ = 256
DEPTH = 4
DEC_BATCH = 4
DEC_SEQ = 1024
PAST_LEN = 256
GRID_W = 64
HA, DA = 4, 64
HB, KVB, DB = 8, 2, 64
HC, DC = 8, 64
NA_ROWS, NA_COLS = 8, 16
D_FF = 4 * D_MODEL
ROPE_BASE = 10000.0
EPS = 1e-6
NEG = -1e30
HEAD_DIM = 64
SCALE = HEAD_DIM ** -0.5

F32 = jnp.float32
BF16 = jnp.bfloat16

LANES = 128
N_QKV_GROUPS = 30
N_GATE_GROUPS = 24
COL_TILE = 768
GROUPS_PER_TILE = COL_TILE // LANES
N_QKV_TILES = N_QKV_GROUPS // GROUPS_PER_TILE
N_COL_TILES = N_QKV_TILES + N_GATE_GROUPS // GROUPS_PER_TILE
ROW_TILE = 1024
ROW_CHUNK = 256
N_TOK = BATCH * SEQ
VMEM_LIMIT_V7X = 58 * 1024 * 1024

G_QA, G_KA, G_VA, G_QB, G_KB, G_VB, G_QC, G_KC, G_VC = 0, 4, 8, 12, 16, 17, 18, 22, 26
CACHE_WIDTHS = (2 * HA * DA, HA * 2 * DA, KVB * DB, KVB * DB, HC * DC, HC * DC)

_NT = (((1,), (1,)), ((), ()))


def _params(sem, vmem=VMEM_LIMIT_V7X):
    return pltpu.CompilerParams(dimension_semantics=sem, vmem_limit_bytes=vmem)


def _lane_head(shape):
    return lax.shift_right_logical(lax.broadcasted_iota(jnp.int32, shape, len(shape) - 1), 6)


def _mod_kernel(c_ref, w_ref, b_ref, o_ref):
    c = c_ref[...]
    s = (c * jax.nn.sigmoid(c)).astype(BF16)
    o_ref[...] = jnp.dot(s, w_ref[...].astype(BF16), preferred_element_type=F32) + b_ref[...]


def _modulation(cvec, w_mod, b_mod):
    tn = 1536
    n6 = 6 * D_MODEL
    return pl.pallas_call(
        _mod_kernel,
        grid=(DEPTH, n6 // tn),
        in_specs=[pl.BlockSpec((8, D_MODEL), lambda l, n: (0, 0)),
                  pl.BlockSpec((None, D_MODEL, tn), lambda l, n: (l, 0, n)),
                  pl.BlockSpec((None, 1, tn), lambda l, n: (l, 0, n))],
        out_specs=pl.BlockSpec((None, 8, tn), lambda l, n: (l, 0, n)),
        out_shape=jax.ShapeDtypeStruct((DEPTH, 8, n6), F32),
        compiler_params=_params(("arbitrary", "arbitrary")),
        name="modulation",
    )(cvec, w_mod, b_mod.reshape(DEPTH, 1, n6))


def _rope(v, cos, sin):
    first = (lax.broadcasted_iota(jnp.int32, v.shape, 1) & 16) == 0
    partner = jnp.where(first, pltpu.roll(v, LANES - 16, 1), pltpu.roll(v, 16, 1))
    return v * cos + partner * sin


def _head_rmsnorm(v, bd, g):
    msq = jnp.dot((v * v).astype(BF16), bd, preferred_element_type=F32)
    return v * lax.rsqrt(msq + EPS) * g


def _make_inproj_kernel(latent, n_aliased):
    def kern(*refs):
        if latent:
            (x_ref, mod_ref, g1_ref, w_ref, bg_ref, gq_ref, gk_ref, bd_ref, cos_ref, sin_ref,
             z_ref, gate_ref, h_scr) = refs
        else:
            (x_ref, mod_ref, g1_ref, w_ref, bg_ref, gq_ref, gk_ref, bd_ref) = refs[:8]
            (z_ref, gate_ref, ak_ref, av_ref, bk_ref, bv_ref, ck_ref, cv_ref,
             h_scr) = refs[8 + n_aliased:]
        j = pl.program_id(1)

        def chunks(first=False):
            wbf = w_ref[...].astype(BF16)
            for rc in range(ROW_TILE // ROW_CHUNK):
                rows = pl.ds(rc * ROW_CHUNK, ROW_CHUNK)
                if first:
                    x = x_ref[rows, :]
                    ms = jnp.mean(x * x, axis=-1, keepdims=True)
                    y = x * lax.rsqrt(ms + EPS) * g1_ref[...]
                    m = mod_ref[0]
                    h = (y * (1.0 + m[:, D_MODEL:2 * D_MODEL]) + m[:, 0:D_MODEL]).astype(BF16)
                    h_scr[rows, :] = h
                else:
                    h = h_scr[rows, :]
                yield rc, rows, jnp.dot(h, wbf, preferred_element_type=F32)

        def grp(a, c, n=1):
            return a[:, c * LANES:(c + n) * LANES]

        def rot(v, rows):
            return _rope(v, cos_ref[rows, :], sin_ref[rows, :]) if latent else v

        @pl.when(j == 0)
        def _():
            for rc, rows, a in chunks(first=True):
                for c in range(GROUPS_PER_TILE):
                    z_ref[c, rows, :] = rot(grp(a, c), rows).astype(BF16)
                if not latent:
                    ak_ref[rc, :, 0:256] = grp(a, 4, 2)

        @pl.when(j == 1)
        def _():
            for rc, rows, a in chunks():
                for c in range(2):
                    z_ref[c, rows, :] = rot(grp(a, c), rows).astype(BF16)
                for c in range(2, GROUPS_PER_TILE):
                    z_ref[c, rows, :] = grp(a, c).astype(BF16)
                if not latent:
                    ak_ref[rc, :, 256:512] = grp(a, 0, 2)
                    av_ref[rc] = grp(a, 2, 4)

        @pl.when(j == 2)
        def _():
            bd = bd_ref[...]
            for rc, rows, a in chunks():
                for half in range(2):
                    qn = _head_rmsnorm(grp(a, 2 * half, 2), bd, gq_ref[...])
                    for c in range(2):
                        z_ref[2 * half + c, rows, :] = rot(grp(qn, c), rows).astype(BF16)
                kn = _head_rmsnorm(grp(a, 4), bd[0:LANES, 0:LANES], gk_ref[...])
                z_ref[4, rows, :] = rot(kn, rows).astype(BF16)
                z_ref[5, rows, :] = grp(a, 5).astype(BF16)
                if not latent:
                    bk_ref[rc] = kn
                    bv_ref[rc] = grp(a, 5)

        @pl.when(j == 3)
        def _():
            for rc, rows, a in chunks():
                for c in range(GROUPS_PER_TILE):
                    z_ref[c, rows, :] = grp(a, c).astype(BF16)
                if not latent:
                    ck_ref[rc, :, 0:256] = grp(a, 4, 2)

        @pl.when(j == 4)
        def _():
            for rc, rows, a in chunks():
                for c in range(GROUPS_PER_TILE):
                    z_ref[c, rows, :] = grp(a, c).astype(BF16)
                if not latent:
                    ck_ref[rc, :, 256:512] = grp(a, 0, 2)
                    cv_ref[rc] = grp(a, 2, 4)

        @pl.when(j >= N_QKV_TILES)
        def _():
            for rc, rows, a in chunks():
                a = a + bg_ref[...]
                for c in range(GROUPS_PER_TILE):
                    gate_ref[c, rows, :] = jax.nn.sigmoid(grp(a, c)).astype(BF16)

    return kern


def _inproj(latent, l, x, mod, g_norm1, w_in, b_gate, gq, gk, bd, rope_cos, rope_sin, caches):
    n_row = N_TOK // ROW_TILE
    mod_idx = (lambda i, j: (1 + i, 0, 0)) if latent else (lambda i, j: (0, 0, 0))
    in_specs = [
        pl.BlockSpec((ROW_TILE, D_MODEL), lambda i, j: (i, 0)),
        pl.BlockSpec((1, 1, 6 * D_MODEL), mod_idx),
        pl.BlockSpec((None, 1, D_MODEL), lambda i, j: (l, 0, 0)),
        pl.BlockSpec((None, D_MODEL, COL_TILE), lambda i, j: (l, 0, j)),
        pl.BlockSpec((None, 1, COL_TILE), lambda i, j: (l, 0, jnp.maximum(j - N_QKV_TILES, 0))),
        pl.BlockSpec((None, 1, 2 * LANES), lambda i, j: (l, 0, 0)),
        pl.BlockSpec((None, 1, LANES), lambda i, j: (l, 0, 0)),
        pl.BlockSpec((2 * LANES, 2 * LANES), lambda i, j: (0, 0)),
    ]
    args = [x, mod, g_norm1, w_in, b_gate, gq, gk, bd]
    out_specs = [
        pl.BlockSpec((GROUPS_PER_TILE, ROW_TILE, LANES), lambda i, j: (jnp.minimum(j, N_QKV_TILES - 1), i, 0)),
        pl.BlockSpec((GROUPS_PER_TILE, ROW_TILE, LANES), lambda i, j: (jnp.maximum(j - N_QKV_TILES, 0), i, 0)),
    ]
    out_shape = [jax.ShapeDtypeStruct((N_QKV_GROUPS, N_TOK, LANES), BF16),
                 jax.ShapeDtypeStruct((N_GATE_GROUPS, N_TOK, LANES), BF16)]
    aliases = {}
    if latent:
        in_specs += [pl.BlockSpec((DEC_SEQ, LANES), lambda i, j: (0, 0))] * 2
        args += [rope_cos, rope_sin]
    else:
        nb = ROW_TILE // SEQ
        for k, w in enumerate(CACHE_WIDTHS):
            if caches is not None:
                in_specs.append(pl.BlockSpec(memory_space=pl.ANY))
                args.append(caches[k])
                aliases[8 + k] = 2 + k
            out_specs.append(pl.BlockSpec((nb, None, SEQ, w), lambda i, j: (i, l, 0, 0)))
            out_shape.append(jax.ShapeDtypeStruct((BATCH, DEPTH, SEQ, w), F32))
    return pl.pallas_call(
        _make_inproj_kernel(latent, len(aliases)),
        grid=(n_row, N_COL_TILES),
        in_specs=in_specs,
        out_specs=out_specs,
        out_shape=out_shape,
        scratch_shapes=[pltpu.VMEM((ROW_TILE, D_MODEL), BF16)],
        input_output_aliases=aliases,
        compiler_params=_params(("arbitrary", "arbitrary")),
        name="inproj_lat" if latent else "inproj_ctx",
    )(*args)


def _mask_head(q, head):
    qf = q.astype(F32)
    keep = _lane_head(qf.shape) == head
    return (jnp.where(keep, qf, 0.0) * SCALE).astype(BF16)


def _stack_heads(q):
    return jnp.concatenate([_mask_head(q, 0), _mask_head(q, 1)], axis=0)


def _unstack_heads(o, rows):
    return jnp.where(_lane_head((rows, LANES)) == 1, o[rows:2 * rows], o[0:rows])


def _dup_head(kv, head):
    f = kv.astype(F32)
    r = pltpu.roll(f, HEAD_DIM, 1)
    return jnp.where(_lane_head(f.shape) == head, f, r).astype(BF16)


def _attend(qm, ks, vs, biases=None):
    ss = [lax.dot_general(qm, k, _NT, preferred_element_type=F32) for k in ks]
    if biases is not None:
        ss = [s if b is None else s + b for s, b in zip(ss, biases)]
    m = functools.reduce(jnp.maximum, [jnp.max(s, axis=-1, keepdims=True) for s in ss])
    ps = [jnp.exp(s - m) for s in ss]
    den = functools.reduce(jnp.add, [jnp.sum(p, axis=-1, keepdims=True) for p in ps])
    o = functools.reduce(jnp.add, [jnp.dot(p.astype(BF16), v, preferred_element_type=F32)
                                    for p, v in zip(ps, vs)])
    return o * (1.0 / den)


def _diff_lambda(lam_ref, lam_init):
    lq = lam_ref[...]
    a = jnp.sum(lq[0:1] * lq[1:2], axis=-1, keepdims=True)
    b = jnp.sum(lq[2:3] * lq[3:4], axis=-1, keepdims=True)
    return jnp.exp(a) - jnp.exp(b) + lam_init


def _diff_combine(o1, o2, lam, gsub, lam_init):
    o = o1 - lam * o2
    ms = jnp.mean(o * o, axis=-1, keepdims=True)
    return (o * lax.rsqrt(ms + EPS) * gsub) * (1.0 - lam_init)


def _make_attn_ctx_kernel(lam_init):
    def kern(z_ref, lam_ref, gsub_ref, oa_ref, ob_ref, oc_ref):
        lam = _diff_lambda(lam_ref, lam_init)
        gsub = gsub_ref[...]
        for vh in range(HA):
            hi = vh % 2
            v = z_ref[G_VA + vh]
            o1 = _attend(_mask_head(z_ref[G_QA + vh // 2], hi), [z_ref[G_KA + vh // 2]], [v])
            o2 = _attend(_mask_head(z_ref[G_QA + 2 + vh // 2], hi), [z_ref[G_KA + 2 + vh // 2]], [v])
            oa_ref[vh] = _diff_combine(o1, o2, lam, gsub, lam_init).astype(BF16)
        for g in range(KVB):
            kd = _dup_head(z_ref[G_KB], g)
            vd = _dup_head(z_ref[G_VB], g)
            for c in range(2 * g, 2 * g + 2):
                o = _attend(_stack_heads(z_ref[G_QB + c]), [kd], [vd])
                ob_ref[c] = _unstack_heads(o, SEQ).astype(BF16)
        for c in range(HC // 2):
            o = _attend(_stack_heads(z_ref[G_QC + c]), [z_ref[G_KC + c]], [z_ref[G_VC + c]])
            oc_ref[c] = _unstack_heads(o, SEQ).astype(BF16)
    return kern


def _attn_ctx(l, z, lambda_qk, g_subln):
    lam_init = 0.8 - 0.6 * math.exp(-0.3 * l)
    o_spec = pl.BlockSpec((4, SEQ, LANES), lambda b: (0, b, 0))
    o_shape = jax.ShapeDtypeStruct((4, N_TOK, LANES), BF16)
    return pl.pallas_call(
        _make_attn_ctx_kernel(lam_init),
        grid=(BATCH,),
        in_specs=[pl.BlockSpec((N_QKV_GROUPS, SEQ, LANES), lambda b: (0, b, 0)),
                  pl.BlockSpec((None, 4, DA), lambda b: (l, 0, 0)),
                  pl.BlockSpec((None, 1, 2 * DA), lambda b: (l, 0, 0))],
        out_specs=[o_spec, o_spec, o_spec],
        out_shape=[o_shape, o_shape, o_shape],
        compiler_params=_params(("arbitrary",)),
        name="attn_ctx",
    )(z, lambda_qk, g_subln)


Q_BLK = 256


def _make_attn_lat_a_kernel(lam_init):
    def kern(q1_ref, q2_ref, k1_ref, k2_ref, v_ref, ck1_ref, ck2_ref, cv_ref, lam_ref, gsub_ref, o_ref):
        hi = jnp.bitwise_and(pl.program_id(1), 1)
        lam = _diff_lambda(lam_ref, lam_init)
        gsub = gsub_ref[...]
        ck1 = ck1_ref[...].astype(BF16)
        ck2 = ck2_ref[...].astype(BF16)
        cv = cv_ref[...].astype(BF16)
        k1, k2, v = k1_ref[0], k2_ref[0], v_ref[0]
        for qb in range(DEC_SEQ // Q_BLK):
            rows = pl.ds(qb * Q_BLK, Q_BLK)
            o1 = _attend(_mask_head(q1_ref[0, rows, :], hi), [k1, ck1], [v, cv])
            o2 = _attend(_mask_head(q2_ref[0, rows, :], hi), [k2, ck2], [v, cv])
            o_ref[0, rows, :] = _diff_combine(o1, o2, lam, gsub, lam_init).astype(BF16)
    return kern


def _attn_lat_a(l, z, cache_k, cache_v, lambda_qk, g_subln):
    lam_init = 0.8 - 0.6 * math.exp(-0.3 * l)

    def zspec(fn):
        return pl.BlockSpec((1, DEC_SEQ, LANES), lambda b, h: (fn(h), b, 0))

    def cspec(fn):
        return pl.BlockSpec((None, None, PAST_LEN, LANES), lambda b, h: (b, l, 0, fn(h)))

    return pl.pallas_call(
        _make_attn_lat_a_kernel(lam_init),
        grid=(DEC_BATCH, HA),
        in_specs=[zspec(lambda h: G_QA + h // 2), zspec(lambda h: G_QA + 2 + h // 2),
                  zspec(lambda h: G_KA + h // 2), zspec(lambda h: G_KA + 2 + h // 2),
                  zspec(lambda h: G_VA + h),
                  cspec(lambda h: h // 2), cspec(lambda h: 2 + h // 2), cspec(lambda h: h),
                  pl.BlockSpec((None, 4, DA), lambda b, h: (l, 0, 0)),
                  pl.BlockSpec((None, 1, 2 * DA), lambda b, h: (l, 0, 0))],
        out_specs=pl.BlockSpec((1, DEC_SEQ, LANES), lambda b, h: (h, b, 0)),
        out_shape=jax.ShapeDtypeStruct((4, N_TOK, LANES), BF16),
        compiler_params=_params(("arbitrary", "arbitrary")),
        name="attn_lat_a",
    )(z, z, z, z, z, cache_k, cache_k, cache_v, lambda_qk, g_subln)


def _attn_lat_b_kernel(q_ref, k_ref, v_ref, ck_ref, cv_ref, o_ref):
    hi = lax.shift_right_logical(pl.program_id(1), 1)
    kd = _dup_head(k_ref[0], hi)
    vd = _dup_head(v_ref[0], hi)
    ckd = _dup_head(ck_ref[...].astype(BF16), hi)
    cvd = _dup_head(cv_ref[...].astype(BF16), hi)
    for qb in range(DEC_SEQ // Q_BLK):
        rows = pl.ds(qb * Q_BLK, Q_BLK)
        o = _attend(_stack_heads(q_ref[0, rows, :]), [kd, ckd], [vd, cvd])
        o_ref[0, rows, :] = _unstack_heads(o, Q_BLK).astype(BF16)


def _attn_lat_b(l, z, cache_k, cache_v):
    cspec = pl.BlockSpec((None, None, PAST_LEN, LANES), lambda b, c: (b, l, 0, 0))
    return pl.pallas_call(
        _attn_lat_b_kernel,
        grid=(DEC_BATCH, HB // 2),
        in_specs=[pl.BlockSpec((1, DEC_SEQ, LANES), lambda b, c: (G_QB + c, b, 0)),
                  pl.BlockSpec((1, DEC_SEQ, LANES), lambda b, c: (G_KB, b, 0)),
                  pl.BlockSpec((1, DEC_SEQ, LANES), lambda b, c: (G_VB, b, 0)),
                  cspec, cspec],
        out_specs=pl.BlockSpec((1, DEC_SEQ, LANES), lambda b, c: (c, b, 0)),
        out_shape=jax.ShapeDtypeStruct((4, N_TOK, LANES), BF16),
        compiler_params=_params(("arbitrary", "arbitrary")),
        name="attn_lat_b",
    )(z, z, z, cache_k, cache_v)


N_GRID_ROWS = DEC_SEQ // GRID_W
NA_KEYS = NA_ROWS * GRID_W


def _row_class(r):
    return r - min(max(r - NA_ROWS // 2, 0), N_GRID_ROWS - NA_ROWS)


def _build_window_bias(vp_ref, bias_scr):
    qcol = lax.broadcasted_iota(jnp.int32, (GRID_W, LANES), 0)
    kcol = lax.broadcasted_iota(jnp.int32, (GRID_W, LANES), 1) & (GRID_W - 1)
    c0 = jnp.clip(qcol - NA_COLS // 2, 0, GRID_W - NA_COLS)
    in_win = (kcol >= c0) & (kcol < c0 + NA_COLS)
    for head in range(2):
        for cls in range(NA_ROWS):
            for m in range(NA_ROWS // 2):
                a = 2 * m - cls + NA_ROWS - 1
                row = jnp.broadcast_to(vp_ref[head, a:a + 1, :], (GRID_W, LANES))
                toep = pltpu.roll(row, 0, 1, stride=1, stride_axis=0)
                bias_scr[head, cls, :, m * LANES:(m + 1) * LANES] = jnp.where(in_win, toep, NEG)


def _attn_lat_c_kernel(q_ref, k_ref, v_ref, ck_ref, cv_ref, vp_ref, o_ref, bias_scr):
    @pl.when(pl.program_id(1) == 0)
    def _():
        _build_window_bias(vp_ref, bias_scr)

    ck = ck_ref[...].astype(BF16)
    cv = cv_ref[...].astype(BF16)
    for r in range(N_GRID_ROWS):
        r0 = min(max(r - NA_ROWS // 2, 0), N_GRID_ROWS - NA_ROWS)
        rows = pl.ds(r * GRID_W, GRID_W)
        keys = pl.ds(r0 * GRID_W, NA_KEYS)
        cls = _row_class(r)
        bias = jnp.concatenate([bias_scr[0, cls], bias_scr[1, cls]], axis=0)
        o = _attend(_stack_heads(q_ref[0, rows, :]), [k_ref[0, keys, :], ck], [v_ref[0, keys, :], cv],
                    biases=[bias, None])
        o_ref[0, rows, :] = _unstack_heads(o, GRID_W).astype(BF16)


def _attn_lat_c(l, z, cache_k, cache_v, bias_rows):
    def zspec(g0):
        return pl.BlockSpec((1, DEC_SEQ, LANES), lambda c, b: (g0 + c, b, 0))

    cspec = pl.BlockSpec((None, None, PAST_LEN, LANES), lambda c, b: (b, l, 0, c))
    return pl.pallas_call(
        _attn_lat_c_kernel,
        grid=(HC // 2, DEC_BATCH),
        in_specs=[zspec(G_QC), zspec(G_KC), zspec(G_VC), cspec, cspec,
                  pl.BlockSpec((None, 2, 2 * NA_ROWS, LANES), lambda c, b: (l, c, 0, 0))],
        out_specs=pl.BlockSpec((1, DEC_SEQ, LANES), lambda c, b: (c, b, 0)),
        out_shape=jax.ShapeDtypeStruct((4, N_TOK, LANES), BF16),
        scratch_shapes=[pltpu.VMEM((2, NA_ROWS, GRID_W, NA_KEYS), F32)],
        compiler_params=_params(("arbitrary", "arbitrary")),
        name="attn_lat_c",
    )(z, z, z, cache_k, cache_v, bias_rows)


MERGE_TILE = 512


def _merge_kernel(x_ref, oa_ref, ob_ref, oc_ref, gate_ref, mod_ref, g2_ref, wa_ref, wb_ref, wc_ref, wo_ref,
                  xo_ref, h2_ref, wbr_scr, wo_scr):
    @pl.when(pl.program_id(0) == 0)
    def _():
        wbr_scr[0] = wa_ref[...].astype(BF16)
        wbr_scr[1] = wb_ref[...].astype(BF16)
        wbr_scr[2] = wc_ref[...].astype(BF16)
        wo_scr[...] = wo_ref[...].astype(BF16)

    y = None
    for k, o_ref in enumerate((oa_ref, ob_ref, oc_ref)):
        o = jnp.concatenate([o_ref[c] for c in range(4)], axis=-1)
        p = jnp.dot(o, wbr_scr[k], preferred_element_type=F32)
        g = jnp.concatenate([gate_ref[8 * k + c] for c in range(8)], axis=-1).astype(F32)
        y = g * p if y is None else y + g * p
    out = jnp.dot(y.astype(BF16), wo_scr[...], preferred_element_type=F32)
    m = mod_ref[0]
    x = x_ref[...] + m[:, 2 * D_MODEL:3 * D_MODEL] * out
    xo_ref[...] = x
    ms = jnp.mean(x * x, axis=-1, keepdims=True)
    h2 = x * lax.rsqrt(ms + EPS) * g2_ref[...]
    h2_ref[...] = (h2 * (1.0 + m[:, 4 * D_MODEL:5 * D_MODEL]) + m[:, 3 * D_MODEL:4 * D_MODEL]).astype(BF16)


def _merge(latent, l, x, oa, ob, oc, gates, mod, g_norm2, w_a, w_b, w_c, w_o):
    tm = MERGE_TILE
    if latent:
        mod_idx = lambda i: (1 + (i * tm) // DEC_SEQ, 0, 0)
    else:
        mod_idx = lambda i: (0, 0, 0)
    o_spec = pl.BlockSpec((4, tm, LANES), lambda i: (0, i, 0))
    wbr_spec = pl.BlockSpec((None, 4 * LANES, D_MODEL), lambda i: (l, 0, 0))
    return pl.pallas_call(
        _merge_kernel,
        grid=(N_TOK // tm,),
        in_specs=[pl.BlockSpec((tm, D_MODEL), lambda i: (i, 0)),
                  o_spec, o_spec, o_spec,
                  pl.BlockSpec((N_GATE_GROUPS, tm, LANES), lambda i: (0, i, 0)),
                  pl.BlockSpec((1, 1, 6 * D_MODEL), mod_idx),
                  pl.BlockSpec((None, 1, D_MODEL), lambda i: (l, 0, 0)),
                  wbr_spec, wbr_spec, wbr_spec,
                  pl.BlockSpec((None, D_MODEL, D_MODEL), lambda i: (l, 0, 0))],
        out_specs=[pl.BlockSpec((tm, D_MODEL), lambda i: (i, 0)),
                   pl.BlockSpec((tm, D_MODEL), lambda i: (i, 0))],
        out_shape=[jax.ShapeDtypeStruct((N_TOK, D_MODEL), F32),
                   jax.ShapeDtypeStruct((N_TOK, D_MODEL), BF16)],
        scratch_shapes=[pltpu.VMEM((3, 4 * LANES, D_MODEL), BF16), pltpu.VMEM((D_MODEL, D_MODEL), BF16)],
        compiler_params=_params(("arbitrary",)),
        name="merge_lat" if latent else "merge_ctx",
    )(x, oa, ob, oc, gates, mod, g_norm2, w_a, w_b, w_c, w_o)


FF_TILE = 1024


def _ffn_kernel(x_ref, h2_ref, mod_ref, w1_ref, w2_ref, xo_ref, acc_scr):
    f = pl.program_id(1)

    @pl.when(f == 0)
    def _():
        acc_scr[...] = jnp.zeros_like(acc_scr)

    w1 = w1_ref[...].astype(BF16)
    w2 = w2_ref[...].astype(BF16)
    for rc in range(ROW_TILE // ROW_CHUNK):
        rows = pl.ds(rc * ROW_CHUNK, ROW_CHUNK)
        u = jnp.dot(h2_ref[rows, :], w1, preferred_element_type=F32)
        u = jnp.square(jnp.maximum(u, 0.0)).astype(BF16)
        acc_scr[rows, :] += jnp.dot(u, w2, preferred_element_type=F32)

    @pl.when(f == D_FF // FF_TILE - 1)
    def _():
        xo_ref[...] = x_ref[...] + mod_ref[0][:, 5 * D_MODEL:6 * D_MODEL] * acc_scr[...]


def _ffn(latent, l, x, h2, mod, w1, w2):
    tm = ROW_TILE
    mod_idx = (lambda i, f: (1 + i, 0, 0)) if latent else (lambda i, f: (0, 0, 0))
    return pl.pallas_call(
        _ffn_kernel,
        grid=(N_TOK // tm, D_FF // FF_TILE),
        in_specs=[pl.BlockSpec((tm, D_MODEL), lambda i, f: (i, 0)),
                  pl.BlockSpec((tm, D_MODEL), lambda i, f: (i, 0)),
                  pl.BlockSpec((1, 1, 6 * D_MODEL), mod_idx),
                  pl.BlockSpec((None, D_MODEL, FF_TILE), lambda i, f: (l, 0, f)),
                  pl.BlockSpec((None, FF_TILE, D_MODEL), lambda i, f: (l, f, 0))],
        out_specs=pl.BlockSpec((tm, D_MODEL), lambda i, f: (i, 0)),
        out_shape=jax.ShapeDtypeStruct((N_TOK, D_MODEL), F32),
        scratch_shapes=[pltpu.VMEM((tm, D_MODEL), F32)],
        compiler_params=_params(("arbitrary", "arbitrary")),
        name="ffn_lat" if latent else "ffn_ctx",
    )(x, h2, mod, w1, w2)


def _final_norm_kernel(x_ref, g_ref, o_ref):
    x = x_ref[...]
    ms = jnp.mean(x * x, axis=-1, keepdims=True)
    o_ref[...] = x * lax.rsqrt(ms + EPS) * g_ref[...]


def _final_norm(x, g):
    tm = ROW_TILE
    return pl.pallas_call(
        _final_norm_kernel,
        grid=(N_TOK // tm,),
        in_specs=[pl.BlockSpec((tm, D_MODEL), lambda i: (i, 0)),
                  pl.BlockSpec((1, D_MODEL), lambda i: (0, 0))],
        out_specs=pl.BlockSpec((tm, D_MODEL), lambda i: (i, 0)),
        out_shape=jax.ShapeDtypeStruct((N_TOK, D_MODEL), F32),
        compiler_params=_params(("arbitrary",)),
        name="final_norm",
    )(x, g)


def _rope_tables():
    nf = HEAD_DIM // 4
    t = jnp.arange(DEC_SEQ)
    row = (t // GRID_W).astype(F32)
    col = (t % GRID_W).astype(F32)
    inv = ROPE_BASE ** (-jnp.arange(nf, dtype=F32) / nf)
    ar = row[:, None] * inv[None, :]
    ac = col[:, None] * inv[None, :]
    cos = jnp.concatenate([jnp.cos(ar), jnp.cos(ar), jnp.cos(ac), jnp.cos(ac)], axis=-1)
    sin = jnp.concatenate([-jnp.sin(ar), jnp.sin(ar), -jnp.sin(ac), jnp.sin(ac)], axis=-1)
    return jnp.tile(cos, (1, 2)), jnp.tile(sin, (1, 2))


def _packed_bias_rows(rel_bias):
    n = 2 * NA_ROWS - 2
    gap = jnp.zeros((DEPTH, HC, n, 33), F32)
    rows = jnp.concatenate([rel_bias[:, :, 0:n, NA_COLS - 1:], gap, rel_bias[:, :, 1:n + 1, :], gap,
                            rel_bias[:, :, 0:n, 0:NA_COLS - 1]], axis=-1)
    return jnp.pad(rows, ((0, 0), (0, 0), (0, 2 * NA_ROWS - n), (0, 0)))


def kernel(x_prompt, x_sample, c, cache_a_k, cache_a_v, cache_b_k, cache_b_v, cache_c_k, cache_c_v, c_ctx, w_mod, b_mod, g_norm1, g_norm2, w_in, b_gate, lambda_qk, g_subln, g_qnorm, g_knorm, rel_bias, w_branch_a, w_branch_b, w_branch_c, w_out, w_ff1, w_ff2, g_final):
    xp = x_prompt.reshape(N_TOK, D_MODEL)
    xs = x_sample.reshape(N_TOK, D_MODEL)

    cvec = jnp.concatenate([c_ctx[None, :], c, jnp.zeros((3, D_MODEL), F32)], axis=0)
    mods = _modulation(cvec, w_mod, b_mod)

    rope_cos, rope_sin = _rope_tables()
    bias_rows = _packed_bias_rows(rel_bias)
    bd = jnp.kron(jnp.eye(2 * LANES // HEAD_DIM, dtype=F32),
                  jnp.full((HEAD_DIM, HEAD_DIM), 1.0 / HEAD_DIM, F32)).astype(BF16)
    gq = jnp.tile(g_qnorm, (1, 2 * LANES // DB)).reshape(DEPTH, 1, 2 * LANES)
    gk = jnp.tile(g_knorm, (1, LANES // DB)).reshape(DEPTH, 1, LANES)
    g1 = g_norm1.reshape(DEPTH, 1, D_MODEL)
    g2 = g_norm2.reshape(DEPTH, 1, D_MODEL)
    bg = b_gate.reshape(DEPTH, 1, 3 * D_MODEL)
    gsub = g_subln.reshape(DEPTH, 1, 2 * DA)

    lat_ak = cache_a_k.reshape(DEC_BATCH, DEPTH, PAST_LEN, 2 * HA * DA)
    lat_av = cache_a_v.reshape(DEC_BATCH, DEPTH, PAST_LEN, HA * 2 * DA)
    lat_bk = cache_b_k.reshape(DEC_BATCH, DEPTH, PAST_LEN, KVB * DB)
    lat_bv = cache_b_v.reshape(DEC_BATCH, DEPTH, PAST_LEN, KVB * DB)
    lat_ck = cache_c_k.reshape(DEC_BATCH, DEPTH, PAST_LEN, HC * DC)
    lat_cv = cache_c_v.reshape(DEC_BATCH, DEPTH, PAST_LEN, HC * DC)

    new_caches = None

    for l in range(DEPTH):
        mod = mods[l].reshape(8, 1, 6 * D_MODEL)

        outs = _inproj(False, l, xp, mod, g1, w_in, bg, gq, gk, bd, None, None, new_caches)
        z, gates, new_caches = outs[0], outs[1], list(outs[2:])
        oa, ob, oc = _attn_ctx(l, z, lambda_qk, gsub)
        xp, h2 = _merge(False, l, xp, oa, ob, oc, gates, mod, g2, w_branch_a, w_branch_b, w_branch_c, w_out)
        xp = _ffn(False, l, xp, h2, mod, w_ff1, w_ff2)

        z, gates = _inproj(True, l, xs, mod, g1, w_in, bg, gq, gk, bd, rope_cos, rope_sin, None)
        oa = _attn_lat_a(l, z, lat_ak, lat_av, lambda_qk, gsub)
        ob = _attn_lat_b(l, z, lat_bk, lat_bv)
        oc = _attn_lat_c(l, z, lat_ck, lat_cv, bias_rows)
        xs, h2 = _merge(True, l, xs, oa, ob, oc, gates, mod, g2, w_branch_a, w_branch_b, w_branch_c, w_out)
        xs = _ffn(True, l, xs, h2, mod, w_ff1, w_ff2)

    gf = g_final.reshape(1, D_MODEL)
    y_prompt = _final_norm(xp, gf).reshape(BATCH, SEQ, D_MODEL)
    y_sample = _final_norm(xs, gf).reshape(DEC_BATCH, DEC_SEQ, D_MODEL)
    ak, av, bk, bv, ck, cv = new_caches
    return (y_prompt, y_sample,
            ak.reshape(BATCH, DEPTH, SEQ, 2 * HA, DA), av.reshape(BATCH, DEPTH, SEQ, HA, 2 * DA),
            bk.reshape(BATCH, DEPTH, SEQ, KVB, DB), bv.reshape(BATCH, DEPTH, SEQ, KVB, DB),
            ck.reshape(BATCH, DEPTH, SEQ, HC, DC), cv.reshape(BATCH, DEPTH, SEQ, HC, DC))
```

```python
import functools
import math

import jax
import jax.numpy as jnp
from jax import lax
from jax.experimental import pallas as pl
from jax.experimental.pallas import tpu as pltpu

D_MODEL = 1024
BATCH = 16
SEQ = 256
DEPTH = 4
DEC_BATCH = 4
DEC_SEQ = 1024
PAST_LEN = 256
GRID_W = 64
HA, DA = 4, 64
HB, KVB, DB = 8, 2, 64
HC, DC = 8, 64
NA_ROWS, NA_COLS = 8, 16
D_FF = 4 * D_MODEL
ROPE_BASE = 10000.0
EPS = 1e-6
NEG = -1e30
HEAD_DIM = 64
SCALE = HEAD_DIM ** -0.5

F32 = jnp.float32
BF16 = jnp.bfloat16

LANES = 128
N_QKV_GROUPS = 30
N_GATE_GROUPS = 24
COL_TILE = 768
GROUPS_PER_TILE = COL_TILE // LANES
N_QKV_TILES = N_QKV_GROUPS // GROUPS_PER_TILE
N_COL_TILES = N_QKV_TILES + N_GATE_GROUPS // GROUPS_PER_TILE
ROW_TILE = 1024
ROW_CHUNK = 256
N_TOK = BATCH * SEQ
VMEM_LIMIT_V7X = 58 * 1024 * 1024

G_QA, G_KA, G_VA, G_QB, G_KB, G_VB, G_QC, G_KC, G_VC = 0, 4, 8, 12, 16, 17, 18, 22, 26
CACHE_WIDTHS = (2 * HA * DA, HA * 2 * DA, KVB * DB, KVB * DB, HC * DC, HC * DC)

_NT = (((1,), (1,)), ((), ()))


def _params(sem, vmem=VMEM_LIMIT_V7X):
    return pltpu.CompilerParams(dimension_semantics=sem, vmem_limit_bytes=vmem)


def _lane_head(shape):
    return lax.shift_right_logical(lax.broadcasted_iota(jnp.int32, shape, len(shape) - 1), 6)


def _mod_kernel(c_ref, w_ref, b_ref, o_ref):
    c = c_ref[...]
    s = (c * jax.nn.sigmoid(c)).astype(BF16)
    o_ref[...] = jnp.dot(s, w_ref[...].astype(BF16), preferred_element_type=F32) + b_ref[...]


def _modulation(cvec, w_mod, b_mod):
    tn = 1536
    n6 = 6 * D_MODEL
    return pl.pallas_call(
        _mod_kernel,
        grid=(DEPTH, n6 // tn),
        in_specs=[pl.BlockSpec((8, D_MODEL), lambda l, n: (0, 0)),
                  pl.BlockSpec((None, D_MODEL, tn), lambda l, n: (l, 0, n)),
                  pl.BlockSpec((None, 1, tn), lambda l, n: (l, 0, n))],
        out_specs=pl.BlockSpec((None, 8, tn), lambda l, n: (l, 0, n)),
        out_shape=jax.ShapeDtypeStruct((DEPTH, 8, n6), F32),
        compiler_params=_params(("arbitrary", "arbitrary")),
        name="modulation",
    )(cvec, w_mod, b_mod.reshape(DEPTH, 1, n6))


def _rope(v, cos, sin):
    first = (lax.broadcasted_iota(jnp.int32, v.shape, 1) & 16) == 0
    partner = jnp.where(first, pltpu.roll(v, LANES - 16, 1), pltpu.roll(v, 16, 1))
    return v * cos + partner * sin


def _head_rmsnorm(v, bd, g):
    msq = jnp.dot((v * v).astype(BF16), bd, preferred_element_type=F32)
    return v * lax.rsqrt(msq + EPS) * g


def _make_inproj_kernel(latent, n_aliased):
    def kern(*refs):
        if latent:
            (x_ref, mod_ref, g1_ref, w_ref, bg_ref, gq_ref, gk_ref, bd_ref, cos_ref, sin_ref,
             z_ref, gate_ref, h_scr, wbf_scr) = refs
        else:
            (x_ref, mod_ref, g1_ref, w_ref, bg_ref, gq_ref, gk_ref, bd_ref) = refs[:8]
            (z_ref, gate_ref, ak_ref, av_ref, bk_ref, bv_ref, ck_ref, cv_ref,
             h_scr, wbf_scr) = refs[8 + n_aliased:]
        j = pl.program_id(0)
        i = pl.program_id(1)

        @pl.when(i == 0)
        def _():
            wbf_scr[...] = w_ref[...].astype(BF16)

        def chunks(first=False):
            for rc in range(ROW_TILE // ROW_CHUNK):
                rows = pl.ds(rc * ROW_CHUNK, ROW_CHUNK)
                tok = pl.ds(pl.multiple_of(i * ROW_TILE + rc * ROW_CHUNK, ROW_CHUNK), ROW_CHUNK)
                if first:
                    x = x_ref[rows, :]
                    ms = jnp.mean(x * x, axis=-1, keepdims=True)
                    y = x * lax.rsqrt(ms + EPS) * g1_ref[...]
                    m = mod_ref[0]
                    h = (y * (1.0 + m[:, D_MODEL:2 * D_MODEL]) + m[:, 0:D_MODEL]).astype(BF16)
                    h_scr[tok, :] = h
                else:
                    h = h_scr[tok, :]
                yield rc, rows, jnp.dot(h, wbf_scr[...], preferred_element_type=F32)

        def grp(a, c, n=1):
            return a[:, c * LANES:(c + n) * LANES]

        def rot(v, rows):
            return _rope(v, cos_ref[rows, :], sin_ref[rows, :]) if latent else v

        @pl.when(j == 0)
        def _():
            for rc, rows, a in chunks(first=True):
                for c in range(GROUPS_PER_TILE):
                    z_ref[c, rows, :] = rot(grp(a, c), rows).astype(BF16)
                if not latent:
                    ak_ref[rc] = grp(a, 4, 2)

        @pl.when(j == 1)
        def _():
            for rc, rows, a in chunks():
                for c in range(2):
                    z_ref[c, rows, :] = rot(grp(a, c), rows).astype(BF16)
                for c in range(2, GROUPS_PER_TILE):
                    z_ref[c, rows, :] = grp(a, c).astype(BF16)
                if not latent:
                    ak_ref[rc] = grp(a, 0, 2)
                    av_ref[rc] = grp(a, 2, 4)

        @pl.when(j == 2)
        def _():
            bd = bd_ref[...]
            for rc, rows, a in chunks():
                for half in range(2):
                    qn = _head_rmsnorm(grp(a, 2 * half, 2), bd, gq_ref[...])
                    for c in range(2):
                        z_ref[2 * half + c, rows, :] = rot(grp(qn, c), rows).astype(BF16)
                kn = _head_rmsnorm(grp(a, 4), bd[0:LANES, 0:LANES], gk_ref[...])
                z_ref[4, rows, :] = rot(kn, rows).astype(BF16)
                z_ref[5, rows, :] = grp(a, 5).astype(BF16)
                if not latent:
                    bk_ref[rc] = kn
                    bv_ref[rc] = grp(a, 5)

        @pl.when(j == 3)
        def _():
            for rc, rows, a in chunks():
                for c in range(GROUPS_PER_TILE):
                    z_ref[c, rows, :] = grp(a, c).astype(BF16)
                if not latent:
                    ck_ref[rc] = grp(a, 4, 2)

        @pl.when(j == 4)
        def _():
            for rc, rows, a in chunks():
                for c in range(GROUPS_PER_TILE):
                    z_ref[c, rows, :] = grp(a, c).astype(BF16)
                if not latent:
                    ck_ref[rc] = grp(a, 0, 2)
                    cv_ref[rc] = grp(a, 2, 4)

        @pl.when(j >= N_QKV_TILES)
        def _():
            for rc, rows, a in chunks():
                a = a + bg_ref[...]
                for c in range(GROUPS_PER_TILE):
                    gate_ref[c, rows, :] = jax.nn.sigmoid(grp(a, c)).astype(BF16)

    return kern


def _inproj(latent, l, x, mod, g_norm1, w_in, b_gate, gq, gk, bd, rope_cos, rope_sin, caches):
    n_row = N_TOK // ROW_TILE
    last = n_row - 1

    def row_block(j, i, first_tile, last_tile):
        return jnp.where(j < first_tile, 0, jnp.where(j <= last_tile, i, last))

    mod_idx = (lambda j, i: (1 + i, 0, 0)) if latent else (lambda j, i: (0, 0, 0))
    in_specs = [
        pl.BlockSpec((ROW_TILE, D_MODEL), lambda j, i: (row_block(j, i, 0, 0), 0)),
        pl.BlockSpec((1, 1, 6 * D_MODEL), mod_idx),
        pl.BlockSpec((None, 1, D_MODEL), lambda j, i: (l, 0, 0)),
        pl.BlockSpec((None, D_MODEL, COL_TILE), lambda j, i: (l, 0, j)),
        pl.BlockSpec((None, 1, COL_TILE), lambda j, i: (l, 0, jnp.maximum(j - N_QKV_TILES, 0))),
        pl.BlockSpec((None, 1, 2 * LANES), lambda j, i: (l, 0, 0)),
        pl.BlockSpec((None, 1, LANES), lambda j, i: (l, 0, 0)),
        pl.BlockSpec((2 * LANES, 2 * LANES), lambda j, i: (0, 0)),
    ]
    args = [x, mod, g_norm1, w_in, b_gate, gq, gk, bd]
    out_specs = [
        pl.BlockSpec((GROUPS_PER_TILE, ROW_TILE, LANES),
                     lambda j, i: (jnp.minimum(j, N_QKV_TILES - 1), row_block(j, i, 0, N_QKV_TILES - 1), 0)),
        pl.BlockSpec((GROUPS_PER_TILE, ROW_TILE, LANES),
                     lambda j, i: (jnp.maximum(j - N_QKV_TILES, 0), row_block(j, i, N_QKV_TILES, N_COL_TILES), 0)),
    ]
    out_shape = [jax.ShapeDtypeStruct((N_QKV_GROUPS, N_TOK, LANES), BF16),
                 jax.ShapeDtypeStruct((N_GATE_GROUPS, N_TOK, LANES), BF16)]
    aliases = {}
    if latent:
        in_specs += [pl.BlockSpec((DEC_SEQ, LANES), lambda j, i: (0, 0))] * 2
        args += [rope_cos, rope_sin]
    else:
        nb = ROW_TILE // SEQ
        tiles = ((0, 1, 256), (1, 1, 512), (2, 2, 128), (2, 2, 128), (3, 4, 256), (4, 4, 512))
        for k, (w, (t0, t1, bw)) in enumerate(zip(CACHE_WIDTHS, tiles)):
            if caches is not None:
                in_specs.append(pl.BlockSpec(memory_space=pl.ANY))
                args.append(caches[k])
                aliases[8 + k] = 2 + k

            def cache_idx(j, i, t0=t0, t1=t1):
                return (row_block(j, i, t0, t1), l, 0, jnp.where(j <= t0, 0, (t1 - t0)))

            out_specs.append(pl.BlockSpec((nb, None, SEQ, bw), cache_idx))
            out_shape.append(jax.ShapeDtypeStruct((BATCH, DEPTH, SEQ, w), F32))
    return pl.pallas_call(
        _make_inproj_kernel(latent, len(aliases)),
        grid=(N_COL_TILES, n_row),
        in_specs=in_specs,
        out_specs=out_specs,
        out_shape=out_shape,
        scratch_shapes=[pltpu.VMEM((N_TOK, D_MODEL), BF16), pltpu.VMEM((D_MODEL, COL_TILE), BF16)],
        input_output_aliases=aliases,
        compiler_params=_params(("arbitrary", "arbitrary")),
        name="inproj_lat" if latent else "inproj_ctx",
    )(*args)


def _mask_head(q, head):
    qf = q.astype(F32)
    keep = _lane_head(qf.shape) == head
    return (jnp.where(keep, qf, 0.0) * SCALE).astype(BF16)


def _stack_heads(q):
    return jnp.concatenate([_mask_head(q, 0), _mask_head(q, 1)], axis=0)


def _unstack_heads(o, rows):
    return jnp.where(_lane_head((rows, LANES)) == 1, o[rows:2 * rows], o[0:rows])


def _dup_head(kv, head):
    f = kv.astype(F32)
    r = pltpu.roll(f, HEAD_DIM, 1)
    return jnp.where(_lane_head(f.shape) == head, f, r).astype(BF16)


def _attend(qm, ks, vs, biases=None):
    ss = [lax.dot_general(qm, k, _NT, preferred_element_type=F32) for k in ks]
    if biases is not None:
        ss = [s if b is None else s + b for s, b in zip(ss, biases)]
    m = functools.reduce(jnp.maximum, [jnp.max(s, axis=-1, keepdims=True) for s in ss])
    ps = [jnp.exp(s - m) for s in ss]
    den = functools.reduce(jnp.add, [jnp.sum(p, axis=-1, keepdims=True) for p in ps])
    o = functools.reduce(jnp.add, [jnp.dot(p.astype(BF16), v, preferred_element_type=F32)
                                    for p, v in zip(ps, vs)])
    return o * (1.0 / den)


def _diff_lambda(lam_ref, lam_init):
    lq = lam_ref[...]
    a = jnp.sum(lq[0:1] * lq[1:2], axis=-1, keepdims=True)
    b = jnp.sum(lq[2:3] * lq[3:4], axis=-1, keepdims=True)
    return jnp.exp(a) - jnp.exp(b) + lam_init


def _diff_combine(o1, o2, lam, gsub, lam_init):
    o = o1 - lam * o2
    ms = jnp.mean(o * o, axis=-1, keepdims=True)
    return (o * lax.rsqrt(ms + EPS) * gsub) * (1.0 - lam_init)


def _make_attn_ctx_kernel(lam_init):
    def kern(z_ref, lam_ref, gsub_ref, oa_ref, ob_ref, oc_ref):
        lam = _diff_lambda(lam_ref, lam_init)
        gsub = gsub_ref[...]
        for vh in range(HA):
            hi = vh % 2
            v = z_ref[G_VA + vh]
            o1 = _attend(_mask_head(z_ref[G_QA + vh // 2], hi), [z_ref[G_KA + vh // 2]], [v])
            o2 = _attend(_mask_head(z_ref[G_QA + 2 + vh // 2], hi), [z_ref[G_KA + 2 + vh // 2]], [v])
            oa_ref[vh] = _diff_combine(o1, o2, lam, gsub, lam_init).astype(BF16)
        for g in range(KVB):
            kd = _dup_head(z_ref[G_KB], g)
            vd = _dup_head(z_ref[G_VB], g)
            for c in range(2 * g, 2 * g + 2):
                o = _attend(_stack_heads(z_ref[G_QB + c]), [kd], [vd])
                ob_ref[c] = _unstack_heads(o, SEQ).astype(BF16)
        for c in range(HC // 2):
            o = _attend(_stack_heads(z_ref[G_QC + c]), [z_ref[G_KC + c]], [z_ref[G_VC + c]])
            oc_ref[c] = _unstack_heads(o, SEQ).astype(BF16)
    return kern


def _attn_ctx(l, z, lambda_qk, g_subln):
    lam_init = 0.8 - 0.6 * math.exp(-0.3 * l)
    o_spec = pl.BlockSpec((4, SEQ, LANES), lambda b: (0, b, 0))
    o_shape = jax.ShapeDtypeStruct((4, N_TOK, LANES), BF16)
    return pl.pallas_call(
        _make_attn_ctx_kernel(lam_init),
        grid=(BATCH,),
        in_specs=[pl.BlockSpec((N_QKV_GROUPS, SEQ, LANES), lambda b: (0, b, 0)),
                  pl.BlockSpec((None, 4, DA), lambda b: (l, 0, 0)),
                  pl.BlockSpec((None, 1, 2 * DA), lambda b: (l, 0, 0))],
        out_specs=[o_spec, o_spec, o_spec],
        out_shape=[o_shape, o_shape, o_shape],
        compiler_params=_params(("arbitrary",)),
        name="attn_ctx",
    )(z, lambda_qk, g_subln)


Q_BLK = 256


def _make_attn_lat_a_kernel(lam_init):
    def kern(q1_ref, q2_ref, k1_ref, k2_ref, v_ref, ck1_ref, ck2_ref, cv_ref, lam_ref, gsub_ref, o_ref):
        hi = jnp.bitwise_and(pl.program_id(1), 1)
        lam = _diff_lambda(lam_ref, lam_init)
        gsub = gsub_ref[...]
        ck1 = ck1_ref[...].astype(BF16)
        ck2 = ck2_ref[...].astype(BF16)
        cv = cv_ref[...].astype(BF16)
        k1, k2, v = k1_ref[0], k2_ref[0], v_ref[0]
        for qb in range(DEC_SEQ // Q_BLK):
            rows = pl.ds(qb * Q_BLK, Q_BLK)
            o1 = _attend(_mask_head(q1_ref[0, rows, :], hi), [k1, ck1], [v, cv])
            o2 = _attend(_mask_head(q2_ref[0, rows, :], hi), [k2, ck2], [v, cv])
            o_ref[0, rows, :] = _diff_combine(o1, o2, lam, gsub, lam_init).astype(BF16)
    return kern


def _attn_lat_a(l, z, cache_k, cache_v, lambda_qk, g_subln):
    lam_init = 0.8 - 0.6 * math.exp(-0.3 * l)

    def zspec(fn):
        return pl.BlockSpec((1, DEC_SEQ, LANES), lambda b, h: (fn(h), b, 0))

    def cspec(fn):
        return pl.BlockSpec((None, None, PAST_LEN, LANES), lambda b, h: (b, l, 0, fn(h)))

    return pl.pallas_call(
        _make_attn_lat_a_kernel(lam_init),
        grid=(DEC_BATCH, HA),
        in_specs=[zspec(lambda h: G_QA + h // 2), zspec(lambda h: G_QA + 2 + h // 2),
                  zspec(lambda h: G_KA + h // 2), zspec(lambda h: G_KA + 2 + h // 2),
                  zspec(lambda h: G_VA + h),
                  cspec(lambda h: h // 2), cspec(lambda h: 2 + h // 2), cspec(lambda h: h),
                  pl.BlockSpec((None, 4, DA), lambda b, h: (l, 0, 0)),
                  pl.BlockSpec((None, 1, 2 * DA), lambda b, h: (l, 0, 0))],
        out_specs=pl.BlockSpec((1, DEC_SEQ, LANES), lambda b, h: (h, b, 0)),
        out_shape=jax.ShapeDtypeStruct((4, N_TOK, LANES), BF16),
        compiler_params=_params(("arbitrary", "arbitrary")),
        name="attn_lat_a",
    )(z, z, z, z, z, cache_k, cache_k, cache_v, lambda_qk, g_subln)


def _attn_lat_b_kernel(q_ref, k_ref, v_ref, ck_ref, cv_ref, o_ref):
    hi = lax.shift_right_logical(pl.program_id(1), 1)
    kd = _dup_head(k_ref[0], hi)
    vd = _dup_head(v_ref[0], hi)
    ckd = _dup_head(ck_ref[...].astype(BF16), hi)
    cvd = _dup_head(cv_ref[...].astype(BF16), hi)
    for qb in range(DEC_SEQ // Q_BLK):
        rows = pl.ds(qb * Q_BLK, Q_BLK)
        o = _attend(_stack_heads(q_ref[0, rows, :]), [kd, ckd], [vd, cvd])
        o_ref[0, rows, :] = _unstack_heads(o, Q_BLK).astype(BF16)


def _attn_lat_b(l, z, cache_k, cache_v):
    cspec = pl.BlockSpec((None, None, PAST_LEN, LANES), lambda b, c: (b, l, 0, 0))
    return pl.pallas_call(
        _attn_lat_b_kernel,
        grid=(DEC_BATCH, HB // 2),
        in_specs=[pl.BlockSpec((1, DEC_SEQ, LANES), lambda b, c: (G_QB + c, b, 0)),
                  pl.BlockSpec((1, DEC_SEQ, LANES), lambda b, c: (G_KB, b, 0)),
                  pl.BlockSpec((1, DEC_SEQ, LANES), lambda b, c: (G_VB, b, 0)),
                  cspec, cspec],
        out_specs=pl.BlockSpec((1, DEC_SEQ, LANES), lambda b, c: (c, b, 0)),
        out_shape=jax.ShapeDtypeStruct((4, N_TOK, LANES), BF16),
        compiler_params=_params(("arbitrary", "arbitrary")),
        name="attn_lat_b",
    )(z, z, z, cache_k, cache_v)


N_GRID_ROWS = DEC_SEQ // GRID_W
NA_KEYS = NA_ROWS * GRID_W


def _row_class(r):
    return r - min(max(r - NA_ROWS // 2, 0), N_GRID_ROWS - NA_ROWS)


def _build_window_bias(vp_ref, bias_scr):
    qcol = lax.broadcasted_iota(jnp.int32, (GRID_W, LANES), 0)
    kcol = lax.broadcasted_iota(jnp.int32, (GRID_W, LANES), 1) & (GRID_W - 1)
    c0 = jnp.clip(qcol - NA_COLS // 2, 0, GRID_W - NA_COLS)
    in_win = (kcol >= c0) & (kcol < c0 + NA_COLS)
    for head in range(2):
        for cls in range(NA_ROWS):
            for m in range(NA_ROWS // 2):
                a = 2 * m - cls + NA_ROWS - 1
                row = jnp.broadcast_to(vp_ref[head, a:a + 1, :], (GRID_W, LANES))
                toep = pltpu.roll(row, 0, 1, stride=1, stride_axis=0)
                bias_scr[head, cls, :, m * LANES:(m + 1) * LANES] = jnp.where(in_win, toep, NEG)


def _attn_lat_c_kernel(q_ref, k_ref, v_ref, ck_ref, cv_ref, vp_ref, o_ref, bias_scr):
    @pl.when(pl.program_id(1) == 0)
    def _():
        _build_window_bias(vp_ref, bias_scr)

    ck = ck_ref[...].astype(BF16)
    cv = cv_ref[...].astype(BF16)
    for r in range(N_GRID_ROWS):
        r0 = min(max(r - NA_ROWS // 2, 0), N_GRID_ROWS - NA_ROWS)
        rows = pl.ds(r * GRID_W, GRID_W)
        keys = pl.ds(r0 * GRID_W, NA_KEYS)
        cls = _row_class(r)
        bias = jnp.concatenate([bias_scr[0, cls], bias_scr[1, cls]], axis=0)
        o = _attend(_stack_heads(q_ref[0, rows, :]), [k_ref[0, keys, :], ck], [v_ref[0, keys, :], cv],
                    biases=[bias, None])
        o_ref[0, rows, :] = _unstack_heads(o, GRID_W).astype(BF16)


def _attn_lat_c(l, z, cache_k, cache_v, bias_rows):
    def zspec(g0):
        return pl.BlockSpec((1, DEC_SEQ, LANES), lambda c, b: (g0 + c, b, 0))

    cspec = pl.BlockSpec((None, None, PAST_LEN, LANES), lambda c, b: (b, l, 0, c))
    return pl.pallas_call(
        _attn_lat_c_kernel,
        grid=(HC // 2, DEC_BATCH),
        in_specs=[zspec(G_QC), zspec(G_KC), zspec(G_VC), cspec, cspec,
                  pl.BlockSpec((None, 2, 2 * NA_ROWS, LANES), lambda c, b: (l, c, 0, 0))],
        out_specs=pl.BlockSpec((1, DEC_SEQ, LANES), lambda c, b: (c, b, 0)),
        out_shape=jax.ShapeDtypeStruct((4, N_TOK, LANES), BF16),
        scratch_shapes=[pltpu.VMEM((2, NA_ROWS, GRID_W, NA_KEYS), F32)],
        compiler_params=_params(("arbitrary", "arbitrary")),
        name="attn_lat_c",
    )(z, z, z, cache_k, cache_v, bias_rows)


MERGE_TILE = 512


def _merge_kernel(x_ref, oa_ref, ob_ref, oc_ref, gate_ref, mod_ref, g2_ref, wa_ref, wb_ref, wc_ref, wo_ref,
                  xo_ref, h2_ref, wbr_scr, wo_scr):
    @pl.when(pl.program_id(0) == 0)
    def _():
        wbr_scr[0] = wa_ref[...].astype(BF16)
        wbr_scr[1] = wb_ref[...].astype(BF16)
        wbr_scr[2] = wc_ref[...].astype(BF16)
        wo_scr[...] = wo_ref[...].astype(BF16)

    y = None
    for k, o_ref in enumerate((oa_ref, ob_ref, oc_ref)):
        o = jnp.concatenate([o_ref[c] for c in range(4)], axis=-1)
        p = jnp.dot(o, wbr_scr[k], preferred_element_type=F32)
        g = jnp.concatenate([gate_ref[8 * k + c] for c in range(8)], axis=-1).astype(F32)
        y = g * p if y is None else y + g * p
    out = jnp.dot(y.astype(BF16), wo_scr[...], preferred_element_type=F32)
    m = mod_ref[0]
    x = x_ref[...] + m[:, 2 * D_MODEL:3 * D_MODEL] * out
    xo_ref[...] = x
    ms = jnp.mean(x * x, axis=-1, keepdims=True)
    h2 = x * lax.rsqrt(ms + EPS) * g2_ref[...]
    h2_ref[...] = (h2 * (1.0 + m[:, 4 * D_MODEL:5 * D_MODEL]) + m[:, 3 * D_MODEL:4 * D_MODEL]).astype(BF16)


def _merge(latent, l, x, oa, ob, oc, gates, mod, g_norm2, w_a, w_b, w_c, w_o):
    tm = MERGE_TILE
    if latent:
        mod_idx = lambda i: (1 + (i * tm) // DEC_SEQ, 0, 0)
    else:
        mod_idx = lambda i: (0, 0, 0)
    o_spec = pl.BlockSpec((4, tm, LANES), lambda i: (0, i, 0))
    wbr_spec = pl.BlockSpec((None, 4 * LANES, D_MODEL), lambda i: (l, 0, 0))
    return pl.pallas_call(
        _merge_kernel,
        grid=(N_TOK // tm,),
        in_specs=[pl.BlockSpec((tm, D_MODEL), lambda i: (i, 0)),
                  o_spec, o_spec, o_spec,
                  pl.BlockSpec((N_GATE_GROUPS, tm, LANES), lambda i: (0, i, 0)),
                  pl.BlockSpec((1, 1, 6 * D_MODEL), mod_idx),
                  pl.BlockSpec((None, 1, D_MODEL), lambda i: (l, 0, 0)),
                  wbr_spec, wbr_spec, wbr_spec,
                  pl.BlockSpec((None, D_MODEL, D_MODEL), lambda i: (l, 0, 0))],
        out_specs=[pl.BlockSpec((tm, D_MODEL), lambda i: (i, 0)),
                   pl.BlockSpec((tm, D_MODEL), lambda i: (i, 0))],
        out_shape=[jax.ShapeDtypeStruct((N_TOK, D_MODEL), F32),
                   jax.ShapeDtypeStruct((N_TOK, D_MODEL), BF16)],
        scratch_shapes=[pltpu.VMEM((3, 4 * LANES, D_MODEL), BF16), pltpu.VMEM((D_MODEL, D_MODEL), BF16)],
        compiler_params=_params(("arbitrary",)),
        name="merge_lat" if latent else "merge_ctx",
    )(x, oa, ob, oc, gates, mod, g_norm2, w_a, w_b, w_c, w_o)


FF_TILE = 1024


def _ffn_kernel(x_ref, h2_ref, mod_ref, w1_ref, w2_ref, gf_ref, xo_ref, acc_scr, *, final):
    f = pl.program_id(1)

    @pl.when(f == 0)
    def _():
        acc_scr[...] = jnp.zeros_like(acc_scr)

    w1 = w1_ref[...].astype(BF16)
    w2 = w2_ref[...].astype(BF16)
    for rc in range(ROW_TILE // ROW_CHUNK):
        rows = pl.ds(rc * ROW_CHUNK, ROW_CHUNK)
        u = jnp.dot(h2_ref[rows, :], w1, preferred_element_type=F32)
        u = jnp.square(jnp.maximum(u, 0.0)).astype(BF16)
        acc_scr[rows, :] += jnp.dot(u, w2, preferred_element_type=F32)

    @pl.when(f == D_FF // FF_TILE - 1)
    def _():
        x = x_ref[...] + mod_ref[0][:, 5 * D_MODEL:6 * D_MODEL] * acc_scr[...]
        if final:
            ms = jnp.mean(x * x, axis=-1, keepdims=True)
            x = x * lax.rsqrt(ms + EPS) * gf_ref[...]
        xo_ref[...] = x


def _ffn(latent, l, x, h2, mod, w1, w2, g_final):
    tm = ROW_TILE
    mod_idx = (lambda i, f: (1 + i, 0, 0)) if latent else (lambda i, f: (0, 0, 0))
    return pl.pallas_call(
        functools.partial(_ffn_kernel, final=(l == DEPTH - 1)),
        grid=(N_TOK // tm, D_FF // FF_TILE),
        in_specs=[pl.BlockSpec((tm, D_MODEL), lambda i, f: (i, 0)),
                  pl.BlockSpec((tm, D_MODEL), lambda i, f: (i, 0)),
                  pl.BlockSpec((1, 1, 6 * D_MODEL), mod_idx),
                  pl.BlockSpec((None, D_MODEL, FF_TILE), lambda i, f: (l, 0, f)),
                  pl.BlockSpec((None, FF_TILE, D_MODEL), lambda i, f: (l, f, 0)),
                  pl.BlockSpec((1, D_MODEL), lambda i, f: (0, 0))],
        out_specs=pl.BlockSpec((tm, D_MODEL), lambda i, f: (i, 0)),
        out_shape=jax.ShapeDtypeStruct((N_TOK, D_MODEL), F32),
        scratch_shapes=[pltpu.VMEM((tm, D_MODEL), F32)],
        compiler_params=_params(("arbitrary", "arbitrary")),
        name="ffn_lat" if latent else "ffn_ctx",
    )(x, h2, mod, w1, w2, g_final)


def _rope_tables():
    nf = HEAD_DIM // 4
    t = jnp.arange(DEC_SEQ)
    row = (t // GRID_W).astype(F32)
    col = (t % GRID_W).astype(F32)
    inv = ROPE_BASE ** (-jnp.arange(nf, dtype=F32) / nf)
    ar = row[:, None] * inv[None, :]
    ac = col[:, None] * inv[None, :]
    cos = jnp.concatenate([jnp.cos(ar), jnp.cos(ar), jnp.cos(ac), jnp.cos(ac)], axis=-1)
    sin = jnp.concatenate([-jnp.sin(ar), jnp.sin(ar), -jnp.sin(ac), jnp.sin(ac)], axis=-1)
    return jnp.tile(cos, (1, 2)), jnp.tile(sin, (1, 2))


def _packed_bias_rows(rel_bias):
    n = 2 * NA_ROWS - 2
    gap = jnp.zeros((DEPTH, HC, n, 33), F32)
    rows = jnp.concatenate([rel_bias[:, :, 0:n, NA_COLS - 1:], gap, rel_bias[:, :, 1:n + 1, :], gap,
                            rel_bias[:, :, 0:n, 0:NA_COLS - 1]], axis=-1)
    return jnp.pad(rows, ((0, 0), (0, 0), (0, 2 * NA_ROWS - n), (0, 0)))


def kernel(x_prompt, x_sample, c, cache_a_k, cache_a_v, cache_b_k, cache_b_v, cache_c_k, cache_c_v, c_ctx, w_mod, b_mod, g_norm1, g_norm2, w_in, b_gate, lambda_qk, g_subln, g_qnorm, g_knorm, rel_bias, w_branch_a, w_branch_b, w_branch_c, w_out, w_ff1, w_ff2, g_final):
    xp = x_prompt.reshape(N_TOK, D_MODEL)
    xs = x_sample.reshape(N_TOK, D_MODEL)

    cvec = jnp.concatenate([c_ctx[None, :], c, jnp.zeros((3, D_MODEL), F32)], axis=0)
    mods = _modulation(cvec, w_mod, b_mod)

    rope_cos, rope_sin = _rope_tables()
    bias_rows = _packed_bias_rows(rel_bias)
    bd = jnp.kron(jnp.eye(2 * LANES // HEAD_DIM, dtype=F32),
                  jnp.full((HEAD_DIM, HEAD_DIM), 1.0 / HEAD_DIM, F32)).astype(BF16)
    gq = jnp.tile(g_qnorm, (1, 2 * LANES // DB)).reshape(DEPTH, 1, 2 * LANES)
    gk = jnp.tile(g_knorm, (1, LANES // DB)).reshape(DEPTH, 1, LANES)
    g1 = g_norm1.reshape(DEPTH, 1, D_MODEL)
    g2 = g_norm2.reshape(DEPTH, 1, D_MODEL)
    bg = b_gate.reshape(DEPTH, 1, 3 * D_MODEL)
    gsub = g_subln.reshape(DEPTH, 1, 2 * DA)
    gf = g_final.reshape(1, D_MODEL)

    lat_ak = cache_a_k.reshape(DEC_BATCH, DEPTH, PAST_LEN, 2 * HA * DA)
    lat_av = cache_a_v.reshape(DEC_BATCH, DEPTH, PAST_LEN, HA * 2 * DA)
    lat_bk = cache_b_k.reshape(DEC_BATCH, DEPTH, PAST_LEN, KVB * DB)
    lat_bv = cache_b_v.reshape(DEC_BATCH, DEPTH, PAST_LEN, KVB * DB)
    lat_ck = cache_c_k.reshape(DEC_BATCH, DEPTH, PAST_LEN, HC * DC)
    lat_cv = cache_c_v.reshape(DEC_BATCH, DEPTH, PAST_LEN, HC * DC)

    new_caches = None

    for l in range(DEPTH):
        mod = mods[l].reshape(8, 1, 6 * D_MODEL)

        outs = _inproj(False, l, xp, mod, g1, w_in, bg, gq, gk, bd, None, None, new_caches)
        z, gates, new_caches = outs[0], outs[1], list(outs[2:])
        oa, ob, oc = _attn_ctx(l, z, lambda_qk, gsub)
        xp, h2 = _merge(False, l, xp, oa, ob, oc, gates, mod, g2, w_branch_a, w_branch_b, w_branch_c, w_out)
        xp = _ffn(False, l, xp, h2, mod, w_ff1, w_ff2, gf)

        z, gates = _inproj(True, l, xs, mod, g1, w_in, bg, gq, gk, bd, rope_cos, rope_sin, None)
        oa = _attn_lat_a(l, z, lat_ak, lat_av, lambda_qk, gsub)
        ob = _attn_lat_b(l, z, lat_bk, lat_bv)
        oc = _attn_lat_c(l, z, lat_ck, lat_cv, bias_rows)
        xs, h2 = _merge(True, l, xs, oa, ob, oc, gates, mod, g2, w_branch_a, w_branch_b, w_branch_c, w_out)
        xs = _ffn(True, l, xs, h2, mod, w_ff1, w_ff2, gf)

    y_prompt = xp.reshape(BATCH, SEQ, D_MODEL)
    y_sample = xs.reshape(DEC_BATCH, DEC_SEQ, D_MODEL)
    ak, av, bk, bv, ck, cv = new_caches
    return (y_prompt, y_sample,
            ak.reshape(BATCH, DEPTH, SEQ, 2 * HA, DA), av.reshape(BATCH, DEPTH, SEQ, HA, 2 * DA),
            bk.reshape(BATCH, DEPTH, SEQ, KVB, DB), bv.reshape(BATCH, DEPTH, SEQ, KVB, DB),
            ck.reshape(BATCH, DEPTH, SEQ, HC, DC), cv.reshape(BATCH, DEPTH, SEQ, HC, DC))
```

```python
import functools
import math

import jax
import jax.numpy as jnp
from jax import lax
from jax.experimental import pallas as pl
from jax.experimental.pallas import tpu as pltpu

D_MODEL = 1024
BATCH = 16
SEQ = 256
DEPTH = 4
DEC_BATCH = 4
DEC_SEQ = 1024
PAST_LEN = 256
GRID_W = 64
HA, DA = 4, 64
HB, KVB, DB = 8, 2, 64
HC, DC = 8, 64
NA_ROWS, NA_COLS = 8, 16
D_FF = 4 * D_MODEL
ROPE_BASE = 10000.0
EPS = 1e-6
NEG = -1e30
HEAD_DIM = 64
LOG2E = math.log2(math.e)
Q_SCALE = HEAD_DIM ** -0.5 * LOG2E

F32 = jnp.float32
BF16 = jnp.bfloat16

LANES = 128
N_QKV_GROUPS = 30
N_GATE_GROUPS = 24
COL_TILE = 768
GROUPS_PER_TILE = COL_TILE // LANES
N_QKV_TILES = N_QKV_GROUPS // GROUPS_PER_TILE
N_COL_TILES = N_QKV_TILES + N_GATE_GROUPS // GROUPS_PER_TILE
ROW_TILE = 1024
ROW_CHUNK = 256
N_TOK = BATCH * SEQ
VMEM_LIMIT_V7X = 58 * 1024 * 1024

G_QA, G_KA, G_VA, G_QB, G_KB, G_VB, G_QC, G_KC, G_VC = 0, 4, 8, 12, 16, 17, 18, 22, 26
CACHE_WIDTHS = (2 * HA * DA, HA * 2 * DA, KVB * DB, KVB * DB, HC * DC, HC * DC)

_NT = (((1,), (1,)), ((), ()))


def _params(sem, vmem=VMEM_LIMIT_V7X):
    return pltpu.CompilerParams(dimension_semantics=sem, vmem_limit_bytes=vmem)


def _lane_head(shape):
    return lax.shift_right_logical(lax.broadcasted_iota(jnp.int32, shape, len(shape) - 1), 6)


def _mod_kernel(c_ref, w_ref, b_ref, o_ref):
    c = c_ref[...]
    s = (c * jax.nn.sigmoid(c)).astype(BF16)
    o_ref[...] = jnp.dot(s, w_ref[...].astype(BF16), preferred_element_type=F32) + b_ref[...]


def _modulation(cvec, w_mod, b_mod):
    tn = 1536
    n6 = 6 * D_MODEL
    return pl.pallas_call(
        _mod_kernel,
        grid=(DEPTH, n6 // tn),
        in_specs=[pl.BlockSpec((8, D_MODEL), lambda l, n: (0, 0)),
                  pl.BlockSpec((None, D_MODEL, tn), lambda l, n: (l, 0, n)),
                  pl.BlockSpec((None, 1, tn), lambda l, n: (l, 0, n))],
        out_specs=pl.BlockSpec((None, 8, tn), lambda l, n: (l, 0, n)),
        out_shape=jax.ShapeDtypeStruct((DEPTH, 8, n6), F32),
        compiler_params=_params(("arbitrary", "arbitrary")),
        name="modulation",
    )(cvec, w_mod, b_mod.reshape(DEPTH, 1, n6))


def _rope(v, cos, sin):
    first = (lax.broadcasted_iota(jnp.int32, v.shape, 1) & 16) == 0
    partner = jnp.where(first, pltpu.roll(v, LANES - 16, 1), pltpu.roll(v, 16, 1))
    return v * cos + partner * sin


def _head_rmsnorm(v, bd, g):
    msq = jnp.dot((v * v).astype(BF16), bd, preferred_element_type=F32)
    return v * lax.rsqrt(msq + EPS) * g


def _make_inproj_kernel(latent, n_aliased):
    def kern(*refs):
        if latent:
            (x_ref, mod_ref, g1_ref, w_ref, bg_ref, gq_ref, gk_ref, bd_ref, cos_ref, sin_ref,
             z_ref, gate_ref, h_scr, wbf_scr) = refs
        else:
            (x_ref, mod_ref, g1_ref, w_ref, bg_ref, gq_ref, gk_ref, bd_ref) = refs[:8]
            (z_ref, gate_ref, ak_ref, av_ref, bk_ref, bv_ref, ck_ref, cv_ref,
             h_scr, wbf_scr) = refs[8 + n_aliased:]
        j = pl.program_id(0)
        i = pl.program_id(1)

        @pl.when(i == 0)
        def _():
            wbf_scr[...] = w_ref[...].astype(BF16)

        def chunks(first=False):
            for rc in range(ROW_TILE // ROW_CHUNK):
                rows = pl.ds(rc * ROW_CHUNK, ROW_CHUNK)
                tok = pl.ds(pl.multiple_of(i * ROW_TILE + rc * ROW_CHUNK, ROW_CHUNK), ROW_CHUNK)
                if first:
                    x = x_ref[rows, :]
                    ms = jnp.mean(x * x, axis=-1, keepdims=True)
                    y = x * lax.rsqrt(ms + EPS) * g1_ref[...]
                    m = mod_ref[0]
                    h = (y * (1.0 + m[:, D_MODEL:2 * D_MODEL]) + m[:, 0:D_MODEL]).astype(BF16)
                    h_scr[tok, :] = h
                else:
                    h = h_scr[tok, :]
                yield rc, rows, jnp.dot(h, wbf_scr[...], preferred_element_type=F32)

        def grp(a, c, n=1):
            return a[:, c * LANES:(c + n) * LANES]

        def rot(v, rows):
            return _rope(v, cos_ref[rows, :], sin_ref[rows, :]) if latent else v

        @pl.when(j == 0)
        def _():
            for rc, rows, a in chunks(first=True):
                for c in range(GROUPS_PER_TILE):
                    v = rot(grp(a, c), rows)
                    z_ref[c, rows, :] = (v * Q_SCALE if c < 4 else v).astype(BF16)
                if not latent:
                    ak_ref[rc] = grp(a, 4, 2)

        @pl.when(j == 1)
        def _():
            for rc, rows, a in chunks():
                for c in range(2):
                    z_ref[c, rows, :] = rot(grp(a, c), rows).astype(BF16)
                for c in range(2, GROUPS_PER_TILE):
                    z_ref[c, rows, :] = grp(a, c).astype(BF16)
                if not latent:
                    ak_ref[rc] = grp(a, 0, 2)
                    av_ref[rc] = grp(a, 2, 4)

        @pl.when(j == 2)
        def _():
            bd = bd_ref[...]
            for rc, rows, a in chunks():
                for half in range(2):
                    qn = _head_rmsnorm(grp(a, 2 * half, 2), bd, gq_ref[...])
                    for c in range(2):
                        z_ref[2 * half + c, rows, :] = (rot(grp(qn, c), rows) * Q_SCALE).astype(BF16)
                kn = _head_rmsnorm(grp(a, 4), bd[0:LANES, 0:LANES], gk_ref[...])
                z_ref[4, rows, :] = rot(kn, rows).astype(BF16)
                z_ref[5, rows, :] = grp(a, 5).astype(BF16)
                if not latent:
                    bk_ref[rc] = kn
                    bv_ref[rc] = grp(a, 5)

        @pl.when(j == 3)
        def _():
            for rc, rows, a in chunks():
                for c in range(GROUPS_PER_TILE):
                    v = grp(a, c)
                    z_ref[c, rows, :] = (v * Q_SCALE if c < 4 else v).astype(BF16)
                if not latent:
                    ck_ref[rc] = grp(a, 4, 2)

        @pl.when(j == 4)
        def _():
            for rc, rows, a in chunks():
                for c in range(GROUPS_PER_TILE):
                    z_ref[c, rows, :] = grp(a, c).astype(BF16)
                if not latent:
                    ck_ref[rc] = grp(a, 0, 2)
                    cv_ref[rc] = grp(a, 2, 4)

        @pl.when(j >= N_QKV_TILES)
        def _():
            for rc, rows, a in chunks():
                a = a + bg_ref[...]
                for c in range(GROUPS_PER_TILE):
                    gate_ref[c, rows, :] = jax.nn.sigmoid(grp(a, c)).astype(BF16)

    return kern


def _inproj(latent, l, x, mod, g_norm1, w_in, b_gate, gq, gk, bd, rope_cos, rope_sin, caches):
    n_row = N_TOK // ROW_TILE
    last = n_row - 1

    def row_block(j, i, first_tile, last_tile):
        return jnp.where(j < first_tile, 0, jnp.where(j <= last_tile, i, last))

    mod_idx = (lambda j, i: (1 + i, 0, 0)) if latent else (lambda j, i: (0, 0, 0))
    in_specs = [
        pl.BlockSpec((ROW_TILE, D_MODEL), lambda j, i: (row_block(j, i, 0, 0), 0)),
        pl.BlockSpec((1, 1, 6 * D_MODEL), mod_idx),
        pl.BlockSpec((None, 1, D_MODEL), lambda j, i: (l, 0, 0)),
        pl.BlockSpec((None, D_MODEL, COL_TILE), lambda j, i: (l, 0, j)),
        pl.BlockSpec((None, 1, COL_TILE), lambda j, i: (l, 0, jnp.maximum(j - N_QKV_TILES, 0))),
        pl.BlockSpec((None, 1, 2 * LANES), lambda j, i: (l, 0, 0)),
        pl.BlockSpec((None, 1, LANES), lambda j, i: (l, 0, 0)),
        pl.BlockSpec((2 * LANES, 2 * LANES), lambda j, i: (0, 0)),
    ]
    args = [x, mod, g_norm1, w_in, b_gate, gq, gk, bd]
    out_specs = [
        pl.BlockSpec((GROUPS_PER_TILE, ROW_TILE, LANES),
                     lambda j, i: (jnp.minimum(j, N_QKV_TILES - 1), row_block(j, i, 0, N_QKV_TILES - 1), 0)),
        pl.BlockSpec((GROUPS_PER_TILE, ROW_TILE, LANES),
                     lambda j, i: (jnp.maximum(j - N_QKV_TILES, 0), row_block(j, i, N_QKV_TILES, N_COL_TILES), 0)),
    ]
    out_shape = [jax.ShapeDtypeStruct((N_QKV_GROUPS, N_TOK, LANES), BF16),
                 jax.ShapeDtypeStruct((N_GATE_GROUPS, N_TOK, LANES), BF16)]
    aliases = {}
    if latent:
        in_specs += [pl.BlockSpec((DEC_SEQ, LANES), lambda j, i: (0, 0))] * 2
        args += [rope_cos, rope_sin]
    else:
        nb = ROW_TILE // SEQ
        tiles = ((0, 1, 256), (1, 1, 512), (2, 2, 128), (2, 2, 128), (3, 4, 256), (4, 4, 512))
        for k, (w, (t0, t1, bw)) in enumerate(zip(CACHE_WIDTHS, tiles)):
            if caches is not None:
                in_specs.append(pl.BlockSpec(memory_space=pl.ANY))
                args.append(caches[k])
                aliases[8 + k] = 2 + k

            def cache_idx(j, i, t0=t0, t1=t1):
                return (row_block(j, i, t0, t1), l, 0, jnp.where(j <= t0, 0, (t1 - t0)))

            out_specs.append(pl.BlockSpec((nb, None, SEQ, bw), cache_idx))
            out_shape.append(jax.ShapeDtypeStruct((BATCH, DEPTH, SEQ, w), F32))
    return pl.pallas_call(
        _make_inproj_kernel(latent, len(aliases)),
        grid=(N_COL_TILES, n_row),
        in_specs=in_specs,
        out_specs=out_specs,
        out_shape=out_shape,
        scratch_shapes=[pltpu.VMEM((N_TOK, D_MODEL), BF16), pltpu.VMEM((D_MODEL, COL_TILE), BF16)],
        input_output_aliases=aliases,
        compiler_params=_params(("arbitrary", "arbitrary")),
        name="inproj_lat" if latent else "inproj_ctx",
    )(*args)


def _mask_head(q, head):
    qf = q.astype(F32)
    keep = _lane_head(qf.shape) == head
    return jnp.where(keep, qf, 0.0).astype(BF16)


def _stack_heads(q):
    return jnp.concatenate([_mask_head(q, 0), _mask_head(q, 1)], axis=0)


def _unstack_heads(o, rows):
    return jnp.where(_lane_head((rows, LANES)) == 1, o[rows:2 * rows], o[0:rows])


def _dup_head(kv, head):
    f = kv.astype(F32)
    r = pltpu.roll(f, HEAD_DIM, 1)
    return jnp.where(_lane_head(f.shape) == head, f, r).astype(BF16)


def _attend(qm, ks, vs, biases=None):
    ss = [lax.dot_general(qm, k, _NT, preferred_element_type=F32) for k in ks]
    if biases is not None:
        ss = [s if b is None else s + b for s, b in zip(ss, biases)]
    m = functools.reduce(jnp.maximum, [jnp.max(s, axis=-1, keepdims=True) for s in ss])
    ps = [jnp.exp2(s - m) for s in ss]
    den = functools.reduce(jnp.add, [jnp.sum(p, axis=-1, keepdims=True) for p in ps])
    o = functools.reduce(jnp.add, [jnp.dot(p.astype(BF16), v, preferred_element_type=F32)
                                    for p, v in zip(ps, vs)])
    return o * (1.0 / den)


def _diff_lambda(lam_ref, lam_init):
    lq = lam_ref[...]
    a = jnp.sum(lq[0:1] * lq[1:2], axis=-1, keepdims=True)
    b = jnp.sum(lq[2:3] * lq[3:4], axis=-1, keepdims=True)
    return jnp.exp(a) - jnp.exp(b) + lam_init


def _diff_combine(o1, o2, lam, gsub, lam_init):
    o = o1 - lam * o2
    ms = jnp.mean(o * o, axis=-1, keepdims=True)
    return (o * lax.rsqrt(ms + EPS) * gsub) * (1.0 - lam_init)


def _make_attn_ctx_kernel(lam_init):
    def kern(z_ref, lam_ref, gsub_ref, oa_ref, ob_ref, oc_ref):
        lam = _diff_lambda(lam_ref, lam_init)
        gsub = gsub_ref[...]
        for vh in range(HA):
            hi = vh % 2
            v = z_ref[G_VA + vh]
            o1 = _attend(_mask_head(z_ref[G_QA + vh // 2], hi), [z_ref[G_KA + vh // 2]], [v])
            o2 = _attend(_mask_head(z_ref[G_QA + 2 + vh // 2], hi), [z_ref[G_KA + 2 + vh // 2]], [v])
            oa_ref[vh] = _diff_combine(o1, o2, lam, gsub, lam_init).astype(BF16)
        for g in range(KVB):
            kd = _dup_head(z_ref[G_KB], g)
            vd = _dup_head(z_ref[G_VB], g)
            for c in range(2 * g, 2 * g + 2):
                o = _attend(_stack_heads(z_ref[G_QB + c]), [kd], [vd])
                ob_ref[c] = _unstack_heads(o, SEQ).astype(BF16)
        for c in range(HC // 2):
            o = _attend(_stack_heads(z_ref[G_QC + c]), [z_ref[G_KC + c]], [z_ref[G_VC + c]])
            oc_ref[c] = _unstack_heads(o, SEQ).astype(BF16)
    return kern


def _attn_ctx(l, z, lambda_qk, g_subln):
    lam_init = 0.8 - 0.6 * math.exp(-0.3 * l)
    o_spec = pl.BlockSpec((4, SEQ, LANES), lambda b: (0, b, 0))
    o_shape = jax.ShapeDtypeStruct((4, N_TOK, LANES), BF16)
    return pl.pallas_call(
        _make_attn_ctx_kernel(lam_init),
        grid=(BATCH,),
        in_specs=[pl.BlockSpec((N_QKV_GROUPS, SEQ, LANES), lambda b: (0, b, 0)),
                  pl.BlockSpec((None, 4, DA), lambda b: (l, 0, 0)),
                  pl.BlockSpec((None, 1, 2 * DA), lambda b: (l, 0, 0))],
        out_specs=[o_spec, o_spec, o_spec],
        out_shape=[o_shape, o_shape, o_shape],
        compiler_params=_params(("arbitrary",)),
        name="attn_ctx",
    )(z, lambda_qk, g_subln)


Q_BLK = 256


def _make_attn_lat_a_kernel(lam_init):
    def kern(q1_ref, q2_ref, k1_ref, k2_ref, v_ref, ck1_ref, ck2_ref, cv_ref, lam_ref, gsub_ref, o_ref):
        hi = jnp.bitwise_and(pl.program_id(1), 1)
        lam = _diff_lambda(lam_ref, lam_init)
        gsub = gsub_ref[...]
        ck1 = ck1_ref[...].astype(BF16)
        ck2 = ck2_ref[...].astype(BF16)
        cv = cv_ref[...].astype(BF16)
        k1, k2, v = k1_ref[0], k2_ref[0], v_ref[0]
        for qb in range(DEC_SEQ // Q_BLK):
            rows = pl.ds(qb * Q_BLK, Q_BLK)
            o1 = _attend(_mask_head(q1_ref[0, rows, :], hi), [k1, ck1], [v, cv])
            o2 = _attend(_mask_head(q2_ref[0, rows, :], hi), [k2, ck2], [v, cv])
            o_ref[0, rows, :] = _diff_combine(o1, o2, lam, gsub, lam_init).astype(BF16)
    return kern


def _attn_lat_a(l, z, cache_k, cache_v, lambda_qk, g_subln):
    lam_init = 0.8 - 0.6 * math.exp(-0.3 * l)

    def zspec(fn):
        return pl.BlockSpec((1, DEC_SEQ, LANES), lambda b, h: (fn(h), b, 0))

    def cspec(fn):
        return pl.BlockSpec((None, None, PAST_LEN, LANES), lambda b, h: (b, l, 0, fn(h)))

    return pl.pallas_call(
        _make_attn_lat_a_kernel(lam_init),
        grid=(DEC_BATCH, HA),
        in_specs=[zspec(lambda h: G_QA + h // 2), zspec(lambda h: G_QA + 2 + h // 2),
                  zspec(lambda h: G_KA + h // 2), zspec(lambda h: G_KA + 2 + h // 2),
                  zspec(lambda h: G_VA + h),
                  cspec(lambda h: h // 2), cspec(lambda h: 2 + h // 2), cspec(lambda h: h),
                  pl.BlockSpec((None, 4, DA), lambda b, h: (l, 0, 0)),
                  pl.BlockSpec((None, 1, 2 * DA), lambda b, h: (l, 0, 0))],
        out_specs=pl.BlockSpec((1, DEC_SEQ, LANES), lambda b, h: (h, b, 0)),
        out_shape=jax.ShapeDtypeStruct((4, N_TOK, LANES), BF16),
        compiler_params=_params(("arbitrary", "arbitrary")),
        name="attn_lat_a",
    )(z, z, z, z, z, cache_k, cache_k, cache_v, lambda_qk, g_subln)


def _attn_lat_b_kernel(q_ref, k_ref, v_ref, ck_ref, cv_ref, o_ref):
    hi = lax.shift_right_logical(pl.program_id(1), 1)
    kd = _dup_head(k_ref[0], hi)
    vd = _dup_head(v_ref[0], hi)
    ckd = _dup_head(ck_ref[...].astype(BF16), hi)
    cvd = _dup_head(cv_ref[...].astype(BF16), hi)
    for qb in range(DEC_SEQ // Q_BLK):
        rows = pl.ds(qb * Q_BLK, Q_BLK)
        o = _attend(_stack_heads(q_ref[0, rows, :]), [kd, ckd], [vd, cvd])
        o_ref[0, rows, :] = _unstack_heads(o, Q_BLK).astype(BF16)


def _attn_lat_b(l, z, cache_k, cache_v):
    cspec = pl.BlockSpec((None, None, PAST_LEN, LANES), lambda b, c: (b, l, 0, 0))
    return pl.pallas_call(
        _attn_lat_b_kernel,
        grid=(DEC_BATCH, HB // 2),
        in_specs=[pl.BlockSpec((1, DEC_SEQ, LANES), lambda b, c: (G_QB + c, b, 0)),
                  pl.BlockSpec((1, DEC_SEQ, LANES), lambda b, c: (G_KB, b, 0)),
                  pl.BlockSpec((1, DEC_SEQ, LANES), lambda b, c: (G_VB, b, 0)),
                  cspec, cspec],
        out_specs=pl.BlockSpec((1, DEC_SEQ, LANES), lambda b, c: (c, b, 0)),
        out_shape=jax.ShapeDtypeStruct((4, N_TOK, LANES), BF16),
        compiler_params=_params(("arbitrary", "arbitrary")),
        name="attn_lat_b",
    )(z, z, z, cache_k, cache_v)


N_GRID_ROWS = DEC_SEQ // GRID_W
NA_KEYS = NA_ROWS * GRID_W


NA_GROUP = 4
NA_MAX_KEY_ROWS = 12
BIAS_TABLE_ROWS = 16


def _window_start(r):
    return min(max(r - NA_ROWS // 2, 0), N_GRID_ROWS - NA_ROWS)


def _group_keys(g):
    starts = [_window_start(r) for r in range(g * NA_GROUP, (g + 1) * NA_GROUP)]
    n = max(starts) + NA_ROWS - min(starts)
    n += n % 2
    return min(min(starts), N_GRID_ROWS - n), n


def _build_window_bias(tab_ref, bias_scr):
    qcol = lax.broadcasted_iota(jnp.int32, (GRID_W, LANES), 0)
    lane = lax.broadcasted_iota(jnp.int32, (GRID_W, LANES), 1)
    kcol = lane & (GRID_W - 1)
    c0 = jnp.clip(qcol - NA_COLS // 2, 0, GRID_W - NA_COLS)
    in_win = (kcol >= c0) & (kcol < c0 + NA_COLS)
    masks = {(True, True): in_win, (True, False): in_win & (lane < GRID_W), (False, True): in_win & (lane >= GRID_W)}
    for head in range(2):
        for r in range(N_GRID_ROWS):
            u0, n = _group_keys(r // NA_GROUP)
            rows = pl.ds(r * GRID_W, GRID_W)
            for m in range(n // 2):
                key_rows = (u0 + 2 * m, u0 + 2 * m + 1)
                valid = tuple(_window_start(r) <= kr < _window_start(r) + NA_ROWS for kr in key_rows)
                cols = pl.ds(m * LANES, LANES)
                if not any(valid):
                    bias_scr[head, rows, cols] = jnp.full((GRID_W, LANES), NEG, F32)
                    continue
                vec = None
                for half, (kr, ok) in enumerate(zip(key_rows, valid)):
                    if ok:
                        dr = kr - r + NA_ROWS - 1
                        part = tab_ref[head, pl.ds(half * BIAS_TABLE_ROWS + dr, 1), :]
                        vec = part if vec is None else vec + part
                toep = pltpu.roll(jnp.broadcast_to(vec, (GRID_W, LANES)), 0, 1, stride=1, stride_axis=0)
                bias_scr[head, rows, cols] = jnp.where(masks[valid], toep * LOG2E, NEG)


def _attn_lat_c_kernel(q_ref, k_ref, v_ref, ck_ref, cv_ref, tab_ref, o_ref, bias_scr):
    @pl.when(pl.program_id(1) == 0)
    def _():
        _build_window_bias(tab_ref, bias_scr)

    ck = ck_ref[...].astype(BF16)
    cv = cv_ref[...].astype(BF16)
    n_rows = NA_GROUP * GRID_W
    for g in range(N_GRID_ROWS // NA_GROUP):
        u0, n = _group_keys(g)
        rows = pl.ds(g * n_rows, n_rows)
        keys = pl.ds(u0 * GRID_W, n * GRID_W)
        bias = jnp.concatenate([bias_scr[0, rows, 0:n * GRID_W], bias_scr[1, rows, 0:n * GRID_W]], axis=0)
        o = _attend(_stack_heads(q_ref[0, rows, :]), [k_ref[0, keys, :], ck], [v_ref[0, keys, :], cv],
                    biases=[bias, None])
        o_ref[0, rows, :] = _unstack_heads(o, n_rows).astype(BF16)


def _attn_lat_c(l, z, cache_k, cache_v, bias_rows):
    def zspec(g0):
        return pl.BlockSpec((1, DEC_SEQ, LANES), lambda c, b: (g0 + c, b, 0))

    cspec = pl.BlockSpec((None, None, PAST_LEN, LANES), lambda c, b: (b, l, 0, c))
    return pl.pallas_call(
        _attn_lat_c_kernel,
        grid=(HC // 2, DEC_BATCH),
        in_specs=[zspec(G_QC), zspec(G_KC), zspec(G_VC), cspec, cspec,
                  pl.BlockSpec((None, 2, 2 * BIAS_TABLE_ROWS, LANES), lambda c, b: (l, c, 0, 0))],
        out_specs=pl.BlockSpec((1, DEC_SEQ, LANES), lambda c, b: (c, b, 0)),
        out_shape=jax.ShapeDtypeStruct((4, N_TOK, LANES), BF16),
        scratch_shapes=[pltpu.VMEM((2, DEC_SEQ, NA_MAX_KEY_ROWS * GRID_W), F32)],
        compiler_params=_params(("arbitrary", "arbitrary")),
        name="attn_lat_c",
    )(z, z, z, cache_k, cache_v, bias_rows)


MERGE_TILE = 512


def _merge_kernel(x_ref, oa_ref, ob_ref, oc_ref, gate_ref, mod_ref, g2_ref, wa_ref, wb_ref, wc_ref, wo_ref,
                  xo_ref, h2_ref, wbr_scr, wo_scr):
    @pl.when(pl.program_id(0) == 0)
    def _():
        wbr_scr[0] = wa_ref[...].astype(BF16)
        wbr_scr[1] = wb_ref[...].astype(BF16)
        wbr_scr[2] = wc_ref[...].astype(BF16)
        wo_scr[...] = wo_ref[...].astype(BF16)

    y = None
    for k, o_ref in enumerate((oa_ref, ob_ref, oc_ref)):
        o = jnp.concatenate([o_ref[c] for c in range(4)], axis=-1)
        p = jnp.dot(o, wbr_scr[k], preferred_element_type=F32)
        g = jnp.concatenate([gate_ref[8 * k + c] for c in range(8)], axis=-1).astype(F32)
        y = g * p if y is None else y + g * p
    out = jnp.dot(y.astype(BF16), wo_scr[...], preferred_element_type=F32)
    m = mod_ref[0]
    x = x_ref[...] + m[:, 2 * D_MODEL:3 * D_MODEL] * out
    xo_ref[...] = x
    ms = jnp.mean(x * x, axis=-1, keepdims=True)
    h2 = x * lax.rsqrt(ms + EPS) * g2_ref[...]
    h2_ref[...] = (h2 * (1.0 + m[:, 4 * D_MODEL:5 * D_MODEL]) + m[:, 3 * D_MODEL:4 * D_MODEL]).astype(BF16)


def _merge(latent, l, x, oa, ob, oc, gates, mod, g_norm2, w_a, w_b, w_c, w_o):
    tm = MERGE_TILE
    if latent:
        mod_idx = lambda i: (1 + (i * tm) // DEC_SEQ, 0, 0)
    else:
        mod_idx = lambda i: (0, 0, 0)
    o_spec = pl.BlockSpec((4, tm, LANES), lambda i: (0, i, 0))
    wbr_spec = pl.BlockSpec((None, 4 * LANES, D_MODEL), lambda i: (l, 0, 0))
    return pl.pallas_call(
        _merge_kernel,
        grid=(N_TOK // tm,),
        in_specs=[pl.BlockSpec((tm, D_MODEL), lambda i: (i, 0)),
                  o_spec, o_spec, o_spec,
                  pl.BlockSpec((N_GATE_GROUPS, tm, LANES), lambda i: (0, i, 0)),
                  pl.BlockSpec((1, 1, 6 * D_MODEL), mod_idx),
                  pl.BlockSpec((None, 1, D_MODEL), lambda i: (l, 0, 0)),
                  wbr_spec, wbr_spec, wbr_spec,
                  pl.BlockSpec((None, D_MODEL, D_MODEL), lambda i: (l, 0, 0))],
        out_specs=[pl.BlockSpec((tm, D_MODEL), lambda i: (i, 0)),
                   pl.BlockSpec((tm, D_MODEL), lambda i: (i, 0))],
        out_shape=[jax.ShapeDtypeStruct((N_TOK, D_MODEL), F32),
                   jax.ShapeDtypeStruct((N_TOK, D_MODEL), BF16)],
        scratch_shapes=[pltpu.VMEM((3, 4 * LANES, D_MODEL), BF16), pltpu.VMEM((D_MODEL, D_MODEL), BF16)],
        compiler_params=_params(("arbitrary",)),
        name="merge_lat" if latent else "merge_ctx",
    )(x, oa, ob, oc, gates, mod, g_norm2, w_a, w_b, w_c, w_o)


FF_TILE = 1024


def _ffn_kernel(x_ref, h2_ref, mod_ref, w1_ref, w2_ref, gf_ref, xo_ref, acc_scr, *, final):
    f = pl.program_id(1)

    @pl.when(f == 0)
    def _():
        acc_scr[...] = jnp.zeros_like(acc_scr)

    w1 = w1_ref[...].astype(BF16)
    w2 = w2_ref[...].astype(BF16)
    for rc in range(ROW_TILE // ROW_CHUNK):
        rows = pl.ds(rc * ROW_CHUNK, ROW_CHUNK)
        u = jnp.dot(h2_ref[rows, :], w1, preferred_element_type=F32)
        u = jnp.square(jnp.maximum(u, 0.0)).astype(BF16)
        acc_scr[rows, :] += jnp.dot(u, w2, preferred_element_type=F32)

    @pl.when(f == D_FF // FF_TILE - 1)
    def _():
        x = x_ref[...] + mod_ref[0][:, 5 * D_MODEL:6 * D_MODEL] * acc_scr[...]
        if final:
            ms = jnp.mean(x * x, axis=-1, keepdims=True)
            x = x * lax.rsqrt(ms + EPS) * gf_ref[...]
        xo_ref[...] = x


def _ffn(latent, l, x, h2, mod, w1, w2, g_final):
    tm = ROW_TILE
    mod_idx = (lambda i, f: (1 + i, 0, 0)) if latent else (lambda i, f: (0, 0, 0))
    return pl.pallas_call(
        functools.partial(_ffn_kernel, final=(l == DEPTH - 1)),
        grid=(N_TOK // tm, D_FF // FF_TILE),
        in_specs=[pl.BlockSpec((tm, D_MODEL), lambda i, f: (i, 0)),
                  pl.BlockSpec((tm, D_MODEL), lambda i, f: (i, 0)),
                  pl.BlockSpec((1, 1, 6 * D_MODEL), mod_idx),
                  pl.BlockSpec((None, D_MODEL, FF_TILE), lambda i, f: (l, 0, f)),
                  pl.BlockSpec((None, FF_TILE, D_MODEL), lambda i, f: (l, f, 0)),
                  pl.BlockSpec((1, D_MODEL), lambda i, f: (0, 0))],
        out_specs=pl.BlockSpec((tm, D_MODEL), lambda i, f: (i, 0)),
        out_shape=jax.ShapeDtypeStruct((N_TOK, D_MODEL), F32),
        scratch_shapes=[pltpu.VMEM((tm, D_MODEL), F32)],
        compiler_params=_params(("arbitrary", "arbitrary")),
        name="ffn_lat" if latent else "ffn_ctx",
    )(x, h2, mod, w1, w2, g_final)


def _rope_tables():
    nf = HEAD_DIM // 4
    t = jnp.arange(DEC_SEQ)
    row = (t // GRID_W).astype(F32)
    col = (t % GRID_W).astype(F32)
    inv = ROPE_BASE ** (-jnp.arange(nf, dtype=F32) / nf)
    ar = row[:, None] * inv[None, :]
    ac = col[:, None] * inv[None, :]
    cos = jnp.concatenate([jnp.cos(ar), jnp.cos(ar), jnp.cos(ac), jnp.cos(ac)], axis=-1)
    sin = jnp.concatenate([-jnp.sin(ar), jnp.sin(ar), -jnp.sin(ac), jnp.sin(ac)], axis=-1)
    return jnp.tile(cos, (1, 2)), jnp.tile(sin, (1, 2))


def _packed_bias_rows(rel_bias):
    n = 2 * NA_ROWS - 1
    first = jnp.concatenate([rel_bias[..., NA_COLS - 1:], jnp.zeros((DEPTH, HC, n, LANES - (2 * NA_COLS - 1)), F32),
                             rel_bias[..., :NA_COLS - 1]], axis=-1)
    lo = GRID_W - NA_COLS + 1
    second = jnp.pad(rel_bias, ((0, 0), (0, 0), (0, 0), (lo, LANES - lo - (2 * NA_COLS - 1))))
    pad_rows = ((0, 0), (0, 0), (0, BIAS_TABLE_ROWS - n), (0, 0))
    return jnp.concatenate([jnp.pad(first, pad_rows), jnp.pad(second, pad_rows)], axis=2)


def kernel(x_prompt, x_sample, c, cache_a_k, cache_a_v, cache_b_k, cache_b_v, cache_c_k, cache_c_v, c_ctx, w_mod, b_mod, g_norm1, g_norm2, w_in, b_gate, lambda_qk, g_subln, g_qnorm, g_knorm, rel_bias, w_branch_a, w_branch_b, w_branch_c, w_out, w_ff1, w_ff2, g_final):
    xp = x_prompt.reshape(N_TOK, D_MODEL)
    xs = x_sample.reshape(N_TOK, D_MODEL)

    cvec = jnp.concatenate([c_ctx[None, :], c, jnp.zeros((3, D_MODEL), F32)], axis=0)
    mods = _modulation(cvec, w_mod, b_mod)

    rope_cos, rope_sin = _rope_tables()
    bias_rows = _packed_bias_rows(rel_bias)
    bd = jnp.kron(jnp.eye(2 * LANES // HEAD_DIM, dtype=F32),
                  jnp.full((HEAD_DIM, HEAD_DIM), 1.0 / HEAD_DIM, F32)).astype(BF16)
    gq = jnp.tile(g_qnorm, (1, 2 * LANES // DB)).reshape(DEPTH, 1, 2 * LANES)
    gk = jnp.tile(g_knorm, (1, LANES // DB)).reshape(DEPTH, 1, LANES)
    g1 = g_norm1.reshape(DEPTH, 1, D_MODEL)
    g2 = g_norm2.reshape(DEPTH, 1, D_MODEL)
    bg = b_gate.reshape(DEPTH, 1, 3 * D_MODEL)
    gsub = g_subln.reshape(DEPTH, 1, 2 * DA)
    gf = g_final.reshape(1, D_MODEL)

    lat_ak = cache_a_k.reshape(DEC_BATCH, DEPTH, PAST_LEN, 2 * HA * DA)
    lat_av = cache_a_v.reshape(DEC_BATCH, DEPTH, PAST_LEN, HA * 2 * DA)
    lat_bk = cache_b_k.reshape(DEC_BATCH, DEPTH, PAST_LEN, KVB * DB)
    lat_bv = cache_b_v.reshape(DEC_BATCH, DEPTH, PAST_LEN, KVB * DB)
    lat_ck = cache_c_k.reshape(DEC_BATCH, DEPTH, PAST_LEN, HC * DC)
    lat_cv = cache_c_v.reshape(DEC_BATCH, DEPTH, PAST_LEN, HC * DC)

    new_caches = None

    for l in range(DEPTH):
        mod = mods[l].reshape(8, 1, 6 * D_MODEL)

        outs = _inproj(False, l, xp, mod, g1, w_in, bg, gq, gk, bd, None, None, new_caches)
        z, gates, new_caches = outs[0], outs[1], list(outs[2:])
        oa, ob, oc = _attn_ctx(l, z, lambda_qk, gsub)
        xp, h2 = _merge(False, l, xp, oa, ob, oc, gates, mod, g2, w_branch_a, w_branch_b, w_branch_c, w_out)
        xp = _ffn(False, l, xp, h2, mod, w_ff1, w_ff2, gf)

        z, gates = _inproj(True, l, xs, mod, g1, w_in, bg, gq, gk, bd, rope_cos, rope_sin, None)
        oa = _attn_lat_a(l, z, lat_ak, lat_av, lambda_qk, gsub)
        ob = _attn_lat_b(l, z, lat_bk, lat_bv)
        oc = _attn_lat_c(l, z, lat_ck, lat_cv, bias_rows)
        xs, h2 = _merge(True, l, xs, oa, ob, oc, gates, mod, g2, w_branch_a, w_branch_b, w_branch_c, w_out)
        xs = _ffn(True, l, xs, h2, mod, w_ff1, w_ff2, gf)

    y_prompt = xp.reshape(BATCH, SEQ, D_MODEL)
    y_sample = xs.reshape(DEC_BATCH, DEC_SEQ, D_MODEL)
    ak, av, bk, bv, ck, cv = new_caches
    return (y_prompt, y_sample,
            ak.reshape(BATCH, DEPTH, SEQ, 2 * HA, DA), av.reshape(BATCH, DEPTH, SEQ, HA, 2 * DA),
            bk.reshape(BATCH, DEPTH, SEQ, KVB, DB), bv.reshape(BATCH, DEPTH, SEQ, KVB, DB),
            ck.reshape(BATCH, DEPTH, SEQ, HC, DC), cv.reshape(BATCH, DEPTH, SEQ, HC, DC))
```

```python
import functools
import math

import jax
import jax.numpy as jnp
from jax import lax
from jax.experimental import pallas as pl
from jax.experimental.pallas import tpu as pltpu

D_MODEL = 1024
BATCH = 16
SEQ = 256
DEPTH = 4
DEC_BATCH = 4
DEC_SEQ = 1024
PAST_LEN = 256
GRID_W = 64
HA, DA = 4, 64
HB, KVB, DB = 8, 2, 64
HC, DC = 8, 64
NA_ROWS, NA_COLS = 8, 16
D_FF = 4 * D_MODEL
ROPE_BASE = 10000.0
EPS = 1e-6
NEG = -1e30
HEAD_DIM = 64
LOG2E = math.log2(math.e)
Q_SCALE = HEAD_DIM ** -0.5 * LOG2E

F32 = jnp.float32
BF16 = jnp.bfloat16

LANES = 128
N_QKV_GROUPS = 30
N_GATE_GROUPS = 24
COL_TILE = 768
GROUPS_PER_TILE = COL_TILE // LANES
N_QKV_TILES = N_QKV_GROUPS // GROUPS_PER_TILE
N_COL_TILES = N_QKV_TILES + N_GATE_GROUPS // GROUPS_PER_TILE
ROW_TILE = 1024
ROW_CHUNK = 256
N_TOK = BATCH * SEQ
VMEM_LIMIT_V7X = 58 * 1024 * 1024

G_QA, G_KA, G_VA, G_QB, G_KB, G_VB, G_QC, G_KC, G_VC = 0, 4, 8, 12, 16, 17, 18, 22, 26
CACHE_WIDTHS = (2 * HA * DA, HA * 2 * DA, KVB * DB, KVB * DB, HC * DC, HC * DC)

_NT = (((1,), (1,)), ((), ()))


def _params(sem, vmem=VMEM_LIMIT_V7X):
    return pltpu.CompilerParams(dimension_semantics=sem, vmem_limit_bytes=vmem)


def _lane_head(shape):
    return lax.shift_right_logical(lax.broadcasted_iota(jnp.int32, shape, len(shape) - 1), 6)


def _mod_kernel(c_ref, w_ref, b_ref, o_ref):
    c = c_ref[...]
    s = (c * jax.nn.sigmoid(c)).astype(BF16)
    o_ref[...] = jnp.dot(s, w_ref[...].astype(BF16), preferred_element_type=F32) + b_ref[...]


def _modulation(cvec, w_mod, b_mod):
    tn = 1536
    n6 = 6 * D_MODEL
    return pl.pallas_call(
        _mod_kernel,
        grid=(DEPTH, n6 // tn),
        in_specs=[pl.BlockSpec((8, D_MODEL), lambda l, n: (0, 0)),
                  pl.BlockSpec((None, D_MODEL, tn), lambda l, n: (l, 0, n)),
                  pl.BlockSpec((None, 1, tn), lambda l, n: (l, 0, n))],
        out_specs=pl.BlockSpec((None, 8, tn), lambda l, n: (l, 0, n)),
        out_shape=jax.ShapeDtypeStruct((DEPTH, 8, n6), F32),
        compiler_params=_params(("arbitrary", "arbitrary")),
        name="modulation",
    )(cvec, w_mod, b_mod.reshape(DEPTH, 1, n6))


def _rope(v, cos, sin):
    first = (lax.broadcasted_iota(jnp.int32, v.shape, 1) & 16) == 0
    partner = jnp.where(first, pltpu.roll(v, LANES - 16, 1), pltpu.roll(v, 16, 1))
    return v * cos + partner * sin


def _head_rmsnorm(v, bd, g):
    msq = jnp.dot((v * v).astype(BF16), bd, preferred_element_type=F32)
    return v * lax.rsqrt(msq + EPS) * g


def _make_inproj_kernel(latent, n_aliased):
    def kern(*refs):
        if latent:
            (x_ref, mod_ref, g1_ref, w_ref, bg_ref, gq_ref, gk_ref, bd_ref, cos_ref, sin_ref,
             z_ref, gate_ref, h_scr, wbf_scr) = refs
        else:
            (x_ref, mod_ref, g1_ref, w_ref, bg_ref, gq_ref, gk_ref, bd_ref) = refs[:8]
            (z_ref, gate_ref, ak_ref, av_ref, bk_ref, bv_ref, ck_ref, cv_ref,
             h_scr, wbf_scr) = refs[8 + n_aliased:]
        j = pl.program_id(0)
        i = pl.program_id(1)

        @pl.when(i == 0)
        def _():
            wbf_scr[...] = w_ref[...].astype(BF16)

        def chunks(first=False):
            for rc in range(ROW_TILE // ROW_CHUNK):
                rows = pl.ds(rc * ROW_CHUNK, ROW_CHUNK)
                tok = pl.ds(pl.multiple_of(i * ROW_TILE + rc * ROW_CHUNK, ROW_CHUNK), ROW_CHUNK)
                if first:
                    x = x_ref[rows, :]
                    ms = jnp.mean(x * x, axis=-1, keepdims=True)
                    y = x * lax.rsqrt(ms + EPS) * g1_ref[...]
                    m = mod_ref[0]
                    h = (y * (1.0 + m[:, D_MODEL:2 * D_MODEL]) + m[:, 0:D_MODEL]).astype(BF16)
                    h_scr[tok, :] = h
                else:
                    h = h_scr[tok, :]
                yield rc, rows, jnp.dot(h, wbf_scr[...], preferred_element_type=F32)

        def grp(a, c, n=1):
            return a[:, c * LANES:(c + n) * LANES]

        def rot(v, rows):
            return _rope(v, cos_ref[rows, :], sin_ref[rows, :]) if latent else v

        @pl.when(j == 0)
        def _():
            for rc, rows, a in chunks(first=True):
                for c in range(GROUPS_PER_TILE):
                    v = rot(grp(a, c), rows)
                    z_ref[c, rows, :] = (v * Q_SCALE if c < 4 else v).astype(BF16)
                if not latent:
                    ak_ref[rc] = grp(a, 4, 2)

        @pl.when(j == 1)
        def _():
            for rc, rows, a in chunks():
                for c in range(2):
                    z_ref[c, rows, :] = rot(grp(a, c), rows).astype(BF16)
                for c in range(2, GROUPS_PER_TILE):
                    z_ref[c, rows, :] = grp(a, c).astype(BF16)
                if not latent:
                    ak_ref[rc] = grp(a, 0, 2)
                    av_ref[rc] = grp(a, 2, 4)

        @pl.when(j == 2)
        def _():
            bd = bd_ref[...]
            for rc, rows, a in chunks():
                for half in range(2):
                    qn = _head_rmsnorm(grp(a, 2 * half, 2), bd, gq_ref[...])
                    for c in range(2):
                        z_ref[2 * half + c, rows, :] = (rot(grp(qn, c), rows) * Q_SCALE).astype(BF16)
                kn = _head_rmsnorm(grp(a, 4), bd[0:LANES, 0:LANES], gk_ref[...])
                z_ref[4, rows, :] = rot(kn, rows).astype(BF16)
                z_ref[5, rows, :] = grp(a, 5).astype(BF16)
                if not latent:
                    bk_ref[rc] = kn
                    bv_ref[rc] = grp(a, 5)

        @pl.when(j == 3)
        def _():
            for rc, rows, a in chunks():
                for c in range(GROUPS_PER_TILE):
                    v = grp(a, c)
                    z_ref[c, rows, :] = (v * Q_SCALE if c < 4 else v).astype(BF16)
                if not latent:
                    ck_ref[rc] = grp(a, 4, 2)

        @pl.when(j == 4)
        def _():
            for rc, rows, a in chunks():
                for c in range(GROUPS_PER_TILE):
                    z_ref[c, rows, :] = grp(a, c).astype(BF16)
                if not latent:
                    ck_ref[rc] = grp(a, 0, 2)
                    cv_ref[rc] = grp(a, 2, 4)

        @pl.when(j >= N_QKV_TILES)
        def _():
            for rc, rows, a in chunks():
                a = a + bg_ref[...]
                for c in range(GROUPS_PER_TILE):
                    gate_ref[c, rows, :] = jax.nn.sigmoid(grp(a, c)).astype(BF16)

    return kern


def _inproj(latent, l, x, mod, g_norm1, w_in, b_gate, gq, gk, bd, rope_cos, rope_sin, caches):
    n_row = N_TOK // ROW_TILE
    last = n_row - 1

    def row_block(j, i, first_tile, last_tile):
        return jnp.where(j < first_tile, 0, jnp.where(j <= last_tile, i, last))

    mod_idx = (lambda j, i: (1 + i, 0, 0)) if latent else (lambda j, i: (0, 0, 0))
    in_specs = [
        pl.BlockSpec((ROW_TILE, D_MODEL), lambda j, i: (row_block(j, i, 0, 0), 0)),
        pl.BlockSpec((1, 1, 6 * D_MODEL), mod_idx),
        pl.BlockSpec((None, 1, D_MODEL), lambda j, i: (l, 0, 0)),
        pl.BlockSpec((None, D_MODEL, COL_TILE), lambda j, i: (l, 0, j)),
        pl.BlockSpec((None, 1, COL_TILE), lambda j, i: (l, 0, jnp.maximum(j - N_QKV_TILES, 0))),
        pl.BlockSpec((None, 1, 2 * LANES), lambda j, i: (l, 0, 0)),
        pl.BlockSpec((None, 1, LANES), lambda j, i: (l, 0, 0)),
        pl.BlockSpec((2 * LANES, 2 * LANES), lambda j, i: (0, 0)),
    ]
    args = [x, mod, g_norm1, w_in, b_gate, gq, gk, bd]
    out_specs = [
        pl.BlockSpec((GROUPS_PER_TILE, ROW_TILE, LANES),
                     lambda j, i: (jnp.minimum(j, N_QKV_TILES - 1), row_block(j, i, 0, N_QKV_TILES - 1), 0)),
        pl.BlockSpec((GROUPS_PER_TILE, ROW_TILE, LANES),
                     lambda j, i: (jnp.maximum(j - N_QKV_TILES, 0), row_block(j, i, N_QKV_TILES, N_COL_TILES), 0)),
    ]
    out_shape = [jax.ShapeDtypeStruct((N_QKV_GROUPS, N_TOK, LANES), BF16),
                 jax.ShapeDtypeStruct((N_GATE_GROUPS, N_TOK, LANES), BF16)]
    aliases = {}
    if latent:
        in_specs += [pl.BlockSpec((DEC_SEQ, LANES), lambda j, i: (0, 0))] * 2
        args += [rope_cos, rope_sin]
    else:
        nb = ROW_TILE // SEQ
        tiles = ((0, 1, 256), (1, 1, 512), (2, 2, 128), (2, 2, 128), (3, 4, 256), (4, 4, 512))
        for k, (w, (t0, t1, bw)) in enumerate(zip(CACHE_WIDTHS, tiles)):
            if caches is not None:
                in_specs.append(pl.BlockSpec(memory_space=pl.ANY))
                args.append(caches[k])
                aliases[8 + k] = 2 + k

            def cache_idx(j, i, t0=t0, t1=t1):
                return (row_block(j, i, t0, t1), l, 0, jnp.where(j <= t0, 0, (t1 - t0)))

            out_specs.append(pl.BlockSpec((nb, None, SEQ, bw), cache_idx))
            out_shape.append(jax.ShapeDtypeStruct((BATCH, DEPTH, SEQ, w), F32))
    return pl.pallas_call(
        _make_inproj_kernel(latent, len(aliases)),
        grid=(N_COL_TILES, n_row),
        in_specs=in_specs,
        out_specs=out_specs,
        out_shape=out_shape,
        scratch_shapes=[pltpu.VMEM((N_TOK, D_MODEL), BF16), pltpu.VMEM((D_MODEL, COL_TILE), BF16)],
        input_output_aliases=aliases,
        compiler_params=_params(("arbitrary", "arbitrary")),
        name="inproj_lat" if latent else "inproj_ctx",
    )(*args)


def _mask_head(q, head):
    qf = q.astype(F32)
    keep = _lane_head(qf.shape) == head
    return jnp.where(keep, qf, 0.0).astype(BF16)


def _stack_heads(q):
    return jnp.concatenate([_mask_head(q, 0), _mask_head(q, 1)], axis=0)


def _unstack_heads(o, rows):
    return jnp.where(_lane_head((rows, LANES)) == 1, o[rows:2 * rows], o[0:rows])


def _dup_head(kv, head):
    f = kv.astype(F32)
    r = pltpu.roll(f, HEAD_DIM, 1)
    return jnp.where(_lane_head(f.shape) == head, f, r).astype(BF16)


def _attend(qm, ks, vs, biases=None):
    ss = [lax.dot_general(qm, k, _NT, preferred_element_type=F32) for k in ks]
    if biases is not None:
        ss = [s if b is None else s + b for s, b in zip(ss, biases)]
    m = functools.reduce(jnp.maximum, [jnp.max(s, axis=-1, keepdims=True) for s in ss])
    ps = [jnp.exp2(s - m) for s in ss]
    den = functools.reduce(jnp.add, [jnp.sum(p, axis=-1, keepdims=True) for p in ps])
    o = functools.reduce(jnp.add, [jnp.dot(p.astype(BF16), v, preferred_element_type=F32)
                                    for p, v in zip(ps, vs)])
    return o * (1.0 / den)


def _diff_lambda(lam_ref, lam_init):
    lq = lam_ref[...]
    a = jnp.sum(lq[0:1] * lq[1:2], axis=-1, keepdims=True)
    b = jnp.sum(lq[2:3] * lq[3:4], axis=-1, keepdims=True)
    return jnp.exp(a) - jnp.exp(b) + lam_init


def _diff_combine(o1, o2, lam, gsub, lam_init):
    o = o1 - lam * o2
    ms = jnp.mean(o * o, axis=-1, keepdims=True)
    return (o * lax.rsqrt(ms + EPS) * gsub) * (1.0 - lam_init)


def _make_attn_ctx_kernel(lam_init):
    def kern(z_ref, lam_ref, gsub_ref, oa_ref, ob_ref, oc_ref):
        lam = _diff_lambda(lam_ref, lam_init)
        gsub = gsub_ref[...]
        for vh in range(HA):
            hi = vh % 2
            v = z_ref[G_VA + vh]
            o1 = _attend(_mask_head(z_ref[G_QA + vh // 2], hi), [z_ref[G_KA + vh // 2]], [v])
            o2 = _attend(_mask_head(z_ref[G_QA + 2 + vh // 2], hi), [z_ref[G_KA + 2 + vh // 2]], [v])
            oa_ref[vh] = _diff_combine(o1, o2, lam, gsub, lam_init).astype(BF16)
        for g in range(KVB):
            kd = _dup_head(z_ref[G_KB], g)
            vd = _dup_head(z_ref[G_VB], g)
            for c in range(2 * g, 2 * g + 2):
                o = _attend(_stack_heads(z_ref[G_QB + c]), [kd], [vd])
                ob_ref[c] = _unstack_heads(o, SEQ).astype(BF16)
        for c in range(HC // 2):
            o = _attend(_stack_heads(z_ref[G_QC + c]), [z_ref[G_KC + c]], [z_ref[G_VC + c]])
            oc_ref[c] = _unstack_heads(o, SEQ).astype(BF16)
    return kern


def _attn_ctx(l, z, lambda_qk, g_subln):
    lam_init = 0.8 - 0.6 * math.exp(-0.3 * l)
    o_spec = pl.BlockSpec((4, SEQ, LANES), lambda b: (0, b, 0))
    o_shape = jax.ShapeDtypeStruct((4, N_TOK, LANES), BF16)
    return pl.pallas_call(
        _make_attn_ctx_kernel(lam_init),
        grid=(BATCH,),
        in_specs=[pl.BlockSpec((N_QKV_GROUPS, SEQ, LANES), lambda b: (0, b, 0)),
                  pl.BlockSpec((None, 4, DA), lambda b: (l, 0, 0)),
                  pl.BlockSpec((None, 1, 2 * DA), lambda b: (l, 0, 0))],
        out_specs=[o_spec, o_spec, o_spec],
        out_shape=[o_shape, o_shape, o_shape],
        compiler_params=_params(("arbitrary",)),
        name="attn_ctx",
    )(z, lambda_qk, g_subln)


Q_BLK_GQA = 256
Q_BLK_DIFF = 512


def _make_attn_lat_a_kernel(lam_init):
    def kern(q1_ref, q2_ref, k1_ref, k2_ref, v_ref, ck1_ref, ck2_ref, cv_ref, lam_ref, gsub_ref, o_ref):
        hi = jnp.bitwise_and(pl.program_id(1), 1)
        lam = _diff_lambda(lam_ref, lam_init)
        gsub = gsub_ref[...]
        ck1 = ck1_ref[...].astype(BF16)
        ck2 = ck2_ref[...].astype(BF16)
        cv = cv_ref[...].astype(BF16)
        k1, k2, v = k1_ref[0], k2_ref[0], v_ref[0]
        for qb in range(DEC_SEQ // Q_BLK_DIFF):
            rows = pl.ds(qb * Q_BLK_DIFF, Q_BLK_DIFF)
            o1 = _attend(_mask_head(q1_ref[0, rows, :], hi), [k1, ck1], [v, cv])
            o2 = _attend(_mask_head(q2_ref[0, rows, :], hi), [k2, ck2], [v, cv])
            o_ref[0, rows, :] = _diff_combine(o1, o2, lam, gsub, lam_init).astype(BF16)
    return kern


def _attn_lat_a(l, z, cache_k, cache_v, lambda_qk, g_subln):
    lam_init = 0.8 - 0.6 * math.exp(-0.3 * l)

    def zspec(fn):
        return pl.BlockSpec((1, DEC_SEQ, LANES), lambda b, h: (fn(h), b, 0))

    def cspec(fn):
        return pl.BlockSpec((None, None, PAST_LEN, LANES), lambda b, h: (b, l, 0, fn(h)))

    return pl.pallas_call(
        _make_attn_lat_a_kernel(lam_init),
        grid=(DEC_BATCH, HA),
        in_specs=[zspec(lambda h: G_QA + h // 2), zspec(lambda h: G_QA + 2 + h // 2),
                  zspec(lambda h: G_KA + h // 2), zspec(lambda h: G_KA + 2 + h // 2),
                  zspec(lambda h: G_VA + h),
                  cspec(lambda h: h // 2), cspec(lambda h: 2 + h // 2), cspec(lambda h: h),
                  pl.BlockSpec((None, 4, DA), lambda b, h: (l, 0, 0)),
                  pl.BlockSpec((None, 1, 2 * DA), lambda b, h: (l, 0, 0))],
        out_specs=pl.BlockSpec((1, DEC_SEQ, LANES), lambda b, h: (h, b, 0)),
        out_shape=jax.ShapeDtypeStruct((4, N_TOK, LANES), BF16),
        compiler_params=_params(("arbitrary", "arbitrary")),
        name="attn_lat_a",
    )(z, z, z, z, z, cache_k, cache_k, cache_v, lambda_qk, g_subln)


def _attn_lat_b_kernel(q_ref, k_ref, v_ref, ck_ref, cv_ref, o_ref):
    hi = lax.shift_right_logical(pl.program_id(1), 1)
    kd = _dup_head(k_ref[0], hi)
    vd = _dup_head(v_ref[0], hi)
    ckd = _dup_head(ck_ref[...].astype(BF16), hi)
    cvd = _dup_head(cv_ref[...].astype(BF16), hi)
    for qb in range(DEC_SEQ // Q_BLK_GQA):
        rows = pl.ds(qb * Q_BLK_GQA, Q_BLK_GQA)
        o = _attend(_stack_heads(q_ref[0, rows, :]), [kd, ckd], [vd, cvd])
        o_ref[0, rows, :] = _unstack_heads(o, Q_BLK_GQA).astype(BF16)


def _attn_lat_b(l, z, cache_k, cache_v):
    cspec = pl.BlockSpec((None, None, PAST_LEN, LANES), lambda b, c: (b, l, 0, 0))
    return pl.pallas_call(
        _attn_lat_b_kernel,
        grid=(DEC_BATCH, HB // 2),
        in_specs=[pl.BlockSpec((1, DEC_SEQ, LANES), lambda b, c: (G_QB + c, b, 0)),
                  pl.BlockSpec((1, DEC_SEQ, LANES), lambda b, c: (G_KB, b, 0)),
                  pl.BlockSpec((1, DEC_SEQ, LANES), lambda b, c: (G_VB, b, 0)),
                  cspec, cspec],
        out_specs=pl.BlockSpec((1, DEC_SEQ, LANES), lambda b, c: (c, b, 0)),
        out_shape=jax.ShapeDtypeStruct((4, N_TOK, LANES), BF16),
        compiler_params=_params(("arbitrary", "arbitrary")),
        name="attn_lat_b",
    )(z, z, z, cache_k, cache_v)


N_GRID_ROWS = DEC_SEQ // GRID_W
NA_KEYS = NA_ROWS * GRID_W


NA_GROUP = 4
NA_MAX_KEY_ROWS = 12
BIAS_TABLE_ROWS = 16


def _window_start(r):
    return min(max(r - NA_ROWS // 2, 0), N_GRID_ROWS - NA_ROWS)


def _group_keys(g):
    starts = [_window_start(r) for r in range(g * NA_GROUP, (g + 1) * NA_GROUP)]
    n = max(starts) + NA_ROWS - min(starts)
    n += n % 2
    return min(min(starts), N_GRID_ROWS - n), n


def _build_window_bias(tab_ref, bias_scr):
    qcol = lax.broadcasted_iota(jnp.int32, (GRID_W, LANES), 0)
    lane = lax.broadcasted_iota(jnp.int32, (GRID_W, LANES), 1)
    kcol = lane & (GRID_W - 1)
    c0 = jnp.clip(qcol - NA_COLS // 2, 0, GRID_W - NA_COLS)
    in_win = (kcol >= c0) & (kcol < c0 + NA_COLS)
    masks = {(True, True): in_win, (True, False): in_win & (lane < GRID_W), (False, True): in_win & (lane >= GRID_W)}
    for head in range(2):
        for r in range(N_GRID_ROWS):
            u0, n = _group_keys(r // NA_GROUP)
            rows = pl.ds(r * GRID_W, GRID_W)
            for m in range(n // 2):
                key_rows = (u0 + 2 * m, u0 + 2 * m + 1)
                valid = tuple(_window_start(r) <= kr < _window_start(r) + NA_ROWS for kr in key_rows)
                cols = pl.ds(m * LANES, LANES)
                if not any(valid):
                    bias_scr[head, rows, cols] = jnp.full((GRID_W, LANES), NEG, F32)
                    continue
                vec = None
                for half, (kr, ok) in enumerate(zip(key_rows, valid)):
                    if ok:
                        dr = kr - r + NA_ROWS - 1
                        part = tab_ref[head, pl.ds(half * BIAS_TABLE_ROWS + dr, 1), :]
                        vec = part if vec is None else vec + part
                toep = pltpu.roll(jnp.broadcast_to(vec, (GRID_W, LANES)), 0, 1, stride=1, stride_axis=0)
                bias_scr[head, rows, cols] = jnp.where(masks[valid], toep * LOG2E, NEG)


def _attn_lat_c_kernel(q_ref, k_ref, v_ref, ck_ref, cv_ref, tab_ref, o_ref, bias_scr):
    @pl.when(pl.program_id(1) == 0)
    def _():
        _build_window_bias(tab_ref, bias_scr)

    ck = ck_ref[...].astype(BF16)
    cv = cv_ref[...].astype(BF16)
    n_rows = NA_GROUP * GRID_W
    for g in range(N_GRID_ROWS // NA_GROUP):
        u0, n = _group_keys(g)
        rows = pl.ds(g * n_rows, n_rows)
        keys = pl.ds(u0 * GRID_W, n * GRID_W)
        bias = jnp.concatenate([bias_scr[0, rows, 0:n * GRID_W], bias_scr[1, rows, 0:n * GRID_W]], axis=0)
        o = _attend(_stack_heads(q_ref[0, rows, :]), [k_ref[0, keys, :], ck], [v_ref[0, keys, :], cv],
                    biases=[bias, None])
        o_ref[0, rows, :] = _unstack_heads(o, n_rows).astype(BF16)


def _attn_lat_c(l, z, cache_k, cache_v, bias_rows):
    def zspec(g0):
        return pl.BlockSpec((1, DEC_SEQ, LANES), lambda c, b: (g0 + c, b, 0))

    cspec = pl.BlockSpec((None, None, PAST_LEN, LANES), lambda c, b: (b, l, 0, c))
    return pl.pallas_call(
        _attn_lat_c_kernel,
        grid=(HC // 2, DEC_BATCH),
        in_specs=[zspec(G_QC), zspec(G_KC), zspec(G_VC), cspec, cspec,
                  pl.BlockSpec((None, 2, 2 * BIAS_TABLE_ROWS, LANES), lambda c, b: (l, c, 0, 0))],
        out_specs=pl.BlockSpec((1, DEC_SEQ, LANES), lambda c, b: (c, b, 0)),
        out_shape=jax.ShapeDtypeStruct((4, N_TOK, LANES), BF16),
        scratch_shapes=[pltpu.VMEM((2, DEC_SEQ, NA_MAX_KEY_ROWS * GRID_W), F32)],
        compiler_params=_params(("arbitrary", "arbitrary")),
        name="attn_lat_c",
    )(z, z, z, cache_k, cache_v, bias_rows)


MERGE_TILE = 512


def _merge_kernel(x_ref, oa_ref, ob_ref, oc_ref, gate_ref, mod_ref, g2_ref, wa_ref, wb_ref, wc_ref, wo_ref,
                  xo_ref, h2_ref, wbr_scr, wo_scr):
    @pl.when(pl.program_id(0) == 0)
    def _():
        wbr_scr[0] = wa_ref[...].astype(BF16)
        wbr_scr[1] = wb_ref[...].astype(BF16)
        wbr_scr[2] = wc_ref[...].astype(BF16)
        wo_scr[...] = wo_ref[...].astype(BF16)

    m = mod_ref[0]
    for rc in range(MERGE_TILE // ROW_CHUNK):
        rows = pl.ds(rc * ROW_CHUNK, ROW_CHUNK)
        y = None
        for k, o_ref in enumerate((oa_ref, ob_ref, oc_ref)):
            o = jnp.concatenate([o_ref[c, rows, :] for c in range(4)], axis=-1)
            p = jnp.dot(o, wbr_scr[k], preferred_element_type=F32)
            g = jnp.concatenate([gate_ref[8 * k + c, rows, :] for c in range(8)], axis=-1).astype(F32)
            y = g * p if y is None else y + g * p
        out = jnp.dot(y.astype(BF16), wo_scr[...], preferred_element_type=F32)
        x = x_ref[rows, :] + m[:, 2 * D_MODEL:3 * D_MODEL] * out
        xo_ref[rows, :] = x
        ms = jnp.mean(x * x, axis=-1, keepdims=True)
        h2 = x * lax.rsqrt(ms + EPS) * g2_ref[...]
        h2_ref[rows, :] = (h2 * (1.0 + m[:, 4 * D_MODEL:5 * D_MODEL])
                           + m[:, 3 * D_MODEL:4 * D_MODEL]).astype(BF16)


def _merge(latent, l, x, oa, ob, oc, gates, mod, g_norm2, w_a, w_b, w_c, w_o):
    tm = MERGE_TILE
    if latent:
        mod_idx = lambda i: (1 + (i * tm) // DEC_SEQ, 0, 0)
    else:
        mod_idx = lambda i: (0, 0, 0)
    o_spec = pl.BlockSpec((4, tm, LANES), lambda i: (0, i, 0))
    wbr_spec = pl.BlockSpec((None, 4 * LANES, D_MODEL), lambda i: (l, 0, 0))
    return pl.pallas_call(
        _merge_kernel,
        grid=(N_TOK // tm,),
        in_specs=[pl.BlockSpec((tm, D_MODEL), lambda i: (i, 0)),
                  o_spec, o_spec, o_spec,
                  pl.BlockSpec((N_GATE_GROUPS, tm, LANES), lambda i: (0, i, 0)),
                  pl.BlockSpec((1, 1, 6 * D_MODEL), mod_idx),
                  pl.BlockSpec((None, 1, D_MODEL), lambda i: (l, 0, 0)),
                  wbr_spec, wbr_spec, wbr_spec,
                  pl.BlockSpec((None, D_MODEL, D_MODEL), lambda i: (l, 0, 0))],
        out_specs=[pl.BlockSpec((tm, D_MODEL), lambda i: (i, 0)),
                   pl.BlockSpec((tm, D_MODEL), lambda i: (i, 0))],
        out_shape=[jax.ShapeDtypeStruct((N_TOK, D_MODEL), F32),
                   jax.ShapeDtypeStruct((N_TOK, D_MODEL), BF16)],
        scratch_shapes=[pltpu.VMEM((3, 4 * LANES, D_MODEL), BF16), pltpu.VMEM((D_MODEL, D_MODEL), BF16)],
        compiler_params=_params(("arbitrary",)),
        name="merge_lat" if latent else "merge_ctx",
    )(x, oa, ob, oc, gates, mod, g_norm2, w_a, w_b, w_c, w_o)


FF_TILE = 1024


def _ffn_kernel(x_ref, h2_ref, mod_ref, w1_ref, w2_ref, gf_ref, xo_ref, acc_scr, *, final):
    f = pl.program_id(1)

    @pl.when(f == 0)
    def _():
        acc_scr[...] = jnp.zeros_like(acc_scr)

    w1 = w1_ref[...].astype(BF16)
    w2 = w2_ref[...].astype(BF16)
    for rc in range(ROW_TILE // ROW_CHUNK):
        rows = pl.ds(rc * ROW_CHUNK, ROW_CHUNK)
        u = jnp.dot(h2_ref[rows, :], w1, preferred_element_type=F32)
        u = jnp.square(jnp.maximum(u, 0.0)).astype(BF16)
        acc_scr[rows, :] += jnp.dot(u, w2, preferred_element_type=F32)

    @pl.when(f == D_FF // FF_TILE - 1)
    def _():
        x = x_ref[...] + mod_ref[0][:, 5 * D_MODEL:6 * D_MODEL] * acc_scr[...]
        if final:
            ms = jnp.mean(x * x, axis=-1, keepdims=True)
            x = x * lax.rsqrt(ms + EPS) * gf_ref[...]
        xo_ref[...] = x


def _ffn(latent, l, x, h2, mod, w1, w2, g_final):
    tm = ROW_TILE
    mod_idx = (lambda i, f: (1 + i, 0, 0)) if latent else (lambda i, f: (0, 0, 0))
    return pl.pallas_call(
        functools.partial(_ffn_kernel, final=(l == DEPTH - 1)),
        grid=(N_TOK // tm, D_FF // FF_TILE),
        in_specs=[pl.BlockSpec((tm, D_MODEL), lambda i, f: (i, 0)),
                  pl.BlockSpec((tm, D_MODEL), lambda i, f: (i, 0)),
                  pl.BlockSpec((1, 1, 6 * D_MODEL), mod_idx),
                  pl.BlockSpec((None, D_MODEL, FF_TILE), lambda i, f: (l, 0, f)),
                  pl.BlockSpec((None, FF_TILE, D_MODEL), lambda i, f: (l, f, 0)),
                  pl.BlockSpec((1, D_MODEL), lambda i, f: (0, 0))],
        out_specs=pl.BlockSpec((tm, D_MODEL), lambda i, f: (i, 0)),
        out_shape=jax.ShapeDtypeStruct((N_TOK, D_MODEL), F32),
        scratch_shapes=[pltpu.VMEM((tm, D_MODEL), F32)],
        compiler_params=_params(("arbitrary", "arbitrary")),
        name="ffn_lat" if latent else "ffn_ctx",
    )(x, h2, mod, w1, w2, g_final)


def _rope_tables():
    nf = HEAD_DIM // 4
    t = jnp.arange(DEC_SEQ)
    row = (t // GRID_W).astype(F32)
    col = (t % GRID_W).astype(F32)
    inv = ROPE_BASE ** (-jnp.arange(nf, dtype=F32) / nf)
    ar = row[:, None] * inv[None, :]
    ac = col[:, None] * inv[None, :]
    cos = jnp.concatenate([jnp.cos(ar), jnp.cos(ar), jnp.cos(ac), jnp.cos(ac)], axis=-1)
    sin = jnp.concatenate([-jnp.sin(ar), jnp.sin(ar), -jnp.sin(ac), jnp.sin(ac)], axis=-1)
    return jnp.tile(cos, (1, 2)), jnp.tile(sin, (1, 2))


def _packed_bias_rows(rel_bias):
    n = 2 * NA_ROWS - 1
    first = jnp.concatenate([rel_bias[..., NA_COLS - 1:], jnp.zeros((DEPTH, HC, n, LANES - (2 * NA_COLS - 1)), F32),
                             rel_bias[..., :NA_COLS - 1]], axis=-1)
    lo = GRID_W - NA_COLS + 1
    second = jnp.pad(rel_bias, ((0, 0), (0, 0), (0, 0), (lo, LANES - lo - (2 * NA_COLS - 1))))
    pad_rows = ((0, 0), (0, 0), (0, BIAS_TABLE_ROWS - n), (0, 0))
    return jnp.concatenate([jnp.pad(first, pad_rows), jnp.pad(second, pad_rows)], axis=2)


def kernel(x_prompt, x_sample, c, cache_a_k, cache_a_v, cache_b_k, cache_b_v, cache_c_k, cache_c_v, c_ctx, w_mod, b_mod, g_norm1, g_norm2, w_in, b_gate, lambda_qk, g_subln, g_qnorm, g_knorm, rel_bias, w_branch_a, w_branch_b, w_branch_c, w_out, w_ff1, w_ff2, g_final):
    xp = x_prompt.reshape(N_TOK, D_MODEL)
    xs = x_sample.reshape(N_TOK, D_MODEL)

    cvec = jnp.concatenate([c_ctx[None, :], c, jnp.zeros((3, D_MODEL), F32)], axis=0)
    mods = _modulation(cvec, w_mod, b_mod)

    rope_cos, rope_sin = _rope_tables()
    bias_rows = _packed_bias_rows(rel_bias)
    bd = jnp.kron(jnp.eye(2 * LANES // HEAD_DIM, dtype=F32),
                  jnp.full((HEAD_DIM, HEAD_DIM), 1.0 / HEAD_DIM, F32)).astype(BF16)
    gq = jnp.tile(g_qnorm, (1, 2 * LANES // DB)).reshape(DEPTH, 1, 2 * LANES)
    gk = jnp.tile(g_knorm, (1, LANES // DB)).reshape(DEPTH, 1, LANES)
    g1 = g_norm1.reshape(DEPTH, 1, D_MODEL)
    g2 = g_norm2.reshape(DEPTH, 1, D_MODEL)
    bg = b_gate.reshape(DEPTH, 1, 3 * D_MODEL)
    gsub = g_subln.reshape(DEPTH, 1, 2 * DA)
    gf = g_final.reshape(1, D_MODEL)

    lat_ak = cache_a_k.reshape(DEC_BATCH, DEPTH, PAST_LEN, 2 * HA * DA)
    lat_av = cache_a_v.reshape(DEC_BATCH, DEPTH, PAST_LEN, HA * 2 * DA)
    lat_bk = cache_b_k.reshape(DEC_BATCH, DEPTH, PAST_LEN, KVB * DB)
    lat_bv = cache_b_v.reshape(DEC_BATCH, DEPTH, PAST_LEN, KVB * DB)
    lat_ck = cache_c_k.reshape(DEC_BATCH, DEPTH, PAST_LEN, HC * DC)
    lat_cv = cache_c_v.reshape(DEC_BATCH, DEPTH, PAST_LEN, HC * DC)

    new_caches = None

    for l in range(DEPTH):
        mod = mods[l].reshape(8, 1, 6 * D_MODEL)

        outs = _inproj(False, l, xp, mod, g1, w_in, bg, gq, gk, bd, None, None, new_caches)
        z, gates, new_caches = outs[0], outs[1], list(outs[2:])
        oa, ob, oc = _attn_ctx(l, z, lambda_qk, gsub)
        xp, h2 = _merge(False, l, xp, oa, ob, oc, gates, mod, g2, w_branch_a, w_branch_b, w_branch_c, w_out)
        xp = _ffn(False, l, xp, h2, mod, w_ff1, w_ff2, gf)

        z, gates = _inproj(True, l, xs, mod, g1, w_in, bg, gq, gk, bd, rope_cos, rope_sin, None)
        oa = _attn_lat_a(l, z, lat_ak, lat_av, lambda_qk, gsub)
        ob = _attn_lat_b(l, z, lat_bk, lat_bv)
        oc = _attn_lat_c(l, z, lat_ck, lat_cv, bias_rows)
        xs, h2 = _merge(True, l, xs, oa, ob, oc, gates, mod, g2, w_branch_a, w_branch_b, w_branch_c, w_out)
        xs = _ffn(True, l, xs, h2, mod, w_ff1, w_ff2, gf)

    y_prompt = xp.reshape(BATCH, SEQ, D_MODEL)
    y_sample = xs.reshape(DEC_BATCH, DEC_SEQ, D_MODEL)
    ak, av, bk, bv, ck, cv = new_caches
    return (y_prompt, y_sample,
            ak.reshape(BATCH, DEPTH, SEQ, 2 * HA, DA), av.reshape(BATCH, DEPTH, SEQ, HA, 2 * DA),
            bk.reshape(BATCH, DEPTH, SEQ, KVB, DB), bv.reshape(BATCH, DEPTH, SEQ, KVB, DB),
            ck.reshape(BATCH, DEPTH, SEQ, HC, DC), cv.reshape(BATCH, DEPTH, SEQ, HC, DC))
```

```python
import functools
import math

import jax
import jax.numpy as jnp
from jax import lax
from jax.experimental import pallas as pl
from jax.experimental.pallas import tpu as pltpu

D_MODEL = 1024
BATCH = 16
SEQ = 256
DEPTH = 4
DEC_BATCH = 4
DEC_SEQ = 1024
PAST_LEN = 256
GRID_W = 64
HA, DA = 4, 64
HB, KVB, DB = 8, 2, 64
HC, DC = 8, 64
NA_ROWS, NA_COLS = 8, 16
D_FF = 4 * D_MODEL
ROPE_BASE = 10000.0
EPS = 1e-6
NEG = -1e30
HEAD_DIM = 64
LOG2E = math.log2(math.e)
Q_SCALE = HEAD_DIM ** -0.5 * LOG2E

F32 = jnp.float32
BF16 = jnp.bfloat16

LANES = 128
N_QKV_GROUPS = 30
N_GATE_GROUPS = 24
COL_TILE = 768
GROUPS_PER_TILE = COL_TILE // LANES
N_QKV_TILES = N_QKV_GROUPS // GROUPS_PER_TILE
N_COL_TILES = N_QKV_TILES + N_GATE_GROUPS // GROUPS_PER_TILE
ROW_TILE = 1024
ROW_CHUNK = 256
N_TOK = BATCH * SEQ
VMEM_LIMIT_V7X = 58 * 1024 * 1024

G_QA, G_KA, G_VA, G_QB, G_KB, G_VB, G_QC, G_KC, G_VC = 0, 4, 8, 12, 16, 17, 18, 22, 26
CACHE_WIDTHS = (2 * HA * DA, HA * 2 * DA, KVB * DB, KVB * DB, HC * DC, HC * DC)

_NT = (((1,), (1,)), ((), ()))


def _params(sem, vmem=VMEM_LIMIT_V7X):
    return pltpu.CompilerParams(dimension_semantics=sem, vmem_limit_bytes=vmem)


def _lane_head(shape):
    return lax.shift_right_logical(lax.broadcasted_iota(jnp.int32, shape, len(shape) - 1), 6)


def _mod_kernel(c_ref, w_ref, b_ref, o_ref):
    c = c_ref[...]
    s = (c * jax.nn.sigmoid(c)).astype(BF16)
    o_ref[...] = jnp.dot(s, w_ref[...].astype(BF16), preferred_element_type=F32) + b_ref[...]


def _modulation(cvec, w_mod, b_mod):
    tn = 1536
    n6 = 6 * D_MODEL
    return pl.pallas_call(
        _mod_kernel,
        grid=(DEPTH, n6 // tn),
        in_specs=[pl.BlockSpec((8, D_MODEL), lambda l, n: (0, 0)),
                  pl.BlockSpec((None, D_MODEL, tn), lambda l, n: (l, 0, n)),
                  pl.BlockSpec((None, 1, tn), lambda l, n: (l, 0, n))],
        out_specs=pl.BlockSpec((None, 8, tn), lambda l, n: (l, 0, n)),
        out_shape=jax.ShapeDtypeStruct((DEPTH, 8, n6), F32),
        compiler_params=_params(("arbitrary", "arbitrary")),
        name="modulation",
    )(cvec, w_mod, b_mod.reshape(DEPTH, 1, n6))


def _rope(v, cos, sin):
    first = (lax.broadcasted_iota(jnp.int32, v.shape, 1) & 16) == 0
    partner = jnp.where(first, pltpu.roll(v, LANES - 16, 1), pltpu.roll(v, 16, 1))
    return v * cos + partner * sin


def _head_rmsnorm(v, bd, g):
    msq = jnp.dot((v * v).astype(BF16), bd, preferred_element_type=F32)
    return v * lax.rsqrt(msq + EPS) * g


def _make_inproj_kernel(latent, n_aliased):
    def kern(*refs):
        if latent:
            (x_ref, mod_ref, g1_ref, w_ref, bg_ref, gq_ref, gk_ref, bd_ref, cos_ref, sin_ref,
             z_ref, gate_ref, h_scr, wbf_scr) = refs
        else:
            (x_ref, mod_ref, g1_ref, w_ref, bg_ref, gq_ref, gk_ref, bd_ref) = refs[:8]
            (z_ref, gate_ref, ak_ref, av_ref, bk_ref, bv_ref, ck_ref, cv_ref,
             h_scr, wbf_scr) = refs[8 + n_aliased:]
        j = pl.program_id(0)
        i = pl.program_id(1)

        @pl.when(i == 0)
        def _():
            wbf_scr[...] = w_ref[...].astype(BF16)

        def chunks(first=False):
            for rc in range(ROW_TILE // ROW_CHUNK):
                rows = pl.ds(rc * ROW_CHUNK, ROW_CHUNK)
                tok = pl.ds(pl.multiple_of(i * ROW_TILE + rc * ROW_CHUNK, ROW_CHUNK), ROW_CHUNK)
                if first:
                    x = x_ref[rows, :]
                    ms = jnp.mean(x * x, axis=-1, keepdims=True)
                    y = x * lax.rsqrt(ms + EPS) * g1_ref[...]
                    m = mod_ref[0]
                    h = (y * (1.0 + m[:, D_MODEL:2 * D_MODEL]) + m[:, 0:D_MODEL]).astype(BF16)
                    h_scr[tok, :] = h
                else:
                    h = h_scr[tok, :]
                yield rc, rows, jnp.dot(h, wbf_scr[...], preferred_element_type=F32)

        def grp(a, c, n=1):
            return a[:, c * LANES:(c + n) * LANES]

        def rot(v, rows):
            return _rope(v, cos_ref[rows, :], sin_ref[rows, :]) if latent else v

        @pl.when(j == 0)
        def _():
            for rc, rows, a in chunks(first=True):
                for c in range(GROUPS_PER_TILE):
                    v = rot(grp(a, c), rows)
                    z_ref[c, rows, :] = (v * Q_SCALE if c < 4 else v).astype(BF16)
                if not latent:
                    ak_ref[rc] = grp(a, 4, 2)

        @pl.when(j == 1)
        def _():
            for rc, rows, a in chunks():
                for c in range(2):
                    z_ref[c, rows, :] = rot(grp(a, c), rows).astype(BF16)
                for c in range(2, GROUPS_PER_TILE):
                    z_ref[c, rows, :] = grp(a, c).astype(BF16)
                if not latent:
                    ak_ref[rc] = grp(a, 0, 2)
                    av_ref[rc] = grp(a, 2, 4)

        @pl.when(j == 2)
        def _():
            bd = bd_ref[...]
            for rc, rows, a in chunks():
                for half in range(2):
                    qn = _head_rmsnorm(grp(a, 2 * half, 2), bd, gq_ref[...])
                    for c in range(2):
                        z_ref[2 * half + c, rows, :] = (rot(grp(qn, c), rows) * Q_SCALE).astype(BF16)
                kn = _head_rmsnorm(grp(a, 4), bd[0:LANES, 0:LANES], gk_ref[...])
                z_ref[4, rows, :] = rot(kn, rows).astype(BF16)
                z_ref[5, rows, :] = grp(a, 5).astype(BF16)
                if not latent:
                    bk_ref[rc] = kn
                    bv_ref[rc] = grp(a, 5)

        @pl.when(j == 3)
        def _():
            for rc, rows, a in chunks():
                for c in range(GROUPS_PER_TILE):
                    v = grp(a, c)
                    z_ref[c, rows, :] = (v * Q_SCALE if c < 4 else v).astype(BF16)
                if not latent:
                    ck_ref[rc] = grp(a, 4, 2)

        @pl.when(j == 4)
        def _():
            for rc, rows, a in chunks():
                for c in range(GROUPS_PER_TILE):
                    z_ref[c, rows, :] = grp(a, c).astype(BF16)
                if not latent:
                    ck_ref[rc] = grp(a, 0, 2)
                    cv_ref[rc] = grp(a, 2, 4)

        @pl.when(j >= N_QKV_TILES)
        def _():
            for rc, rows, a in chunks():
                a = a + bg_ref[...]
                for c in range(GROUPS_PER_TILE):
                    gate_ref[c, rows, :] = jax.nn.sigmoid(grp(a, c)).astype(BF16)

    return kern


def _inproj(latent, l, x, mod, g_norm1, w_in, b_gate, gq, gk, bd, rope_cos, rope_sin, caches):
    n_row = N_TOK // ROW_TILE
    last = n_row - 1

    def row_block(j, i, first_tile, last_tile):
        return jnp.where(j < first_tile, 0, jnp.where(j <= last_tile, i, last))

    mod_idx = (lambda j, i: (1 + i, 0, 0)) if latent else (lambda j, i: (0, 0, 0))
    in_specs = [
        pl.BlockSpec((ROW_TILE, D_MODEL), lambda j, i: (row_block(j, i, 0, 0), 0)),
        pl.BlockSpec((1, 1, 6 * D_MODEL), mod_idx),
        pl.BlockSpec((None, 1, D_MODEL), lambda j, i: (l, 0, 0)),
        pl.BlockSpec((None, D_MODEL, COL_TILE), lambda j, i: (l, 0, j)),
        pl.BlockSpec((None, 1, COL_TILE), lambda j, i: (l, 0, jnp.maximum(j - N_QKV_TILES, 0))),
        pl.BlockSpec((None, 1, 2 * LANES), lambda j, i: (l, 0, 0)),
        pl.BlockSpec((None, 1, LANES), lambda j, i: (l, 0, 0)),
        pl.BlockSpec((2 * LANES, 2 * LANES), lambda j, i: (0, 0)),
    ]
    args = [x, mod, g_norm1, w_in, b_gate, gq, gk, bd]
    out_specs = [
        pl.BlockSpec((GROUPS_PER_TILE, ROW_TILE, LANES),
                     lambda j, i: (jnp.minimum(j, N_QKV_TILES - 1), row_block(j, i, 0, N_QKV_TILES - 1), 0)),
        pl.BlockSpec((GROUPS_PER_TILE, ROW_TILE, LANES),
                     lambda j, i: (jnp.maximum(j - N_QKV_TILES, 0), row_block(j, i, N_QKV_TILES, N_COL_TILES), 0)),
    ]
    out_shape = [jax.ShapeDtypeStruct((N_QKV_GROUPS, N_TOK, LANES), BF16),
                 jax.ShapeDtypeStruct((N_GATE_GROUPS, N_TOK, LANES), BF16)]
    aliases = {}
    if latent:
        in_specs += [pl.BlockSpec((DEC_SEQ, LANES), lambda j, i: (0, 0))] * 2
        args += [rope_cos, rope_sin]
    else:
        nb = ROW_TILE // SEQ
        tiles = ((0, 1, 256), (1, 1, 512), (2, 2, 128), (2, 2, 128), (3, 4, 256), (4, 4, 512))
        for k, (w, (t0, t1, bw)) in enumerate(zip(CACHE_WIDTHS, tiles)):
            if caches is not None:
                in_specs.append(pl.BlockSpec(memory_space=pl.ANY))
                args.append(caches[k])
                aliases[8 + k] = 2 + k

            def cache_idx(j, i, t0=t0, t1=t1):
                return (row_block(j, i, t0, t1), l, 0, jnp.where(j <= t0, 0, (t1 - t0)))

            out_specs.append(pl.BlockSpec((nb, None, SEQ, bw), cache_idx))
            out_shape.append(jax.ShapeDtypeStruct((BATCH, DEPTH, SEQ, w), F32))
    return pl.pallas_call(
        _make_inproj_kernel(latent, len(aliases)),
        grid=(N_COL_TILES, n_row),
        in_specs=in_specs,
        out_specs=out_specs,
        out_shape=out_shape,
        scratch_shapes=[pltpu.VMEM((N_TOK, D_MODEL), BF16), pltpu.VMEM((D_MODEL, COL_TILE), BF16)],
        input_output_aliases=aliases,
        compiler_params=_params(("arbitrary", "arbitrary")),
        name="inproj_lat" if latent else "inproj_ctx",
    )(*args)


def _mask_head(q, head):
    qf = q.astype(F32)
    keep = _lane_head(qf.shape) == head
    return jnp.where(keep, qf, 0.0).astype(BF16)


def _stack_heads(q):
    return jnp.concatenate([_mask_head(q, 0), _mask_head(q, 1)], axis=0)


def _unstack_heads(o, rows):
    return jnp.where(_lane_head((rows, LANES)) == 1, o[rows:2 * rows], o[0:rows])


def _dup_head(kv, head):
    f = kv.astype(F32)
    r = pltpu.roll(f, HEAD_DIM, 1)
    return jnp.where(_lane_head(f.shape) == head, f, r).astype(BF16)


def _attend(qm, ks, vs, biases=None):
    ss = [lax.dot_general(qm, k, _NT, preferred_element_type=F32) for k in ks]
    if biases is not None:
        ss = [s if b is None else s + b for s, b in zip(ss, biases)]
    m = functools.reduce(jnp.maximum, [jnp.max(s, axis=-1, keepdims=True) for s in ss])
    ps = [jnp.exp2(s - m) for s in ss]
    den = functools.reduce(jnp.add, [jnp.sum(p, axis=-1, keepdims=True) for p in ps])
    o = functools.reduce(jnp.add, [jnp.dot(p.astype(BF16), v, preferred_element_type=F32)
                                    for p, v in zip(ps, vs)])
    return o * (1.0 / den)


def _diff_lambda(lam_ref, lam_init):
    lq = lam_ref[...]
    a = jnp.sum(lq[0:1] * lq[1:2], axis=-1, keepdims=True)
    b = jnp.sum(lq[2:3] * lq[3:4], axis=-1, keepdims=True)
    return jnp.exp(a) - jnp.exp(b) + lam_init


def _diff_combine(o1, o2, lam, gsub, lam_init):
    o = o1 - lam * o2
    ms = jnp.mean(o * o, axis=-1, keepdims=True)
    return (o * lax.rsqrt(ms + EPS) * gsub) * (1.0 - lam_init)


CTX_BATCH_PER_STEP = 2


def _make_attn_ctx_kernel(lam_init):
    def kern(z_ref, lam_ref, gsub_ref, oa_ref, ob_ref, oc_ref):
        lam = _diff_lambda(lam_ref, lam_init)
        gsub = gsub_ref[...]
        for bb in range(CTX_BATCH_PER_STEP):
            rows = pl.ds(bb * SEQ, SEQ)

            def z(g):
                return z_ref[g, rows, :]

            for vh in range(HA):
                hi = vh % 2
                v = z(G_VA + vh)
                o1 = _attend(_mask_head(z(G_QA + vh // 2), hi), [z(G_KA + vh // 2)], [v])
                o2 = _attend(_mask_head(z(G_QA + 2 + vh // 2), hi), [z(G_KA + 2 + vh // 2)], [v])
                oa_ref[vh, rows, :] = _diff_combine(o1, o2, lam, gsub, lam_init).astype(BF16)
            for g in range(KVB):
                kd = _dup_head(z(G_KB), g)
                vd = _dup_head(z(G_VB), g)
                for c in range(2 * g, 2 * g + 2):
                    o = _attend(_stack_heads(z(G_QB + c)), [kd], [vd])
                    ob_ref[c, rows, :] = _unstack_heads(o, SEQ).astype(BF16)
            for c in range(HC // 2):
                o = _attend(_stack_heads(z(G_QC + c)), [z(G_KC + c)], [z(G_VC + c)])
                oc_ref[c, rows, :] = _unstack_heads(o, SEQ).astype(BF16)
    return kern


def _attn_ctx(l, z, lambda_qk, g_subln):
    lam_init = 0.8 - 0.6 * math.exp(-0.3 * l)
    rows = CTX_BATCH_PER_STEP * SEQ
    o_spec = pl.BlockSpec((4, rows, LANES), lambda b: (0, b, 0))
    o_shape = jax.ShapeDtypeStruct((4, N_TOK, LANES), BF16)
    return pl.pallas_call(
        _make_attn_ctx_kernel(lam_init),
        grid=(BATCH // CTX_BATCH_PER_STEP,),
        in_specs=[pl.BlockSpec((N_QKV_GROUPS, rows, LANES), lambda b: (0, b, 0)),
                  pl.BlockSpec((None, 4, DA), lambda b: (l, 0, 0)),
                  pl.BlockSpec((None, 1, 2 * DA), lambda b: (l, 0, 0))],
        out_specs=[o_spec, o_spec, o_spec],
        out_shape=[o_shape, o_shape, o_shape],
        compiler_params=_params(("arbitrary",)),
        name="attn_ctx",
    )(z, lambda_qk, g_subln)


Q_BLK_GQA = 256
Q_BLK_DIFF = 512


def _make_attn_lat_a_kernel(lam_init):
    def kern(q1_ref, q2_ref, k1_ref, k2_ref, v_ref, ck1_ref, ck2_ref, cv_ref, lam_ref, gsub_ref, o_ref):
        hi = jnp.bitwise_and(pl.program_id(1), 1)
        lam = _diff_lambda(lam_ref, lam_init)
        gsub = gsub_ref[...]
        ck1 = ck1_ref[...].astype(BF16)
        ck2 = ck2_ref[...].astype(BF16)
        cv = cv_ref[...].astype(BF16)
        k1, k2, v = k1_ref[0], k2_ref[0], v_ref[0]
        for qb in range(DEC_SEQ // Q_BLK_DIFF):
            rows = pl.ds(qb * Q_BLK_DIFF, Q_BLK_DIFF)
            o1 = _attend(_mask_head(q1_ref[0, rows, :], hi), [k1, ck1], [v, cv])
            o2 = _attend(_mask_head(q2_ref[0, rows, :], hi), [k2, ck2], [v, cv])
            o_ref[0, rows, :] = _diff_combine(o1, o2, lam, gsub, lam_init).astype(BF16)
    return kern


def _attn_lat_a(l, z, cache_k, cache_v, lambda_qk, g_subln):
    lam_init = 0.8 - 0.6 * math.exp(-0.3 * l)

    def zspec(fn):
        return pl.BlockSpec((1, DEC_SEQ, LANES), lambda b, h: (fn(h), b, 0))

    def cspec(fn):
        return pl.BlockSpec((None, None, PAST_LEN, LANES), lambda b, h: (b, l, 0, fn(h)))

    return pl.pallas_call(
        _make_attn_lat_a_kernel(lam_init),
        grid=(DEC_BATCH, HA),
        in_specs=[zspec(lambda h: G_QA + h // 2), zspec(lambda h: G_QA + 2 + h // 2),
                  zspec(lambda h: G_KA + h // 2), zspec(lambda h: G_KA + 2 + h // 2),
                  zspec(lambda h: G_VA + h),
                  cspec(lambda h: h // 2), cspec(lambda h: 2 + h // 2), cspec(lambda h: h),
                  pl.BlockSpec((None, 4, DA), lambda b, h: (l, 0, 0)),
                  pl.BlockSpec((None, 1, 2 * DA), lambda b, h: (l, 0, 0))],
        out_specs=pl.BlockSpec((1, DEC_SEQ, LANES), lambda b, h: (h, b, 0)),
        out_shape=jax.ShapeDtypeStruct((4, N_TOK, LANES), BF16),
        compiler_params=_params(("arbitrary", "arbitrary")),
        name="attn_lat_a",
    )(z, z, z, z, z, cache_k, cache_k, cache_v, lambda_qk, g_subln)


def _attn_lat_b_kernel(q_ref, k_ref, v_ref, ck_ref, cv_ref, o_ref):
    hi = lax.shift_right_logical(pl.program_id(1), 1)
    kd = _dup_head(k_ref[0], hi)
    vd = _dup_head(v_ref[0], hi)
    ckd = _dup_head(ck_ref[...].astype(BF16), hi)
    cvd = _dup_head(cv_ref[...].astype(BF16), hi)
    for qb in range(DEC_SEQ // Q_BLK_GQA):
        rows = pl.ds(qb * Q_BLK_GQA, Q_BLK_GQA)
        o = _attend(_stack_heads(q_ref[0, rows, :]), [kd, ckd], [vd, cvd])
        o_ref[0, rows, :] = _unstack_heads(o, Q_BLK_GQA).astype(BF16)


def _attn_lat_b(l, z, cache_k, cache_v):
    cspec = pl.BlockSpec((None, None, PAST_LEN, LANES), lambda b, c: (b, l, 0, 0))
    return pl.pallas_call(
        _attn_lat_b_kernel,
        grid=(DEC_BATCH, HB // 2),
        in_specs=[pl.BlockSpec((1, DEC_SEQ, LANES), lambda b, c: (G_QB + c, b, 0)),
                  pl.BlockSpec((1, DEC_SEQ, LANES), lambda b, c: (G_KB, b, 0)),
                  pl.BlockSpec((1, DEC_SEQ, LANES), lambda b, c: (G_VB, b, 0)),
                  cspec, cspec],
        out_specs=pl.BlockSpec((1, DEC_SEQ, LANES), lambda b, c: (c, b, 0)),
        out_shape=jax.ShapeDtypeStruct((4, N_TOK, LANES), BF16),
        compiler_params=_params(("arbitrary", "arbitrary")),
        name="attn_lat_b",
    )(z, z, z, cache_k, cache_v)


N_GRID_ROWS = DEC_SEQ // GRID_W
NA_KEYS = NA_ROWS * GRID_W


NA_GROUP = 4
NA_MAX_KEY_ROWS = 12
BIAS_TABLE_ROWS = 16


def _window_start(r):
    return min(max(r - NA_ROWS // 2, 0), N_GRID_ROWS - NA_ROWS)


def _group_keys(g):
    starts = [_window_start(r) for r in range(g * NA_GROUP, (g + 1) * NA_GROUP)]
    n = max(starts) + NA_ROWS - min(starts)
    n += n % 2
    return min(min(starts), N_GRID_ROWS - n), n


def _build_window_bias(tab_ref, bias_scr):
    qcol = lax.broadcasted_iota(jnp.int32, (GRID_W, LANES), 0)
    lane = lax.broadcasted_iota(jnp.int32, (GRID_W, LANES), 1)
    kcol = lane & (GRID_W - 1)
    c0 = jnp.clip(qcol - NA_COLS // 2, 0, GRID_W - NA_COLS)
    in_win = (kcol >= c0) & (kcol < c0 + NA_COLS)
    masks = {(True, True): in_win, (True, False): in_win & (lane < GRID_W), (False, True): in_win & (lane >= GRID_W)}
    for head in range(2):
        for r in range(N_GRID_ROWS):
            u0, n = _group_keys(r // NA_GROUP)
            rows = pl.ds(r * GRID_W, GRID_W)
            for m in range(n // 2):
                key_rows = (u0 + 2 * m, u0 + 2 * m + 1)
                valid = tuple(_window_start(r) <= kr < _window_start(r) + NA_ROWS for kr in key_rows)
                cols = pl.ds(m * LANES, LANES)
                if not any(valid):
                    bias_scr[head, rows, cols] = jnp.full((GRID_W, LANES), NEG, F32)
                    continue
                vec = None
                for half, (kr, ok) in enumerate(zip(key_rows, valid)):
                    if ok:
                        dr = kr - r + NA_ROWS - 1
                        part = tab_ref[head, pl.ds(half * BIAS_TABLE_ROWS + dr, 1), :]
                        vec = part if vec is None else vec + part
                toep = pltpu.roll(jnp.broadcast_to(vec, (GRID_W, LANES)), 0, 1, stride=1, stride_axis=0)
                bias_scr[head, rows, cols] = jnp.where(masks[valid], toep * LOG2E, NEG)


def _attn_lat_c_kernel(q_ref, k_ref, v_ref, ck_ref, cv_ref, tab_ref, o_ref, bias_scr):
    @pl.when(pl.program_id(1) == 0)
    def _():
        _build_window_bias(tab_ref, bias_scr)

    ck = ck_ref[...].astype(BF16)
    cv = cv_ref[...].astype(BF16)
    n_rows = NA_GROUP * GRID_W
    for g in range(N_GRID_ROWS // NA_GROUP):
        u0, n = _group_keys(g)
        rows = pl.ds(g * n_rows, n_rows)
        keys = pl.ds(u0 * GRID_W, n * GRID_W)
        bias = jnp.concatenate([bias_scr[0, rows, 0:n * GRID_W], bias_scr[1, rows, 0:n * GRID_W]], axis=0)
        o = _attend(_stack_heads(q_ref[0, rows, :]), [k_ref[0, keys, :], ck], [v_ref[0, keys, :], cv],
                    biases=[bias, None])
        o_ref[0, rows, :] = _unstack_heads(o, n_rows).astype(BF16)


def _attn_lat_c(l, z, cache_k, cache_v, bias_rows):
    def zspec(g0):
        return pl.BlockSpec((1, DEC_SEQ, LANES), lambda c, b: (g0 + c, b, 0))

    cspec = pl.BlockSpec((None, None, PAST_LEN, LANES), lambda c, b: (b, l, 0, c))
    return pl.pallas_call(
        _attn_lat_c_kernel,
        grid=(HC // 2, DEC_BATCH),
        in_specs=[zspec(G_QC), zspec(G_KC), zspec(G_VC), cspec, cspec,
                  pl.BlockSpec((None, 2, 2 * BIAS_TABLE_ROWS, LANES), lambda c, b: (l, c, 0, 0))],
        out_specs=pl.BlockSpec((1, DEC_SEQ, LANES), lambda c, b: (c, b, 0)),
        out_shape=jax.ShapeDtypeStruct((4, N_TOK, LANES), BF16),
        scratch_shapes=[pltpu.VMEM((2, DEC_SEQ, NA_MAX_KEY_ROWS * GRID_W), F32)],
        compiler_params=_params(("arbitrary", "arbitrary")),
        name="attn_lat_c",
    )(z, z, z, cache_k, cache_v, bias_rows)


MERGE_TILE = 512


def _merge_kernel(x_ref, oa_ref, ob_ref, oc_ref, gate_ref, mod_ref, wa_ref, wb_ref, wc_ref, wo_ref,
                  xo_ref, wbr_scr, wo_scr):
    @pl.when(pl.program_id(0) == 0)
    def _():
        wbr_scr[0] = wa_ref[...].astype(BF16)
        wbr_scr[1] = wb_ref[...].astype(BF16)
        wbr_scr[2] = wc_ref[...].astype(BF16)
        wo_scr[...] = wo_ref[...].astype(BF16)

    m = mod_ref[0]
    for rc in range(MERGE_TILE // ROW_CHUNK):
        rows = pl.ds(rc * ROW_CHUNK, ROW_CHUNK)
        y = None
        for k, o_ref in enumerate((oa_ref, ob_ref, oc_ref)):
            o = jnp.concatenate([o_ref[c, rows, :] for c in range(4)], axis=-1)
            p = jnp.dot(o, wbr_scr[k], preferred_element_type=F32)
            g = jnp.concatenate([gate_ref[8 * k + c, rows, :] for c in range(8)], axis=-1).astype(F32)
            y = g * p if y is None else y + g * p
        out = jnp.dot(y.astype(BF16), wo_scr[...], preferred_element_type=F32)
        xo_ref[rows, :] = x_ref[rows, :] + m[:, 2 * D_MODEL:3 * D_MODEL] * out


def _merge(latent, l, x, oa, ob, oc, gates, mod, w_a, w_b, w_c, w_o):
    tm = MERGE_TILE
    if latent:
        mod_idx = lambda i: (1 + (i * tm) // DEC_SEQ, 0, 0)
    else:
        mod_idx = lambda i: (0, 0, 0)
    o_spec = pl.BlockSpec((4, tm, LANES), lambda i: (0, i, 0))
    wbr_spec = pl.BlockSpec((None, 4 * LANES, D_MODEL), lambda i: (l, 0, 0))
    return pl.pallas_call(
        _merge_kernel,
        grid=(N_TOK // tm,),
        in_specs=[pl.BlockSpec((tm, D_MODEL), lambda i: (i, 0)),
                  o_spec, o_spec, o_spec,
                  pl.BlockSpec((N_GATE_GROUPS, tm, LANES), lambda i: (0, i, 0)),
                  pl.BlockSpec((1, 1, 6 * D_MODEL), mod_idx),
                  wbr_spec, wbr_spec, wbr_spec,
                  pl.BlockSpec((None, D_MODEL, D_MODEL), lambda i: (l, 0, 0))],
        out_specs=pl.BlockSpec((tm, D_MODEL), lambda i: (i, 0)),
        out_shape=jax.ShapeDtypeStruct((N_TOK, D_MODEL), F32),
        scratch_shapes=[pltpu.VMEM((3, 4 * LANES, D_MODEL), BF16), pltpu.VMEM((D_MODEL, D_MODEL), BF16)],
        compiler_params=_params(("arbitrary",)),
        name="merge_lat" if latent else "merge_ctx",
    )(x, oa, ob, oc, gates, mod, w_a, w_b, w_c, w_o)


FF_TILE = 1024


def _ffn_kernel(x_ref, mod_ref, g2_ref, w1_ref, w2_ref, gf_ref, xo_ref, acc_scr, h2_scr, *, final):
    f = pl.program_id(1)

    def run(first):
        w1 = w1_ref[...].astype(BF16)
        w2 = w2_ref[...].astype(BF16)
        m = mod_ref[0]
        for rc in range(ROW_TILE // ROW_CHUNK):
            rows = pl.ds(rc * ROW_CHUNK, ROW_CHUNK)
            if first:
                x = x_ref[rows, :]
                ms = jnp.mean(x * x, axis=-1, keepdims=True)
                y = x * lax.rsqrt(ms + EPS) * g2_ref[...]
                h2 = (y * (1.0 + m[:, 4 * D_MODEL:5 * D_MODEL]) + m[:, 3 * D_MODEL:4 * D_MODEL]).astype(BF16)
                h2_scr[rows, :] = h2
            else:
                h2 = h2_scr[rows, :]
            u = jnp.dot(h2, w1, preferred_element_type=F32)
            u = jnp.square(jnp.maximum(u, 0.0)).astype(BF16)
            d = jnp.dot(u, w2, preferred_element_type=F32)
            if first:
                acc_scr[rows, :] = d
            else:
                acc_scr[rows, :] += d

    @pl.when(f == 0)
    def _():
        run(True)

    @pl.when(f > 0)
    def _():
        run(False)

    @pl.when(f == D_FF // FF_TILE - 1)
    def _():
        x = x_ref[...] + mod_ref[0][:, 5 * D_MODEL:6 * D_MODEL] * acc_scr[...]
        if final:
            ms = jnp.mean(x * x, axis=-1, keepdims=True)
            x = x * lax.rsqrt(ms + EPS) * gf_ref[...]
        xo_ref[...] = x


def _ffn(latent, l, x, mod, g_norm2, w1, w2, g_final):
    tm = ROW_TILE
    mod_idx = (lambda i, f: (1 + i, 0, 0)) if latent else (lambda i, f: (0, 0, 0))
    return pl.pallas_call(
        functools.partial(_ffn_kernel, final=(l == DEPTH - 1)),
        grid=(N_TOK // tm, D_FF // FF_TILE),
        in_specs=[pl.BlockSpec((tm, D_MODEL), lambda i, f: (i, 0)),
                  pl.BlockSpec((1, 1, 6 * D_MODEL), mod_idx),
                  pl.BlockSpec((None, 1, D_MODEL), lambda i, f: (l, 0, 0)),
                  pl.BlockSpec((None, D_MODEL, FF_TILE), lambda i, f: (l, 0, f)),
                  pl.BlockSpec((None, FF_TILE, D_MODEL), lambda i, f: (l, f, 0)),
                  pl.BlockSpec((1, D_MODEL), lambda i, f: (0, 0))],
        out_specs=pl.BlockSpec((tm, D_MODEL), lambda i, f: (i, 0)),
        out_shape=jax.ShapeDtypeStruct((N_TOK, D_MODEL), F32),
        scratch_shapes=[pltpu.VMEM((tm, D_MODEL), F32), pltpu.VMEM((tm, D_MODEL), BF16)],
        compiler_params=_params(("arbitrary", "arbitrary")),
        name="ffn_lat" if latent else "ffn_ctx",
    )(x, mod, g_norm2, w1, w2, g_final)


def _rope_tables():
    nf = HEAD_DIM // 4
    t = jnp.arange(DEC_SEQ)
    row = (t // GRID_W).astype(F32)
    col = (t % GRID_W).astype(F32)
    inv = ROPE_BASE ** (-jnp.arange(nf, dtype=F32) / nf)
    ar = row[:, None] * inv[None, :]
    ac = col[:, None] * inv[None, :]
    cos = jnp.concatenate([jnp.cos(ar), jnp.cos(ar), jnp.cos(ac), jnp.cos(ac)], axis=-1)
    sin = jnp.concatenate([-jnp.sin(ar), jnp.sin(ar), -jnp.sin(ac), jnp.sin(ac)], axis=-1)
    return jnp.tile(cos, (1, 2)), jnp.tile(sin, (1, 2))


def _packed_bias_rows(rel_bias):
    n = 2 * NA_ROWS - 1
    first = jnp.concatenate([rel_bias[..., NA_COLS - 1:], jnp.zeros((DEPTH, HC, n, LANES - (2 * NA_COLS - 1)), F32),
                             rel_bias[..., :NA_COLS - 1]], axis=-1)
    lo = GRID_W - NA_COLS + 1
    second = jnp.pad(rel_bias, ((0, 0), (0, 0), (0, 0), (lo, LANES - lo - (2 * NA_COLS - 1))))
    pad_rows = ((0, 0), (0, 0), (0, BIAS_TABLE_ROWS - n), (0, 0))
    return jnp.concatenate([jnp.pad(first, pad_rows), jnp.pad(second, pad_rows)], axis=2)


def kernel(x_prompt, x_sample, c, cache_a_k, cache_a_v, cache_b_k, cache_b_v, cache_c_k, cache_c_v, c_ctx, w_mod, b_mod, g_norm1, g_norm2, w_in, b_gate, lambda_qk, g_subln, g_qnorm, g_knorm, rel_bias, w_branch_a, w_branch_b, w_branch_c, w_out, w_ff1, w_ff2, g_final):
    xp = x_prompt.reshape(N_TOK, D_MODEL)
    xs = x_sample.reshape(N_TOK, D_MODEL)

    cvec = jnp.concatenate([c_ctx[None, :], c, jnp.zeros((3, D_MODEL), F32)], axis=0)
    mods = _modulation(cvec, w_mod, b_mod)

    rope_cos, rope_sin = _rope_tables()
    bias_rows = _packed_bias_rows(rel_bias)
    bd = jnp.kron(jnp.eye(2 * LANES // HEAD_DIM, dtype=F32),
                  jnp.full((HEAD_DIM, HEAD_DIM), 1.0 / HEAD_DIM, F32)).astype(BF16)
    gq = jnp.tile(g_qnorm, (1, 2 * LANES // DB)).reshape(DEPTH, 1, 2 * LANES)
    gk = jnp.tile(g_knorm, (1, LANES // DB)).reshape(DEPTH, 1, LANES)
    g1 = g_norm1.reshape(DEPTH, 1, D_MODEL)
    g2 = g_norm2.reshape(DEPTH, 1, D_MODEL)
    bg = b_gate.reshape(DEPTH, 1, 3 * D_MODEL)
    gsub = g_subln.reshape(DEPTH, 1, 2 * DA)
    gf = g_final.reshape(1, D_MODEL)

    lat_ak = cache_a_k.reshape(DEC_BATCH, DEPTH, PAST_LEN, 2 * HA * DA)
    lat_av = cache_a_v.reshape(DEC_BATCH, DEPTH, PAST_LEN, HA * 2 * DA)
    lat_bk = cache_b_k.reshape(DEC_BATCH, DEPTH, PAST_LEN, KVB * DB)
    lat_bv = cache_b_v.reshape(DEC_BATCH, DEPTH, PAST_LEN, KVB * DB)
    lat_ck = cache_c_k.reshape(DEC_BATCH, DEPTH, PAST_LEN, HC * DC)
    lat_cv = cache_c_v.reshape(DEC_BATCH, DEPTH, PAST_LEN, HC * DC)

    new_caches = None

    for l in range(DEPTH):
        mod = mods[l].reshape(8, 1, 6 * D_MODEL)

        outs = _inproj(False, l, xp, mod, g1, w_in, bg, gq, gk, bd, None, None, new_caches)
        z, gates, new_caches = outs[0], outs[1], list(outs[2:])
        oa, ob, oc = _attn_ctx(l, z, lambda_qk, gsub)
        xp = _merge(False, l, xp, oa, ob, oc, gates, mod, w_branch_a, w_branch_b, w_branch_c, w_out)
        xp = _ffn(False, l, xp, mod, g2, w_ff1, w_ff2, gf)

        z, gates = _inproj(True, l, xs, mod, g1, w_in, bg, gq, gk, bd, rope_cos, rope_sin, None)
        oa = _attn_lat_a(l, z, lat_ak, lat_av, lambda_qk, gsub)
        ob = _attn_lat_b(l, z, lat_bk, lat_bv)
        oc = _attn_lat_c(l, z, lat_ck, lat_cv, bias_rows)
        xs = _merge(True, l, xs, oa, ob, oc, gates, mod, w_branch_a, w_branch_b, w_branch_c, w_out)
        xs = _ffn(True, l, xs, mod, g2, w_ff1, w_ff2, gf)

    y_prompt = xp.reshape(BATCH, SEQ, D_MODEL)
    y_sample = xs.reshape(DEC_BATCH, DEC_SEQ, D_MODEL)
    ak, av, bk, bv, ck, cv = new_caches
    return (y_prompt, y_sample,
            ak.reshape(BATCH, DEPTH, SEQ, 2 * HA, DA), av.reshape(BATCH, DEPTH, SEQ, HA, 2 * DA),
            bk.reshape(BATCH, DEPTH, SEQ, KVB, DB), bv.reshape(BATCH, DEPTH, SEQ, KVB, DB),
            ck.reshape(BATCH, DEPTH, SEQ, HC, DC), cv.reshape(BATCH, DEPTH, SEQ, HC, DC))
```

```python
import functools
import math

import jax
import jax.numpy as jnp
from jax import lax
from jax.experimental import pallas as pl
from jax.experimental.pallas import tpu as pltpu

D_MODEL = 1024
BATCH = 16
SEQ = 256
DEPTH = 4
DEC_BATCH = 4
DEC_SEQ = 1024
PAST_LEN = 256
GRID_W = 64
HA, DA = 4, 64
HB, KVB, DB = 8, 2, 64
HC, DC = 8, 64
NA_ROWS, NA_COLS = 8, 16
D_FF = 4 * D_MODEL
ROPE_BASE = 10000.0
EPS = 1e-6
NEG = -1e30
HEAD_DIM = 64
LOG2E = math.log2(math.e)
Q_SCALE = HEAD_DIM ** -0.5 * LOG2E

F32 = jnp.float32
BF16 = jnp.bfloat16

LANES = 128
N_QKV_GROUPS = 30
N_GATE_GROUPS = 24
COL_TILE = 768
GROUPS_PER_TILE = COL_TILE // LANES
N_QKV_TILES = N_QKV_GROUPS // GROUPS_PER_TILE
N_COL_TILES = N_QKV_TILES + N_GATE_GROUPS // GROUPS_PER_TILE
ROW_TILE = 1024
ROW_CHUNK = 256
N_TOK = BATCH * SEQ
VMEM_LIMIT_V7X = 58 * 1024 * 1024

G_QA, G_KA, G_VA, G_QB, G_KB, G_VB, G_QC, G_KC, G_VC = 0, 4, 8, 12, 16, 17, 18, 22, 26
CACHE_WIDTHS = (2 * HA * DA, HA * 2 * DA, KVB * DB, KVB * DB, HC * DC, HC * DC)

_NT = (((1,), (1,)), ((), ()))


def _params(sem, vmem=VMEM_LIMIT_V7X):
    return pltpu.CompilerParams(dimension_semantics=sem, vmem_limit_bytes=vmem)


def _lane_head(shape):
    return lax.shift_right_logical(lax.broadcasted_iota(jnp.int32, shape, len(shape) - 1), 6)


def _mod_kernel(c_ref, w_ref, b_ref, o_ref):
    c = c_ref[...]
    s = (c * jax.nn.sigmoid(c)).astype(BF16)
    o_ref[...] = jnp.dot(s, w_ref[...].astype(BF16), preferred_element_type=F32) + b_ref[...]


def _modulation(cvec, w_mod, b_mod):
    tn = 1536
    n6 = 6 * D_MODEL
    return pl.pallas_call(
        _mod_kernel,
        grid=(DEPTH, n6 // tn),
        in_specs=[pl.BlockSpec((8, D_MODEL), lambda l, n: (0, 0)),
                  pl.BlockSpec((None, D_MODEL, tn), lambda l, n: (l, 0, n)),
                  pl.BlockSpec((None, 1, tn), lambda l, n: (l, 0, n))],
        out_specs=pl.BlockSpec((None, 8, tn), lambda l, n: (l, 0, n)),
        out_shape=jax.ShapeDtypeStruct((DEPTH, 8, n6), F32),
        compiler_params=_params(("arbitrary", "arbitrary")),
        name="modulation",
    )(cvec, w_mod, b_mod.reshape(DEPTH, 1, n6))


def _rope(v, cos, sin):
    first = (lax.broadcasted_iota(jnp.int32, v.shape, 1) & 16) == 0
    partner = jnp.where(first, pltpu.roll(v, LANES - 16, 1), pltpu.roll(v, 16, 1))
    return v * cos + partner * sin


def _head_rmsnorm(v, bd, g):
    msq = jnp.dot((v * v).astype(BF16), bd, preferred_element_type=F32)
    return v * lax.rsqrt(msq + EPS) * g


def _make_inproj_kernel(latent, n_aliased):
    def kern(*refs):
        if latent:
            (x_ref, mod_ref, g1_ref, w_ref, bg_ref, gq_ref, gk_ref, bd_ref, cos_ref, sin_ref,
             z_ref, gate_ref, h_scr, wbf_scr) = refs
        else:
            (x_ref, mod_ref, g1_ref, w_ref, bg_ref, gq_ref, gk_ref, bd_ref) = refs[:8]
            (z_ref, gate_ref, ak_ref, av_ref, bk_ref, bv_ref, ck_ref, cv_ref,
             h_scr, wbf_scr) = refs[8 + n_aliased:]
        j = pl.program_id(0)
        i = pl.program_id(1)

        @pl.when(i == 0)
        def _():
            wbf_scr[...] = w_ref[...].astype(BF16)

        def chunks(first=False):
            for rc in range(ROW_TILE // ROW_CHUNK):
                rows = pl.ds(rc * ROW_CHUNK, ROW_CHUNK)
                tok = pl.ds(pl.multiple_of(i * ROW_TILE + rc * ROW_CHUNK, ROW_CHUNK), ROW_CHUNK)
                if first:
                    x = x_ref[rows, :]
                    ms = jnp.mean(x * x, axis=-1, keepdims=True)
                    y = x * lax.rsqrt(ms + EPS) * g1_ref[...]
                    m = mod_ref[0]
                    h = (y * (1.0 + m[:, D_MODEL:2 * D_MODEL]) + m[:, 0:D_MODEL]).astype(BF16)
                    h_scr[tok, :] = h
                else:
                    h = h_scr[tok, :]
                yield rc, rows, jnp.dot(h, wbf_scr[...], preferred_element_type=F32)

        def grp(a, c, n=1):
            return a[:, c * LANES:(c + n) * LANES]

        def rot(v, rows):
            return _rope(v, cos_ref[rows, :], sin_ref[rows, :]) if latent else v

        @pl.when(j == 0)
        def _():
            for rc, rows, a in chunks(first=True):
                for c in range(GROUPS_PER_TILE):
                    v = rot(grp(a, c), rows)
                    z_ref[c, rows, :] = (v * Q_SCALE if c < 4 else v).astype(BF16)
                if not latent:
                    ak_ref[rc] = grp(a, 4, 2)

        @pl.when(j == 1)
        def _():
            for rc, rows, a in chunks():
                for c in range(2):
                    z_ref[c, rows, :] = rot(grp(a, c), rows).astype(BF16)
                for c in range(2, GROUPS_PER_TILE):
                    z_ref[c, rows, :] = grp(a, c).astype(BF16)
                if not latent:
                    ak_ref[rc] = grp(a, 0, 2)
                    for hd in range(HA):
                        av_ref[rc, :, hd, :] = grp(a, 2 + hd)

        @pl.when(j == 2)
        def _():
            bd = bd_ref[...]
            for rc, rows, a in chunks():
                for half in range(2):
                    qn = _head_rmsnorm(grp(a, 2 * half, 2), bd, gq_ref[...])
                    for c in range(2):
                        z_ref[2 * half + c, rows, :] = (rot(grp(qn, c), rows) * Q_SCALE).astype(BF16)
                kn = _head_rmsnorm(grp(a, 4), bd[0:LANES, 0:LANES], gk_ref[...])
                z_ref[4, rows, :] = rot(kn, rows).astype(BF16)
                z_ref[5, rows, :] = grp(a, 5).astype(BF16)
                if not latent:
                    bk_ref[rc] = kn
                    bv_ref[rc] = grp(a, 5)

        @pl.when(j == 3)
        def _():
            for rc, rows, a in chunks():
                for c in range(GROUPS_PER_TILE):
                    v = grp(a, c)
                    z_ref[c, rows, :] = (v * Q_SCALE if c < 4 else v).astype(BF16)
                if not latent:
                    ck_ref[rc] = grp(a, 4, 2)

        @pl.when(j == 4)
        def _():
            for rc, rows, a in chunks():
                for c in range(GROUPS_PER_TILE):
                    z_ref[c, rows, :] = grp(a, c).astype(BF16)
                if not latent:
                    ck_ref[rc] = grp(a, 0, 2)
                    cv_ref[rc] = grp(a, 2, 4)

        @pl.when(j >= N_QKV_TILES)
        def _():
            for rc, rows, a in chunks():
                a = a + bg_ref[...]
                for c in range(GROUPS_PER_TILE):
                    gate_ref[c, rows, :] = jax.nn.sigmoid(grp(a, c)).astype(BF16)

    return kern


def _inproj(latent, l, x, mod, g_norm1, w_in, b_gate, gq, gk, bd, rope_cos, rope_sin, caches):
    n_row = N_TOK // ROW_TILE
    last = n_row - 1

    def row_block(j, i, first_tile, last_tile):
        return jnp.where(j < first_tile, 0, jnp.where(j <= last_tile, i, last))

    mod_idx = (lambda j, i: (1 + i, 0, 0)) if latent else (lambda j, i: (0, 0, 0))
    in_specs = [
        pl.BlockSpec((ROW_TILE, D_MODEL), lambda j, i: (row_block(j, i, 0, 0), 0)),
        pl.BlockSpec((1, 1, 6 * D_MODEL), mod_idx),
        pl.BlockSpec((None, 1, D_MODEL), lambda j, i: (l, 0, 0)),
        pl.BlockSpec((None, D_MODEL, COL_TILE), lambda j, i: (l, 0, j)),
        pl.BlockSpec((None, 1, COL_TILE), lambda j, i: (l, 0, jnp.maximum(j - N_QKV_TILES, 0))),
        pl.BlockSpec((None, 1, 2 * LANES), lambda j, i: (l, 0, 0)),
        pl.BlockSpec((None, 1, LANES), lambda j, i: (l, 0, 0)),
        pl.BlockSpec((2 * LANES, 2 * LANES), lambda j, i: (0, 0)),
    ]
    args = [x, mod, g_norm1, w_in, b_gate, gq, gk, bd]
    out_specs = [
        pl.BlockSpec((GROUPS_PER_TILE, ROW_TILE, LANES),
                     lambda j, i: (jnp.minimum(j, N_QKV_TILES - 1), row_block(j, i, 0, N_QKV_TILES - 1), 0)),
        pl.BlockSpec((GROUPS_PER_TILE, ROW_TILE, LANES),
                     lambda j, i: (jnp.maximum(j - N_QKV_TILES, 0), row_block(j, i, N_QKV_TILES, N_COL_TILES), 0)),
    ]
    out_shape = [jax.ShapeDtypeStruct((N_QKV_GROUPS, N_TOK, LANES), BF16),
                 jax.ShapeDtypeStruct((N_GATE_GROUPS, N_TOK, LANES), BF16)]
    aliases = {}
    if latent:
        in_specs += [pl.BlockSpec((DEC_SEQ, LANES), lambda j, i: (0, 0))] * 2
        args += [rope_cos, rope_sin]
    else:
        nb = ROW_TILE // SEQ
        tiles = ((0, 1, 256), (1, 1, 512), (2, 2, 128), (2, 2, 128), (3, 4, 256), (4, 4, 512))
        for k, (w, (t0, t1, bw)) in enumerate(zip(CACHE_WIDTHS, tiles)):
            if caches is not None:
                in_specs.append(pl.BlockSpec(memory_space=pl.ANY))
                args.append(caches[k])
                aliases[8 + k] = 2 + k

            def cache_idx(j, i, t0=t0, t1=t1):
                return (row_block(j, i, t0, t1), l, 0, jnp.where(j <= t0, 0, (t1 - t0)))

            if k == 1:
                out_specs.append(pl.BlockSpec((nb, None, SEQ, HA, 2 * DA),
                                              lambda j, i, f=cache_idx: f(j, i) + (0,)))
                out_shape.append(jax.ShapeDtypeStruct((BATCH, DEPTH, SEQ, HA, 2 * DA), F32))
                continue
            out_specs.append(pl.BlockSpec((nb, None, SEQ, bw), cache_idx))
            out_shape.append(jax.ShapeDtypeStruct((BATCH, DEPTH, SEQ, w), F32))
    return pl.pallas_call(
        _make_inproj_kernel(latent, len(aliases)),
        grid=(N_COL_TILES, n_row),
        in_specs=in_specs,
        out_specs=out_specs,
        out_shape=out_shape,
        scratch_shapes=[pltpu.VMEM((N_TOK, D_MODEL), BF16), pltpu.VMEM((D_MODEL, COL_TILE), BF16)],
        input_output_aliases=aliases,
        compiler_params=_params(("arbitrary", "arbitrary")),
        name="inproj_lat" if latent else "inproj_ctx",
    )(*args)


def _mask_head(q, head):
    qf = q.astype(F32)
    keep = _lane_head(qf.shape) == head
    return jnp.where(keep, qf, 0.0).astype(BF16)


def _stack_heads(q):
    return jnp.concatenate([_mask_head(q, 0), _mask_head(q, 1)], axis=0)


def _unstack_heads(o, rows):
    return jnp.where(_lane_head((rows, LANES)) == 1, o[rows:2 * rows], o[0:rows])


def _dup_head(kv, head):
    f = kv.astype(F32)
    r = pltpu.roll(f, HEAD_DIM, 1)
    return jnp.where(_lane_head(f.shape) == head, f, r).astype(BF16)


def _attend(qm, ks, vs, biases=None, k_t=None, v_t=None):
    k_t = k_t or (False,) * len(ks)
    v_t = v_t or (False,) * len(vs)
    ss = [jnp.dot(qm, k, preferred_element_type=F32) if t else
          lax.dot_general(qm, k, _NT, preferred_element_type=F32) for k, t in zip(ks, k_t)]
    if biases is not None:
        ss = [s if b is None else s + b for s, b in zip(ss, biases)]
    m = functools.reduce(jnp.maximum, [jnp.max(s, axis=-1, keepdims=True) for s in ss])
    ps = [jnp.exp2(s - m) for s in ss]
    den = functools.reduce(jnp.add, [jnp.sum(p, axis=-1, keepdims=True) for p in ps])
    o = functools.reduce(jnp.add, [lax.dot_general(p.astype(BF16), v, _NT, preferred_element_type=F32) if t else
                                    jnp.dot(p.astype(BF16), v, preferred_element_type=F32)
                                    for p, v, t in zip(ps, vs, v_t)])
    return o * (1.0 / den)


_CTX_T = (False, True)


def _pair_t(ref):
    return ref[...].reshape(2 * HEAD_DIM, ref.shape[-1]).astype(BF16)


def _diff_lambda(lam_ref, lam_init):
    lq = lam_ref[...]
    a = jnp.sum(lq[0:1] * lq[1:2], axis=-1, keepdims=True)
    b = jnp.sum(lq[2:3] * lq[3:4], axis=-1, keepdims=True)
    return jnp.exp(a) - jnp.exp(b) + lam_init


def _diff_combine(o1, o2, lam, gsub, lam_init):
    o = o1 - lam * o2
    ms = jnp.mean(o * o, axis=-1, keepdims=True)
    return (o * lax.rsqrt(ms + EPS) * gsub) * (1.0 - lam_init)


CTX_BATCH_PER_STEP = 2


def _make_attn_ctx_kernel(lam_init):
    def kern(z_ref, lam_ref, gsub_ref, oa_ref, ob_ref, oc_ref):
        lam = _diff_lambda(lam_ref, lam_init)
        gsub = gsub_ref[...]
        for bb in range(CTX_BATCH_PER_STEP):
            rows = pl.ds(bb * SEQ, SEQ)

            def z(g):
                return z_ref[g, rows, :]

            for vh in range(HA):
                hi = vh % 2
                v = z(G_VA + vh)
                o1 = _attend(_mask_head(z(G_QA + vh // 2), hi), [z(G_KA + vh // 2)], [v])
                o2 = _attend(_mask_head(z(G_QA + 2 + vh // 2), hi), [z(G_KA + 2 + vh // 2)], [v])
                oa_ref[vh, rows, :] = _diff_combine(o1, o2, lam, gsub, lam_init).astype(BF16)
            for g in range(KVB):
                kd = _dup_head(z(G_KB), g)
                vd = _dup_head(z(G_VB), g)
                for c in range(2 * g, 2 * g + 2):
                    o = _attend(_stack_heads(z(G_QB + c)), [kd], [vd])
                    ob_ref[c, rows, :] = _unstack_heads(o, SEQ).astype(BF16)
            for c in range(HC // 2):
                o = _attend(_stack_heads(z(G_QC + c)), [z(G_KC + c)], [z(G_VC + c)])
                oc_ref[c, rows, :] = _unstack_heads(o, SEQ).astype(BF16)
    return kern


def _attn_ctx(l, z, lambda_qk, g_subln):
    lam_init = 0.8 - 0.6 * math.exp(-0.3 * l)
    rows = CTX_BATCH_PER_STEP * SEQ
    o_spec = pl.BlockSpec((4, rows, LANES), lambda b: (0, b, 0))
    o_shape = jax.ShapeDtypeStruct((4, N_TOK, LANES), BF16)
    return pl.pallas_call(
        _make_attn_ctx_kernel(lam_init),
        grid=(BATCH // CTX_BATCH_PER_STEP,),
        in_specs=[pl.BlockSpec((N_QKV_GROUPS, rows, LANES), lambda b: (0, b, 0)),
                  pl.BlockSpec((None, 4, DA), lambda b: (l, 0, 0)),
                  pl.BlockSpec((None, 1, 2 * DA), lambda b: (l, 0, 0))],
        out_specs=[o_spec, o_spec, o_spec],
        out_shape=[o_shape, o_shape, o_shape],
        compiler_params=_params(("arbitrary",)),
        name="attn_ctx",
    )(z, lambda_qk, g_subln)


Q_BLK_GQA = 256
Q_BLK_DIFF = 512


def _make_attn_lat_a_kernel(lam_init):
    def kern(q1_ref, q2_ref, k1_ref, k2_ref, v_ref, ck1_ref, ck2_ref, cv_ref, lam_ref, gsub_ref, o_ref):
        vh = pl.program_id(1)
        hi = jnp.bitwise_and(vh, 1)
        lam = _diff_lambda(lam_ref, lam_init)
        gsub = gsub_ref[...]
        ck1 = _pair_t(ck1_ref)
        ck2 = _pair_t(ck2_ref)
        cv = cv_ref[:, vh, :].astype(BF16)
        k1, k2, v = k1_ref[0], k2_ref[0], v_ref[0]
        for qb in range(DEC_SEQ // Q_BLK_DIFF):
            rows = pl.ds(qb * Q_BLK_DIFF, Q_BLK_DIFF)
            o1 = _attend(_mask_head(q1_ref[0, rows, :], hi), [k1, ck1], [v, cv], k_t=_CTX_T)
            o2 = _attend(_mask_head(q2_ref[0, rows, :], hi), [k2, ck2], [v, cv], k_t=_CTX_T)
            o_ref[0, rows, :] = _diff_combine(o1, o2, lam, gsub, lam_init).astype(BF16)
    return kern


def _attn_lat_a(l, z, cache_k_t, cache_v, lambda_qk, g_subln):
    lam_init = 0.8 - 0.6 * math.exp(-0.3 * l)

    def zspec(fn):
        return pl.BlockSpec((1, DEC_SEQ, LANES), lambda b, h: (fn(h), b, 0))

    return pl.pallas_call(
        _make_attn_lat_a_kernel(lam_init),
        grid=(DEC_BATCH, HA),
        in_specs=[zspec(lambda h: G_QA + h // 2), zspec(lambda h: G_QA + 2 + h // 2),
                  zspec(lambda h: G_KA + h // 2), zspec(lambda h: G_KA + 2 + h // 2),
                  zspec(lambda h: G_VA + h),
                  pl.BlockSpec((None, None, 2, DA, PAST_LEN), lambda b, h: (b, l, h // 2, 0, 0)),
                  pl.BlockSpec((None, None, 2, DA, PAST_LEN), lambda b, h: (b, l, HA // 2 + h // 2, 0, 0)),
                  pl.BlockSpec((None, None, PAST_LEN, HA, 2 * DA), lambda b, h: (b, l, 0, 0, 0)),
                  pl.BlockSpec((None, 4, DA), lambda b, h: (l, 0, 0)),
                  pl.BlockSpec((None, 1, 2 * DA), lambda b, h: (l, 0, 0))],
        out_specs=pl.BlockSpec((1, DEC_SEQ, LANES), lambda b, h: (h, b, 0)),
        out_shape=jax.ShapeDtypeStruct((4, N_TOK, LANES), BF16),
        compiler_params=_params(("arbitrary", "arbitrary")),
        name="attn_lat_a",
    )(z, z, z, z, z, cache_k_t, cache_k_t, cache_v, lambda_qk, g_subln)


def _attn_lat_b_kernel(q_ref, k_ref, v_ref, ck_ref, cv_ref, o_ref):
    hi = lax.shift_right_logical(pl.program_id(1), 1)
    kd = _dup_head(k_ref[0], hi)
    vd = _dup_head(v_ref[0], hi)
    ck = ck_ref[hi].astype(BF16)
    cv = cv_ref[hi].astype(BF16)
    ckd = jnp.concatenate([ck, ck], axis=0)
    cvd = jnp.concatenate([cv, cv], axis=0)
    for qb in range(DEC_SEQ // Q_BLK_GQA):
        rows = pl.ds(qb * Q_BLK_GQA, Q_BLK_GQA)
        o = _attend(_stack_heads(q_ref[0, rows, :]), [kd, ckd], [vd, cvd], k_t=_CTX_T, v_t=_CTX_T)
        o_ref[0, rows, :] = _unstack_heads(o, Q_BLK_GQA).astype(BF16)


def _attn_lat_b(l, z, cache_k, cache_v):
    cspec = pl.BlockSpec((None, None, KVB, DB, PAST_LEN), lambda b, c: (b, l, 0, 0, 0))
    return pl.pallas_call(
        _attn_lat_b_kernel,
        grid=(DEC_BATCH, HB // 2),
        in_specs=[pl.BlockSpec((1, DEC_SEQ, LANES), lambda b, c: (G_QB + c, b, 0)),
                  pl.BlockSpec((1, DEC_SEQ, LANES), lambda b, c: (G_KB, b, 0)),
                  pl.BlockSpec((1, DEC_SEQ, LANES), lambda b, c: (G_VB, b, 0)),
                  cspec, cspec],
        out_specs=pl.BlockSpec((1, DEC_SEQ, LANES), lambda b, c: (c, b, 0)),
        out_shape=jax.ShapeDtypeStruct((4, N_TOK, LANES), BF16),
        compiler_params=_params(("arbitrary", "arbitrary")),
        name="attn_lat_b",
    )(z, z, z, cache_k, cache_v)


N_GRID_ROWS = DEC_SEQ // GRID_W
NA_KEYS = NA_ROWS * GRID_W


NA_GROUP = 4
NA_MAX_KEY_ROWS = 12
BIAS_TABLE_ROWS = 16


def _window_start(r):
    return min(max(r - NA_ROWS // 2, 0), N_GRID_ROWS - NA_ROWS)


def _group_keys(g):
    starts = [_window_start(r) for r in range(g * NA_GROUP, (g + 1) * NA_GROUP)]
    n = max(starts) + NA_ROWS - min(starts)
    n += n % 2
    return min(min(starts), N_GRID_ROWS - n), n


def _build_window_bias(tab_ref, bias_scr):
    qcol = lax.broadcasted_iota(jnp.int32, (GRID_W, LANES), 0)
    lane = lax.broadcasted_iota(jnp.int32, (GRID_W, LANES), 1)
    kcol = lane & (GRID_W - 1)
    c0 = jnp.clip(qcol - NA_COLS // 2, 0, GRID_W - NA_COLS)
    in_win = (kcol >= c0) & (kcol < c0 + NA_COLS)
    masks = {(True, True): in_win, (True, False): in_win & (lane < GRID_W), (False, True): in_win & (lane >= GRID_W)}
    for head in range(2):
        for r in range(N_GRID_ROWS):
            u0, n = _group_keys(r // NA_GROUP)
            rows = pl.ds(r * GRID_W, GRID_W)
            for m in range(n // 2):
                key_rows = (u0 + 2 * m, u0 + 2 * m + 1)
                valid = tuple(_window_start(r) <= kr < _window_start(r) + NA_ROWS for kr in key_rows)
                cols = pl.ds(m * LANES, LANES)
                if not any(valid):
                    bias_scr[head, rows, cols] = jnp.full((GRID_W, LANES), NEG, F32)
                    continue
                vec = None
                for half, (kr, ok) in enumerate(zip(key_rows, valid)):
                    if ok:
                        dr = kr - r + NA_ROWS - 1
                        part = tab_ref[head, pl.ds(half * BIAS_TABLE_ROWS + dr, 1), :]
                        vec = part if vec is None else vec + part
                toep = pltpu.roll(jnp.broadcast_to(vec, (GRID_W, LANES)), 0, 1, stride=1, stride_axis=0)
                bias_scr[head, rows, cols] = jnp.where(masks[valid], toep * LOG2E, NEG)


def _attn_lat_c_kernel(q_ref, k_ref, v_ref, ck_ref, cv_ref, tab_ref, o_ref, bias_scr):
    @pl.when(pl.program_id(1) == 0)
    def _():
        _build_window_bias(tab_ref, bias_scr)

    ck = _pair_t(ck_ref)
    cv = _pair_t(cv_ref)
    n_rows = NA_GROUP * GRID_W
    for g in range(N_GRID_ROWS // NA_GROUP):
        u0, n = _group_keys(g)
        rows = pl.ds(g * n_rows, n_rows)
        keys = pl.ds(u0 * GRID_W, n * GRID_W)
        bias = jnp.concatenate([bias_scr[0, rows, 0:n * GRID_W], bias_scr[1, rows, 0:n * GRID_W]], axis=0)
        o = _attend(_stack_heads(q_ref[0, rows, :]), [k_ref[0, keys, :], ck], [v_ref[0, keys, :], cv],
                    biases=[bias, None], k_t=_CTX_T, v_t=_CTX_T)
        o_ref[0, rows, :] = _unstack_heads(o, n_rows).astype(BF16)


def _attn_lat_c(l, z, cache_k, cache_v, bias_rows):
    def zspec(g0):
        return pl.BlockSpec((1, DEC_SEQ, LANES), lambda c, b: (g0 + c, b, 0))

    cspec = pl.BlockSpec((None, None, 2, DC, PAST_LEN), lambda c, b: (b, l, c, 0, 0))
    return pl.pallas_call(
        _attn_lat_c_kernel,
        grid=(HC // 2, DEC_BATCH),
        in_specs=[zspec(G_QC), zspec(G_KC), zspec(G_VC), cspec, cspec,
                  pl.BlockSpec((None, 2, 2 * BIAS_TABLE_ROWS, LANES), lambda c, b: (l, c, 0, 0))],
        out_specs=pl.BlockSpec((1, DEC_SEQ, LANES), lambda c, b: (c, b, 0)),
        out_shape=jax.ShapeDtypeStruct((4, N_TOK, LANES), BF16),
        scratch_shapes=[pltpu.VMEM((2, DEC_SEQ, NA_MAX_KEY_ROWS * GRID_W), F32)],
        compiler_params=_params(("arbitrary", "arbitrary")),
        name="attn_lat_c",
    )(z, z, z, cache_k, cache_v, bias_rows)


MERGE_TILE = 512


def _merge_kernel(x_ref, oa_ref, ob_ref, oc_ref, gate_ref, mod_ref, wa_ref, wb_ref, wc_ref, wo_ref,
                  xo_ref, wbr_scr, wo_scr):
    @pl.when(pl.program_id(0) == 0)
    def _():
        wbr_scr[0] = wa_ref[...].astype(BF16)
        wbr_scr[1] = wb_ref[...].astype(BF16)
        wbr_scr[2] = wc_ref[...].astype(BF16)
        wo_scr[...] = wo_ref[...].astype(BF16)

    m = mod_ref[0]
    for rc in range(MERGE_TILE // ROW_CHUNK):
        rows = pl.ds(rc * ROW_CHUNK, ROW_CHUNK)
        y = None
        for k, o_ref in enumerate((oa_ref, ob_ref, oc_ref)):
            o = jnp.concatenate([o_ref[c, rows, :] for c in range(4)], axis=-1)
            p = jnp.dot(o, wbr_scr[k], preferred_element_type=F32)
            g = jnp.concatenate([gate_ref[8 * k + c, rows, :] for c in range(8)], axis=-1).astype(F32)
            y = g * p if y is None else y + g * p
        out = jnp.dot(y.astype(BF16), wo_scr[...], preferred_element_type=F32)
        xo_ref[rows, :] = x_ref[rows, :] + m[:, 2 * D_MODEL:3 * D_MODEL] * out


def _merge(latent, l, x, oa, ob, oc, gates, mod, w_a, w_b, w_c, w_o):
    tm = MERGE_TILE
    if latent:
        mod_idx = lambda i: (1 + (i * tm) // DEC_SEQ, 0, 0)
    else:
        mod_idx = lambda i: (0, 0, 0)
    o_spec = pl.BlockSpec((4, tm, LANES), lambda i: (0, i, 0))
    wbr_spec = pl.BlockSpec((None, 4 * LANES, D_MODEL), lambda i: (l, 0, 0))
    return pl.pallas_call(
        _merge_kernel,
        grid=(N_TOK // tm,),
        in_specs=[pl.BlockSpec((tm, D_MODEL), lambda i: (i, 0)),
                  o_spec, o_spec, o_spec,
                  pl.BlockSpec((N_GATE_GROUPS, tm, LANES), lambda i: (0, i, 0)),
                  pl.BlockSpec((1, 1, 6 * D_MODEL), mod_idx),
                  wbr_spec, wbr_spec, wbr_spec,
                  pl.BlockSpec((None, D_MODEL, D_MODEL), lambda i: (l, 0, 0))],
        out_specs=pl.BlockSpec((tm, D_MODEL), lambda i: (i, 0)),
        out_shape=jax.ShapeDtypeStruct((N_TOK, D_MODEL), F32),
        scratch_shapes=[pltpu.VMEM((3, 4 * LANES, D_MODEL), BF16), pltpu.VMEM((D_MODEL, D_MODEL), BF16)],
        compiler_params=_params(("arbitrary",)),
        name="merge_lat" if latent else "merge_ctx",
    )(x, oa, ob, oc, gates, mod, w_a, w_b, w_c, w_o)


FF_TILE = 1024
FF_CHUNK = 256


def _ffn_kernel(x_ref, mod_ref, g2_ref, w1_ref, w2_ref, gf_ref, xo_ref, acc_scr, h2_scr, *, final):
    f = pl.program_id(1)

    def run(first):
        w1 = w1_ref[...].astype(BF16)
        w2 = w2_ref[...].astype(BF16)
        m = mod_ref[0]
        for rc in range(ROW_TILE // FF_CHUNK):
            rows = pl.ds(rc * FF_CHUNK, FF_CHUNK)
            if first:
                x = x_ref[rows, :]
                ms = jnp.mean(x * x, axis=-1, keepdims=True)
                y = x * lax.rsqrt(ms + EPS) * g2_ref[...]
                h2 = (y * (1.0 + m[:, 4 * D_MODEL:5 * D_MODEL]) + m[:, 3 * D_MODEL:4 * D_MODEL]).astype(BF16)
                h2_scr[rows, :] = h2
            else:
                h2 = h2_scr[rows, :]
            u = jnp.dot(h2, w1, preferred_element_type=F32)
            u = jnp.square(jnp.maximum(u, 0.0)).astype(BF16)
            d = jnp.dot(u, w2, preferred_element_type=F32)
            if first:
                acc_scr[rows, :] = d
            else:
                acc_scr[rows, :] += d

    @pl.when(f == 0)
    def _():
        run(True)

    @pl.when(f > 0)
    def _():
        run(False)

    @pl.when(f == D_FF // FF_TILE - 1)
    def _():
        x = x_ref[...] + mod_ref[0][:, 5 * D_MODEL:6 * D_MODEL] * acc_scr[...]
        if final:
            ms = jnp.mean(x * x, axis=-1, keepdims=True)
            x = x * lax.rsqrt(ms + EPS) * gf_ref[...]
        xo_ref[...] = x


def _ffn(latent, l, x, mod, g_norm2, w1, w2, g_final):
    tm = ROW_TILE
    mod_idx = (lambda i, f: (1 + i, 0, 0)) if latent else (lambda i, f: (0, 0, 0))
    return pl.pallas_call(
        functools.partial(_ffn_kernel, final=(l == DEPTH - 1)),
        grid=(N_TOK // tm, D_FF // FF_TILE),
        in_specs=[pl.BlockSpec((tm, D_MODEL), lambda i, f: (i, 0)),
                  pl.BlockSpec((1, 1, 6 * D_MODEL), mod_idx),
                  pl.BlockSpec((None, 1, D_MODEL), lambda i, f: (l, 0, 0)),
                  pl.BlockSpec((None, D_MODEL, FF_TILE), lambda i, f: (l, 0, f)),
                  pl.BlockSpec((None, FF_TILE, D_MODEL), lambda i, f: (l, f, 0)),
                  pl.BlockSpec((1, D_MODEL), lambda i, f: (0, 0))],
        out_specs=pl.BlockSpec((tm, D_MODEL), lambda i, f: (i, 0)),
        out_shape=jax.ShapeDtypeStruct((N_TOK, D_MODEL), F32),
        scratch_shapes=[pltpu.VMEM((tm, D_MODEL), F32), pltpu.VMEM((tm, D_MODEL), BF16)],
        compiler_params=_params(("arbitrary", "arbitrary")),
        name="ffn_lat" if latent else "ffn_ctx",
    )(x, mod, g_norm2, w1, w2, g_final)


def _rope_tables():
    nf = HEAD_DIM // 4
    t = jnp.arange(DEC_SEQ)
    row = (t // GRID_W).astype(F32)
    col = (t % GRID_W).astype(F32)
    inv = ROPE_BASE ** (-jnp.arange(nf, dtype=F32) / nf)
    ar = row[:, None] * inv[None, :]
    ac = col[:, None] * inv[None, :]
    cos = jnp.concatenate([jnp.cos(ar), jnp.cos(ar), jnp.cos(ac), jnp.cos(ac)], axis=-1)
    sin = jnp.concatenate([-jnp.sin(ar), jnp.sin(ar), -jnp.sin(ac), jnp.sin(ac)], axis=-1)
    return jnp.tile(cos, (1, 2)), jnp.tile(sin, (1, 2))


def _packed_bias_rows(rel_bias):
    n = 2 * NA_ROWS - 1
    first = jnp.concatenate([rel_bias[..., NA_COLS - 1:], jnp.zeros((DEPTH, HC, n, LANES - (2 * NA_COLS - 1)), F32),
                             rel_bias[..., :NA_COLS - 1]], axis=-1)
    lo = GRID_W - NA_COLS + 1
    second = jnp.pad(rel_bias, ((0, 0), (0, 0), (0, 0), (lo, LANES - lo - (2 * NA_COLS - 1))))
    pad_rows = ((0, 0), (0, 0), (0, BIAS_TABLE_ROWS - n), (0, 0))
    return jnp.concatenate([jnp.pad(first, pad_rows), jnp.pad(second, pad_rows)], axis=2)


def kernel(x_prompt, x_sample, c, cache_a_k, cache_a_v, cache_b_k, cache_b_v, cache_c_k, cache_c_v, c_ctx, w_mod, b_mod, g_norm1, g_norm2, w_in, b_gate, lambda_qk, g_subln, g_qnorm, g_knorm, rel_bias, w_branch_a, w_branch_b, w_branch_c, w_out, w_ff1, w_ff2, g_final):
    xp = x_prompt.reshape(N_TOK, D_MODEL)
    xs = x_sample.reshape(N_TOK, D_MODEL)

    cvec = jnp.concatenate([c_ctx[None, :], c, jnp.zeros((3, D_MODEL), F32)], axis=0)
    mods = _modulation(cvec, w_mod, b_mod)

    rope_cos, rope_sin = _rope_tables()
    bias_rows = _packed_bias_rows(rel_bias)
    bd = jnp.kron(jnp.eye(2 * LANES // HEAD_DIM, dtype=F32),
                  jnp.full((HEAD_DIM, HEAD_DIM), 1.0 / HEAD_DIM, F32)).astype(BF16)
    gq = jnp.tile(g_qnorm, (1, 2 * LANES // DB)).reshape(DEPTH, 1, 2 * LANES)
    gk = jnp.tile(g_knorm, (1, LANES // DB)).reshape(DEPTH, 1, LANES)
    g1 = g_norm1.reshape(DEPTH, 1, D_MODEL)
    g2 = g_norm2.reshape(DEPTH, 1, D_MODEL)
    bg = b_gate.reshape(DEPTH, 1, 3 * D_MODEL)
    gsub = g_subln.reshape(DEPTH, 1, 2 * DA)
    gf = g_final.reshape(1, D_MODEL)


    def heads_t(cache):
        return cache.transpose(0, 1, 3, 4, 2)

    ctx_ak, ctx_bk, ctx_bv = heads_t(cache_a_k), heads_t(cache_b_k), heads_t(cache_b_v)
    ctx_ck, ctx_cv = heads_t(cache_c_k), heads_t(cache_c_v)

    new_caches = None

    for l in range(DEPTH):
        mod = mods[l].reshape(8, 1, 6 * D_MODEL)

        outs = _inproj(False, l, xp, mod, g1, w_in, bg, gq, gk, bd, None, None, new_caches)
        z, gates, new_caches = outs[0], outs[1], list(outs[2:])
        oa, ob, oc = _attn_ctx(l, z, lambda_qk, gsub)
        xp = _merge(False, l, xp, oa, ob, oc, gates, mod, w_branch_a, w_branch_b, w_branch_c, w_out)
        xp = _ffn(False, l, xp, mod, g2, w_ff1, w_ff2, gf)

        z, gates = _inproj(True, l, xs, mod, g1, w_in, bg, gq, gk, bd, rope_cos, rope_sin, None)
        oa = _attn_lat_a(l, z, ctx_ak, cache_a_v, lambda_qk, gsub)
        ob = _attn_lat_b(l, z, ctx_bk, ctx_bv)
        oc = _attn_lat_c(l, z, ctx_ck, ctx_cv, bias_rows)
        xs = _merge(True, l, xs, oa, ob, oc, gates, mod, w_branch_a, w_branch_b, w_branch_c, w_out)
        xs = _ffn(True, l, xs, mod, g2, w_ff1, w_ff2, gf)

    y_prompt = xp.reshape(BATCH, SEQ, D_MODEL)
    y_sample = xs.reshape(DEC_BATCH, DEC_SEQ, D_MODEL)
    ak, av, bk, bv, ck, cv = new_caches
    return (y_prompt, y_sample,
            ak.reshape(BATCH, DEPTH, SEQ, 2 * HA, DA), av.reshape(BATCH, DEPTH, SEQ, HA, 2 * DA),
            bk.reshape(BATCH, DEPTH, SEQ, KVB, DB), bv.reshape(BATCH, DEPTH, SEQ, KVB, DB),
            ck.reshape(BATCH, DEPTH, SEQ, HC, DC), cv.reshape(BATCH, DEPTH, SEQ, HC, DC))
```

```python
import functools
import math

import jax
import jax.numpy as jnp
from jax import lax
from jax.experimental import pallas as pl
from jax.experimental.pallas import tpu as pltpu

D_MODEL = 1024
BATCH = 16
SEQ = 256
DEPTH = 4
DEC_BATCH = 4
DEC_SEQ = 1024
PAST_LEN = 256
GRID_W = 64
HA, DA = 4, 64
HB, KVB, DB = 8, 2, 64
HC, DC = 8, 64
NA_ROWS, NA_COLS = 8, 16
D_FF = 4 * D_MODEL
ROPE_BASE = 10000.0
EPS = 1e-6
NEG = -1e30
HEAD_DIM = 64
LOG2E = math.log2(math.e)
Q_SCALE = HEAD_DIM ** -0.5 * LOG2E

F32 = jnp.float32
BF16 = jnp.bfloat16

LANES = 128
N_QKV_GROUPS = 30
N_GATE_GROUPS = 24
COL_TILE = 768
GROUPS_PER_TILE = COL_TILE // LANES
N_QKV_TILES = N_QKV_GROUPS // GROUPS_PER_TILE
N_COL_TILES = N_QKV_TILES + N_GATE_GROUPS // GROUPS_PER_TILE
ROW_TILE = 1024
ROW_CHUNK = 256
N_TOK = BATCH * SEQ
VMEM_LIMIT_V7X = 58 * 1024 * 1024

G_QA, G_KA, G_VA, G_QB, G_KB, G_VB, G_QC, G_KC, G_VC = 0, 4, 8, 12, 16, 17, 18, 22, 26
CACHE_WIDTHS = (2 * HA * DA, HA * 2 * DA, KVB * DB, KVB * DB, HC * DC, HC * DC)

_NT = (((1,), (1,)), ((), ()))


def _params(sem, vmem=VMEM_LIMIT_V7X):
    return pltpu.CompilerParams(dimension_semantics=sem, vmem_limit_bytes=vmem)


def _lane_head(shape):
    return lax.shift_right_logical(lax.broadcasted_iota(jnp.int32, shape, len(shape) - 1), 6)


def _mod_kernel(c_ref, w_ref, b_ref, o_ref):
    c = c_ref[...]
    s = (c * jax.nn.sigmoid(c)).astype(BF16)
    o_ref[...] = jnp.dot(s, w_ref[...].astype(BF16), preferred_element_type=F32) + b_ref[...]


def _modulation(cvec, w_mod, b_mod):
    tn = 1536
    n6 = 6 * D_MODEL
    return pl.pallas_call(
        _mod_kernel,
        grid=(DEPTH, n6 // tn),
        in_specs=[pl.BlockSpec((8, D_MODEL), lambda l, n: (0, 0)),
                  pl.BlockSpec((None, D_MODEL, tn), lambda l, n: (l, 0, n)),
                  pl.BlockSpec((None, 1, tn), lambda l, n: (l, 0, n))],
        out_specs=pl.BlockSpec((None, 8, tn), lambda l, n: (l, 0, n)),
        out_shape=jax.ShapeDtypeStruct((DEPTH, 8, n6), F32),
        compiler_params=_params(("arbitrary", "arbitrary")),
        name="modulation",
    )(cvec, w_mod, b_mod.reshape(DEPTH, 1, n6))


def _rope(v, cos, sin):
    first = (lax.broadcasted_iota(jnp.int32, v.shape, 1) & 16) == 0
    partner = jnp.where(first, pltpu.roll(v, LANES - 16, 1), pltpu.roll(v, 16, 1))
    return v * cos + partner * sin


def _head_rmsnorm(v, bd, g):
    msq = jnp.dot((v * v).astype(BF16), bd, preferred_element_type=F32)
    return v * lax.rsqrt(msq + EPS) * g


def _make_inproj_kernel(latent, n_aliased):
    def kern(*refs):
        if latent:
            (x_ref, mod_ref, g1_ref, w_ref, bg_ref, gq_ref, gk_ref, bd_ref, cos_ref, sin_ref,
             z_ref, gate_ref, h_scr, wbf_scr) = refs
        else:
            (x_ref, mod_ref, g1_ref, w_ref, bg_ref, gq_ref, gk_ref, bd_ref) = refs[:8]
            (z_ref, gate_ref, ak_ref, av_ref, bk_ref, bv_ref, ck_ref, cv_ref,
             h_scr, wbf_scr) = refs[8 + n_aliased:]
        j = pl.program_id(0)
        i = pl.program_id(1)

        @pl.when(i == 0)
        def _():
            wbf_scr[...] = w_ref[...].astype(BF16)

        def chunks(first=False):
            for rc in range(ROW_TILE // ROW_CHUNK):
                rows = pl.ds(rc * ROW_CHUNK, ROW_CHUNK)
                tok = pl.ds(pl.multiple_of(i * ROW_TILE + rc * ROW_CHUNK, ROW_CHUNK), ROW_CHUNK)
                if first:
                    x = x_ref[rows, :]
                    ms = jnp.mean(x * x, axis=-1, keepdims=True)
                    y = x * lax.rsqrt(ms + EPS) * g1_ref[...]
                    m = mod_ref[0]
                    h = (y * (1.0 + m[:, D_MODEL:2 * D_MODEL]) + m[:, 0:D_MODEL]).astype(BF16)
                    h_scr[tok, :] = h
                else:
                    h = h_scr[tok, :]
                yield rc, rows, jnp.dot(h, wbf_scr[...], preferred_element_type=F32)

        def grp(a, c, n=1):
            return a[:, c * LANES:(c + n) * LANES]

        def rot(v, rows):
            return _rope(v, cos_ref[rows, :], sin_ref[rows, :]) if latent else v

        @pl.when(j == 0)
        def _():
            for rc, rows, a in chunks(first=True):
                for c in range(GROUPS_PER_TILE):
                    v = rot(grp(a, c), rows)
                    z_ref[c, rows, :] = (v * Q_SCALE if c < 4 else v).astype(BF16)
                if not latent:
                    ak_ref[rc] = grp(a, 4, 2)

        @pl.when(j == 1)
        def _():
            for rc, rows, a in chunks():
                for c in range(2):
                    z_ref[c, rows, :] = rot(grp(a, c), rows).astype(BF16)
                for c in range(2, GROUPS_PER_TILE):
                    z_ref[c, rows, :] = grp(a, c).astype(BF16)
                if not latent:
                    ak_ref[rc] = grp(a, 0, 2)
                    for hd in range(HA):
                        av_ref[rc, :, hd, :] = grp(a, 2 + hd)

        @pl.when(j == 2)
        def _():
            bd = bd_ref[...]
            for rc, rows, a in chunks():
                for half in range(2):
                    qn = _head_rmsnorm(grp(a, 2 * half, 2), bd, gq_ref[...])
                    for c in range(2):
                        z_ref[2 * half + c, rows, :] = (rot(grp(qn, c), rows) * Q_SCALE).astype(BF16)
                kn = _head_rmsnorm(grp(a, 4), bd[0:LANES, 0:LANES], gk_ref[...])
                z_ref[4, rows, :] = rot(kn, rows).astype(BF16)
                z_ref[5, rows, :] = grp(a, 5).astype(BF16)
                if not latent:
                    bk_ref[rc] = kn
                    bv_ref[rc] = grp(a, 5)

        @pl.when(j == 3)
        def _():
            for rc, rows, a in chunks():
                for c in range(GROUPS_PER_TILE):
                    v = grp(a, c)
                    z_ref[c, rows, :] = (v * Q_SCALE if c < 4 else v).astype(BF16)
                if not latent:
                    ck_ref[rc] = grp(a, 4, 2)

        @pl.when(j == 4)
        def _():
            for rc, rows, a in chunks():
                for c in range(GROUPS_PER_TILE):
                    z_ref[c, rows, :] = grp(a, c).astype(BF16)
                if not latent:
                    ck_ref[rc] = grp(a, 0, 2)
                    cv_ref[rc] = grp(a, 2, 4)

        @pl.when(j >= N_QKV_TILES)
        def _():
            for rc, rows, a in chunks():
                a = a + bg_ref[...]
                for c in range(GROUPS_PER_TILE):
                    gate_ref[c, rows, :] = jax.nn.sigmoid(grp(a, c)).astype(BF16)

    return kern


def _inproj(latent, l, x, mod, g_norm1, w_in, b_gate, gq, gk, bd, rope_cos, rope_sin, caches):
    n_row = N_TOK // ROW_TILE
    last = n_row - 1

    def row_block(j, i, first_tile, last_tile):
        return jnp.where(j < first_tile, 0, jnp.where(j <= last_tile, i, last))

    mod_idx = (lambda j, i: (1 + i, 0, 0)) if latent else (lambda j, i: (0, 0, 0))
    in_specs = [
        pl.BlockSpec((ROW_TILE, D_MODEL), lambda j, i: (row_block(j, i, 0, 0), 0)),
        pl.BlockSpec((1, 1, 6 * D_MODEL), mod_idx),
        pl.BlockSpec((None, 1, D_MODEL), lambda j, i: (l, 0, 0)),
        pl.BlockSpec((None, D_MODEL, COL_TILE), lambda j, i: (l, 0, j)),
        pl.BlockSpec((None, 1, COL_TILE), lambda j, i: (l, 0, jnp.maximum(j - N_QKV_TILES, 0))),
        pl.BlockSpec((None, 1, 2 * LANES), lambda j, i: (l, 0, 0)),
        pl.BlockSpec((None, 1, LANES), lambda j, i: (l, 0, 0)),
        pl.BlockSpec((2 * LANES, 2 * LANES), lambda j, i: (0, 0)),
    ]
    args = [x, mod, g_norm1, w_in, b_gate, gq, gk, bd]
    out_specs = [
        pl.BlockSpec((GROUPS_PER_TILE, ROW_TILE, LANES),
                     lambda j, i: (jnp.minimum(j, N_QKV_TILES - 1), row_block(j, i, 0, N_QKV_TILES - 1), 0)),
        pl.BlockSpec((GROUPS_PER_TILE, ROW_TILE, LANES),
                     lambda j, i: (jnp.maximum(j - N_QKV_TILES, 0), row_block(j, i, N_QKV_TILES, N_COL_TILES), 0)),
    ]
    out_shape = [jax.ShapeDtypeStruct((N_QKV_GROUPS, N_TOK, LANES), BF16),
                 jax.ShapeDtypeStruct((N_GATE_GROUPS, N_TOK, LANES), BF16)]
    aliases = {}
    if latent:
        in_specs += [pl.BlockSpec((DEC_SEQ, LANES), lambda j, i: (0, 0))] * 2
        args += [rope_cos, rope_sin]
    else:
        nb = ROW_TILE // SEQ
        tiles = ((0, 1, 256), (1, 1, 512), (2, 2, 128), (2, 2, 128), (3, 4, 256), (4, 4, 512))
        for k, (w, (t0, t1, bw)) in enumerate(zip(CACHE_WIDTHS, tiles)):
            if caches is not None:
                in_specs.append(pl.BlockSpec(memory_space=pl.ANY))
                args.append(caches[k])
                aliases[8 + k] = 2 + k

            def cache_idx(j, i, t0=t0, t1=t1):
                return (row_block(j, i, t0, t1), l, 0, jnp.where(j <= t0, 0, (t1 - t0)))

            if k == 1:
                out_specs.append(pl.BlockSpec((nb, None, SEQ, HA, 2 * DA),
                                              lambda j, i, f=cache_idx: f(j, i) + (0,)))
                out_shape.append(jax.ShapeDtypeStruct((BATCH, DEPTH, SEQ, HA, 2 * DA), F32))
                continue
            out_specs.append(pl.BlockSpec((nb, None, SEQ, bw), cache_idx))
            out_shape.append(jax.ShapeDtypeStruct((BATCH, DEPTH, SEQ, w), F32))
    return pl.pallas_call(
        _make_inproj_kernel(latent, len(aliases)),
        grid=(N_COL_TILES, n_row),
        in_specs=in_specs,
        out_specs=out_specs,
        out_shape=out_shape,
        scratch_shapes=[pltpu.VMEM((N_TOK, D_MODEL), BF16), pltpu.VMEM((D_MODEL, COL_TILE), BF16)],
        input_output_aliases=aliases,
        compiler_params=_params(("arbitrary", "arbitrary")),
        name="inproj_lat" if latent else "inproj_ctx",
    )(*args)


def _mask_head(q, head):
    qf = q.astype(F32)
    keep = _lane_head(qf.shape) == head
    return jnp.where(keep, qf, 0.0).astype(BF16)


def _stack_heads(q):
    return jnp.concatenate([_mask_head(q, 0), _mask_head(q, 1)], axis=0)


def _unstack_heads(o, rows):
    return jnp.where(_lane_head((rows, LANES)) == 1, o[rows:2 * rows], o[0:rows])


def _dup_head(kv, head):
    f = kv.astype(F32)
    r = pltpu.roll(f, HEAD_DIM, 1)
    return jnp.where(_lane_head(f.shape) == head, f, r).astype(BF16)


def _attend(qm, ks, vs):
    nt = (False,) * len(ks)
    return _attend_multi([(qm, ks, vs, None, nt, nt)])[0]


def _attend_multi(jobs):
    sss = []
    for qm, ks, _, biases, k_t, _ in jobs:
        ss = [jnp.dot(qm, k, preferred_element_type=F32) if t else
              lax.dot_general(qm, k, _NT, preferred_element_type=F32) for k, t in zip(ks, k_t)]
        if biases is not None:
            ss = [s if b is None else s + b for s, b in zip(ss, biases)]
        sss.append(ss)
    pss, dens = [], []
    for ss in sss:
        m = functools.reduce(jnp.maximum, [jnp.max(s, axis=-1, keepdims=True) for s in ss])
        ps = [jnp.exp2(s - m) for s in ss]
        dens.append(functools.reduce(jnp.add, [jnp.sum(p, axis=-1, keepdims=True) for p in ps]))
        pss.append(ps)
    outs = []
    for (_, _, vs, _, _, v_t), ps, den in zip(jobs, pss, dens):
        o = functools.reduce(jnp.add, [lax.dot_general(p.astype(BF16), v, _NT, preferred_element_type=F32) if t else
                                        jnp.dot(p.astype(BF16), v, preferred_element_type=F32)
                                        for p, v, t in zip(ps, vs, v_t)])
        outs.append(o * (1.0 / den))
    return outs


_CTX_T = (False, True)


def _pair_t(ref):
    return ref[...].reshape(2 * HEAD_DIM, ref.shape[-1]).astype(BF16)


def _diff_lambda(lam_ref, lam_init):
    lq = lam_ref[...]
    a = jnp.sum(lq[0:1] * lq[1:2], axis=-1, keepdims=True)
    b = jnp.sum(lq[2:3] * lq[3:4], axis=-1, keepdims=True)
    return jnp.exp(a) - jnp.exp(b) + lam_init


def _diff_combine(o1, o2, lam, gsub, lam_init):
    o = o1 - lam * o2
    ms = jnp.mean(o * o, axis=-1, keepdims=True)
    return (o * lax.rsqrt(ms + EPS) * gsub) * (1.0 - lam_init)


CTX_BATCH_PER_STEP = 2


def _make_attn_ctx_kernel(lam_init):
    def kern(z_ref, lam_ref, gsub_ref, oa_ref, ob_ref, oc_ref):
        lam = _diff_lambda(lam_ref, lam_init)
        gsub = gsub_ref[...]
        for bb in range(CTX_BATCH_PER_STEP):
            rows = pl.ds(bb * SEQ, SEQ)

            def z(g):
                return z_ref[g, rows, :]

            for vh in range(HA):
                hi = vh % 2
                v = z(G_VA + vh)
                o1 = _attend(_mask_head(z(G_QA + vh // 2), hi), [z(G_KA + vh // 2)], [v])
                o2 = _attend(_mask_head(z(G_QA + 2 + vh // 2), hi), [z(G_KA + 2 + vh // 2)], [v])
                oa_ref[vh, rows, :] = _diff_combine(o1, o2, lam, gsub, lam_init).astype(BF16)
            for g in range(KVB):
                kd = _dup_head(z(G_KB), g)
                vd = _dup_head(z(G_VB), g)
                for c in range(2 * g, 2 * g + 2):
                    o = _attend(_stack_heads(z(G_QB + c)), [kd], [vd])
                    ob_ref[c, rows, :] = _unstack_heads(o, SEQ).astype(BF16)
            for c in range(HC // 2):
                o = _attend(_stack_heads(z(G_QC + c)), [z(G_KC + c)], [z(G_VC + c)])
                oc_ref[c, rows, :] = _unstack_heads(o, SEQ).astype(BF16)
    return kern


def _attn_ctx(l, z, lambda_qk, g_subln):
    lam_init = 0.8 - 0.6 * math.exp(-0.3 * l)
    rows = CTX_BATCH_PER_STEP * SEQ
    o_spec = pl.BlockSpec((4, rows, LANES), lambda b: (0, b, 0))
    o_shape = jax.ShapeDtypeStruct((4, N_TOK, LANES), BF16)
    return pl.pallas_call(
        _make_attn_ctx_kernel(lam_init),
        grid=(BATCH // CTX_BATCH_PER_STEP,),
        in_specs=[pl.BlockSpec((N_QKV_GROUPS, rows, LANES), lambda b: (0, b, 0)),
                  pl.BlockSpec((None, 4, DA), lambda b: (l, 0, 0)),
                  pl.BlockSpec((None, 1, 2 * DA), lambda b: (l, 0, 0))],
        out_specs=[o_spec, o_spec, o_spec],
        out_shape=[o_shape, o_shape, o_shape],
        compiler_params=_params(("arbitrary",)),
        name="attn_ctx",
    )(z, lambda_qk, g_subln)


Q_BLK_GQA = 256
Q_BLK_DIFF = 512


def _make_attn_lat_a_kernel(lam_init):
    def kern(q1_ref, q2_ref, k1_ref, k2_ref, v_ref, ck1_ref, ck2_ref, cv_ref, lam_ref, gsub_ref, o_ref):
        vh = pl.program_id(1)
        hi = jnp.bitwise_and(vh, 1)
        lam = _diff_lambda(lam_ref, lam_init)
        gsub = gsub_ref[...]
        ck1 = _pair_t(ck1_ref)
        ck2 = _pair_t(ck2_ref)
        cv = cv_ref[:, vh, :].astype(BF16)
        k1, k2, v = k1_ref[0], k2_ref[0], v_ref[0]
        rows = [pl.ds(qb * Q_BLK_DIFF, Q_BLK_DIFF) for qb in range(DEC_SEQ // Q_BLK_DIFF)]
        jobs = []
        for r in rows:
            jobs.append((_mask_head(q1_ref[0, r, :], hi), [k1, ck1], [v, cv], None, _CTX_T, (False, False)))
            jobs.append((_mask_head(q2_ref[0, r, :], hi), [k2, ck2], [v, cv], None, _CTX_T, (False, False)))
        outs = _attend_multi(jobs)
        for t, r in enumerate(rows):
            o_ref[0, r, :] = _diff_combine(outs[2 * t], outs[2 * t + 1], lam, gsub, lam_init).astype(BF16)
    return kern


def _attn_lat_a(l, z, cache_k_t, cache_v, lambda_qk, g_subln):
    lam_init = 0.8 - 0.6 * math.exp(-0.3 * l)

    def zspec(fn):
        return pl.BlockSpec((1, DEC_SEQ, LANES), lambda b, h: (fn(h), b, 0))

    return pl.pallas_call(
        _make_attn_lat_a_kernel(lam_init),
        grid=(DEC_BATCH, HA),
        in_specs=[zspec(lambda h: G_QA + h // 2), zspec(lambda h: G_QA + 2 + h // 2),
                  zspec(lambda h: G_KA + h // 2), zspec(lambda h: G_KA + 2 + h // 2),
                  zspec(lambda h: G_VA + h),
                  pl.BlockSpec((None, None, 2, DA, PAST_LEN), lambda b, h: (b, l, h // 2, 0, 0)),
                  pl.BlockSpec((None, None, 2, DA, PAST_LEN), lambda b, h: (b, l, HA // 2 + h // 2, 0, 0)),
                  pl.BlockSpec((None, None, PAST_LEN, HA, 2 * DA), lambda b, h: (b, l, 0, 0, 0)),
                  pl.BlockSpec((None, 4, DA), lambda b, h: (l, 0, 0)),
                  pl.BlockSpec((None, 1, 2 * DA), lambda b, h: (l, 0, 0))],
        out_specs=pl.BlockSpec((1, DEC_SEQ, LANES), lambda b, h: (h, b, 0)),
        out_shape=jax.ShapeDtypeStruct((4, N_TOK, LANES), BF16),
        compiler_params=_params(("arbitrary", "arbitrary")),
        name="attn_lat_a",
    )(z, z, z, z, z, cache_k_t, cache_k_t, cache_v, lambda_qk, g_subln)


def _attn_lat_b_kernel(q_ref, k_ref, v_ref, ck_ref, cv_ref, o_ref):
    hi = lax.shift_right_logical(pl.program_id(1), 1)
    kd = _dup_head(k_ref[0], hi)
    vd = _dup_head(v_ref[0], hi)
    ck = ck_ref[hi].astype(BF16)
    cv = cv_ref[hi].astype(BF16)
    ckd = jnp.concatenate([ck, ck], axis=0)
    cvd = jnp.concatenate([cv, cv], axis=0)
    rows = [pl.ds(qb * Q_BLK_GQA, Q_BLK_GQA) for qb in range(DEC_SEQ // Q_BLK_GQA)]
    outs = _attend_multi([(_stack_heads(q_ref[0, r, :]), [kd, ckd], [vd, cvd], None, _CTX_T, _CTX_T)
                          for r in rows])
    for r, o in zip(rows, outs):
        o_ref[0, r, :] = _unstack_heads(o, Q_BLK_GQA).astype(BF16)


def _attn_lat_b(l, z, cache_k, cache_v):
    cspec = pl.BlockSpec((None, None, KVB, DB, PAST_LEN), lambda b, c: (b, l, 0, 0, 0))
    return pl.pallas_call(
        _attn_lat_b_kernel,
        grid=(DEC_BATCH, HB // 2),
        in_specs=[pl.BlockSpec((1, DEC_SEQ, LANES), lambda b, c: (G_QB + c, b, 0)),
                  pl.BlockSpec((1, DEC_SEQ, LANES), lambda b, c: (G_KB, b, 0)),
                  pl.BlockSpec((1, DEC_SEQ, LANES), lambda b, c: (G_VB, b, 0)),
                  cspec, cspec],
        out_specs=pl.BlockSpec((1, DEC_SEQ, LANES), lambda b, c: (c, b, 0)),
        out_shape=jax.ShapeDtypeStruct((4, N_TOK, LANES), BF16),
        compiler_params=_params(("arbitrary", "arbitrary")),
        name="attn_lat_b",
    )(z, z, z, cache_k, cache_v)


N_GRID_ROWS = DEC_SEQ // GRID_W
NA_KEYS = NA_ROWS * GRID_W


NA_GROUP = 4
NA_MAX_KEY_ROWS = 12
BIAS_TABLE_ROWS = 16


def _window_start(r):
    return min(max(r - NA_ROWS // 2, 0), N_GRID_ROWS - NA_ROWS)


def _group_keys(g):
    starts = [_window_start(r) for r in range(g * NA_GROUP, (g + 1) * NA_GROUP)]
    n = max(starts) + NA_ROWS - min(starts)
    n += n % 2
    return min(min(starts), N_GRID_ROWS - n), n


def _build_window_bias(tab_ref, bias_scr):
    qcol = lax.broadcasted_iota(jnp.int32, (GRID_W, LANES), 0)
    lane = lax.broadcasted_iota(jnp.int32, (GRID_W, LANES), 1)
    kcol = lane & (GRID_W - 1)
    c0 = jnp.clip(qcol - NA_COLS // 2, 0, GRID_W - NA_COLS)
    in_win = (kcol >= c0) & (kcol < c0 + NA_COLS)
    masks = {(True, True): in_win, (True, False): in_win & (lane < GRID_W), (False, True): in_win & (lane >= GRID_W)}
    for head in range(2):
        for r in range(N_GRID_ROWS):
            u0, n = _group_keys(r // NA_GROUP)
            rows = pl.ds(r * GRID_W, GRID_W)
            for m in range(n // 2):
                key_rows = (u0 + 2 * m, u0 + 2 * m + 1)
                valid = tuple(_window_start(r) <= kr < _window_start(r) + NA_ROWS for kr in key_rows)
                cols = pl.ds(m * LANES, LANES)
                if not any(valid):
                    bias_scr[head, rows, cols] = jnp.full((GRID_W, LANES), NEG, F32)
                    continue
                vec = None
                for half, (kr, ok) in enumerate(zip(key_rows, valid)):
                    if ok:
                        dr = kr - r + NA_ROWS - 1
                        part = tab_ref[head, pl.ds(half * BIAS_TABLE_ROWS + dr, 1), :]
                        vec = part if vec is None else vec + part
                toep = pltpu.roll(jnp.broadcast_to(vec, (GRID_W, LANES)), 0, 1, stride=1, stride_axis=0)
                bias_scr[head, rows, cols] = jnp.where(masks[valid], toep * LOG2E, NEG)


def _attn_lat_c_kernel(q_ref, k_ref, v_ref, ck_ref, cv_ref, tab_ref, o_ref, bias_scr):
    @pl.when(pl.program_id(1) == 0)
    def _():
        _build_window_bias(tab_ref, bias_scr)

    ck = _pair_t(ck_ref)
    cv = _pair_t(cv_ref)
    n_rows = NA_GROUP * GRID_W
    jobs, row_slices = [], []
    for g in range(N_GRID_ROWS // NA_GROUP):
        u0, n = _group_keys(g)
        rows = pl.ds(g * n_rows, n_rows)
        keys = pl.ds(u0 * GRID_W, n * GRID_W)
        bias = jnp.concatenate([bias_scr[0, rows, 0:n * GRID_W], bias_scr[1, rows, 0:n * GRID_W]], axis=0)
        jobs.append((_stack_heads(q_ref[0, rows, :]), [k_ref[0, keys, :], ck], [v_ref[0, keys, :], cv],
                     [bias, None], _CTX_T, _CTX_T))
        row_slices.append(rows)
    for rows, o in zip(row_slices, _attend_multi(jobs)):
        o_ref[0, rows, :] = _unstack_heads(o, n_rows).astype(BF16)


def _attn_lat_c(l, z, cache_k, cache_v, bias_rows):
    def zspec(g0):
        return pl.BlockSpec((1, DEC_SEQ, LANES), lambda c, b: (g0 + c, b, 0))

    cspec = pl.BlockSpec((None, None, 2, DC, PAST_LEN), lambda c, b: (b, l, c, 0, 0))
    return pl.pallas_call(
        _attn_lat_c_kernel,
        grid=(HC // 2, DEC_BATCH),
        in_specs=[zspec(G_QC), zspec(G_KC), zspec(G_VC), cspec, cspec,
                  pl.BlockSpec((None, 2, 2 * BIAS_TABLE_ROWS, LANES), lambda c, b: (l, c, 0, 0))],
        out_specs=pl.BlockSpec((1, DEC_SEQ, LANES), lambda c, b: (c, b, 0)),
        out_shape=jax.ShapeDtypeStruct((4, N_TOK, LANES), BF16),
        scratch_shapes=[pltpu.VMEM((2, DEC_SEQ, NA_MAX_KEY_ROWS * GRID_W), F32)],
        compiler_params=_params(("arbitrary", "arbitrary")),
        name="attn_lat_c",
    )(z, z, z, cache_k, cache_v, bias_rows)


MERGE_TILE = 512


def _merge_kernel(x_ref, oa_ref, ob_ref, oc_ref, gate_ref, mod_ref, wa_ref, wb_ref, wc_ref, wo_ref,
                  xo_ref, wbr_scr, wo_scr):
    @pl.when(pl.program_id(0) == 0)
    def _():
        wbr_scr[0] = wa_ref[...].astype(BF16)
        wbr_scr[1] = wb_ref[...].astype(BF16)
        wbr_scr[2] = wc_ref[...].astype(BF16)
        wo_scr[...] = wo_ref[...].astype(BF16)

    m = mod_ref[0]
    for rc in range(MERGE_TILE // ROW_CHUNK):
        rows = pl.ds(rc * ROW_CHUNK, ROW_CHUNK)
        y = None
        for k, o_ref in enumerate((oa_ref, ob_ref, oc_ref)):
            o = jnp.concatenate([o_ref[c, rows, :] for c in range(4)], axis=-1)
            p = jnp.dot(o, wbr_scr[k], preferred_element_type=F32)
            g = jnp.concatenate([gate_ref[8 * k + c, rows, :] for c in range(8)], axis=-1).astype(F32)
            y = g * p if y is None else y + g * p
        out = jnp.dot(y.astype(BF16), wo_scr[...], preferred_element_type=F32)
        xo_ref[rows, :] = x_ref[rows, :] + m[:, 2 * D_MODEL:3 * D_MODEL] * out


def _merge(latent, l, x, oa, ob, oc, gates, mod, w_a, w_b, w_c, w_o):
    tm = MERGE_TILE
    if latent:
        mod_idx = lambda i: (1 + (i * tm) // DEC_SEQ, 0, 0)
    else:
        mod_idx = lambda i: (0, 0, 0)
    o_spec = pl.BlockSpec((4, tm, LANES), lambda i: (0, i, 0))
    wbr_spec = pl.BlockSpec((None, 4 * LANES, D_MODEL), lambda i: (l, 0, 0))
    return pl.pallas_call(
        _merge_kernel,
        grid=(N_TOK // tm,),
        in_specs=[pl.BlockSpec((tm, D_MODEL), lambda i: (i, 0)),
                  o_spec, o_spec, o_spec,
                  pl.BlockSpec((N_GATE_GROUPS, tm, LANES), lambda i: (0, i, 0)),
                  pl.BlockSpec((1, 1, 6 * D_MODEL), mod_idx),
                  wbr_spec, wbr_spec, wbr_spec,
                  pl.BlockSpec((None, D_MODEL, D_MODEL), lambda i: (l, 0, 0))],
        out_specs=pl.BlockSpec((tm, D_MODEL), lambda i: (i, 0)),
        out_shape=jax.ShapeDtypeStruct((N_TOK, D_MODEL), F32),
        scratch_shapes=[pltpu.VMEM((3, 4 * LANES, D_MODEL), BF16), pltpu.VMEM((D_MODEL, D_MODEL), BF16)],
        compiler_params=_params(("arbitrary",)),
        name="merge_lat" if latent else "merge_ctx",
    )(x, oa, ob, oc, gates, mod, w_a, w_b, w_c, w_o)


FF_TILE = 1024
FF_CHUNK = 256


def _ffn_kernel(x_ref, mod_ref, g2_ref, w1_ref, w2_ref, gf_ref, xo_ref, acc_scr, h2_scr, *, final):
    f = pl.program_id(1)

    def run(first):
        w1 = w1_ref[...].astype(BF16)
        w2 = w2_ref[...].astype(BF16)
        m = mod_ref[0]
        for rc in range(ROW_TILE // FF_CHUNK):
            rows = pl.ds(rc * FF_CHUNK, FF_CHUNK)
            if first:
                x = x_ref[rows, :]
                ms = jnp.mean(x * x, axis=-1, keepdims=True)
                y = x * lax.rsqrt(ms + EPS) * g2_ref[...]
                h2 = (y * (1.0 + m[:, 4 * D_MODEL:5 * D_MODEL]) + m[:, 3 * D_MODEL:4 * D_MODEL]).astype(BF16)
                h2_scr[rows, :] = h2
            else:
                h2 = h2_scr[rows, :]
            u = jnp.dot(h2, w1, preferred_element_type=F32)
            u = jnp.square(jnp.maximum(u, 0.0)).astype(BF16)
            d = jnp.dot(u, w2, preferred_element_type=F32)
            if first:
                acc_scr[rows, :] = d
            else:
                acc_scr[rows, :] += d

    @pl.when(f == 0)
    def _():
        run(True)

    @pl.when(f > 0)
    def _():
        run(False)

    @pl.when(f == D_FF // FF_TILE - 1)
    def _():
        x = x_ref[...] + mod_ref[0][:, 5 * D_MODEL:6 * D_MODEL] * acc_scr[...]
        if final:
            ms = jnp.mean(x * x, axis=-1, keepdims=True)
            x = x * lax.rsqrt(ms + EPS) * gf_ref[...]
        xo_ref[...] = x


def _ffn(latent, l, x, mod, g_norm2, w1, w2, g_final):
    tm = ROW_TILE
    mod_idx = (lambda i, f: (1 + i, 0, 0)) if latent else (lambda i, f: (0, 0, 0))
    return pl.pallas_call(
        functools.partial(_ffn_kernel, final=(l == DEPTH - 1)),
        grid=(N_TOK // tm, D_FF // FF_TILE),
        in_specs=[pl.BlockSpec((tm, D_MODEL), lambda i, f: (i, 0)),
                  pl.BlockSpec((1, 1, 6 * D_MODEL), mod_idx),
                  pl.BlockSpec((None, 1, D_MODEL), lambda i, f: (l, 0, 0)),
                  pl.BlockSpec((None, D_MODEL, FF_TILE), lambda i, f: (l, 0, f)),
                  pl.BlockSpec((None, FF_TILE, D_MODEL), lambda i, f: (l, f, 0)),
                  pl.BlockSpec((1, D_MODEL), lambda i, f: (0, 0))],
        out_specs=pl.BlockSpec((tm, D_MODEL), lambda i, f: (i, 0)),
        out_shape=jax.ShapeDtypeStruct((N_TOK, D_MODEL), F32),
        scratch_shapes=[pltpu.VMEM((tm, D_MODEL), F32), pltpu.VMEM((tm, D_MODEL), BF16)],
        compiler_params=_params(("arbitrary", "arbitrary")),
        name="ffn_lat" if latent else "ffn_ctx",
    )(x, mod, g_norm2, w1, w2, g_final)


def _rope_tables():
    nf = HEAD_DIM // 4
    t = jnp.arange(DEC_SEQ)
    row = (t // GRID_W).astype(F32)
    col = (t % GRID_W).astype(F32)
    inv = ROPE_BASE ** (-jnp.arange(nf, dtype=F32) / nf)
    ar = row[:, None] * inv[None, :]
    ac = col[:, None] * inv[None, :]
    cos = jnp.concatenate([jnp.cos(ar), jnp.cos(ar), jnp.cos(ac), jnp.cos(ac)], axis=-1)
    sin = jnp.concatenate([-jnp.sin(ar), jnp.sin(ar), -jnp.sin(ac), jnp.sin(ac)], axis=-1)
    return jnp.tile(cos, (1, 2)), jnp.tile(sin, (1, 2))


def _packed_bias_rows(rel_bias):
    n = 2 * NA_ROWS - 1
    first = jnp.concatenate([rel_bias[..., NA_COLS - 1:], jnp.zeros((DEPTH, HC, n, LANES - (2 * NA_COLS - 1)), F32),
                             rel_bias[..., :NA_COLS - 1]], axis=-1)
    lo = GRID_W - NA_COLS + 1
    second = jnp.pad(rel_bias, ((0, 0), (0, 0), (0, 0), (lo, LANES - lo - (2 * NA_COLS - 1))))
    pad_rows = ((0, 0), (0, 0), (0, BIAS_TABLE_ROWS - n), (0, 0))
    return jnp.concatenate([jnp.pad(first, pad_rows), jnp.pad(second, pad_rows)], axis=2)


def kernel(x_prompt, x_sample, c, cache_a_k, cache_a_v, cache_b_k, cache_b_v, cache_c_k, cache_c_v, c_ctx, w_mod, b_mod, g_norm1, g_norm2, w_in, b_gate, lambda_qk, g_subln, g_qnorm, g_knorm, rel_bias, w_branch_a, w_branch_b, w_branch_c, w_out, w_ff1, w_ff2, g_final):
    xp = x_prompt.reshape(N_TOK, D_MODEL)
    xs = x_sample.reshape(N_TOK, D_MODEL)

    cvec = jnp.concatenate([c_ctx[None, :], c, jnp.zeros((3, D_MODEL), F32)], axis=0)
    mods = _modulation(cvec, w_mod, b_mod)

    rope_cos, rope_sin = _rope_tables()
    bias_rows = _packed_bias_rows(rel_bias)
    bd = jnp.kron(jnp.eye(2 * LANES // HEAD_DIM, dtype=F32),
                  jnp.full((HEAD_DIM, HEAD_DIM), 1.0 / HEAD_DIM, F32)).astype(BF16)
    gq = jnp.tile(g_qnorm, (1, 2 * LANES // DB)).reshape(DEPTH, 1, 2 * LANES)
    gk = jnp.tile(g_knorm, (1, LANES // DB)).reshape(DEPTH, 1, LANES)
    g1 = g_norm1.reshape(DEPTH, 1, D_MODEL)
    g2 = g_norm2.reshape(DEPTH, 1, D_MODEL)
    bg = b_gate.reshape(DEPTH, 1, 3 * D_MODEL)
    gsub = g_subln.reshape(DEPTH, 1, 2 * DA)
    gf = g_final.reshape(1, D_MODEL)


    def heads_t(cache):
        return cache.transpose(0, 1, 3, 4, 2)

    ctx_ak, ctx_bk, ctx_bv = heads_t(cache_a_k), heads_t(cache_b_k), heads_t(cache_b_v)
    ctx_ck, ctx_cv = heads_t(cache_c_k), heads_t(cache_c_v)

    new_caches = None

    for l in range(DEPTH):
        mod = mods[l].reshape(8, 1, 6 * D_MODEL)

        outs = _inproj(False, l, xp, mod, g1, w_in, bg, gq, gk, bd, None, None, new_caches)
        z, gates, new_caches = outs[0], outs[1], list(outs[2:])
        oa, ob, oc = _attn_ctx(l, z, lambda_qk, gsub)
        xp = _merge(False, l, xp, oa, ob, oc, gates, mod, w_branch_a, w_branch_b, w_branch_c, w_out)
        xp = _ffn(False, l, xp, mod, g2, w_ff1, w_ff2, gf)

        z, gates = _inproj(True, l, xs, mod, g1, w_in, bg, gq, gk, bd, rope_cos, rope_sin, None)
        oa = _attn_lat_a(l, z, ctx_ak, cache_a_v, lambda_qk, gsub)
        ob = _attn_lat_b(l, z, ctx_bk, ctx_bv)
        oc = _attn_lat_c(l, z, ctx_ck, ctx_cv, bias_rows)
        xs = _merge(True, l, xs, oa, ob, oc, gates, mod, w_branch_a, w_branch_b, w_branch_c, w_out)
        xs = _ffn(True, l, xs, mod, g2, w_ff1, w_ff2, gf)

    y_prompt = xp.reshape(BATCH, SEQ, D_MODEL)
    y_sample = xs.reshape(DEC_BATCH, DEC_SEQ, D_MODEL)
    ak, av, bk, bv, ck, cv = new_caches
    return (y_prompt, y_sample,
            ak.reshape(BATCH, DEPTH, SEQ, 2 * HA, DA), av.reshape(BATCH, DEPTH, SEQ, HA, 2 * DA),
            bk.reshape(BATCH, DEPTH, SEQ, KVB, DB), bv.reshape(BATCH, DEPTH, SEQ, KVB, DB),
            ck.reshape(BATCH, DEPTH, SEQ, HC, DC), cv.reshape(BATCH, DEPTH, SEQ, HC, DC))
```

```python
import functools
import math

import jax
import jax.numpy as jnp
from jax import lax
from jax.experimental import pallas as pl
from jax.experimental.pallas import tpu as pltpu

D_MODEL = 1024
BATCH = 16
SEQ = 256
DEPTH = 4
DEC_BATCH = 4
DEC_SEQ = 1024
PAST_LEN = 256
GRID_W = 64
HA, DA = 4, 64
HB, KVB, DB = 8, 2, 64
HC, DC = 8, 64
NA_ROWS, NA_COLS = 8, 16
D_FF = 4 * D_MODEL
ROPE_BASE = 10000.0
EPS = 1e-6
NEG = -1e30
HEAD_DIM = 64
LOG2E = math.log2(math.e)
Q_SCALE = HEAD_DIM ** -0.5 * LOG2E

F32 = jnp.float32
BF16 = jnp.bfloat16

LANES = 128
N_QKV_GROUPS = 30
N_GATE_GROUPS = 24
COL_TILE = 768
GROUPS_PER_TILE = COL_TILE // LANES
N_QKV_TILES = N_QKV_GROUPS // GROUPS_PER_TILE
N_COL_TILES = N_QKV_TILES + N_GATE_GROUPS // GROUPS_PER_TILE
ROW_TILE = 1024
ROW_CHUNK = 256
N_TOK = BATCH * SEQ
VMEM_LIMIT_V7X = 58 * 1024 * 1024

G_QA, G_KA, G_VA, G_QB, G_KB, G_VB, G_QC, G_KC, G_VC = 0, 4, 8, 12, 16, 17, 18, 22, 26
CACHE_WIDTHS = (2 * HA * DA, HA * 2 * DA, KVB * DB, KVB * DB, HC * DC, HC * DC)

_NT = (((1,), (1,)), ((), ()))


def _params(sem, vmem=VMEM_LIMIT_V7X):
    return pltpu.CompilerParams(dimension_semantics=sem, vmem_limit_bytes=vmem)


def _lane_head(shape):
    return lax.shift_right_logical(lax.broadcasted_iota(jnp.int32, shape, len(shape) - 1), 6)


def _mod_kernel(c_ref, w_ref, b_ref, o_ref):
    c = c_ref[...]
    s = (c * jax.nn.sigmoid(c)).astype(BF16)
    o_ref[...] = jnp.dot(s, w_ref[...].astype(BF16), preferred_element_type=F32) + b_ref[...]


def _modulation(cvec, w_mod, b_mod):
    tn = 1536
    n6 = 6 * D_MODEL
    return pl.pallas_call(
        _mod_kernel,
        grid=(DEPTH, n6 // tn),
        in_specs=[pl.BlockSpec((8, D_MODEL), lambda l, n: (0, 0)),
                  pl.BlockSpec((None, D_MODEL, tn), lambda l, n: (l, 0, n)),
                  pl.BlockSpec((None, 1, tn), lambda l, n: (l, 0, n))],
        out_specs=pl.BlockSpec((None, 8, tn), lambda l, n: (l, 0, n)),
        out_shape=jax.ShapeDtypeStruct((DEPTH, 8, n6), F32),
        compiler_params=_params(("arbitrary", "arbitrary")),
        name="modulation",
    )(cvec, w_mod, b_mod.reshape(DEPTH, 1, n6))


def _rope(v, cos, sin):
    first = (lax.broadcasted_iota(jnp.int32, v.shape, 1) & 16) == 0
    partner = jnp.where(first, pltpu.roll(v, LANES - 16, 1), pltpu.roll(v, 16, 1))
    return v * cos + partner * sin


def _head_rmsnorm(v, bd, g):
    msq = jnp.dot((v * v).astype(BF16), bd, preferred_element_type=F32)
    return v * lax.rsqrt(msq + EPS) * g


def _make_inproj_kernel(latent, n_aliased):
    def kern(*refs):
        if latent:
            (x_ref, mod_ref, g1_ref, w_ref, bg_ref, gq_ref, gk_ref, bd_ref, cos_ref, sin_ref,
             z_ref, gate_ref, h_scr, wbf_scr) = refs
        else:
            (x_ref, mod_ref, g1_ref, w_ref, bg_ref, gq_ref, gk_ref, bd_ref) = refs[:8]
            (z_ref, gate_ref, ak_ref, av_ref, bk_ref, bv_ref, ck_ref, cv_ref,
             h_scr, wbf_scr) = refs[8 + n_aliased:]
        j = pl.program_id(0)
        i = pl.program_id(1)

        @pl.when(i == 0)
        def _():
            wbf_scr[...] = w_ref[...].astype(BF16)

        def chunks(first=False):
            for rc in range(ROW_TILE // ROW_CHUNK):
                rows = pl.ds(rc * ROW_CHUNK, ROW_CHUNK)
                tok = pl.ds(pl.multiple_of(i * ROW_TILE + rc * ROW_CHUNK, ROW_CHUNK), ROW_CHUNK)
                if first:
                    x = x_ref[rows, :]
                    ms = jnp.mean(x * x, axis=-1, keepdims=True)
                    y = x * lax.rsqrt(ms + EPS) * g1_ref[...]
                    m = mod_ref[0]
                    h = (y * (1.0 + m[:, D_MODEL:2 * D_MODEL]) + m[:, 0:D_MODEL]).astype(BF16)
                    h_scr[tok, :] = h
                else:
                    h = h_scr[tok, :]
                yield rc, rows, jnp.dot(h, wbf_scr[...], preferred_element_type=F32)

        def grp(a, c, n=1):
            return a[:, c * LANES:(c + n) * LANES]

        def rot(v, rows):
            return _rope(v, cos_ref[rows, :], sin_ref[rows, :]) if latent else v

        @pl.when(j == 0)
        def _():
            for rc, rows, a in chunks(first=True):
                for c in range(GROUPS_PER_TILE):
                    v = rot(grp(a, c), rows)
                    z_ref[c, rows, :] = (v * Q_SCALE if c < 4 else v).astype(BF16)
                if not latent:
                    ak_ref[rc] = grp(a, 4, 2)

        @pl.when(j == 1)
        def _():
            for rc, rows, a in chunks():
                for c in range(2):
                    z_ref[c, rows, :] = rot(grp(a, c), rows).astype(BF16)
                for c in range(2, GROUPS_PER_TILE):
                    z_ref[c, rows, :] = grp(a, c).astype(BF16)
                if not latent:
                    ak_ref[rc] = grp(a, 0, 2)
                    for hd in range(HA):
                        av_ref[rc, :, hd, :] = grp(a, 2 + hd)

        @pl.when(j == 2)
        def _():
            bd = bd_ref[...]
            for rc, rows, a in chunks():
                for half in range(2):
                    qn = _head_rmsnorm(grp(a, 2 * half, 2), bd, gq_ref[...])
                    for c in range(2):
                        z_ref[2 * half + c, rows, :] = (rot(grp(qn, c), rows) * Q_SCALE).astype(BF16)
                kn = _head_rmsnorm(grp(a, 4), bd[0:LANES, 0:LANES], gk_ref[...])
                z_ref[4, rows, :] = rot(kn, rows).astype(BF16)
                z_ref[5, rows, :] = grp(a, 5).astype(BF16)
                if not latent:
                    bk_ref[rc] = kn
                    bv_ref[rc] = grp(a, 5)

        @pl.when(j == 3)
        def _():
            for rc, rows, a in chunks():
                for c in range(GROUPS_PER_TILE):
                    v = grp(a, c)
                    z_ref[c, rows, :] = (v * Q_SCALE if c < 4 else v).astype(BF16)
                if not latent:
                    ck_ref[rc] = grp(a, 4, 2)

        @pl.when(j == 4)
        def _():
            for rc, rows, a in chunks():
                for c in range(GROUPS_PER_TILE):
                    z_ref[c, rows, :] = grp(a, c).astype(BF16)
                if not latent:
                    ck_ref[rc] = grp(a, 0, 2)
                    cv_ref[rc] = grp(a, 2, 4)

        @pl.when(j >= N_QKV_TILES)
        def _():
            for rc, rows, a in chunks():
                a = a + bg_ref[...]
                for c in range(GROUPS_PER_TILE):
                    gate_ref[c, rows, :] = jax.nn.sigmoid(grp(a, c)).astype(BF16)

    return kern


def _inproj(latent, l, x, mod, g_norm1, w_in, b_gate, gq, gk, bd, rope_cos, rope_sin, caches):
    n_row = N_TOK // ROW_TILE
    last = n_row - 1

    def row_block(j, i, first_tile, last_tile):
        return jnp.where(j < first_tile, 0, jnp.where(j <= last_tile, i, last))

    mod_idx = (lambda j, i: (1 + i, 0, 0)) if latent else (lambda j, i: (0, 0, 0))
    in_specs = [
        pl.BlockSpec((ROW_TILE, D_MODEL), lambda j, i: (row_block(j, i, 0, 0), 0)),
        pl.BlockSpec((1, 1, 6 * D_MODEL), mod_idx),
        pl.BlockSpec((None, 1, D_MODEL), lambda j, i: (l, 0, 0)),
        pl.BlockSpec((None, D_MODEL, COL_TILE), lambda j, i: (l, 0, j)),
        pl.BlockSpec((None, 1, COL_TILE), lambda j, i: (l, 0, jnp.maximum(j - N_QKV_TILES, 0))),
        pl.BlockSpec((None, 1, 2 * LANES), lambda j, i: (l, 0, 0)),
        pl.BlockSpec((None, 1, LANES), lambda j, i: (l, 0, 0)),
        pl.BlockSpec((2 * LANES, 2 * LANES), lambda j, i: (0, 0)),
    ]
    args = [x, mod, g_norm1, w_in, b_gate, gq, gk, bd]
    out_specs = [
        pl.BlockSpec((GROUPS_PER_TILE, ROW_TILE, LANES),
                     lambda j, i: (jnp.minimum(j, N_QKV_TILES - 1), row_block(j, i, 0, N_QKV_TILES - 1), 0)),
        pl.BlockSpec((GROUPS_PER_TILE, ROW_TILE, LANES),
                     lambda j, i: (jnp.maximum(j - N_QKV_TILES, 0), row_block(j, i, N_QKV_TILES, N_COL_TILES), 0)),
    ]
    out_shape = [jax.ShapeDtypeStruct((N_QKV_GROUPS, N_TOK, LANES), BF16),
                 jax.ShapeDtypeStruct((N_GATE_GROUPS, N_TOK, LANES), BF16)]
    aliases = {}
    if latent:
        in_specs += [pl.BlockSpec((DEC_SEQ, LANES), lambda j, i: (0, 0))] * 2
        args += [rope_cos, rope_sin]
    else:
        nb = ROW_TILE // SEQ
        tiles = ((0, 1, 256), (1, 1, 512), (2, 2, 128), (2, 2, 128), (3, 4, 256), (4, 4, 512))
        for k, (w, (t0, t1, bw)) in enumerate(zip(CACHE_WIDTHS, tiles)):
            if caches is not None:
                in_specs.append(pl.BlockSpec(memory_space=pl.ANY))
                args.append(caches[k])
                aliases[8 + k] = 2 + k

            def cache_idx(j, i, t0=t0, t1=t1):
                return (row_block(j, i, t0, t1), l, 0, jnp.where(j <= t0, 0, (t1 - t0)))

            if k == 1:
                out_specs.append(pl.BlockSpec((nb, None, SEQ, HA, 2 * DA),
                                              lambda j, i, f=cache_idx: f(j, i) + (0,)))
                out_shape.append(jax.ShapeDtypeStruct((BATCH, DEPTH, SEQ, HA, 2 * DA), F32))
                continue
            out_specs.append(pl.BlockSpec((nb, None, SEQ, bw), cache_idx))
            out_shape.append(jax.ShapeDtypeStruct((BATCH, DEPTH, SEQ, w), F32))
    return pl.pallas_call(
        _make_inproj_kernel(latent, len(aliases)),
        grid=(N_COL_TILES, n_row),
        in_specs=in_specs,
        out_specs=out_specs,
        out_shape=out_shape,
        scratch_shapes=[pltpu.VMEM((N_TOK, D_MODEL), BF16), pltpu.VMEM((D_MODEL, COL_TILE), BF16)],
        input_output_aliases=aliases,
        compiler_params=_params(("arbitrary", "arbitrary")),
        name="inproj_lat" if latent else "inproj_ctx",
    )(*args)


def _mask_head(q, head):
    qf = q.astype(F32)
    keep = _lane_head(qf.shape) == head
    return jnp.where(keep, qf, 0.0).astype(BF16)


def _stack_heads(q):
    return jnp.concatenate([_mask_head(q, 0), _mask_head(q, 1)], axis=0)


def _unstack_heads(o, rows):
    return jnp.where(_lane_head((rows, LANES)) == 1, o[rows:2 * rows], o[0:rows])


def _dup_head(kv, head):
    f = kv.astype(F32)
    r = pltpu.roll(f, HEAD_DIM, 1)
    return jnp.where(_lane_head(f.shape) == head, f, r).astype(BF16)


def _attend(qm, ks, vs):
    nt = (False,) * len(ks)
    return _attend_multi([(qm, ks, vs, None, nt, nt)])[0]


def _attend_multi(jobs):
    sss = []
    for qm, ks, _, biases, k_t, _ in jobs:
        ss = [jnp.dot(qm, k, preferred_element_type=F32) if t else
              lax.dot_general(qm, k, _NT, preferred_element_type=F32) for k, t in zip(ks, k_t)]
        if biases is not None:
            ss = [s if b is None else s + b for s, b in zip(ss, biases)]
        sss.append(ss)
    pss, dens = [], []
    for ss in sss:
        m = functools.reduce(jnp.maximum, [jnp.max(s, axis=-1, keepdims=True) for s in ss])
        ps = [jnp.exp2(s - m) for s in ss]
        dens.append(functools.reduce(jnp.add, [jnp.sum(p, axis=-1, keepdims=True) for p in ps]))
        pss.append(ps)
    outs = []
    for (_, _, vs, _, _, v_t), ps, den in zip(jobs, pss, dens):
        o = functools.reduce(jnp.add, [lax.dot_general(p.astype(BF16), v, _NT, preferred_element_type=F32) if t else
                                        jnp.dot(p.astype(BF16), v, preferred_element_type=F32)
                                        for p, v, t in zip(ps, vs, v_t)])
        outs.append(o * (1.0 / den))
    return outs


_CTX_T = (False, True)


def _pair_t(ref):
    return ref[...].reshape(2 * HEAD_DIM, ref.shape[-1]).astype(BF16)


def _diff_lambda(lam_ref, lam_init):
    lq = lam_ref[...]
    a = jnp.sum(lq[0:1] * lq[1:2], axis=-1, keepdims=True)
    b = jnp.sum(lq[2:3] * lq[3:4], axis=-1, keepdims=True)
    return jnp.exp(a) - jnp.exp(b) + lam_init


def _diff_combine(o1, o2, lam, gsub, lam_init):
    o = o1 - lam * o2
    ms = jnp.mean(o * o, axis=-1, keepdims=True)
    return (o * lax.rsqrt(ms + EPS) * gsub) * (1.0 - lam_init)


CTX_BATCH_PER_STEP = 2


def _make_attn_ctx_kernel(lam_init):
    def kern(z_ref, lam_ref, gsub_ref, oa_ref, ob_ref, oc_ref):
        lam = _diff_lambda(lam_ref, lam_init)
        gsub = gsub_ref[...]
        for bb in range(CTX_BATCH_PER_STEP):
            rows = pl.ds(bb * SEQ, SEQ)

            def z(g):
                return z_ref[g, rows, :]

            for vh in range(HA):
                hi = vh % 2
                v = z(G_VA + vh)
                o1 = _attend(_mask_head(z(G_QA + vh // 2), hi), [z(G_KA + vh // 2)], [v])
                o2 = _attend(_mask_head(z(G_QA + 2 + vh // 2), hi), [z(G_KA + 2 + vh // 2)], [v])
                oa_ref[vh, rows, :] = _diff_combine(o1, o2, lam, gsub, lam_init).astype(BF16)
            for g in range(KVB):
                kd = _dup_head(z(G_KB), g)
                vd = _dup_head(z(G_VB), g)
                for c in range(2 * g, 2 * g + 2):
                    o = _attend(_stack_heads(z(G_QB + c)), [kd], [vd])
                    ob_ref[c, rows, :] = _unstack_heads(o, SEQ).astype(BF16)
            for c in range(HC // 2):
                o = _attend(_stack_heads(z(G_QC + c)), [z(G_KC + c)], [z(G_VC + c)])
                oc_ref[c, rows, :] = _unstack_heads(o, SEQ).astype(BF16)
    return kern


def _attn_ctx(l, z, lambda_qk, g_subln):
    lam_init = 0.8 - 0.6 * math.exp(-0.3 * l)
    rows = CTX_BATCH_PER_STEP * SEQ
    o_spec = pl.BlockSpec((4, rows, LANES), lambda b: (0, b, 0))
    o_shape = jax.ShapeDtypeStruct((4, N_TOK, LANES), BF16)
    return pl.pallas_call(
        _make_attn_ctx_kernel(lam_init),
        grid=(BATCH // CTX_BATCH_PER_STEP,),
        in_specs=[pl.BlockSpec((N_QKV_GROUPS, rows, LANES), lambda b: (0, b, 0)),
                  pl.BlockSpec((None, 4, DA), lambda b: (l, 0, 0)),
                  pl.BlockSpec((None, 1, 2 * DA), lambda b: (l, 0, 0))],
        out_specs=[o_spec, o_spec, o_spec],
        out_shape=[o_shape, o_shape, o_shape],
        compiler_params=_params(("arbitrary",)),
        name="attn_ctx",
    )(z, lambda_qk, g_subln)


Q_BLK_GQA = 512
Q_BLK_DIFF = 1024


def _make_attn_lat_a_kernel(lam_init):
    def kern(q1_ref, q2_ref, k1_ref, k2_ref, v_ref, ck1_ref, ck2_ref, cv_ref, lam_ref, gsub_ref, o_ref):
        vh = pl.program_id(1)
        hi = jnp.bitwise_and(vh, 1)
        lam = _diff_lambda(lam_ref, lam_init)
        gsub = gsub_ref[...]
        ck1 = _pair_t(ck1_ref)
        ck2 = _pair_t(ck2_ref)
        cv = cv_ref[:, vh, :].astype(BF16)
        k1, k2, v = k1_ref[0], k2_ref[0], v_ref[0]
        rows = [pl.ds(qb * Q_BLK_DIFF, Q_BLK_DIFF) for qb in range(DEC_SEQ // Q_BLK_DIFF)]
        jobs = []
        for r in rows:
            jobs.append((_mask_head(q1_ref[0, r, :], hi), [k1, ck1], [v, cv], None, _CTX_T, (False, False)))
            jobs.append((_mask_head(q2_ref[0, r, :], hi), [k2, ck2], [v, cv], None, _CTX_T, (False, False)))
        outs = _attend_multi(jobs)
        for t, r in enumerate(rows):
            o_ref[0, r, :] = _diff_combine(outs[2 * t], outs[2 * t + 1], lam, gsub, lam_init).astype(BF16)
    return kern


def _attn_lat_a(l, z, cache_k_t, cache_v, lambda_qk, g_subln):
    lam_init = 0.8 - 0.6 * math.exp(-0.3 * l)

    def zspec(fn):
        return pl.BlockSpec((1, DEC_SEQ, LANES), lambda b, h: (fn(h), b, 0))

    return pl.pallas_call(
        _make_attn_lat_a_kernel(lam_init),
        grid=(DEC_BATCH, HA),
        in_specs=[zspec(lambda h: G_QA + h // 2), zspec(lambda h: G_QA + 2 + h // 2),
                  zspec(lambda h: G_KA + h // 2), zspec(lambda h: G_KA + 2 + h // 2),
                  zspec(lambda h: G_VA + h),
                  pl.BlockSpec((None, None, 2, DA, PAST_LEN), lambda b, h: (b, l, h // 2, 0, 0)),
                  pl.BlockSpec((None, None, 2, DA, PAST_LEN), lambda b, h: (b, l, HA // 2 + h // 2, 0, 0)),
                  pl.BlockSpec((None, None, PAST_LEN, HA, 2 * DA), lambda b, h: (b, l, 0, 0, 0)),
                  pl.BlockSpec((None, 4, DA), lambda b, h: (l, 0, 0)),
                  pl.BlockSpec((None, 1, 2 * DA), lambda b, h: (l, 0, 0))],
        out_specs=pl.BlockSpec((1, DEC_SEQ, LANES), lambda b, h: (h, b, 0)),
        out_shape=jax.ShapeDtypeStruct((4, N_TOK, LANES), BF16),
        compiler_params=_params(("arbitrary", "arbitrary")),
        name="attn_lat_a",
    )(z, z, z, z, z, cache_k_t, cache_k_t, cache_v, lambda_qk, g_subln)


def _attn_lat_b_kernel(q_ref, k_ref, v_ref, ck_ref, cv_ref, o_ref):
    hi = lax.shift_right_logical(pl.program_id(1), 1)
    kd = _dup_head(k_ref[0], hi)
    vd = _dup_head(v_ref[0], hi)
    ck = ck_ref[hi].astype(BF16)
    cv = cv_ref[hi].astype(BF16)
    ckd = jnp.concatenate([ck, ck], axis=0)
    cvd = jnp.concatenate([cv, cv], axis=0)
    rows = [pl.ds(qb * Q_BLK_GQA, Q_BLK_GQA) for qb in range(DEC_SEQ // Q_BLK_GQA)]
    outs = _attend_multi([(_stack_heads(q_ref[0, r, :]), [kd, ckd], [vd, cvd], None, _CTX_T, _CTX_T)
                          for r in rows])
    for r, o in zip(rows, outs):
        o_ref[0, r, :] = _unstack_heads(o, Q_BLK_GQA).astype(BF16)


def _attn_lat_b(l, z, cache_k, cache_v):
    cspec = pl.BlockSpec((None, None, KVB, DB, PAST_LEN), lambda b, c: (b, l, 0, 0, 0))
    return pl.pallas_call(
        _attn_lat_b_kernel,
        grid=(DEC_BATCH, HB // 2),
        in_specs=[pl.BlockSpec((1, DEC_SEQ, LANES), lambda b, c: (G_QB + c, b, 0)),
                  pl.BlockSpec((1, DEC_SEQ, LANES), lambda b, c: (G_KB, b, 0)),
                  pl.BlockSpec((1, DEC_SEQ, LANES), lambda b, c: (G_VB, b, 0)),
                  cspec, cspec],
        out_specs=pl.BlockSpec((1, DEC_SEQ, LANES), lambda b, c: (c, b, 0)),
        out_shape=jax.ShapeDtypeStruct((4, N_TOK, LANES), BF16),
        compiler_params=_params(("arbitrary", "arbitrary")),
        name="attn_lat_b",
    )(z, z, z, cache_k, cache_v)


N_GRID_ROWS = DEC_SEQ // GRID_W
NA_KEYS = NA_ROWS * GRID_W


NA_GROUP = 4
NA_MAX_KEY_ROWS = 12
BIAS_TABLE_ROWS = 16


def _window_start(r):
    return min(max(r - NA_ROWS // 2, 0), N_GRID_ROWS - NA_ROWS)


def _group_keys(g):
    starts = [_window_start(r) for r in range(g * NA_GROUP, (g + 1) * NA_GROUP)]
    n = max(starts) + NA_ROWS - min(starts)
    n += n % 2
    return min(min(starts), N_GRID_ROWS - n), n


def _build_window_bias(tab_ref, bias_scr):
    qcol = lax.broadcasted_iota(jnp.int32, (GRID_W, LANES), 0)
    lane = lax.broadcasted_iota(jnp.int32, (GRID_W, LANES), 1)
    kcol = lane & (GRID_W - 1)
    c0 = jnp.clip(qcol - NA_COLS // 2, 0, GRID_W - NA_COLS)
    in_win = (kcol >= c0) & (kcol < c0 + NA_COLS)
    masks = {(True, True): in_win, (True, False): in_win & (lane < GRID_W), (False, True): in_win & (lane >= GRID_W)}
    for head in range(2):
        for r in range(N_GRID_ROWS):
            u0, n = _group_keys(r // NA_GROUP)
            rows = pl.ds(r * GRID_W, GRID_W)
            for m in range(n // 2):
                key_rows = (u0 + 2 * m, u0 + 2 * m + 1)
                valid = tuple(_window_start(r) <= kr < _window_start(r) + NA_ROWS for kr in key_rows)
                cols = pl.ds(m * LANES, LANES)
                if not any(valid):
                    bias_scr[head, rows, cols] = jnp.full((GRID_W, LANES), NEG, F32)
                    continue
                vec = None
                for half, (kr, ok) in enumerate(zip(key_rows, valid)):
                    if ok:
                        dr = kr - r + NA_ROWS - 1
                        part = tab_ref[head, pl.ds(half * BIAS_TABLE_ROWS + dr, 1), :]
                        vec = part if vec is None else vec + part
                toep = pltpu.roll(jnp.broadcast_to(vec, (GRID_W, LANES)), 0, 1, stride=1, stride_axis=0)
                bias_scr[head, rows, cols] = jnp.where(masks[valid], toep * LOG2E, NEG)


def _attn_lat_c_kernel(q_ref, k_ref, v_ref, ck_ref, cv_ref, tab_ref, o_ref, bias_scr):
    @pl.when(pl.program_id(1) == 0)
    def _():
        _build_window_bias(tab_ref, bias_scr)

    ck = _pair_t(ck_ref)
    cv = _pair_t(cv_ref)
    n_rows = NA_GROUP * GRID_W
    jobs, row_slices = [], []
    for g in range(N_GRID_ROWS // NA_GROUP):
        u0, n = _group_keys(g)
        rows = pl.ds(g * n_rows, n_rows)
        keys = pl.ds(u0 * GRID_W, n * GRID_W)
        bias = jnp.concatenate([bias_scr[0, rows, 0:n * GRID_W], bias_scr[1, rows, 0:n * GRID_W]], axis=0)
        jobs.append((_stack_heads(q_ref[0, rows, :]), [k_ref[0, keys, :], ck], [v_ref[0, keys, :], cv],
                     [bias, None], _CTX_T, _CTX_T))
        row_slices.append(rows)
    for rows, o in zip(row_slices, _attend_multi(jobs)):
        o_ref[0, rows, :] = _unstack_heads(o, n_rows).astype(BF16)


def _attn_lat_c(l, z, cache_k, cache_v, bias_rows):
    def zspec(g0):
        return pl.BlockSpec((1, DEC_SEQ, LANES), lambda c, b: (g0 + c, b, 0))

    cspec = pl.BlockSpec((None, None, 2, DC, PAST_LEN), lambda c, b: (b, l, c, 0, 0))
    return pl.pallas_call(
        _attn_lat_c_kernel,
        grid=(HC // 2, DEC_BATCH),
        in_specs=[zspec(G_QC), zspec(G_KC), zspec(G_VC), cspec, cspec,
                  pl.BlockSpec((None, 2, 2 * BIAS_TABLE_ROWS, LANES), lambda c, b: (l, c, 0, 0))],
        out_specs=pl.BlockSpec((1, DEC_SEQ, LANES), lambda c, b: (c, b, 0)),
        out_shape=jax.ShapeDtypeStruct((4, N_TOK, LANES), BF16),
        scratch_shapes=[pltpu.VMEM((2, DEC_SEQ, NA_MAX_KEY_ROWS * GRID_W), F32)],
        compiler_params=_params(("arbitrary", "arbitrary")),
        name="attn_lat_c",
    )(z, z, z, cache_k, cache_v, bias_rows)


MERGE_TILE = 512


def _merge_kernel(x_ref, oa_ref, ob_ref, oc_ref, gate_ref, mod_ref, wa_ref, wb_ref, wc_ref, wo_ref,
                  xo_ref, wbr_scr, wo_scr):
    @pl.when(pl.program_id(0) == 0)
    def _():
        wbr_scr[0] = wa_ref[...].astype(BF16)
        wbr_scr[1] = wb_ref[...].astype(BF16)
        wbr_scr[2] = wc_ref[...].astype(BF16)
        wo_scr[...] = wo_ref[...].astype(BF16)

    m = mod_ref[0]
    for rc in range(MERGE_TILE // ROW_CHUNK):
        rows = pl.ds(rc * ROW_CHUNK, ROW_CHUNK)
        y = None
        for k, o_ref in enumerate((oa_ref, ob_ref, oc_ref)):
            o = jnp.concatenate([o_ref[c, rows, :] for c in range(4)], axis=-1)
            p = jnp.dot(o, wbr_scr[k], preferred_element_type=F32)
            g = jnp.concatenate([gate_ref[8 * k + c, rows, :] for c in range(8)], axis=-1).astype(F32)
            y = g * p if y is None else y + g * p
        out = jnp.dot(y.astype(BF16), wo_scr[...], preferred_element_type=F32)
        xo_ref[rows, :] = x_ref[rows, :] + m[:, 2 * D_MODEL:3 * D_MODEL] * out


def _merge(latent, l, x, oa, ob, oc, gates, mod, w_a, w_b, w_c, w_o):
    tm = MERGE_TILE
    if latent:
        mod_idx = lambda i: (1 + (i * tm) // DEC_SEQ, 0, 0)
    else:
        mod_idx = lambda i: (0, 0, 0)
    o_spec = pl.BlockSpec((4, tm, LANES), lambda i: (0, i, 0))
    wbr_spec = pl.BlockSpec((None, 4 * LANES, D_MODEL), lambda i: (l, 0, 0))
    return pl.pallas_call(
        _merge_kernel,
        grid=(N_TOK // tm,),
        in_specs=[pl.BlockSpec((tm, D_MODEL), lambda i: (i, 0)),
                  o_spec, o_spec, o_spec,
                  pl.BlockSpec((N_GATE_GROUPS, tm, LANES), lambda i: (0, i, 0)),
                  pl.BlockSpec((1, 1, 6 * D_MODEL), mod_idx),
                  wbr_spec, wbr_spec, wbr_spec,
                  pl.BlockSpec((None, D_MODEL, D_MODEL), lambda i: (l, 0, 0))],
        out_specs=pl.BlockSpec((tm, D_MODEL), lambda i: (i, 0)),
        out_shape=jax.ShapeDtypeStruct((N_TOK, D_MODEL), F32),
        scratch_shapes=[pltpu.VMEM((3, 4 * LANES, D_MODEL), BF16), pltpu.VMEM((D_MODEL, D_MODEL), BF16)],
        compiler_params=_params(("arbitrary",)),
        name="merge_lat" if latent else "merge_ctx",
    )(x, oa, ob, oc, gates, mod, w_a, w_b, w_c, w_o)


FF_TILE = 1024
FF_CHUNK = 256


def _ffn_kernel(x_ref, mod_ref, g2_ref, w1_ref, w2_ref, gf_ref, xo_ref, acc_scr, h2_scr, *, final):
    f = pl.program_id(1)

    def run(first):
        w1 = w1_ref[...].astype(BF16)
        w2 = w2_ref[...].astype(BF16)
        m = mod_ref[0]
        for rc in range(ROW_TILE // FF_CHUNK):
            rows = pl.ds(rc * FF_CHUNK, FF_CHUNK)
            if first:
                x = x_ref[rows, :]
                ms = jnp.mean(x * x, axis=-1, keepdims=True)
                y = x * lax.rsqrt(ms + EPS) * g2_ref[...]
                h2 = (y * (1.0 + m[:, 4 * D_MODEL:5 * D_MODEL]) + m[:, 3 * D_MODEL:4 * D_MODEL]).astype(BF16)
                h2_scr[rows, :] = h2
            else:
                h2 = h2_scr[rows, :]
            u = jnp.dot(h2, w1, preferred_element_type=F32)
            u = jnp.square(jnp.maximum(u, 0.0)).astype(BF16)
            d = jnp.dot(u, w2, preferred_element_type=F32)
            if first:
                acc_scr[rows, :] = d
            else:
                acc_scr[rows, :] += d

    @pl.when(f == 0)
    def _():
        run(True)

    @pl.when(f > 0)
    def _():
        run(False)

    @pl.when(f == D_FF // FF_TILE - 1)
    def _():
        x = x_ref[...] + mod_ref[0][:, 5 * D_MODEL:6 * D_MODEL] * acc_scr[...]
        if final:
            ms = jnp.mean(x * x, axis=-1, keepdims=True)
            x = x * lax.rsqrt(ms + EPS) * gf_ref[...]
        xo_ref[...] = x


def _ffn(latent, l, x, mod, g_norm2, w1, w2, g_final):
    tm = ROW_TILE
    mod_idx = (lambda i, f: (1 + i, 0, 0)) if latent else (lambda i, f: (0, 0, 0))
    return pl.pallas_call(
        functools.partial(_ffn_kernel, final=(l == DEPTH - 1)),
        grid=(N_TOK // tm, D_FF // FF_TILE),
        in_specs=[pl.BlockSpec((tm, D_MODEL), lambda i, f: (i, 0)),
                  pl.BlockSpec((1, 1, 6 * D_MODEL), mod_idx),
                  pl.BlockSpec((None, 1, D_MODEL), lambda i, f: (l, 0, 0)),
                  pl.BlockSpec((None, D_MODEL, FF_TILE), lambda i, f: (l, 0, f)),
                  pl.BlockSpec((None, FF_TILE, D_MODEL), lambda i, f: (l, f, 0)),
                  pl.BlockSpec((1, D_MODEL), lambda i, f: (0, 0))],
        out_specs=pl.BlockSpec((tm, D_MODEL), lambda i, f: (i, 0)),
        out_shape=jax.ShapeDtypeStruct((N_TOK, D_MODEL), F32),
        scratch_shapes=[pltpu.VMEM((tm, D_MODEL), F32), pltpu.VMEM((tm, D_MODEL), BF16)],
        compiler_params=_params(("arbitrary", "arbitrary")),
        name="ffn_lat" if latent else "ffn_ctx",
    )(x, mod, g_norm2, w1, w2, g_final)


def _rope_tables():
    nf = HEAD_DIM // 4
    t = jnp.arange(DEC_SEQ)
    row = (t // GRID_W).astype(F32)
    col = (t % GRID_W).astype(F32)
    inv = ROPE_BASE ** (-jnp.arange(nf, dtype=F32) / nf)
    ar = row[:, None] * inv[None, :]
    ac = col[:, None] * inv[None, :]
    cos = jnp.concatenate([jnp.cos(ar), jnp.cos(ar), jnp.cos(ac), jnp.cos(ac)], axis=-1)
    sin = jnp.concatenate([-jnp.sin(ar), jnp.sin(ar), -jnp.sin(ac), jnp.sin(ac)], axis=-1)
    return jnp.tile(cos, (1, 2)), jnp.tile(sin, (1, 2))


def _packed_bias_rows(rel_bias):
    n = 2 * NA_ROWS - 1
    first = jnp.concatenate([rel_bias[..., NA_COLS - 1:], jnp.zeros((DEPTH, HC, n, LANES - (2 * NA_COLS - 1)), F32),
                             rel_bias[..., :NA_COLS - 1]], axis=-1)
    lo = GRID_W - NA_COLS + 1
    second = jnp.pad(rel_bias, ((0, 0), (0, 0), (0, 0), (lo, LANES - lo - (2 * NA_COLS - 1))))
    pad_rows = ((0, 0), (0, 0), (0, BIAS_TABLE_ROWS - n), (0, 0))
    return jnp.concatenate([jnp.pad(first, pad_rows), jnp.pad(second, pad_rows)], axis=2)


def kernel(x_prompt, x_sample, c, cache_a_k, cache_a_v, cache_b_k, cache_b_v, cache_c_k, cache_c_v, c_ctx, w_mod, b_mod, g_norm1, g_norm2, w_in, b_gate, lambda_qk, g_subln, g_qnorm, g_knorm, rel_bias, w_branch_a, w_branch_b, w_branch_c, w_out, w_ff1, w_ff2, g_final):
    xp = x_prompt.reshape(N_TOK, D_MODEL)
    xs = x_sample.reshape(N_TOK, D_MODEL)

    cvec = jnp.concatenate([c_ctx[None, :], c, jnp.zeros((3, D_MODEL), F32)], axis=0)
    mods = _modulation(cvec, w_mod, b_mod)

    rope_cos, rope_sin = _rope_tables()
    bias_rows = _packed_bias_rows(rel_bias)
    bd = jnp.kron(jnp.eye(2 * LANES // HEAD_DIM, dtype=F32),
                  jnp.full((HEAD_DIM, HEAD_DIM), 1.0 / HEAD_DIM, F32)).astype(BF16)
    gq = jnp.tile(g_qnorm, (1, 2 * LANES // DB)).reshape(DEPTH, 1, 2 * LANES)
    gk = jnp.tile(g_knorm, (1, LANES // DB)).reshape(DEPTH, 1, LANES)
    g1 = g_norm1.reshape(DEPTH, 1, D_MODEL)
    g2 = g_norm2.reshape(DEPTH, 1, D_MODEL)
    bg = b_gate.reshape(DEPTH, 1, 3 * D_MODEL)
    gsub = g_subln.reshape(DEPTH, 1, 2 * DA)
    gf = g_final.reshape(1, D_MODEL)


    def heads_t(cache):
        return cache.transpose(0, 1, 3, 4, 2)

    ctx_ak, ctx_bk, ctx_bv = heads_t(cache_a_k), heads_t(cache_b_k), heads_t(cache_b_v)
    ctx_ck, ctx_cv = heads_t(cache_c_k), heads_t(cache_c_v)

    new_caches = None

    for l in range(DEPTH):
        mod = mods[l].reshape(8, 1, 6 * D_MODEL)

        outs = _inproj(False, l, xp, mod, g1, w_in, bg, gq, gk, bd, None, None, new_caches)
        z, gates, new_caches = outs[0], outs[1], list(outs[2:])
        oa, ob, oc = _attn_ctx(l, z, lambda_qk, gsub)
        xp = _merge(False, l, xp, oa, ob, oc, gates, mod, w_branch_a, w_branch_b, w_branch_c, w_out)
        xp = _ffn(False, l, xp, mod, g2, w_ff1, w_ff2, gf)

        z, gates = _inproj(True, l, xs, mod, g1, w_in, bg, gq, gk, bd, rope_cos, rope_sin, None)
        oa = _attn_lat_a(l, z, ctx_ak, cache_a_v, lambda_qk, gsub)
        ob = _attn_lat_b(l, z, ctx_bk, ctx_bv)
        oc = _attn_lat_c(l, z, ctx_ck, ctx_cv, bias_rows)
        xs = _merge(True, l, xs, oa, ob, oc, gates, mod, w_branch_a, w_branch_b, w_branch_c, w_out)
        xs = _ffn(True, l, xs, mod, g2, w_ff1, w_ff2, gf)

    y_prompt = xp.reshape(BATCH, SEQ, D_MODEL)
    y_sample = xs.reshape(DEC_BATCH, DEC_SEQ, D_MODEL)
    ak, av, bk, bv, ck, cv = new_caches
    return (y_prompt, y_sample,
            ak.reshape(BATCH, DEPTH, SEQ, 2 * HA, DA), av.reshape(BATCH, DEPTH, SEQ, HA, 2 * DA),
            bk.reshape(BATCH, DEPTH, SEQ, KVB, DB), bv.reshape(BATCH, DEPTH, SEQ, KVB, DB),
            ck.reshape(BATCH, DEPTH, SEQ, HC, DC), cv.reshape(BATCH, DEPTH, SEQ, HC, DC))
```

```python
import functools
import math

import jax
import jax.numpy as jnp
from jax import lax
from jax.experimental import pallas as pl
from jax.experimental.pallas import tpu as pltpu

D_MODEL = 1024
BATCH = 16
SEQ = 256
DEPTH = 4
DEC_BATCH = 4
DEC_SEQ = 1024
PAST_LEN = 256
GRID_W = 64
HA, DA = 4, 64
HB, KVB, DB = 8, 2, 64
HC, DC = 8, 64
NA_ROWS, NA_COLS = 8, 16
D_FF = 4 * D_MODEL
ROPE_BASE = 10000.0
EPS = 1e-6
NEG = -1e30
HEAD_DIM = 64
LOG2E = math.log2(math.e)
Q_SCALE = HEAD_DIM ** -0.5 * LOG2E

F32 = jnp.float32
BF16 = jnp.bfloat16

LANES = 128
N_QKV_GROUPS = 30
N_GATE_GROUPS = 24
COL_TILE = 768
GROUPS_PER_TILE = COL_TILE // LANES
N_QKV_TILES = N_QKV_GROUPS // GROUPS_PER_TILE
N_COL_TILES = N_QKV_TILES + N_GATE_GROUPS // GROUPS_PER_TILE
ROW_TILE = 1024
ROW_CHUNK = 256
N_TOK = BATCH * SEQ
VMEM_LIMIT_V7X = 58 * 1024 * 1024

G_QA, G_KA, G_VA, G_QB, G_KB, G_VB, G_QC, G_KC, G_VC = 0, 4, 8, 12, 16, 17, 18, 22, 26
CACHE_WIDTHS = (2 * HA * DA, HA * 2 * DA, KVB * DB, KVB * DB, HC * DC, HC * DC)

_NT = (((1,), (1,)), ((), ()))


def _params(sem, vmem=VMEM_LIMIT_V7X):
    return pltpu.CompilerParams(dimension_semantics=sem, vmem_limit_bytes=vmem)


def _lane_head(shape):
    return lax.shift_right_logical(lax.broadcasted_iota(jnp.int32, shape, len(shape) - 1), 6)


def _mod_kernel(c_ref, w_ref, b_ref, o_ref):
    c = c_ref[...]
    s = (c * jax.nn.sigmoid(c)).astype(BF16)
    o_ref[...] = jnp.dot(s, w_ref[...].astype(BF16), preferred_element_type=F32) + b_ref[...]


def _modulation(cvec, w_mod, b_mod):
    tn = 1536
    n6 = 6 * D_MODEL
    return pl.pallas_call(
        _mod_kernel,
        grid=(DEPTH, n6 // tn),
        in_specs=[pl.BlockSpec((8, D_MODEL), lambda l, n: (0, 0)),
                  pl.BlockSpec((None, D_MODEL, tn), lambda l, n: (l, 0, n)),
                  pl.BlockSpec((None, 1, tn), lambda l, n: (l, 0, n))],
        out_specs=pl.BlockSpec((None, 8, tn), lambda l, n: (l, 0, n)),
        out_shape=jax.ShapeDtypeStruct((DEPTH, 8, n6), F32),
        compiler_params=_params(("arbitrary", "arbitrary")),
        name="modulation",
    )(cvec, w_mod, b_mod.reshape(DEPTH, 1, n6))


def _rope(v, cos, sin):
    first = (lax.broadcasted_iota(jnp.int32, v.shape, 1) & 16) == 0
    partner = jnp.where(first, pltpu.roll(v, LANES - 16, 1), pltpu.roll(v, 16, 1))
    return v * cos + partner * sin


def _head_rmsnorm(v, bd, g):
    msq = jnp.dot((v * v).astype(BF16), bd, preferred_element_type=F32)
    return v * lax.rsqrt(msq + EPS) * g


def _make_inproj_kernel(latent, n_aliased):
    def kern(*refs):
        if latent:
            (x_ref, mod_ref, g1_ref, w_ref, bg_ref, gq_ref, gk_ref, bd_ref, cos_ref, sin_ref,
             z_ref, gate_ref, h_scr, wbf_scr) = refs
        else:
            (x_ref, mod_ref, g1_ref, w_ref, bg_ref, gq_ref, gk_ref, bd_ref) = refs[:8]
            (z_ref, gate_ref, ak_ref, av_ref, bk_ref, bv_ref, ck_ref, cv_ref,
             h_scr, wbf_scr) = refs[8 + n_aliased:]
        j = pl.program_id(0)
        i = pl.program_id(1)

        @pl.when(i == 0)
        def _():
            wbf_scr[...] = w_ref[...].astype(BF16)

        def chunks(first=False):
            for rc in range(ROW_TILE // ROW_CHUNK):
                rows = pl.ds(rc * ROW_CHUNK, ROW_CHUNK)
                tok = pl.ds(pl.multiple_of(i * ROW_TILE + rc * ROW_CHUNK, ROW_CHUNK), ROW_CHUNK)
                if first:
                    x = x_ref[rows, :]
                    ms = jnp.mean(x * x, axis=-1, keepdims=True)
                    y = x * lax.rsqrt(ms + EPS) * g1_ref[...]
                    m = mod_ref[0]
                    h = (y * (1.0 + m[:, D_MODEL:2 * D_MODEL]) + m[:, 0:D_MODEL]).astype(BF16)
                    h_scr[tok, :] = h
                else:
                    h = h_scr[tok, :]
                yield rc, rows, jnp.dot(h, wbf_scr[...], preferred_element_type=F32)

        def grp(a, c, n=1):
            return a[:, c * LANES:(c + n) * LANES]

        def rot(v, rows):
            return _rope(v, cos_ref[rows, :], sin_ref[rows, :]) if latent else v

        @pl.when(j == 0)
        def _():
            for rc, rows, a in chunks(first=True):
                for c in range(GROUPS_PER_TILE):
                    v = rot(grp(a, c), rows)
                    z_ref[c, rows, :] = (v * Q_SCALE if c < 4 else v).astype(BF16)
                if not latent:
                    ak_ref[rc] = grp(a, 4, 2)

        @pl.when(j == 1)
        def _():
            for rc, rows, a in chunks():
                for c in range(2):
                    z_ref[c, rows, :] = rot(grp(a, c), rows).astype(BF16)
                for c in range(2, GROUPS_PER_TILE):
                    z_ref[c, rows, :] = grp(a, c).astype(BF16)
                if not latent:
                    ak_ref[rc] = grp(a, 0, 2)
                    for hd in range(HA):
                        av_ref[rc, :, hd, :] = grp(a, 2 + hd)

        @pl.when(j == 2)
        def _():
            bd = bd_ref[...]
            for rc, rows, a in chunks():
                for half in range(2):
                    qn = _head_rmsnorm(grp(a, 2 * half, 2), bd, gq_ref[...])
                    for c in range(2):
                        z_ref[2 * half + c, rows, :] = (rot(grp(qn, c), rows) * Q_SCALE).astype(BF16)
                kn = _head_rmsnorm(grp(a, 4), bd[0:LANES, 0:LANES], gk_ref[...])
                z_ref[4, rows, :] = rot(kn, rows).astype(BF16)
                z_ref[5, rows, :] = grp(a, 5).astype(BF16)
                if not latent:
                    bk_ref[rc] = kn
                    bv_ref[rc] = grp(a, 5)

        @pl.when(j == 3)
        def _():
            for rc, rows, a in chunks():
                for c in range(GROUPS_PER_TILE):
                    v = grp(a, c)
                    z_ref[c, rows, :] = (v * Q_SCALE if c < 4 else v).astype(BF16)
                if not latent:
                    ck_ref[rc] = grp(a, 4, 2)

        @pl.when(j == 4)
        def _():
            for rc, rows, a in chunks():
                for c in range(GROUPS_PER_TILE):
                    z_ref[c, rows, :] = grp(a, c).astype(BF16)
                if not latent:
                    ck_ref[rc] = grp(a, 0, 2)
                    cv_ref[rc] = grp(a, 2, 4)

        @pl.when(j >= N_QKV_TILES)
        def _():
            for rc, rows, a in chunks():
                a = a + bg_ref[...]
                for c in range(GROUPS_PER_TILE):
                    gate_ref[c, rows, :] = jax.nn.sigmoid(grp(a, c)).astype(BF16)

    return kern


def _inproj(latent, l, x, mod, g_norm1, w_in, b_gate, gq, gk, bd, rope_cos, rope_sin, caches):
    n_row = N_TOK // ROW_TILE
    last = n_row - 1

    def row_block(j, i, first_tile, last_tile):
        return jnp.where(j < first_tile, 0, jnp.where(j <= last_tile, i, last))

    mod_idx = (lambda j, i: (1 + i, 0, 0)) if latent else (lambda j, i: (0, 0, 0))
    in_specs = [
        pl.BlockSpec((ROW_TILE, D_MODEL), lambda j, i: (row_block(j, i, 0, 0), 0)),
        pl.BlockSpec((1, 1, 6 * D_MODEL), mod_idx),
        pl.BlockSpec((None, 1, D_MODEL), lambda j, i: (l, 0, 0)),
        pl.BlockSpec((None, D_MODEL, COL_TILE), lambda j, i: (l, 0, j)),
        pl.BlockSpec((None, 1, COL_TILE), lambda j, i: (l, 0, jnp.maximum(j - N_QKV_TILES, 0))),
        pl.BlockSpec((None, 1, 2 * LANES), lambda j, i: (l, 0, 0)),
        pl.BlockSpec((None, 1, LANES), lambda j, i: (l, 0, 0)),
        pl.BlockSpec((2 * LANES, 2 * LANES), lambda j, i: (0, 0)),
    ]
    args = [x, mod, g_norm1, w_in, b_gate, gq, gk, bd]
    out_specs = [
        pl.BlockSpec((GROUPS_PER_TILE, ROW_TILE, LANES),
                     lambda j, i: (jnp.minimum(j, N_QKV_TILES - 1), row_block(j, i, 0, N_QKV_TILES - 1), 0)),
        pl.BlockSpec((GROUPS_PER_TILE, ROW_TILE, LANES),
                     lambda j, i: (jnp.maximum(j - N_QKV_TILES, 0), row_block(j, i, N_QKV_TILES, N_COL_TILES), 0)),
    ]
    out_shape = [jax.ShapeDtypeStruct((N_QKV_GROUPS, N_TOK, LANES), BF16),
                 jax.ShapeDtypeStruct((N_GATE_GROUPS, N_TOK, LANES), BF16)]
    aliases = {}
    if latent:
        in_specs += [pl.BlockSpec((DEC_SEQ, LANES), lambda j, i: (0, 0))] * 2
        args += [rope_cos, rope_sin]
    else:
        nb = ROW_TILE // SEQ
        tiles = ((0, 1, 256), (1, 1, 512), (2, 2, 128), (2, 2, 128), (3, 4, 256), (4, 4, 512))
        for k, (w, (t0, t1, bw)) in enumerate(zip(CACHE_WIDTHS, tiles)):
            if caches is not None:
                in_specs.append(pl.BlockSpec(memory_space=pl.ANY))
                args.append(caches[k])
                aliases[8 + k] = 2 + k

            def cache_idx(j, i, t0=t0, t1=t1):
                return (row_block(j, i, t0, t1), l, 0, jnp.where(j <= t0, 0, (t1 - t0)))

            if k == 1:
                out_specs.append(pl.BlockSpec((nb, None, SEQ, HA, 2 * DA),
                                              lambda j, i, f=cache_idx: f(j, i) + (0,)))
                out_shape.append(jax.ShapeDtypeStruct((BATCH, DEPTH, SEQ, HA, 2 * DA), F32))
                continue
            out_specs.append(pl.BlockSpec((nb, None, SEQ, bw), cache_idx))
            out_shape.append(jax.ShapeDtypeStruct((BATCH, DEPTH, SEQ, w), F32))
    return pl.pallas_call(
        _make_inproj_kernel(latent, len(aliases)),
        grid=(N_COL_TILES, n_row),
        in_specs=in_specs,
        out_specs=out_specs,
        out_shape=out_shape,
        scratch_shapes=[pltpu.VMEM((N_TOK, D_MODEL), BF16), pltpu.VMEM((D_MODEL, COL_TILE), BF16)],
        input_output_aliases=aliases,
        compiler_params=_params(("arbitrary", "arbitrary")),
        name="inproj_lat" if latent else "inproj_ctx",
    )(*args)


def _mask_head(q, head):
    qf = q.astype(F32)
    keep = _lane_head(qf.shape) == head
    return jnp.where(keep, qf, 0.0).astype(BF16)


def _stack_heads(q):
    return jnp.concatenate([_mask_head(q, 0), _mask_head(q, 1)], axis=0)


def _unstack_heads(o, rows):
    return jnp.where(_lane_head((rows, LANES)) == 1, o[rows:2 * rows], o[0:rows])


def _dup_head(kv, head):
    f = kv.astype(F32)
    r = pltpu.roll(f, HEAD_DIM, 1)
    return jnp.where(_lane_head(f.shape) == head, f, r).astype(BF16)


def _attend(qm, ks, vs):
    nt = (False,) * len(ks)
    return _attend_multi([(qm, ks, vs, None, nt, nt)])[0]


def _attend_multi(jobs):
    sss = []
    for qm, ks, _, biases, k_t, _ in jobs:
        ss = [jnp.dot(qm, k, preferred_element_type=F32) if t else
              lax.dot_general(qm, k, _NT, preferred_element_type=F32) for k, t in zip(ks, k_t)]
        if biases is not None:
            ss = [s if b is None else s + b for s, b in zip(ss, biases)]
        sss.append(ss)
    pss, dens = [], []
    for ss in sss:
        m = functools.reduce(jnp.maximum, [jnp.max(s, axis=-1, keepdims=True) for s in ss])
        ps = [jnp.exp2(s - m) for s in ss]
        dens.append(functools.reduce(jnp.add, [jnp.sum(p, axis=-1, keepdims=True) for p in ps]))
        pss.append(ps)
    outs = []
    for (_, _, vs, _, _, v_t), ps, den in zip(jobs, pss, dens):
        o = functools.reduce(jnp.add, [lax.dot_general(p.astype(BF16), v, _NT, preferred_element_type=F32) if t else
                                        jnp.dot(p.astype(BF16), v, preferred_element_type=F32)
                                        for p, v, t in zip(ps, vs, v_t)])
        outs.append(o * (1.0 / den))
    return outs


_CTX_T = (False, True)


def _pair_t(ref):
    return ref[...].reshape(2 * HEAD_DIM, ref.shape[-1]).astype(BF16)


def _diff_lambda(lam_ref, lam_init):
    lq = lam_ref[...]
    a = jnp.sum(lq[0:1] * lq[1:2], axis=-1, keepdims=True)
    b = jnp.sum(lq[2:3] * lq[3:4], axis=-1, keepdims=True)
    return jnp.exp(a) - jnp.exp(b) + lam_init


def _diff_combine(o1, o2, lam, gsub, lam_init):
    o = o1 - lam * o2
    ms = jnp.mean(o * o, axis=-1, keepdims=True)
    return (o * lax.rsqrt(ms + EPS) * gsub) * (1.0 - lam_init)


CTX_BATCH_PER_STEP = 2


def _make_attn_ctx_kernel(lam_init):
    def kern(z_ref, lam_ref, gsub_ref, oa_ref, ob_ref, oc_ref):
        lam = _diff_lambda(lam_ref, lam_init)
        gsub = gsub_ref[...]
        for bb in range(CTX_BATCH_PER_STEP):
            rows = pl.ds(bb * SEQ, SEQ)

            def z(g):
                return z_ref[g, rows, :]

            for vh in range(HA):
                hi = vh % 2
                v = z(G_VA + vh)
                o1 = _attend(_mask_head(z(G_QA + vh // 2), hi), [z(G_KA + vh // 2)], [v])
                o2 = _attend(_mask_head(z(G_QA + 2 + vh // 2), hi), [z(G_KA + 2 + vh // 2)], [v])
                oa_ref[vh, rows, :] = _diff_combine(o1, o2, lam, gsub, lam_init).astype(BF16)
            for g in range(KVB):
                kd = _dup_head(z(G_KB), g)
                vd = _dup_head(z(G_VB), g)
                for c in range(2 * g, 2 * g + 2):
                    o = _attend(_stack_heads(z(G_QB + c)), [kd], [vd])
                    ob_ref[c, rows, :] = _unstack_heads(o, SEQ).astype(BF16)
            for c in range(HC // 2):
                o = _attend(_stack_heads(z(G_QC + c)), [z(G_KC + c)], [z(G_VC + c)])
                oc_ref[c, rows, :] = _unstack_heads(o, SEQ).astype(BF16)
    return kern


def _attn_ctx(l, z, lambda_qk, g_subln):
    lam_init = 0.8 - 0.6 * math.exp(-0.3 * l)
    rows = CTX_BATCH_PER_STEP * SEQ
    o_spec = pl.BlockSpec((4, rows, LANES), lambda b: (0, b, 0))
    o_shape = jax.ShapeDtypeStruct((4, N_TOK, LANES), BF16)
    return pl.pallas_call(
        _make_attn_ctx_kernel(lam_init),
        grid=(BATCH // CTX_BATCH_PER_STEP,),
        in_specs=[pl.BlockSpec((N_QKV_GROUPS, rows, LANES), lambda b: (0, b, 0)),
                  pl.BlockSpec((None, 4, DA), lambda b: (l, 0, 0)),
                  pl.BlockSpec((None, 1, 2 * DA), lambda b: (l, 0, 0))],
        out_specs=[o_spec, o_spec, o_spec],
        out_shape=[o_shape, o_shape, o_shape],
        compiler_params=_params(("arbitrary",)),
        name="attn_ctx",
    )(z, lambda_qk, g_subln)


Q_BLK_GQA = 512
Q_BLK_DIFF = 1024


def _make_attn_lat_a_kernel(lam_init):
    def kern(q1_ref, q2_ref, k1_ref, k2_ref, v_ref, ck1_ref, ck2_ref, cv_ref, lam_ref, gsub_ref, o_ref):
        vh = pl.program_id(1)
        hi = jnp.bitwise_and(vh, 1)
        lam = _diff_lambda(lam_ref, lam_init)
        gsub = gsub_ref[...]
        ck1 = _pair_t(ck1_ref)
        ck2 = _pair_t(ck2_ref)
        cv = cv_ref[:, vh, :].astype(BF16)
        k1, k2, v = k1_ref[0], k2_ref[0], v_ref[0]
        rows = [pl.ds(qb * Q_BLK_DIFF, Q_BLK_DIFF) for qb in range(DEC_SEQ // Q_BLK_DIFF)]
        jobs = []
        for r in rows:
            jobs.append((_mask_head(q1_ref[0, r, :], hi), [k1, ck1], [v, cv], None, _CTX_T, (False, False)))
            jobs.append((_mask_head(q2_ref[0, r, :], hi), [k2, ck2], [v, cv], None, _CTX_T, (False, False)))
        outs = _attend_multi(jobs)
        for t, r in enumerate(rows):
            o_ref[0, r, :] = _diff_combine(outs[2 * t], outs[2 * t + 1], lam, gsub, lam_init).astype(BF16)
    return kern


def _attn_lat_a(l, z, cache_k_t, cache_v, lambda_qk, g_subln):
    lam_init = 0.8 - 0.6 * math.exp(-0.3 * l)

    def zspec(fn):
        return pl.BlockSpec((1, DEC_SEQ, LANES), lambda b, h: (fn(h), b, 0))

    return pl.pallas_call(
        _make_attn_lat_a_kernel(lam_init),
        grid=(DEC_BATCH, HA),
        in_specs=[zspec(lambda h: G_QA + h // 2), zspec(lambda h: G_QA + 2 + h // 2),
                  zspec(lambda h: G_KA + h // 2), zspec(lambda h: G_KA + 2 + h // 2),
                  zspec(lambda h: G_VA + h),
                  pl.BlockSpec((None, None, 2, DA, PAST_LEN), lambda b, h: (b, l, h // 2, 0, 0)),
                  pl.BlockSpec((None, None, 2, DA, PAST_LEN), lambda b, h: (b, l, HA // 2 + h // 2, 0, 0)),
                  pl.BlockSpec((None, None, PAST_LEN, HA, 2 * DA), lambda b, h: (b, l, 0, 0, 0)),
                  pl.BlockSpec((None, 4, DA), lambda b, h: (l, 0, 0)),
                  pl.BlockSpec((None, 1, 2 * DA), lambda b, h: (l, 0, 0))],
        out_specs=pl.BlockSpec((1, DEC_SEQ, LANES), lambda b, h: (h, b, 0)),
        out_shape=jax.ShapeDtypeStruct((4, N_TOK, LANES), BF16),
        compiler_params=_params(("arbitrary", "arbitrary")),
        name="attn_lat_a",
    )(z, z, z, z, z, cache_k_t, cache_k_t, cache_v, lambda_qk, g_subln)


def _attn_lat_b_kernel(q_ref, k_ref, v_ref, ck_ref, cv_ref, o_ref):
    hi = lax.shift_right_logical(pl.program_id(1), 1)
    kd = _dup_head(k_ref[0], hi)
    vd = _dup_head(v_ref[0], hi)
    ck = ck_ref[hi].astype(BF16)
    cv = cv_ref[hi].astype(BF16)
    ckd = jnp.concatenate([ck, ck], axis=0)
    cvd = jnp.concatenate([cv, cv], axis=0)
    rows = [pl.ds(qb * Q_BLK_GQA, Q_BLK_GQA) for qb in range(DEC_SEQ // Q_BLK_GQA)]
    outs = _attend_multi([(_stack_heads(q_ref[0, r, :]), [kd, ckd], [vd, cvd], None, _CTX_T, _CTX_T)
                          for r in rows])
    for r, o in zip(rows, outs):
        o_ref[0, r, :] = _unstack_heads(o, Q_BLK_GQA).astype(BF16)


def _attn_lat_b(l, z, cache_k, cache_v):
    cspec = pl.BlockSpec((None, None, KVB, DB, PAST_LEN), lambda b, c: (b, l, 0, 0, 0))
    return pl.pallas_call(
        _attn_lat_b_kernel,
        grid=(DEC_BATCH, HB // 2),
        in_specs=[pl.BlockSpec((1, DEC_SEQ, LANES), lambda b, c: (G_QB + c, b, 0)),
                  pl.BlockSpec((1, DEC_SEQ, LANES), lambda b, c: (G_KB, b, 0)),
                  pl.BlockSpec((1, DEC_SEQ, LANES), lambda b, c: (G_VB, b, 0)),
                  cspec, cspec],
        out_specs=pl.BlockSpec((1, DEC_SEQ, LANES), lambda b, c: (c, b, 0)),
        out_shape=jax.ShapeDtypeStruct((4, N_TOK, LANES), BF16),
        compiler_params=_params(("arbitrary", "arbitrary")),
        name="attn_lat_b",
    )(z, z, z, cache_k, cache_v)


N_GRID_ROWS = DEC_SEQ // GRID_W
NA_KEYS = NA_ROWS * GRID_W


NA_GROUP = 4
NA_MAX_KEY_ROWS = 12
BIAS_TABLE_ROWS = 16


def _window_start(r):
    return min(max(r - NA_ROWS // 2, 0), N_GRID_ROWS - NA_ROWS)


def _group_keys(g):
    starts = [_window_start(r) for r in range(g * NA_GROUP, (g + 1) * NA_GROUP)]
    n = max(starts) + NA_ROWS - min(starts)
    n += n % 2
    return min(min(starts), N_GRID_ROWS - n), n


def _build_window_bias(tab_ref, bias_scr):
    qcol = lax.broadcasted_iota(jnp.int32, (GRID_W, LANES), 0)
    lane = lax.broadcasted_iota(jnp.int32, (GRID_W, LANES), 1)
    kcol = lane & (GRID_W - 1)
    c0 = jnp.clip(qcol - NA_COLS // 2, 0, GRID_W - NA_COLS)
    in_win = (kcol >= c0) & (kcol < c0 + NA_COLS)
    masks = {(True, True): in_win, (True, False): in_win & (lane < GRID_W), (False, True): in_win & (lane >= GRID_W)}
    for head in range(2):
        for r in range(N_GRID_ROWS):
            u0, n = _group_keys(r // NA_GROUP)
            rows = pl.ds(r * GRID_W, GRID_W)
            for m in range(n // 2):
                key_rows = (u0 + 2 * m, u0 + 2 * m + 1)
                valid = tuple(_window_start(r) <= kr < _window_start(r) + NA_ROWS for kr in key_rows)
                cols = pl.ds(m * LANES, LANES)
                if not any(valid):
                    bias_scr[head, rows, cols] = jnp.full((GRID_W, LANES), NEG, F32)
                    continue
                vec = None
                for half, (kr, ok) in enumerate(zip(key_rows, valid)):
                    if ok:
                        dr = kr - r + NA_ROWS - 1
                        part = tab_ref[head, pl.ds(half * BIAS_TABLE_ROWS + dr, 1), :]
                        vec = part if vec is None else vec + part
                toep = pltpu.roll(jnp.broadcast_to(vec, (GRID_W, LANES)), 0, 1, stride=1, stride_axis=0)
                bias_scr[head, rows, cols] = jnp.where(masks[valid], toep * LOG2E, NEG)


def _attn_lat_c_kernel(q_ref, k_ref, v_ref, ck_ref, cv_ref, tab_ref, o_ref, bias_scr):
    @pl.when(pl.program_id(1) == 0)
    def _():
        _build_window_bias(tab_ref, bias_scr)

    ck = _pair_t(ck_ref)
    cv = _pair_t(cv_ref)
    n_rows = NA_GROUP * GRID_W
    jobs, row_slices = [], []
    for g in range(N_GRID_ROWS // NA_GROUP):
        u0, n = _group_keys(g)
        rows = pl.ds(g * n_rows, n_rows)
        keys = pl.ds(u0 * GRID_W, n * GRID_W)
        bias = jnp.concatenate([bias_scr[0, rows, 0:n * GRID_W], bias_scr[1, rows, 0:n * GRID_W]], axis=0)
        jobs.append((_stack_heads(q_ref[0, rows, :]), [k_ref[0, keys, :], ck], [v_ref[0, keys, :], cv],
                     [bias, None], _CTX_T, _CTX_T))
        row_slices.append(rows)
    for rows, o in zip(row_slices, _attend_multi(jobs)):
        o_ref[0, rows, :] = _unstack_heads(o, n_rows).astype(BF16)


def _attn_lat_c(l, z, cache_k, cache_v, bias_rows):
    def zspec(g0):
        return pl.BlockSpec((1, DEC_SEQ, LANES), lambda c, b: (g0 + c, b, 0))

    cspec = pl.BlockSpec((None, None, 2, DC, PAST_LEN), lambda c, b: (b, l, c, 0, 0))
    return pl.pallas_call(
        _attn_lat_c_kernel,
        grid=(HC // 2, DEC_BATCH),
        in_specs=[zspec(G_QC), zspec(G_KC), zspec(G_VC), cspec, cspec,
                  pl.BlockSpec((None, 2, 2 * BIAS_TABLE_ROWS, LANES), lambda c, b: (l, c, 0, 0))],
        out_specs=pl.BlockSpec((1, DEC_SEQ, LANES), lambda c, b: (c, b, 0)),
        out_shape=jax.ShapeDtypeStruct((4, N_TOK, LANES), BF16),
        scratch_shapes=[pltpu.VMEM((2, DEC_SEQ, NA_MAX_KEY_ROWS * GRID_W), F32)],
        compiler_params=_params(("arbitrary", "arbitrary")),
        name="attn_lat_c",
    )(z, z, z, cache_k, cache_v, bias_rows)


MERGE_TILE = 512


def _merge_kernel(x_ref, oa_ref, ob_ref, oc_ref, gate_ref, mod_ref, wa_ref, wb_ref, wc_ref, wo_ref,
                  xo_ref, wbr_scr, wo_scr):
    @pl.when(pl.program_id(0) == 0)
    def _():
        wbr_scr[0] = wa_ref[...].astype(BF16)
        wbr_scr[1] = wb_ref[...].astype(BF16)
        wbr_scr[2] = wc_ref[...].astype(BF16)
        wo_scr[...] = wo_ref[...].astype(BF16)

    m = mod_ref[0]
    for rc in range(MERGE_TILE // ROW_CHUNK):
        rows = pl.ds(rc * ROW_CHUNK, ROW_CHUNK)
        y = None
        for k, o_ref in enumerate((oa_ref, ob_ref, oc_ref)):
            o = jnp.concatenate([o_ref[c, rows, :] for c in range(4)], axis=-1)
            p = jnp.dot(o, wbr_scr[k], preferred_element_type=F32)
            g = jnp.concatenate([gate_ref[8 * k + c, rows, :] for c in range(8)], axis=-1).astype(F32)
            y = g * p if y is None else y + g * p
        out = jnp.dot(y.astype(BF16), wo_scr[...], preferred_element_type=F32)
        xo_ref[rows, :] = x_ref[rows, :] + m[:, 2 * D_MODEL:3 * D_MODEL] * out


def _merge(latent, l, x, oa, ob, oc, gates, mod, w_a, w_b, w_c, w_o):
    tm = MERGE_TILE
    if latent:
        mod_idx = lambda i: (1 + (i * tm) // DEC_SEQ, 0, 0)
    else:
        mod_idx = lambda i: (0, 0, 0)
    o_spec = pl.BlockSpec((4, tm, LANES), lambda i: (0, i, 0))
    wbr_spec = pl.BlockSpec((None, 4 * LANES, D_MODEL), lambda i: (l, 0, 0))
    return pl.pallas_call(
        _merge_kernel,
        grid=(N_TOK // tm,),
        in_specs=[pl.BlockSpec((tm, D_MODEL), lambda i: (i, 0)),
                  o_spec, o_spec, o_spec,
                  pl.BlockSpec((N_GATE_GROUPS, tm, LANES), lambda i: (0, i, 0)),
                  pl.BlockSpec((1, 1, 6 * D_MODEL), mod_idx),
                  wbr_spec, wbr_spec, wbr_spec,
                  pl.BlockSpec((None, D_MODEL, D_MODEL), lambda i: (l, 0, 0))],
        out_specs=pl.BlockSpec((tm, D_MODEL), lambda i: (i, 0)),
        out_shape=jax.ShapeDtypeStruct((N_TOK, D_MODEL), F32),
        scratch_shapes=[pltpu.VMEM((3, 4 * LANES, D_MODEL), BF16), pltpu.VMEM((D_MODEL, D_MODEL), BF16)],
        compiler_params=_params(("arbitrary",)),
        name="merge_lat" if latent else "merge_ctx",
    )(x, oa, ob, oc, gates, mod, w_a, w_b, w_c, w_o)


FF_TILE = 1024
FF_CHUNK = 512


def _ffn_kernel(x_ref, mod_ref, g2_ref, w1_ref, w2_ref, gf_ref, xo_ref, acc_scr, h2_scr, *, final):
    f = pl.program_id(1)

    def run(first):
        w1 = w1_ref[...].astype(BF16)
        w2 = w2_ref[...].astype(BF16)
        m = mod_ref[0]
        for rc in range(ROW_TILE // FF_CHUNK):
            rows = pl.ds(rc * FF_CHUNK, FF_CHUNK)
            if first:
                x = x_ref[rows, :]
                ms = jnp.mean(x * x, axis=-1, keepdims=True)
                y = x * lax.rsqrt(ms + EPS) * g2_ref[...]
                h2 = (y * (1.0 + m[:, 4 * D_MODEL:5 * D_MODEL]) + m[:, 3 * D_MODEL:4 * D_MODEL]).astype(BF16)
                h2_scr[rows, :] = h2
            else:
                h2 = h2_scr[rows, :]
            u = jnp.dot(h2, w1, preferred_element_type=F32)
            u = jnp.square(jnp.maximum(u, 0.0)).astype(BF16)
            d = jnp.dot(u, w2, preferred_element_type=F32)
            if first:
                acc_scr[rows, :] = d
            else:
                acc_scr[rows, :] += d

    @pl.when(f == 0)
    def _():
        run(True)

    @pl.when(f > 0)
    def _():
        run(False)

    @pl.when(f == D_FF // FF_TILE - 1)
    def _():
        x = x_ref[...] + mod_ref[0][:, 5 * D_MODEL:6 * D_MODEL] * acc_scr[...]
        if final:
            ms = jnp.mean(x * x, axis=-1, keepdims=True)
            x = x * lax.rsqrt(ms + EPS) * gf_ref[...]
        xo_ref[...] = x


def _ffn(latent, l, x, mod, g_norm2, w1, w2, g_final):
    tm = ROW_TILE
    mod_idx = (lambda i, f: (1 + i, 0, 0)) if latent else (lambda i, f: (0, 0, 0))
    return pl.pallas_call(
        functools.partial(_ffn_kernel, final=(l == DEPTH - 1)),
        grid=(N_TOK // tm, D_FF // FF_TILE),
        in_specs=[pl.BlockSpec((tm, D_MODEL), lambda i, f: (i, 0)),
                  pl.BlockSpec((1, 1, 6 * D_MODEL), mod_idx),
                  pl.BlockSpec((None, 1, D_MODEL), lambda i, f: (l, 0, 0)),
                  pl.BlockSpec((None, D_MODEL, FF_TILE), lambda i, f: (l, 0, f)),
                  pl.BlockSpec((None, FF_TILE, D_MODEL), lambda i, f: (l, f, 0)),
                  pl.BlockSpec((1, D_MODEL), lambda i, f: (0, 0))],
        out_specs=pl.BlockSpec((tm, D_MODEL), lambda i, f: (i, 0)),
        out_shape=jax.ShapeDtypeStruct((N_TOK, D_MODEL), F32),
        scratch_shapes=[pltpu.VMEM((tm, D_MODEL), F32), pltpu.VMEM((tm, D_MODEL), BF16)],
        compiler_params=_params(("arbitrary", "arbitrary")),
        name="ffn_lat" if latent else "ffn_ctx",
    )(x, mod, g_norm2, w1, w2, g_final)


def _rope_tables():
    nf = HEAD_DIM // 4
    t = jnp.arange(DEC_SEQ)
    row = (t // GRID_W).astype(F32)
    col = (t % GRID_W).astype(F32)
    inv = ROPE_BASE ** (-jnp.arange(nf, dtype=F32) / nf)
    ar = row[:, None] * inv[None, :]
    ac = col[:, None] * inv[None, :]
    cos = jnp.concatenate([jnp.cos(ar), jnp.cos(ar), jnp.cos(ac), jnp.cos(ac)], axis=-1)
    sin = jnp.concatenate([-jnp.sin(ar), jnp.sin(ar), -jnp.sin(ac), jnp.sin(ac)], axis=-1)
    return jnp.tile(cos, (1, 2)), jnp.tile(sin, (1, 2))


def _packed_bias_rows(rel_bias):
    n = 2 * NA_ROWS - 1
    first = jnp.concatenate([rel_bias[..., NA_COLS - 1:], jnp.zeros((DEPTH, HC, n, LANES - (2 * NA_COLS - 1)), F32),
                             rel_bias[..., :NA_COLS - 1]], axis=-1)
    lo = GRID_W - NA_COLS + 1
    second = jnp.pad(rel_bias, ((0, 0), (0, 0), (0, 0), (lo, LANES - lo - (2 * NA_COLS - 1))))
    pad_rows = ((0, 0), (0, 0), (0, BIAS_TABLE_ROWS - n), (0, 0))
    return jnp.concatenate([jnp.pad(first, pad_rows), jnp.pad(second, pad_rows)], axis=2)


def kernel(x_prompt, x_sample, c, cache_a_k, cache_a_v, cache_b_k, cache_b_v, cache_c_k, cache_c_v, c_ctx, w_mod, b_mod, g_norm1, g_norm2, w_in, b_gate, lambda_qk, g_subln, g_qnorm, g_knorm, rel_bias, w_branch_a, w_branch_b, w_branch_c, w_out, w_ff1, w_ff2, g_final):
    xp = x_prompt.reshape(N_TOK, D_MODEL)
    xs = x_sample.reshape(N_TOK, D_MODEL)

    cvec = jnp.concatenate([c_ctx[None, :], c, jnp.zeros((3, D_MODEL), F32)], axis=0)
    mods = _modulation(cvec, w_mod, b_mod)

    rope_cos, rope_sin = _rope_tables()
    bias_rows = _packed_bias_rows(rel_bias)
    bd = jnp.kron(jnp.eye(2 * LANES // HEAD_DIM, dtype=F32),
                  jnp.full((HEAD_DIM, HEAD_DIM), 1.0 / HEAD_DIM, F32)).astype(BF16)
    gq = jnp.tile(g_qnorm, (1, 2 * LANES // DB)).reshape(DEPTH, 1, 2 * LANES)
    gk = jnp.tile(g_knorm, (1, LANES // DB)).reshape(DEPTH, 1, LANES)
    g1 = g_norm1.reshape(DEPTH, 1, D_MODEL)
    g2 = g_norm2.reshape(DEPTH, 1, D_MODEL)
    bg = b_gate.reshape(DEPTH, 1, 3 * D_MODEL)
    gsub = g_subln.reshape(DEPTH, 1, 2 * DA)
    gf = g_final.reshape(1, D_MODEL)


    def heads_t(cache):
        return cache.transpose(0, 1, 3, 4, 2)

    ctx_ak, ctx_bk, ctx_bv = heads_t(cache_a_k), heads_t(cache_b_k), heads_t(cache_b_v)
    ctx_ck, ctx_cv = heads_t(cache_c_k), heads_t(cache_c_v)

    new_caches = None

    for l in range(DEPTH):
        mod = mods[l].reshape(8, 1, 6 * D_MODEL)

        outs = _inproj(False, l, xp, mod, g1, w_in, bg, gq, gk, bd, None, None, new_caches)
        z, gates, new_caches = outs[0], outs[1], list(outs[2:])
        oa, ob, oc = _attn_ctx(l, z, lambda_qk, gsub)
        xp = _merge(False, l, xp, oa, ob, oc, gates, mod, w_branch_a, w_branch_b, w_branch_c, w_out)
        xp = _ffn(False, l, xp, mod, g2, w_ff1, w_ff2, gf)

        z, gates = _inproj(True, l, xs, mod, g1, w_in, bg, gq, gk, bd, rope_cos, rope_sin, None)
        oa = _attn_lat_a(l, z, ctx_ak, cache_a_v, lambda_qk, gsub)
        ob = _attn_lat_b(l, z, ctx_bk, ctx_bv)
        oc = _attn_lat_c(l, z, ctx_ck, ctx_cv, bias_rows)
        xs = _merge(True, l, xs, oa, ob, oc, gates, mod, w_branch_a, w_branch_b, w_branch_c, w_out)
        xs = _ffn(True, l, xs, mod, g2, w_ff1, w_ff2, gf)

    y_prompt = xp.reshape(BATCH, SEQ, D_MODEL)
    y_sample = xs.reshape(DEC_BATCH, DEC_SEQ, D_MODEL)
    ak, av, bk, bv, ck, cv = new_caches
    return (y_prompt, y_sample,
            ak.reshape(BATCH, DEPTH, SEQ, 2 * HA, DA), av.reshape(BATCH, DEPTH, SEQ, HA, 2 * DA),
            bk.reshape(BATCH, DEPTH, SEQ, KVB, DB), bv.reshape(BATCH, DEPTH, SEQ, KVB, DB),
            ck.reshape(BATCH, DEPTH, SEQ, HC, DC), cv.reshape(BATCH, DEPTH, SEQ, HC, DC))
```

```python
import functools
import math

import jax
import jax.numpy as jnp
from jax import lax
from jax.experimental import pallas as pl
from jax.experimental.pallas import tpu as pltpu

D_MODEL = 1024
BATCH = 16
SEQ = 256
DEPTH = 4
DEC_BATCH = 4
DEC_SEQ = 1024
PAST_LEN = 256
GRID_W = 64
HA, DA = 4, 64
HB, KVB, DB = 8, 2, 64
HC, DC = 8, 64
NA_ROWS, NA_COLS = 8, 16
D_FF = 4 * D_MODEL
ROPE_BASE = 10000.0
EPS = 1e-6
NEG = -1e30
HEAD_DIM = 64
LOG2E = math.log2(math.e)
Q_SCALE = HEAD_DIM ** -0.5 * LOG2E

F32 = jnp.float32
BF16 = jnp.bfloat16

LANES = 128
N_QKV_GROUPS = 30
N_GATE_GROUPS = 24
COL_TILE = 768
GROUPS_PER_TILE = COL_TILE // LANES
N_QKV_TILES = N_QKV_GROUPS // GROUPS_PER_TILE
N_COL_TILES = N_QKV_TILES + N_GATE_GROUPS // GROUPS_PER_TILE
ROW_TILE = 1024
ROW_CHUNK = 256
N_TOK = BATCH * SEQ
VMEM_LIMIT_V7X = 58 * 1024 * 1024

G_QA, G_KA, G_VA, G_QB, G_KB, G_VB, G_QC, G_KC, G_VC = 0, 4, 8, 12, 16, 17, 18, 22, 26
CACHE_WIDTHS = (2 * HA * DA, HA * 2 * DA, KVB * DB, KVB * DB, HC * DC, HC * DC)

_NT = (((1,), (1,)), ((), ()))


def _params(sem, vmem=VMEM_LIMIT_V7X):
    return pltpu.CompilerParams(dimension_semantics=sem, vmem_limit_bytes=vmem)


def _lane_head(shape):
    return lax.shift_right_logical(lax.broadcasted_iota(jnp.int32, shape, len(shape) - 1), 6)


def _mod_kernel(c_ref, w_ref, b_ref, o_ref):
    c = c_ref[...]
    s = (c * jax.nn.sigmoid(c)).astype(BF16)
    o_ref[...] = jnp.dot(s, w_ref[...].astype(BF16), preferred_element_type=F32) + b_ref[...]


def _modulation(cvec, w_mod, b_mod):
    tn = 1536
    n6 = 6 * D_MODEL
    return pl.pallas_call(
        _mod_kernel,
        grid=(DEPTH, n6 // tn),
        in_specs=[pl.BlockSpec((8, D_MODEL), lambda l, n: (0, 0)),
                  pl.BlockSpec((None, D_MODEL, tn), lambda l, n: (l, 0, n)),
                  pl.BlockSpec((None, 1, tn), lambda l, n: (l, 0, n))],
        out_specs=pl.BlockSpec((None, 8, tn), lambda l, n: (l, 0, n)),
        out_shape=jax.ShapeDtypeStruct((DEPTH, 8, n6), F32),
        compiler_params=_params(("arbitrary", "arbitrary")),
        name="modulation",
    )(cvec, w_mod, b_mod.reshape(DEPTH, 1, n6))


def _rope(v, cos, sin):
    first = (lax.broadcasted_iota(jnp.int32, v.shape, 1) & 16) == 0
    partner = jnp.where(first, pltpu.roll(v, LANES - 16, 1), pltpu.roll(v, 16, 1))
    return v * cos + partner * sin


def _head_rmsnorm(v, bd, g):
    msq = jnp.dot((v * v).astype(BF16), bd, preferred_element_type=F32)
    return v * lax.rsqrt(msq + EPS) * g


def _make_inproj_kernel(latent, n_aliased):
    def kern(*refs):
        if latent:
            (x_ref, mod_ref, g1_ref, w_ref, bg_ref, gq_ref, gk_ref, bd_ref, cos_ref, sin_ref,
             z_ref, gate_ref, h_scr, wbf_scr) = refs
        else:
            (x_ref, mod_ref, g1_ref, w_ref, bg_ref, gq_ref, gk_ref, bd_ref) = refs[:8]
            (z_ref, gate_ref, ak_ref, av_ref, bk_ref, bv_ref, ck_ref, cv_ref,
             h_scr, wbf_scr) = refs[8 + n_aliased:]
        j = pl.program_id(0)
        i = pl.program_id(1)

        @pl.when(i == 0)
        def _():
            wbf_scr[...] = w_ref[...].astype(BF16)

        def chunks(first=False):
            for rc in range(ROW_TILE // ROW_CHUNK):
                rows = pl.ds(rc * ROW_CHUNK, ROW_CHUNK)
                tok = pl.ds(pl.multiple_of(i * ROW_TILE + rc * ROW_CHUNK, ROW_CHUNK), ROW_CHUNK)
                if first:
                    x = x_ref[rows, :]
                    ms = jnp.mean(x * x, axis=-1, keepdims=True)
                    y = x * lax.rsqrt(ms + EPS) * g1_ref[...]
                    m = mod_ref[0]
                    h = (y * (1.0 + m[:, D_MODEL:2 * D_MODEL]) + m[:, 0:D_MODEL]).astype(BF16)
                    h_scr[tok, :] = h
                else:
                    h = h_scr[tok, :]
                yield rc, rows, jnp.dot(h, wbf_scr[...], preferred_element_type=F32)

        def grp(a, c, n=1):
            return a[:, c * LANES:(c + n) * LANES]

        def rot(v, rows):
            return _rope(v, cos_ref[rows, :], sin_ref[rows, :]) if latent else v

        @pl.when(j == 0)
        def _():
            for rc, rows, a in chunks(first=True):
                for c in range(GROUPS_PER_TILE):
                    v = rot(grp(a, c), rows)
                    z_ref[c, rows, :] = (v * Q_SCALE if c < 4 else v).astype(BF16)
                if not latent:
                    ak_ref[rc] = grp(a, 4, 2)

        @pl.when(j == 1)
        def _():
            for rc, rows, a in chunks():
                for c in range(2):
                    z_ref[c, rows, :] = rot(grp(a, c), rows).astype(BF16)
                for c in range(2, GROUPS_PER_TILE):
                    z_ref[c, rows, :] = grp(a, c).astype(BF16)
                if not latent:
                    ak_ref[rc] = grp(a, 0, 2)
                    for hd in range(HA):
                        av_ref[rc, :, hd, :] = grp(a, 2 + hd)

        @pl.when(j == 2)
        def _():
            bd = bd_ref[...]
            for rc, rows, a in chunks():
                for half in range(2):
                    qn = _head_rmsnorm(grp(a, 2 * half, 2), bd, gq_ref[...])
                    for c in range(2):
                        z_ref[2 * half + c, rows, :] = (rot(grp(qn, c), rows) * Q_SCALE).astype(BF16)
                kn = _head_rmsnorm(grp(a, 4), bd[0:LANES, 0:LANES], gk_ref[...])
                z_ref[4, rows, :] = rot(kn, rows).astype(BF16)
                z_ref[5, rows, :] = grp(a, 5).astype(BF16)
                if not latent:
                    bk_ref[rc] = kn
                    bv_ref[rc] = grp(a, 5)

        @pl.when(j == 3)
        def _():
            for rc, rows, a in chunks():
                for c in range(GROUPS_PER_TILE):
                    v = grp(a, c)
                    z_ref[c, rows, :] = (v * Q_SCALE if c < 4 else v).astype(BF16)
                if not latent:
                    ck_ref[rc] = grp(a, 4, 2)

        @pl.when(j == 4)
        def _():
            for rc, rows, a in chunks():
                for c in range(GROUPS_PER_TILE):
                    z_ref[c, rows, :] = grp(a, c).astype(BF16)
                if not latent:
                    ck_ref[rc] = grp(a, 0, 2)
                    cv_ref[rc] = grp(a, 2, 4)

        @pl.when(j >= N_QKV_TILES)
        def _():
            for rc, rows, a in chunks():
                a = a + bg_ref[...]
                for c in range(GROUPS_PER_TILE):
                    gate_ref[c, rows, :] = jax.nn.sigmoid(grp(a, c)).astype(BF16)

    return kern


def _inproj(latent, l, x, mod, g_norm1, w_in, b_gate, gq, gk, bd, rope_cos, rope_sin, caches):
    n_row = N_TOK // ROW_TILE
    last = n_row - 1

    def row_block(j, i, first_tile, last_tile):
        return jnp.where(j < first_tile, 0, jnp.where(j <= last_tile, i, last))

    mod_idx = (lambda j, i: (1 + i, 0, 0)) if latent else (lambda j, i: (0, 0, 0))
    in_specs = [
        pl.BlockSpec((ROW_TILE, D_MODEL), lambda j, i: (row_block(j, i, 0, 0), 0)),
        pl.BlockSpec((1, 1, 6 * D_MODEL), mod_idx),
        pl.BlockSpec((None, 1, D_MODEL), lambda j, i: (l, 0, 0)),
        pl.BlockSpec((None, D_MODEL, COL_TILE), lambda j, i: (l, 0, j)),
        pl.BlockSpec((None, 1, COL_TILE), lambda j, i: (l, 0, jnp.maximum(j - N_QKV_TILES, 0))),
        pl.BlockSpec((None, 1, 2 * LANES), lambda j, i: (l, 0, 0)),
        pl.BlockSpec((None, 1, LANES), lambda j, i: (l, 0, 0)),
        pl.BlockSpec((2 * LANES, 2 * LANES), lambda j, i: (0, 0)),
    ]
    args = [x, mod, g_norm1, w_in, b_gate, gq, gk, bd]
    out_specs = [
        pl.BlockSpec((GROUPS_PER_TILE, ROW_TILE, LANES),
                     lambda j, i: (jnp.minimum(j, N_QKV_TILES - 1), row_block(j, i, 0, N_QKV_TILES - 1), 0)),
        pl.BlockSpec((GROUPS_PER_TILE, ROW_TILE, LANES),
                     lambda j, i: (jnp.maximum(j - N_QKV_TILES, 0), row_block(j, i, N_QKV_TILES, N_COL_TILES), 0)),
    ]
    out_shape = [jax.ShapeDtypeStruct((N_QKV_GROUPS, N_TOK, LANES), BF16),
                 jax.ShapeDtypeStruct((N_GATE_GROUPS, N_TOK, LANES), BF16)]
    aliases = {}
    if latent:
        in_specs += [pl.BlockSpec((DEC_SEQ, LANES), lambda j, i: (0, 0))] * 2
        args += [rope_cos, rope_sin]
    else:
        nb = ROW_TILE // SEQ
        tiles = ((0, 1, 256), (1, 1, 512), (2, 2, 128), (2, 2, 128), (3, 4, 256), (4, 4, 512))
        for k, (w, (t0, t1, bw)) in enumerate(zip(CACHE_WIDTHS, tiles)):
            if caches is not None:
                in_specs.append(pl.BlockSpec(memory_space=pl.ANY))
                args.append(caches[k])
                aliases[8 + k] = 2 + k

            def cache_idx(j, i, t0=t0, t1=t1):
                return (row_block(j, i, t0, t1), l, 0, jnp.where(j <= t0, 0, (t1 - t0)))

            if k == 1:
                out_specs.append(pl.BlockSpec((nb, None, SEQ, HA, 2 * DA),
                                              lambda j, i, f=cache_idx: f(j, i) + (0,)))
                out_shape.append(jax.ShapeDtypeStruct((BATCH, DEPTH, SEQ, HA, 2 * DA), F32))
                continue
            out_specs.append(pl.BlockSpec((nb, None, SEQ, bw), cache_idx))
            out_shape.append(jax.ShapeDtypeStruct((BATCH, DEPTH, SEQ, w), F32))
    return pl.pallas_call(
        _make_inproj_kernel(latent, len(aliases)),
        grid=(N_COL_TILES, n_row),
        in_specs=in_specs,
        out_specs=out_specs,
        out_shape=out_shape,
        scratch_shapes=[pltpu.VMEM((N_TOK, D_MODEL), BF16), pltpu.VMEM((D_MODEL, COL_TILE), BF16)],
        input_output_aliases=aliases,
        compiler_params=_params(("arbitrary", "arbitrary")),
        name="inproj_lat" if latent else "inproj_ctx",
    )(*args)


def _mask_head(q, head):
    qf = q.astype(F32)
    keep = _lane_head(qf.shape) == head
    return jnp.where(keep, qf, 0.0).astype(BF16)


def _stack_heads(q):
    return jnp.concatenate([_mask_head(q, 0), _mask_head(q, 1)], axis=0)


def _unstack_heads(o, rows):
    return jnp.where(_lane_head((rows, LANES)) == 1, o[rows:2 * rows], o[0:rows])


def _dup_head(kv, head):
    f = kv.astype(F32)
    r = pltpu.roll(f, HEAD_DIM, 1)
    return jnp.where(_lane_head(f.shape) == head, f, r).astype(BF16)


def _attend(qm, ks, vs):
    nt = (False,) * len(ks)
    return _attend_multi([(qm, ks, vs, None, nt, nt)])[0]


def _attend_multi(jobs):
    sss = []
    for qm, ks, _, biases, k_t, _ in jobs:
        ss = [jnp.dot(qm, k, preferred_element_type=F32) if t else
              lax.dot_general(qm, k, _NT, preferred_element_type=F32) for k, t in zip(ks, k_t)]
        if biases is not None:
            ss = [s if b is None else s + b for s, b in zip(ss, biases)]
        sss.append(ss)
    pss, dens = [], []
    for ss in sss:
        m = functools.reduce(jnp.maximum, [jnp.max(s, axis=-1, keepdims=True) for s in ss])
        ps = [jnp.exp2(s - m) for s in ss]
        dens.append(functools.reduce(jnp.add, [jnp.sum(p, axis=-1, keepdims=True) for p in ps]))
        pss.append(ps)
    outs = []
    for (_, _, vs, _, _, v_t), ps, den in zip(jobs, pss, dens):
        o = functools.reduce(jnp.add, [lax.dot_general(p.astype(BF16), v, _NT, preferred_element_type=F32) if t else
                                        jnp.dot(p.astype(BF16), v, preferred_element_type=F32)
                                        for p, v, t in zip(ps, vs, v_t)])
        outs.append(o * (1.0 / den))
    return outs


_CTX_T = (False, True)


def _pair_t(ref):
    return ref[...].reshape(2 * HEAD_DIM, ref.shape[-1]).astype(BF16)


def _diff_lambda(lam_ref, lam_init):
    lq = lam_ref[...]
    a = jnp.sum(lq[0:1] * lq[1:2], axis=-1, keepdims=True)
    b = jnp.sum(lq[2:3] * lq[3:4], axis=-1, keepdims=True)
    return jnp.exp(a) - jnp.exp(b) + lam_init


def _diff_combine(o1, o2, lam, gsub, lam_init):
    o = o1 - lam * o2
    ms = jnp.mean(o * o, axis=-1, keepdims=True)
    return (o * lax.rsqrt(ms + EPS) * gsub) * (1.0 - lam_init)


CTX_BATCH_PER_STEP = 2


def _make_attn_ctx_kernel(lam_init):
    def kern(z_ref, lam_ref, gsub_ref, oa_ref, ob_ref, oc_ref):
        lam = _diff_lambda(lam_ref, lam_init)
        gsub = gsub_ref[...]
        for bb in range(CTX_BATCH_PER_STEP):
            rows = pl.ds(bb * SEQ, SEQ)

            def z(g):
                return z_ref[g, rows, :]

            for vh in range(HA):
                hi = vh % 2
                v = z(G_VA + vh)
                o1 = _attend(_mask_head(z(G_QA + vh // 2), hi), [z(G_KA + vh // 2)], [v])
                o2 = _attend(_mask_head(z(G_QA + 2 + vh // 2), hi), [z(G_KA + 2 + vh // 2)], [v])
                oa_ref[vh, rows, :] = _diff_combine(o1, o2, lam, gsub, lam_init).astype(BF16)
            for g in range(KVB):
                kd = _dup_head(z(G_KB), g)
                vd = _dup_head(z(G_VB), g)
                for c in range(2 * g, 2 * g + 2):
                    o = _attend(_stack_heads(z(G_QB + c)), [kd], [vd])
                    ob_ref[c, rows, :] = _unstack_heads(o, SEQ).astype(BF16)
            for c in range(HC // 2):
                o = _attend(_stack_heads(z(G_QC + c)), [z(G_KC + c)], [z(G_VC + c)])
                oc_ref[c, rows, :] = _unstack_heads(o, SEQ).astype(BF16)
    return kern


def _attn_ctx(l, z, lambda_qk, g_subln):
    lam_init = 0.8 - 0.6 * math.exp(-0.3 * l)
    rows = CTX_BATCH_PER_STEP * SEQ
    o_spec = pl.BlockSpec((4, rows, LANES), lambda b: (0, b, 0))
    o_shape = jax.ShapeDtypeStruct((4, N_TOK, LANES), BF16)
    return pl.pallas_call(
        _make_attn_ctx_kernel(lam_init),
        grid=(BATCH // CTX_BATCH_PER_STEP,),
        in_specs=[pl.BlockSpec((N_QKV_GROUPS, rows, LANES), lambda b: (0, b, 0)),
                  pl.BlockSpec((None, 4, DA), lambda b: (l, 0, 0)),
                  pl.BlockSpec((None, 1, 2 * DA), lambda b: (l, 0, 0))],
        out_specs=[o_spec, o_spec, o_spec],
        out_shape=[o_shape, o_shape, o_shape],
        compiler_params=_params(("arbitrary",)),
        name="attn_ctx",
    )(z, lambda_qk, g_subln)


Q_BLK_GQA = 512
Q_BLK_DIFF = 1024


def _make_attn_lat_a_kernel(lam_init):
    def kern(q1_ref, q2_ref, k1_ref, k2_ref, v_ref, ck1_ref, ck2_ref, cv_ref, lam_ref, gsub_ref, o_ref):
        vh = pl.program_id(1)
        hi = jnp.bitwise_and(vh, 1)
        lam = _diff_lambda(lam_ref, lam_init)
        gsub = gsub_ref[...]
        ck1 = _pair_t(ck1_ref)
        ck2 = _pair_t(ck2_ref)
        cv = cv_ref[:, vh, :].astype(BF16)
        k1, k2, v = k1_ref[0], k2_ref[0], v_ref[0]
        rows = [pl.ds(qb * Q_BLK_DIFF, Q_BLK_DIFF) for qb in range(DEC_SEQ // Q_BLK_DIFF)]
        jobs = []
        for r in rows:
            jobs.append((_mask_head(q1_ref[0, r, :], hi), [k1, ck1], [v, cv], None, _CTX_T, (False, False)))
            jobs.append((_mask_head(q2_ref[0, r, :], hi), [k2, ck2], [v, cv], None, _CTX_T, (False, False)))
        outs = _attend_multi(jobs)
        for t, r in enumerate(rows):
            o_ref[0, r, :] = _diff_combine(outs[2 * t], outs[2 * t + 1], lam, gsub, lam_init).astype(BF16)
    return kern


def _attn_lat_a(l, z, cache_k_t, cache_v, lambda_qk, g_subln):
    lam_init = 0.8 - 0.6 * math.exp(-0.3 * l)

    def zspec(fn):
        return pl.BlockSpec((1, DEC_SEQ, LANES), lambda b, h: (fn(h), b, 0))

    return pl.pallas_call(
        _make_attn_lat_a_kernel(lam_init),
        grid=(DEC_BATCH, HA),
        in_specs=[zspec(lambda h: G_QA + h // 2), zspec(lambda h: G_QA + 2 + h // 2),
                  zspec(lambda h: G_KA + h // 2), zspec(lambda h: G_KA + 2 + h // 2),
                  zspec(lambda h: G_VA + h),
                  pl.BlockSpec((None, None, 2, DA, PAST_LEN), lambda b, h: (b, l, h // 2, 0, 0)),
                  pl.BlockSpec((None, None, 2, DA, PAST_LEN), lambda b, h: (b, l, HA // 2 + h // 2, 0, 0)),
                  pl.BlockSpec((None, None, PAST_LEN, HA, 2 * DA), lambda b, h: (b, l, 0, 0, 0)),
                  pl.BlockSpec((None, 4, DA), lambda b, h: (l, 0, 0)),
                  pl.BlockSpec((None, 1, 2 * DA), lambda b, h: (l, 0, 0))],
        out_specs=pl.BlockSpec((1, DEC_SEQ, LANES), lambda b, h: (h, b, 0)),
        out_shape=jax.ShapeDtypeStruct((4, N_TOK, LANES), BF16),
        compiler_params=_params(("arbitrary", "arbitrary")),
        name="attn_lat_a",
    )(z, z, z, z, z, cache_k_t, cache_k_t, cache_v, lambda_qk, g_subln)


def _attn_lat_b_kernel(q_ref, k_ref, v_ref, ck_ref, cv_ref, o_ref):
    hi = lax.shift_right_logical(pl.program_id(1), 1)
    kd = _dup_head(k_ref[0], hi)
    vd = _dup_head(v_ref[0], hi)
    ck = ck_ref[hi].astype(BF16)
    cv = cv_ref[hi].astype(BF16)
    ckd = jnp.concatenate([ck, ck], axis=0)
    cvd = jnp.concatenate([cv, cv], axis=0)
    rows = [pl.ds(qb * Q_BLK_GQA, Q_BLK_GQA) for qb in range(DEC_SEQ // Q_BLK_GQA)]
    outs = _attend_multi([(_stack_heads(q_ref[0, r, :]), [kd, ckd], [vd, cvd], None, _CTX_T, _CTX_T)
                          for r in rows])
    for r, o in zip(rows, outs):
        o_ref[0, r, :] = _unstack_heads(o, Q_BLK_GQA).astype(BF16)


def _attn_lat_b(l, z, cache_k, cache_v):
    cspec = pl.BlockSpec((None, None, KVB, DB, PAST_LEN), lambda b, c: (b, l, 0, 0, 0))
    return pl.pallas_call(
        _attn_lat_b_kernel,
        grid=(DEC_BATCH, HB // 2),
        in_specs=[pl.BlockSpec((1, DEC_SEQ, LANES), lambda b, c: (G_QB + c, b, 0)),
                  pl.BlockSpec((1, DEC_SEQ, LANES), lambda b, c: (G_KB, b, 0)),
                  pl.BlockSpec((1, DEC_SEQ, LANES), lambda b, c: (G_VB, b, 0)),
                  cspec, cspec],
        out_specs=pl.BlockSpec((1, DEC_SEQ, LANES), lambda b, c: (c, b, 0)),
        out_shape=jax.ShapeDtypeStruct((4, N_TOK, LANES), BF16),
        compiler_params=_params(("arbitrary", "arbitrary")),
        name="attn_lat_b",
    )(z, z, z, cache_k, cache_v)


N_GRID_ROWS = DEC_SEQ // GRID_W
NA_KEYS = NA_ROWS * GRID_W


NA_GROUP = 2
NA_MAX_KEY_ROWS = 12
BIAS_TABLE_ROWS = 16


def _window_start(r):
    return min(max(r - NA_ROWS // 2, 0), N_GRID_ROWS - NA_ROWS)


def _group_keys(g):
    starts = [_window_start(r) for r in range(g * NA_GROUP, (g + 1) * NA_GROUP)]
    n = max(starts) + NA_ROWS - min(starts)
    n += n % 2
    return min(min(starts), N_GRID_ROWS - n), n


def _build_window_bias(tab_ref, bias_scr):
    qcol = lax.broadcasted_iota(jnp.int32, (GRID_W, LANES), 0)
    lane = lax.broadcasted_iota(jnp.int32, (GRID_W, LANES), 1)
    kcol = lane & (GRID_W - 1)
    c0 = jnp.clip(qcol - NA_COLS // 2, 0, GRID_W - NA_COLS)
    in_win = (kcol >= c0) & (kcol < c0 + NA_COLS)
    masks = {(True, True): in_win, (True, False): in_win & (lane < GRID_W), (False, True): in_win & (lane >= GRID_W)}
    for head in range(2):
        tiles = {}

        def tile(drs):
            if drs not in tiles:
                if drs == (None, None):
                    tiles[drs] = jnp.full((GRID_W, LANES), NEG, F32)
                else:
                    vec = None
                    for half, dr in enumerate(drs):
                        if dr is not None:
                            part = tab_ref[head, pl.ds(half * BIAS_TABLE_ROWS + dr, 1), :]
                            vec = part if vec is None else vec + part
                    toep = pltpu.roll(jnp.broadcast_to(vec, (GRID_W, LANES)), 0, 1, stride=1, stride_axis=0)
                    valid = tuple(dr is not None for dr in drs)
                    tiles[drs] = jnp.where(masks[valid], toep * LOG2E, NEG)
            return tiles[drs]

        for r in range(N_GRID_ROWS):
            u0, n = _group_keys(r // NA_GROUP)
            rows = pl.ds(r * GRID_W, GRID_W)
            for m in range(n // 2):
                drs = tuple(kr - r + NA_ROWS - 1 if _window_start(r) <= kr < _window_start(r) + NA_ROWS else None
                            for kr in (u0 + 2 * m, u0 + 2 * m + 1))
                bias_scr[head, rows, pl.ds(m * LANES, LANES)] = tile(drs)


def _attn_lat_c_kernel(q_ref, k_ref, v_ref, ck_ref, cv_ref, tab_ref, o_ref, bias_scr):
    @pl.when(pl.program_id(1) == 0)
    def _():
        _build_window_bias(tab_ref, bias_scr)

    ck = _pair_t(ck_ref)
    cv = _pair_t(cv_ref)
    n_rows = NA_GROUP * GRID_W
    jobs, row_slices = [], []
    for g in range(N_GRID_ROWS // NA_GROUP):
        u0, n = _group_keys(g)
        rows = pl.ds(g * n_rows, n_rows)
        keys = pl.ds(u0 * GRID_W, n * GRID_W)
        bias = jnp.concatenate([bias_scr[0, rows, 0:n * GRID_W], bias_scr[1, rows, 0:n * GRID_W]], axis=0)
        jobs.append((_stack_heads(q_ref[0, rows, :]), [k_ref[0, keys, :], ck], [v_ref[0, keys, :], cv],
                     [bias, None], _CTX_T, _CTX_T))
        row_slices.append(rows)
    for rows, o in zip(row_slices, _attend_multi(jobs)):
        o_ref[0, rows, :] = _unstack_heads(o, n_rows).astype(BF16)


def _attn_lat_c(l, z, cache_k, cache_v, bias_rows):
    def zspec(g0):
        return pl.BlockSpec((1, DEC_SEQ, LANES), lambda c, b: (g0 + c, b, 0))

    cspec = pl.BlockSpec((None, None, 2, DC, PAST_LEN), lambda c, b: (b, l, c, 0, 0))
    return pl.pallas_call(
        _attn_lat_c_kernel,
        grid=(HC // 2, DEC_BATCH),
        in_specs=[zspec(G_QC), zspec(G_KC), zspec(G_VC), cspec, cspec,
                  pl.BlockSpec((None, 2, 2 * BIAS_TABLE_ROWS, LANES), lambda c, b: (l, c, 0, 0))],
        out_specs=pl.BlockSpec((1, DEC_SEQ, LANES), lambda c, b: (c, b, 0)),
        out_shape=jax.ShapeDtypeStruct((4, N_TOK, LANES), BF16),
        scratch_shapes=[pltpu.VMEM((2, DEC_SEQ, NA_MAX_KEY_ROWS * GRID_W), F32)],
        compiler_params=_params(("arbitrary", "arbitrary")),
        name="attn_lat_c",
    )(z, z, z, cache_k, cache_v, bias_rows)


MERGE_TILE = 512


def _merge_kernel(x_ref, oa_ref, ob_ref, oc_ref, gate_ref, mod_ref, wa_ref, wb_ref, wc_ref, wo_ref,
                  xo_ref, wbr_scr, wo_scr):
    @pl.when(pl.program_id(0) == 0)
    def _():
        wbr_scr[0] = wa_ref[...].astype(BF16)
        wbr_scr[1] = wb_ref[...].astype(BF16)
        wbr_scr[2] = wc_ref[...].astype(BF16)
        wo_scr[...] = wo_ref[...].astype(BF16)

    m = mod_ref[0]
    for rc in range(MERGE_TILE // ROW_CHUNK):
        rows = pl.ds(rc * ROW_CHUNK, ROW_CHUNK)
        y = None
        for k, o_ref in enumerate((oa_ref, ob_ref, oc_ref)):
            o = jnp.concatenate([o_ref[c, rows, :] for c in range(4)], axis=-1)
            p = jnp.dot(o, wbr_scr[k], preferred_element_type=F32)
            g = jnp.concatenate([gate_ref[8 * k + c, rows, :] for c in range(8)], axis=-1).astype(F32)
            y = g * p if y is None else y + g * p
        out = jnp.dot(y.astype(BF16), wo_scr[...], preferred_element_type=F32)
        xo_ref[rows, :] = x_ref[rows, :] + m[:, 2 * D_MODEL:3 * D_MODEL] * out


def _merge(latent, l, x, oa, ob, oc, gates, mod, w_a, w_b, w_c, w_o):
    tm = MERGE_TILE
    if latent:
        mod_idx = lambda i: (1 + (i * tm) // DEC_SEQ, 0, 0)
    else:
        mod_idx = lambda i: (0, 0, 0)
    o_spec = pl.BlockSpec((4, tm, LANES), lambda i: (0, i, 0))
    wbr_spec = pl.BlockSpec((None, 4 * LANES, D_MODEL), lambda i: (l, 0, 0))
    return pl.pallas_call(
        _merge_kernel,
        grid=(N_TOK // tm,),
        in_specs=[pl.BlockSpec((tm, D_MODEL), lambda i: (i, 0)),
                  o_spec, o_spec, o_spec,
                  pl.BlockSpec((N_GATE_GROUPS, tm, LANES), lambda i: (0, i, 0)),
                  pl.BlockSpec((1, 1, 6 * D_MODEL), mod_idx),
                  wbr_spec, wbr_spec, wbr_spec,
                  pl.BlockSpec((None, D_MODEL, D_MODEL), lambda i: (l, 0, 0))],
        out_specs=pl.BlockSpec((tm, D_MODEL), lambda i: (i, 0)),
        out_shape=jax.ShapeDtypeStruct((N_TOK, D_MODEL), F32),
        scratch_shapes=[pltpu.VMEM((3, 4 * LANES, D_MODEL), BF16), pltpu.VMEM((D_MODEL, D_MODEL), BF16)],
        compiler_params=_params(("arbitrary",)),
        name="merge_lat" if latent else "merge_ctx",
    )(x, oa, ob, oc, gates, mod, w_a, w_b, w_c, w_o)


FF_TILE = 1024
FF_CHUNK = 512


def _ffn_kernel(x_ref, mod_ref, g2_ref, w1_ref, w2_ref, gf_ref, xo_ref, acc_scr, h2_scr, *, final):
    f = pl.program_id(1)

    def run(first):
        w1 = w1_ref[...].astype(BF16)
        w2 = w2_ref[...].astype(BF16)
        m = mod_ref[0]
        for rc in range(ROW_TILE // FF_CHUNK):
            rows = pl.ds(rc * FF_CHUNK, FF_CHUNK)
            if first:
                x = x_ref[rows, :]
                ms = jnp.mean(x * x, axis=-1, keepdims=True)
                y = x * lax.rsqrt(ms + EPS) * g2_ref[...]
                h2 = (y * (1.0 + m[:, 4 * D_MODEL:5 * D_MODEL]) + m[:, 3 * D_MODEL:4 * D_MODEL]).astype(BF16)
                h2_scr[rows, :] = h2
            else:
                h2 = h2_scr[rows, :]
            u = jnp.dot(h2, w1, preferred_element_type=F32)
            u = jnp.square(jnp.maximum(u, 0.0)).astype(BF16)
            d = jnp.dot(u, w2, preferred_element_type=F32)
            if first:
                acc_scr[rows, :] = d
            else:
                acc_scr[rows, :] += d

    @pl.when(f == 0)
    def _():
        run(True)

    @pl.when(f > 0)
    def _():
        run(False)

    @pl.when(f == D_FF // FF_TILE - 1)
    def _():
        x = x_ref[...] + mod_ref[0][:, 5 * D_MODEL:6 * D_MODEL] * acc_scr[...]
        if final:
            ms = jnp.mean(x * x, axis=-1, keepdims=True)
            x = x * lax.rsqrt(ms + EPS) * gf_ref[...]
        xo_ref[...] = x


def _ffn(latent, l, x, mod, g_norm2, w1, w2, g_final):
    tm = ROW_TILE
    mod_idx = (lambda i, f: (1 + i, 0, 0)) if latent else (lambda i, f: (0, 0, 0))
    return pl.pallas_call(
        functools.partial(_ffn_kernel, final=(l == DEPTH - 1)),
        grid=(N_TOK // tm, D_FF // FF_TILE),
        in_specs=[pl.BlockSpec((tm, D_MODEL), lambda i, f: (i, 0)),
                  pl.BlockSpec((1, 1, 6 * D_MODEL), mod_idx),
                  pl.BlockSpec((None, 1, D_MODEL), lambda i, f: (l, 0, 0)),
                  pl.BlockSpec((None, D_MODEL, FF_TILE), lambda i, f: (l, 0, f)),
                  pl.BlockSpec((None, FF_TILE, D_MODEL), lambda i, f: (l, f, 0)),
                  pl.BlockSpec((1, D_MODEL), lambda i, f: (0, 0))],
        out_specs=pl.BlockSpec((tm, D_MODEL), lambda i, f: (i, 0)),
        out_shape=jax.ShapeDtypeStruct((N_TOK, D_MODEL), F32),
        scratch_shapes=[pltpu.VMEM((tm, D_MODEL), F32), pltpu.VMEM((tm, D_MODEL), BF16)],
        compiler_params=_params(("arbitrary", "arbitrary")),
        name="ffn_lat" if latent else "ffn_ctx",
    )(x, mod, g_norm2, w1, w2, g_final)


def _rope_tables():
    nf = HEAD_DIM // 4
    t = jnp.arange(DEC_SEQ)
    row = (t // GRID_W).astype(F32)
    col = (t % GRID_W).astype(F32)
    inv = ROPE_BASE ** (-jnp.arange(nf, dtype=F32) / nf)
    ar = row[:, None] * inv[None, :]
    ac = col[:, None] * inv[None, :]
    cos = jnp.concatenate([jnp.cos(ar), jnp.cos(ar), jnp.cos(ac), jnp.cos(ac)], axis=-1)
    sin = jnp.concatenate([-jnp.sin(ar), jnp.sin(ar), -jnp.sin(ac), jnp.sin(ac)], axis=-1)
    return jnp.tile(cos, (1, 2)), jnp.tile(sin, (1, 2))


def _packed_bias_rows(rel_bias):
    n = 2 * NA_ROWS - 1
    first = jnp.concatenate([rel_bias[..., NA_COLS - 1:], jnp.zeros((DEPTH, HC, n, LANES - (2 * NA_COLS - 1)), F32),
                             rel_bias[..., :NA_COLS - 1]], axis=-1)
    lo = GRID_W - NA_COLS + 1
    second = jnp.pad(rel_bias, ((0, 0), (0, 0), (0, 0), (lo, LANES - lo - (2 * NA_COLS - 1))))
    pad_rows = ((0, 0), (0, 0), (0, BIAS_TABLE_ROWS - n), (0, 0))
    return jnp.concatenate([jnp.pad(first, pad_rows), jnp.pad(second, pad_rows)], axis=2)


def kernel(x_prompt, x_sample, c, cache_a_k, cache_a_v, cache_b_k, cache_b_v, cache_c_k, cache_c_v, c_ctx, w_mod, b_mod, g_norm1, g_norm2, w_in, b_gate, lambda_qk, g_subln, g_qnorm, g_knorm, rel_bias, w_branch_a, w_branch_b, w_branch_c, w_out, w_ff1, w_ff2, g_final):
    xp = x_prompt.reshape(N_TOK, D_MODEL)
    xs = x_sample.reshape(N_TOK, D_MODEL)

    cvec = jnp.concatenate([c_ctx[None, :], c, jnp.zeros((3, D_MODEL), F32)], axis=0)
    mods = _modulation(cvec, w_mod, b_mod)

    rope_cos, rope_sin = _rope_tables()
    bias_rows = _packed_bias_rows(rel_bias)
    bd = jnp.kron(jnp.eye(2 * LANES // HEAD_DIM, dtype=F32),
                  jnp.full((HEAD_DIM, HEAD_DIM), 1.0 / HEAD_DIM, F32)).astype(BF16)
    gq = jnp.tile(g_qnorm, (1, 2 * LANES // DB)).reshape(DEPTH, 1, 2 * LANES)
    gk = jnp.tile(g_knorm, (1, LANES // DB)).reshape(DEPTH, 1, LANES)
    g1 = g_norm1.reshape(DEPTH, 1, D_MODEL)
    g2 = g_norm2.reshape(DEPTH, 1, D_MODEL)
    bg = b_gate.reshape(DEPTH, 1, 3 * D_MODEL)
    gsub = g_subln.reshape(DEPTH, 1, 2 * DA)
    gf = g_final.reshape(1, D_MODEL)


    def heads_t(cache):
        return cache.transpose(0, 1, 3, 4, 2)

    ctx_ak, ctx_bk, ctx_bv = heads_t(cache_a_k), heads_t(cache_b_k), heads_t(cache_b_v)
    ctx_ck, ctx_cv = heads_t(cache_c_k), heads_t(cache_c_v)

    new_caches = None

    for l in range(DEPTH):
        mod = mods[l].reshape(8, 1, 6 * D_MODEL)

        outs = _inproj(False, l, xp, mod, g1, w_in, bg, gq, gk, bd, None, None, new_caches)
        z, gates, new_caches = outs[0], outs[1], list(outs[2:])
        oa, ob, oc = _attn_ctx(l, z, lambda_qk, gsub)
        xp = _merge(False, l, xp, oa, ob, oc, gates, mod, w_branch_a, w_branch_b, w_branch_c, w_out)
        xp = _ffn(False, l, xp, mod, g2, w_ff1, w_ff2, gf)

        z, gates = _inproj(True, l, xs, mod, g1, w_in, bg, gq, gk, bd, rope_cos, rope_sin, None)
        oa = _attn_lat_a(l, z, ctx_ak, cache_a_v, lambda_qk, gsub)
        ob = _attn_lat_b(l, z, ctx_bk, ctx_bv)
        oc = _attn_lat_c(l, z, ctx_ck, ctx_cv, bias_rows)
        xs = _merge(True, l, xs, oa, ob, oc, gates, mod, w_branch_a, w_branch_b, w_branch_c, w_out)
        xs = _ffn(True, l, xs, mod, g2, w_ff1, w_ff2, gf)

    y_prompt = xp.reshape(BATCH, SEQ, D_MODEL)
    y_sample = xs.reshape(DEC_BATCH, DEC_SEQ, D_MODEL)
    ak, av, bk, bv, ck, cv = new_caches
    return (y_prompt, y_sample,
            ak.reshape(BATCH, DEPTH, SEQ, 2 * HA, DA), av.reshape(BATCH, DEPTH, SEQ, HA, 2 * DA),
            bk.reshape(BATCH, DEPTH, SEQ, KVB, DB), bv.reshape(BATCH, DEPTH, SEQ, KVB, DB),
            ck.reshape(BATCH, DEPTH, SEQ, HC, DC), cv.reshape(BATCH, DEPTH, SEQ, HC, DC))
```

```python
import functools
import math

import jax
import jax.numpy as jnp
from jax import lax
from jax.experimental import pallas as pl
from jax.experimental.pallas import tpu as pltpu

D_MODEL = 1024
BATCH = 16
SEQ = 256
DEPTH = 4
DEC_BATCH = 4
DEC_SEQ = 1024
PAST_LEN = 256
GRID_W = 64
HA, DA = 4, 64
HB, KVB, DB = 8, 2, 64
HC, DC = 8, 64
NA_ROWS, NA_COLS = 8, 16
D_FF = 4 * D_MODEL
ROPE_BASE = 10000.0
EPS = 1e-6
NEG = -1e30
HEAD_DIM = 64
LOG2E = math.log2(math.e)
Q_SCALE = HEAD_DIM ** -0.5 * LOG2E

F32 = jnp.float32
BF16 = jnp.bfloat16

LANES = 128
N_QKV_GROUPS = 30
N_GATE_GROUPS = 24
COL_TILE = 768
GROUPS_PER_TILE = COL_TILE // LANES
N_QKV_TILES = N_QKV_GROUPS // GROUPS_PER_TILE
N_COL_TILES = N_QKV_TILES + N_GATE_GROUPS // GROUPS_PER_TILE
ROW_TILE = 1024
ROW_CHUNK = 256
N_TOK = BATCH * SEQ
Q_GROUPS = 4
ROPE_HALF = HEAD_DIM // 4
HEAD_SHIFT = HEAD_DIM.bit_length() - 1
MOD_TILE = 1536
VMEM_LIMIT_V7X = 58 * 1024 * 1024

G_QA, G_KA, G_VA, G_QB, G_KB, G_VB, G_QC, G_KC, G_VC = 0, 4, 8, 12, 16, 17, 18, 22, 26
CACHE_WIDTHS = (2 * HA * DA, HA * 2 * DA, KVB * DB, KVB * DB, HC * DC, HC * DC)

_NT = (((1,), (1,)), ((), ()))


def _params(sem, vmem=VMEM_LIMIT_V7X):
    return pltpu.CompilerParams(dimension_semantics=sem, vmem_limit_bytes=vmem)


def _lane_head(shape):
    return lax.shift_right_logical(lax.broadcasted_iota(jnp.int32, shape, len(shape) - 1), HEAD_SHIFT)


def _mod_kernel(c_ref, w_ref, b_ref, o_ref):
    c = c_ref[...]
    s = (c * jax.nn.sigmoid(c)).astype(BF16)
    o_ref[...] = jnp.dot(s, w_ref[...].astype(BF16), preferred_element_type=F32) + b_ref[...]


def _modulation(cvec, w_mod, b_mod):
    tn = MOD_TILE
    n6 = 6 * D_MODEL
    return pl.pallas_call(
        _mod_kernel,
        grid=(DEPTH, n6 // tn),
        in_specs=[pl.BlockSpec((8, D_MODEL), lambda l, n: (0, 0)),
                  pl.BlockSpec((None, D_MODEL, tn), lambda l, n: (l, 0, n)),
                  pl.BlockSpec((None, 1, tn), lambda l, n: (l, 0, n))],
        out_specs=pl.BlockSpec((None, 8, tn), lambda l, n: (l, 0, n)),
        out_shape=jax.ShapeDtypeStruct((DEPTH, 8, n6), F32),
        compiler_params=_params(("arbitrary", "arbitrary")),
        name="modulation",
    )(cvec, w_mod, b_mod.reshape(DEPTH, 1, n6))


def _rope(v, cos, sin):
    first = (lax.broadcasted_iota(jnp.int32, v.shape, 1) & ROPE_HALF) == 0
    partner = jnp.where(first, pltpu.roll(v, LANES - ROPE_HALF, 1), pltpu.roll(v, ROPE_HALF, 1))
    return v * cos + partner * sin


def _head_rmsnorm(v, bd, g):
    msq = jnp.dot((v * v).astype(BF16), bd, preferred_element_type=F32)
    return v * lax.rsqrt(msq + EPS) * g


def _make_inproj_kernel(latent, n_aliased):
    def kern(*refs):
        if latent:
            (x_ref, mod_ref, g1_ref, w_ref, bg_ref, gq_ref, gk_ref, bd_ref, cos_ref, sin_ref,
             z_ref, gate_ref, h_scr, wbf_scr) = refs
        else:
            (x_ref, mod_ref, g1_ref, w_ref, bg_ref, gq_ref, gk_ref, bd_ref) = refs[:8]
            (z_ref, gate_ref, ak_ref, av_ref, bk_ref, bv_ref, ck_ref, cv_ref,
             h_scr, wbf_scr) = refs[8 + n_aliased:]
        j = pl.program_id(0)
        i = pl.program_id(1)

        @pl.when(i == 0)
        def _():
            wbf_scr[...] = w_ref[...].astype(BF16)

        def chunks(first=False):
            for rc in range(ROW_TILE // ROW_CHUNK):
                rows = pl.ds(rc * ROW_CHUNK, ROW_CHUNK)
                tok = pl.ds(pl.multiple_of(i * ROW_TILE + rc * ROW_CHUNK, ROW_CHUNK), ROW_CHUNK)
                if first:
                    x = x_ref[rows, :]
                    ms = jnp.mean(x * x, axis=-1, keepdims=True)
                    y = x * lax.rsqrt(ms + EPS) * g1_ref[...]
                    m = mod_ref[0]
                    h = (y * (1.0 + m[:, D_MODEL:2 * D_MODEL]) + m[:, 0:D_MODEL]).astype(BF16)
                    h_scr[tok, :] = h
                else:
                    h = h_scr[tok, :]
                yield rc, rows, jnp.dot(h, wbf_scr[...], preferred_element_type=F32)

        def grp(a, c, n=1):
            return a[:, c * LANES:(c + n) * LANES]

        def rot(v, rows):
            return _rope(v, cos_ref[rows, :], sin_ref[rows, :]) if latent else v

        @pl.when(j == 0)
        def _():
            for rc, rows, a in chunks(first=True):
                for c in range(GROUPS_PER_TILE):
                    v = rot(grp(a, c), rows)
                    z_ref[c, rows, :] = (v * Q_SCALE if c < Q_GROUPS else v).astype(BF16)
                if not latent:
                    ak_ref[rc] = grp(a, 4, 2)

        @pl.when(j == 1)
        def _():
            for rc, rows, a in chunks():
                for c in range(2):
                    z_ref[c, rows, :] = rot(grp(a, c), rows).astype(BF16)
                for c in range(2, GROUPS_PER_TILE):
                    z_ref[c, rows, :] = grp(a, c).astype(BF16)
                if not latent:
                    ak_ref[rc] = grp(a, 0, 2)
                    for hd in range(HA):
                        av_ref[rc, :, hd, :] = grp(a, 2 + hd)

        @pl.when(j == 2)
        def _():
            bd = bd_ref[...]
            for rc, rows, a in chunks():
                for half in range(2):
                    qn = _head_rmsnorm(grp(a, 2 * half, 2), bd, gq_ref[...])
                    for c in range(2):
                        z_ref[2 * half + c, rows, :] = (rot(grp(qn, c), rows) * Q_SCALE).astype(BF16)
                kn = _head_rmsnorm(grp(a, 4), bd[0:LANES, 0:LANES], gk_ref[...])
                z_ref[4, rows, :] = rot(kn, rows).astype(BF16)
                z_ref[5, rows, :] = grp(a, 5).astype(BF16)
                if not latent:
                    bk_ref[rc] = kn
                    bv_ref[rc] = grp(a, 5)

        @pl.when(j == 3)
        def _():
            for rc, rows, a in chunks():
                for c in range(GROUPS_PER_TILE):
                    v = grp(a, c)
                    z_ref[c, rows, :] = (v * Q_SCALE if c < Q_GROUPS else v).astype(BF16)
                if not latent:
                    ck_ref[rc] = grp(a, 4, 2)

        @pl.when(j == 4)
        def _():
            for rc, rows, a in chunks():
                for c in range(GROUPS_PER_TILE):
                    z_ref[c, rows, :] = grp(a, c).astype(BF16)
                if not latent:
                    ck_ref[rc] = grp(a, 0, 2)
                    cv_ref[rc] = grp(a, 2, 4)

        @pl.when(j >= N_QKV_TILES)
        def _():
            for rc, rows, a in chunks():
                a = a + bg_ref[...]
                for c in range(GROUPS_PER_TILE):
                    gate_ref[c, rows, :] = jax.nn.sigmoid(grp(a, c)).astype(BF16)

    return kern


def _inproj(latent, l, x, mod, g_norm1, w_in, b_gate, gq, gk, bd, rope_cos, rope_sin, caches):
    n_row = N_TOK // ROW_TILE
    last = n_row - 1

    def row_block(j, i, first_tile, last_tile):
        return jnp.where(j < first_tile, 0, jnp.where(j <= last_tile, i, last))

    mod_idx = (lambda j, i: (1 + i, 0, 0)) if latent else (lambda j, i: (0, 0, 0))
    in_specs = [
        pl.BlockSpec((ROW_TILE, D_MODEL), lambda j, i: (row_block(j, i, 0, 0), 0)),
        pl.BlockSpec((1, 1, 6 * D_MODEL), mod_idx),
        pl.BlockSpec((None, 1, D_MODEL), lambda j, i: (l, 0, 0)),
        pl.BlockSpec((None, D_MODEL, COL_TILE), lambda j, i: (l, 0, j)),
        pl.BlockSpec((None, 1, COL_TILE), lambda j, i: (l, 0, jnp.maximum(j - N_QKV_TILES, 0))),
        pl.BlockSpec((None, 1, 2 * LANES), lambda j, i: (l, 0, 0)),
        pl.BlockSpec((None, 1, LANES), lambda j, i: (l, 0, 0)),
        pl.BlockSpec((2 * LANES, 2 * LANES), lambda j, i: (0, 0)),
    ]
    args = [x, mod, g_norm1, w_in, b_gate, gq, gk, bd]
    out_specs = [
        pl.BlockSpec((GROUPS_PER_TILE, ROW_TILE, LANES),
                     lambda j, i: (jnp.minimum(j, N_QKV_TILES - 1), row_block(j, i, 0, N_QKV_TILES - 1), 0)),
        pl.BlockSpec((GROUPS_PER_TILE, ROW_TILE, LANES),
                     lambda j, i: (jnp.maximum(j - N_QKV_TILES, 0), row_block(j, i, N_QKV_TILES, N_COL_TILES), 0)),
    ]
    out_shape = [jax.ShapeDtypeStruct((N_QKV_GROUPS, N_TOK, LANES), BF16),
                 jax.ShapeDtypeStruct((N_GATE_GROUPS, N_TOK, LANES), BF16)]
    aliases = {}
    if latent:
        in_specs += [pl.BlockSpec((DEC_SEQ, LANES), lambda j, i: (0, 0))] * 2
        args += [rope_cos, rope_sin]
    else:
        nb = ROW_TILE // SEQ
        tiles = ((0, 1, 256), (1, 1, 512), (2, 2, 128), (2, 2, 128), (3, 4, 256), (4, 4, 512))
        for k, (w, (t0, t1, bw)) in enumerate(zip(CACHE_WIDTHS, tiles)):
            if caches is not None:
                in_specs.append(pl.BlockSpec(memory_space=pl.ANY))
                args.append(caches[k])
                aliases[8 + k] = 2 + k

            def cache_idx(j, i, t0=t0, t1=t1):
                return (row_block(j, i, t0, t1), l, 0, jnp.where(j <= t0, 0, (t1 - t0)))

            if k == 1:
                out_specs.append(pl.BlockSpec((nb, None, SEQ, HA, 2 * DA),
                                              lambda j, i, f=cache_idx: f(j, i) + (0,)))
                out_shape.append(jax.ShapeDtypeStruct((BATCH, DEPTH, SEQ, HA, 2 * DA), F32))
                continue
            out_specs.append(pl.BlockSpec((nb, None, SEQ, bw), cache_idx))
            out_shape.append(jax.ShapeDtypeStruct((BATCH, DEPTH, SEQ, w), F32))
    return pl.pallas_call(
        _make_inproj_kernel(latent, len(aliases)),
        grid=(N_COL_TILES, n_row),
        in_specs=in_specs,
        out_specs=out_specs,
        out_shape=out_shape,
        scratch_shapes=[pltpu.VMEM((N_TOK, D_MODEL), BF16), pltpu.VMEM((D_MODEL, COL_TILE), BF16)],
        input_output_aliases=aliases,
        compiler_params=_params(("arbitrary", "arbitrary")),
        name="inproj_lat" if latent else "inproj_ctx",
    )(*args)


def _mask_head(q, head):
    qf = q.astype(F32)
    keep = _lane_head(qf.shape) == head
    return jnp.where(keep, qf, 0.0).astype(BF16)


def _stack_heads(q):
    return jnp.concatenate([_mask_head(q, 0), _mask_head(q, 1)], axis=0)


def _unstack_heads(o, rows):
    return jnp.where(_lane_head((rows, LANES)) == 1, o[rows:2 * rows], o[0:rows])


def _dup_head(kv, head):
    f = kv.astype(F32)
    r = pltpu.roll(f, HEAD_DIM, 1)
    return jnp.where(_lane_head(f.shape) == head, f, r).astype(BF16)


def _attend(qm, ks, vs):
    nt = (False,) * len(ks)
    return _attend_multi([(qm, ks, vs, None, nt, nt)])[0]


def _attend_multi(jobs):
    sss = []
    for qm, ks, _, biases, k_t, _ in jobs:
        ss = [jnp.dot(qm, k, preferred_element_type=F32) if t else
              lax.dot_general(qm, k, _NT, preferred_element_type=F32) for k, t in zip(ks, k_t)]
        if biases is not None:
            ss = [s if b is None else s + b for s, b in zip(ss, biases)]
        sss.append(ss)
    pss, dens = [], []
    for ss in sss:
        m = functools.reduce(jnp.maximum, [jnp.max(s, axis=-1, keepdims=True) for s in ss])
        ps = [jnp.exp2(s - m) for s in ss]
        dens.append(functools.reduce(jnp.add, [jnp.sum(p, axis=-1, keepdims=True) for p in ps]))
        pss.append(ps)
    outs = []
    for (_, _, vs, _, _, v_t), ps, den in zip(jobs, pss, dens):
        o = functools.reduce(jnp.add, [lax.dot_general(p.astype(BF16), v, _NT, preferred_element_type=F32) if t else
                                        jnp.dot(p.astype(BF16), v, preferred_element_type=F32)
                                        for p, v, t in zip(ps, vs, v_t)])
        outs.append(o * (1.0 / den))
    return outs


_CTX_T = (False, True)


def _pair_t(ref):
    return ref[...].reshape(2 * HEAD_DIM, ref.shape[-1]).astype(BF16)


def _diff_lambda(lam_ref, lam_init):
    lq = lam_ref[...]
    a = jnp.sum(lq[0:1] * lq[1:2], axis=-1, keepdims=True)
    b = jnp.sum(lq[2:3] * lq[3:4], axis=-1, keepdims=True)
    return jnp.exp(a) - jnp.exp(b) + lam_init


def _diff_combine(o1, o2, lam, gsub, lam_init):
    o = o1 - lam * o2
    ms = jnp.mean(o * o, axis=-1, keepdims=True)
    return (o * lax.rsqrt(ms + EPS) * gsub) * (1.0 - lam_init)


CTX_BATCH_PER_STEP = 2


def _make_attn_ctx_kernel(lam_init):
    def kern(z_ref, lam_ref, gsub_ref, oa_ref, ob_ref, oc_ref):
        lam = _diff_lambda(lam_ref, lam_init)
        gsub = gsub_ref[...]
        for bb in range(CTX_BATCH_PER_STEP):
            rows = pl.ds(bb * SEQ, SEQ)

            def z(g):
                return z_ref[g, rows, :]

            for vh in range(HA):
                hi = vh % 2
                v = z(G_VA + vh)
                o1 = _attend(_mask_head(z(G_QA + vh // 2), hi), [z(G_KA + vh // 2)], [v])
                o2 = _attend(_mask_head(z(G_QA + 2 + vh // 2), hi), [z(G_KA + 2 + vh // 2)], [v])
                oa_ref[vh, rows, :] = _diff_combine(o1, o2, lam, gsub, lam_init).astype(BF16)
            for g in range(KVB):
                kd = _dup_head(z(G_KB), g)
                vd = _dup_head(z(G_VB), g)
                for c in range(2 * g, 2 * g + 2):
                    o = _attend(_stack_heads(z(G_QB + c)), [kd], [vd])
                    ob_ref[c, rows, :] = _unstack_heads(o, SEQ).astype(BF16)
            for c in range(HC // 2):
                o = _attend(_stack_heads(z(G_QC + c)), [z(G_KC + c)], [z(G_VC + c)])
                oc_ref[c, rows, :] = _unstack_heads(o, SEQ).astype(BF16)
    return kern


def _attn_ctx(l, z, lambda_qk, g_subln):
    lam_init = 0.8 - 0.6 * math.exp(-0.3 * l)
    rows = CTX_BATCH_PER_STEP * SEQ
    o_spec = pl.BlockSpec((4, rows, LANES), lambda b: (0, b, 0))
    o_shape = jax.ShapeDtypeStruct((4, N_TOK, LANES), BF16)
    return pl.pallas_call(
        _make_attn_ctx_kernel(lam_init),
        grid=(BATCH // CTX_BATCH_PER_STEP,),
        in_specs=[pl.BlockSpec((N_QKV_GROUPS, rows, LANES), lambda b: (0, b, 0)),
                  pl.BlockSpec((None, 4, DA), lambda b: (l, 0, 0)),
                  pl.BlockSpec((None, 1, 2 * DA), lambda b: (l, 0, 0))],
        out_specs=[o_spec, o_spec, o_spec],
        out_shape=[o_shape, o_shape, o_shape],
        compiler_params=_params(("arbitrary",)),
        name="attn_ctx",
    )(z, lambda_qk, g_subln)


Q_BLK_GQA = 512
Q_BLK_DIFF = 1024


def _make_attn_lat_a_kernel(lam_init):
    def kern(q1_ref, q2_ref, k1_ref, k2_ref, v_ref, ck1_ref, ck2_ref, cv_ref, lam_ref, gsub_ref, o_ref):
        vh = pl.program_id(1)
        hi = jnp.bitwise_and(vh, 1)
        lam = _diff_lambda(lam_ref, lam_init)
        gsub = gsub_ref[...]
        ck1 = _pair_t(ck1_ref)
        ck2 = _pair_t(ck2_ref)
        cv = cv_ref[:, vh, :].astype(BF16)
        k1, k2, v = k1_ref[0], k2_ref[0], v_ref[0]
        rows = [pl.ds(qb * Q_BLK_DIFF, Q_BLK_DIFF) for qb in range(DEC_SEQ // Q_BLK_DIFF)]
        jobs = []
        for r in rows:
            jobs.append((_mask_head(q1_ref[0, r, :], hi), [k1, ck1], [v, cv], None, _CTX_T, (False, False)))
            jobs.append((_mask_head(q2_ref[0, r, :], hi), [k2, ck2], [v, cv], None, _CTX_T, (False, False)))
        outs = _attend_multi(jobs)
        for t, r in enumerate(rows):
            o_ref[0, r, :] = _diff_combine(outs[2 * t], outs[2 * t + 1], lam, gsub, lam_init).astype(BF16)
    return kern


def _attn_lat_a(l, z, cache_k_t, cache_v, lambda_qk, g_subln):
    lam_init = 0.8 - 0.6 * math.exp(-0.3 * l)

    def zspec(fn):
        return pl.BlockSpec((1, DEC_SEQ, LANES), lambda b, h: (fn(h), b, 0))

    return pl.pallas_call(
        _make_attn_lat_a_kernel(lam_init),
        grid=(DEC_BATCH, HA),
        in_specs=[zspec(lambda h: G_QA + h // 2), zspec(lambda h: G_QA + 2 + h // 2),
                  zspec(lambda h: G_KA + h // 2), zspec(lambda h: G_KA + 2 + h // 2),
                  zspec(lambda h: G_VA + h),
                  pl.BlockSpec((None, None, 2, DA, PAST_LEN), lambda b, h: (b, l, h // 2, 0, 0)),
                  pl.BlockSpec((None, None, 2, DA, PAST_LEN), lambda b, h: (b, l, HA // 2 + h // 2, 0, 0)),
                  pl.BlockSpec((None, None, PAST_LEN, HA, 2 * DA), lambda b, h: (b, l, 0, 0, 0)),
                  pl.BlockSpec((None, 4, DA), lambda b, h: (l, 0, 0)),
                  pl.BlockSpec((None, 1, 2 * DA), lambda b, h: (l, 0, 0))],
        out_specs=pl.BlockSpec((1, DEC_SEQ, LANES), lambda b, h: (h, b, 0)),
        out_shape=jax.ShapeDtypeStruct((4, N_TOK, LANES), BF16),
        compiler_params=_params(("arbitrary", "arbitrary")),
        name="attn_lat_a",
    )(z, z, z, z, z, cache_k_t, cache_k_t, cache_v, lambda_qk, g_subln)


def _attn_lat_b_kernel(q_ref, k_ref, v_ref, ck_ref, cv_ref, o_ref):
    hi = lax.shift_right_logical(pl.program_id(1), 1)
    kd = _dup_head(k_ref[0], hi)
    vd = _dup_head(v_ref[0], hi)
    ck = ck_ref[hi].astype(BF16)
    cv = cv_ref[hi].astype(BF16)
    ckd = jnp.concatenate([ck, ck], axis=0)
    cvd = jnp.concatenate([cv, cv], axis=0)
    rows = [pl.ds(qb * Q_BLK_GQA, Q_BLK_GQA) for qb in range(DEC_SEQ // Q_BLK_GQA)]
    outs = _attend_multi([(_stack_heads(q_ref[0, r, :]), [kd, ckd], [vd, cvd], None, _CTX_T, _CTX_T)
                          for r in rows])
    for r, o in zip(rows, outs):
        o_ref[0, r, :] = _unstack_heads(o, Q_BLK_GQA).astype(BF16)


def _attn_lat_b(l, z, cache_k, cache_v):
    cspec = pl.BlockSpec((None, None, KVB, DB, PAST_LEN), lambda b, c: (b, l, 0, 0, 0))
    return pl.pallas_call(
        _attn_lat_b_kernel,
        grid=(DEC_BATCH, HB // 2),
        in_specs=[pl.BlockSpec((1, DEC_SEQ, LANES), lambda b, c: (G_QB + c, b, 0)),
                  pl.BlockSpec((1, DEC_SEQ, LANES), lambda b, c: (G_KB, b, 0)),
                  pl.BlockSpec((1, DEC_SEQ, LANES), lambda b, c: (G_VB, b, 0)),
                  cspec, cspec],
        out_specs=pl.BlockSpec((1, DEC_SEQ, LANES), lambda b, c: (c, b, 0)),
        out_shape=jax.ShapeDtypeStruct((4, N_TOK, LANES), BF16),
        compiler_params=_params(("arbitrary", "arbitrary")),
        name="attn_lat_b",
    )(z, z, z, cache_k, cache_v)


N_GRID_ROWS = DEC_SEQ // GRID_W
NA_GROUP = 2
BIAS_TABLE_ROWS = 16


def _window_start(r):
    return min(max(r - NA_ROWS // 2, 0), N_GRID_ROWS - NA_ROWS)


def _group_keys(g):
    starts = [_window_start(r) for r in range(g * NA_GROUP, (g + 1) * NA_GROUP)]
    n = max(starts) + NA_ROWS - min(starts)
    n += n % 2
    return min(min(starts), N_GRID_ROWS - n), n


NA_MAX_KEY_ROWS = max(_group_keys(g)[1] for g in range(N_GRID_ROWS // NA_GROUP))


def _build_window_bias(tab_ref, bias_scr):
    qcol = lax.broadcasted_iota(jnp.int32, (GRID_W, LANES), 0)
    lane = lax.broadcasted_iota(jnp.int32, (GRID_W, LANES), 1)
    kcol = lane & (GRID_W - 1)
    c0 = jnp.clip(qcol - NA_COLS // 2, 0, GRID_W - NA_COLS)
    in_win = (kcol >= c0) & (kcol < c0 + NA_COLS)
    masks = {(True, True): in_win, (True, False): in_win & (lane < GRID_W), (False, True): in_win & (lane >= GRID_W)}
    for head in range(2):
        tiles = {}

        def tile(drs):
            if drs not in tiles:
                if drs == (None, None):
                    tiles[drs] = jnp.full((GRID_W, LANES), NEG, F32)
                else:
                    vec = None
                    for half, dr in enumerate(drs):
                        if dr is not None:
                            part = tab_ref[head, pl.ds(half * BIAS_TABLE_ROWS + dr, 1), :]
                            vec = part if vec is None else vec + part
                    toep = pltpu.roll(jnp.broadcast_to(vec, (GRID_W, LANES)), 0, 1, stride=1, stride_axis=0)
                    valid = tuple(dr is not None for dr in drs)
                    tiles[drs] = jnp.where(masks[valid], toep * LOG2E, NEG)
            return tiles[drs]

        for r in range(N_GRID_ROWS):
            u0, n = _group_keys(r // NA_GROUP)
            rows = pl.ds(r * GRID_W, GRID_W)
            for m in range(n // 2):
                drs = tuple(kr - r + NA_ROWS - 1 if _window_start(r) <= kr < _window_start(r) + NA_ROWS else None
                            for kr in (u0 + 2 * m, u0 + 2 * m + 1))
                bias_scr[head, rows, pl.ds(m * LANES, LANES)] = tile(drs)


def _attn_lat_c_kernel(q_ref, k_ref, v_ref, ck_ref, cv_ref, tab_ref, o_ref, bias_scr):
    @pl.when(pl.program_id(1) == 0)
    def _():
        _build_window_bias(tab_ref, bias_scr)

    ck = _pair_t(ck_ref)
    cv = _pair_t(cv_ref)
    n_rows = NA_GROUP * GRID_W
    jobs, row_slices = [], []
    for g in range(N_GRID_ROWS // NA_GROUP):
        u0, n = _group_keys(g)
        rows = pl.ds(g * n_rows, n_rows)
        keys = pl.ds(u0 * GRID_W, n * GRID_W)
        bias = jnp.concatenate([bias_scr[0, rows, 0:n * GRID_W], bias_scr[1, rows, 0:n * GRID_W]], axis=0)
        jobs.append((_stack_heads(q_ref[0, rows, :]), [k_ref[0, keys, :], ck], [v_ref[0, keys, :], cv],
                     [bias, None], _CTX_T, _CTX_T))
        row_slices.append(rows)
    for rows, o in zip(row_slices, _attend_multi(jobs)):
        o_ref[0, rows, :] = _unstack_heads(o, n_rows).astype(BF16)


def _attn_lat_c(l, z, cache_k, cache_v, bias_rows):
    def zspec(g0):
        return pl.BlockSpec((1, DEC_SEQ, LANES), lambda c, b: (g0 + c, b, 0))

    cspec = pl.BlockSpec((None, None, 2, DC, PAST_LEN), lambda c, b: (b, l, c, 0, 0))
    return pl.pallas_call(
        _attn_lat_c_kernel,
        grid=(HC // 2, DEC_BATCH),
        in_specs=[zspec(G_QC), zspec(G_KC), zspec(G_VC), cspec, cspec,
                  pl.BlockSpec((None, 2, 2 * BIAS_TABLE_ROWS, LANES), lambda c, b: (l, c, 0, 0))],
        out_specs=pl.BlockSpec((1, DEC_SEQ, LANES), lambda c, b: (c, b, 0)),
        out_shape=jax.ShapeDtypeStruct((4, N_TOK, LANES), BF16),
        scratch_shapes=[pltpu.VMEM((2, DEC_SEQ, NA_MAX_KEY_ROWS * GRID_W), F32)],
        compiler_params=_params(("arbitrary", "arbitrary")),
        name="attn_lat_c",
    )(z, z, z, cache_k, cache_v, bias_rows)


MERGE_TILE = 512
BRANCH_GROUPS = 4
MODEL_GROUPS = D_MODEL // LANES


def _merge_kernel(x_ref, oa_ref, ob_ref, oc_ref, gate_ref, mod_ref, wa_ref, wb_ref, wc_ref, wo_ref,
                  xo_ref, wbr_scr, wo_scr):
    @pl.when(pl.program_id(0) == 0)
    def _():
        wbr_scr[0] = wa_ref[...].astype(BF16)
        wbr_scr[1] = wb_ref[...].astype(BF16)
        wbr_scr[2] = wc_ref[...].astype(BF16)
        wo_scr[...] = wo_ref[...].astype(BF16)

    m = mod_ref[0]
    for rc in range(MERGE_TILE // ROW_CHUNK):
        rows = pl.ds(rc * ROW_CHUNK, ROW_CHUNK)
        y = None
        for k, o_ref in enumerate((oa_ref, ob_ref, oc_ref)):
            o = jnp.concatenate([o_ref[c, rows, :] for c in range(BRANCH_GROUPS)], axis=-1)
            p = jnp.dot(o, wbr_scr[k], preferred_element_type=F32)
            g = jnp.concatenate([gate_ref[MODEL_GROUPS * k + c, rows, :] for c in range(MODEL_GROUPS)],
                                axis=-1).astype(F32)
            y = g * p if y is None else y + g * p
        out = jnp.dot(y.astype(BF16), wo_scr[...], preferred_element_type=F32)
        xo_ref[rows, :] = x_ref[rows, :] + m[:, 2 * D_MODEL:3 * D_MODEL] * out


def _merge(latent, l, x, oa, ob, oc, gates, mod, w_a, w_b, w_c, w_o):
    tm = MERGE_TILE
    if latent:
        mod_idx = lambda i: (1 + (i * tm) // DEC_SEQ, 0, 0)
    else:
        mod_idx = lambda i: (0, 0, 0)
    o_spec = pl.BlockSpec((4, tm, LANES), lambda i: (0, i, 0))
    wbr_spec = pl.BlockSpec((None, 4 * LANES, D_MODEL), lambda i: (l, 0, 0))
    return pl.pallas_call(
        _merge_kernel,
        grid=(N_TOK // tm,),
        in_specs=[pl.BlockSpec((tm, D_MODEL), lambda i: (i, 0)),
                  o_spec, o_spec, o_spec,
                  pl.BlockSpec((N_GATE_GROUPS, tm, LANES), lambda i: (0, i, 0)),
                  pl.BlockSpec((1, 1, 6 * D_MODEL), mod_idx),
                  wbr_spec, wbr_spec, wbr_spec,
                  pl.BlockSpec((None, D_MODEL, D_MODEL), lambda i: (l, 0, 0))],
        out_specs=pl.BlockSpec((tm, D_MODEL), lambda i: (i, 0)),
        out_shape=jax.ShapeDtypeStruct((N_TOK, D_MODEL), F32),
        scratch_shapes=[pltpu.VMEM((3, 4 * LANES, D_MODEL), BF16), pltpu.VMEM((D_MODEL, D_MODEL), BF16)],
        compiler_params=_params(("arbitrary",)),
        name="merge_lat" if latent else "merge_ctx",
    )(x, oa, ob, oc, gates, mod, w_a, w_b, w_c, w_o)


FF_TILE = 1024
FF_CHUNK = 512


def _ffn_kernel(x_ref, mod_ref, g2_ref, w1_ref, w2_ref, gf_ref, xo_ref, acc_scr, h2_scr, *, final):
    f = pl.program_id(1)

    def run(first):
        w1 = w1_ref[...].astype(BF16)
        w2 = w2_ref[...].astype(BF16)
        m = mod_ref[0]
        for rc in range(ROW_TILE // FF_CHUNK):
            rows = pl.ds(rc * FF_CHUNK, FF_CHUNK)
            if first:
                x = x_ref[rows, :]
                ms = jnp.mean(x * x, axis=-1, keepdims=True)
                y = x * lax.rsqrt(ms + EPS) * g2_ref[...]
                h2 = (y * (1.0 + m[:, 4 * D_MODEL:5 * D_MODEL]) + m[:, 3 * D_MODEL:4 * D_MODEL]).astype(BF16)
                h2_scr[rows, :] = h2
            else:
                h2 = h2_scr[rows, :]
            u = jnp.dot(h2, w1, preferred_element_type=F32)
            u = jnp.square(jnp.maximum(u, 0.0)).astype(BF16)
            d = jnp.dot(u, w2, preferred_element_type=F32)
            if first:
                acc_scr[rows, :] = d
            else:
                acc_scr[rows, :] += d

    @pl.when(f == 0)
    def _():
        run(True)

    @pl.when(f > 0)
    def _():
        run(False)

    @pl.when(f == D_FF // FF_TILE - 1)
    def _():
        x = x_ref[...] + mod_ref[0][:, 5 * D_MODEL:6 * D_MODEL] * acc_scr[...]
        if final:
            ms = jnp.mean(x * x, axis=-1, keepdims=True)
            x = x * lax.rsqrt(ms + EPS) * gf_ref[...]
        xo_ref[...] = x


def _ffn(latent, l, x, mod, g_norm2, w1, w2, g_final):
    tm = ROW_TILE
    mod_idx = (lambda i, f: (1 + i, 0, 0)) if latent else (lambda i, f: (0, 0, 0))
    return pl.pallas_call(
        functools.partial(_ffn_kernel, final=(l == DEPTH - 1)),
        grid=(N_TOK // tm, D_FF // FF_TILE),
        in_specs=[pl.BlockSpec((tm, D_MODEL), lambda i, f: (i, 0)),
                  pl.BlockSpec((1, 1, 6 * D_MODEL), mod_idx),
                  pl.BlockSpec((None, 1, D_MODEL), lambda i, f: (l, 0, 0)),
                  pl.BlockSpec((None, D_MODEL, FF_TILE), lambda i, f: (l, 0, f)),
                  pl.BlockSpec((None, FF_TILE, D_MODEL), lambda i, f: (l, f, 0)),
                  pl.BlockSpec((1, D_MODEL), lambda i, f: (0, 0))],
        out_specs=pl.BlockSpec((tm, D_MODEL), lambda i, f: (i, 0)),
        out_shape=jax.ShapeDtypeStruct((N_TOK, D_MODEL), F32),
        scratch_shapes=[pltpu.VMEM((tm, D_MODEL), F32), pltpu.VMEM((tm, D_MODEL), BF16)],
        compiler_params=_params(("arbitrary", "arbitrary")),
        name="ffn_lat" if latent else "ffn_ctx",
    )(x, mod, g_norm2, w1, w2, g_final)


def _rope_tables():
    nf = ROPE_HALF
    t = jnp.arange(DEC_SEQ)
    row = (t // GRID_W).astype(F32)
    col = (t % GRID_W).astype(F32)
    inv = ROPE_BASE ** (-jnp.arange(nf, dtype=F32) / nf)
    ar = row[:, None] * inv[None, :]
    ac = col[:, None] * inv[None, :]
    cos = jnp.concatenate([jnp.cos(ar), jnp.cos(ar), jnp.cos(ac), jnp.cos(ac)], axis=-1)
    sin = jnp.concatenate([-jnp.sin(ar), jnp.sin(ar), -jnp.sin(ac), jnp.sin(ac)], axis=-1)
    return jnp.tile(cos, (1, 2)), jnp.tile(sin, (1, 2))


def _packed_bias_rows(rel_bias):
    n = 2 * NA_ROWS - 1
    first = jnp.concatenate([rel_bias[..., NA_COLS - 1:], jnp.zeros((DEPTH, HC, n, LANES - (2 * NA_COLS - 1)), F32),
                             rel_bias[..., :NA_COLS - 1]], axis=-1)
    lo = GRID_W - NA_COLS + 1
    second = jnp.pad(rel_bias, ((0, 0), (0, 0), (0, 0), (lo, LANES - lo - (2 * NA_COLS - 1))))
    pad_rows = ((0, 0), (0, 0), (0, BIAS_TABLE_ROWS - n), (0, 0))
    return jnp.concatenate([jnp.pad(first, pad_rows), jnp.pad(second, pad_rows)], axis=2)


def kernel(x_prompt, x_sample, c, cache_a_k, cache_a_v, cache_b_k, cache_b_v, cache_c_k, cache_c_v, c_ctx, w_mod, b_mod, g_norm1, g_norm2, w_in, b_gate, lambda_qk, g_subln, g_qnorm, g_knorm, rel_bias, w_branch_a, w_branch_b, w_branch_c, w_out, w_ff1, w_ff2, g_final):
    xp = x_prompt.reshape(N_TOK, D_MODEL)
    xs = x_sample.reshape(N_TOK, D_MODEL)

    cvec = jnp.concatenate([c_ctx[None, :], c, jnp.zeros((3, D_MODEL), F32)], axis=0)
    mods = _modulation(cvec, w_mod, b_mod)

    rope_cos, rope_sin = _rope_tables()
    bias_rows = _packed_bias_rows(rel_bias)
    bd = jnp.kron(jnp.eye(2 * LANES // HEAD_DIM, dtype=F32),
                  jnp.full((HEAD_DIM, HEAD_DIM), 1.0 / HEAD_DIM, F32)).astype(BF16)
    gq = jnp.tile(g_qnorm, (1, 2 * LANES // DB)).reshape(DEPTH, 1, 2 * LANES)
    gk = jnp.tile(g_knorm, (1, LANES // DB)).reshape(DEPTH, 1, LANES)
    g1 = g_norm1.reshape(DEPTH, 1, D_MODEL)
    g2 = g_norm2.reshape(DEPTH, 1, D_MODEL)
    bg = b_gate.reshape(DEPTH, 1, 3 * D_MODEL)
    gsub = g_subln.reshape(DEPTH, 1, 2 * DA)
    gf = g_final.reshape(1, D_MODEL)

    def heads_t(cache):
        return cache.transpose(0, 1, 3, 4, 2)

    ctx_ak, ctx_bk, ctx_bv = heads_t(cache_a_k), heads_t(cache_b_k), heads_t(cache_b_v)
    ctx_ck, ctx_cv = heads_t(cache_c_k), heads_t(cache_c_v)

    new_caches = None

    for l in range(DEPTH):
        mod = mods[l].reshape(8, 1, 6 * D_MODEL)

        outs = _inproj(False, l, xp, mod, g1, w_in, bg, gq, gk, bd, None, None, new_caches)
        z, gates, new_caches = outs[0], outs[1], list(outs[2:])
        oa, ob, oc = _attn_ctx(l, z, lambda_qk, gsub)
        xp = _merge(False, l, xp, oa, ob, oc, gates, mod, w_branch_a, w_branch_b, w_branch_c, w_out)
        xp = _ffn(False, l, xp, mod, g2, w_ff1, w_ff2, gf)

        z, gates = _inproj(True, l, xs, mod, g1, w_in, bg, gq, gk, bd, rope_cos, rope_sin, None)
        oa = _attn_lat_a(l, z, ctx_ak, cache_a_v, lambda_qk, gsub)
        ob = _attn_lat_b(l, z, ctx_bk, ctx_bv)
        oc = _attn_lat_c(l, z, ctx_ck, ctx_cv, bias_rows)
        xs = _merge(True, l, xs, oa, ob, oc, gates, mod, w_branch_a, w_branch_b, w_branch_c, w_out)
        xs = _ffn(True, l, xs, mod, g2, w_ff1, w_ff2, gf)

    y_prompt = xp.reshape(BATCH, SEQ, D_MODEL)
    y_sample = xs.reshape(DEC_BATCH, DEC_SEQ, D_MODEL)
    ak, av, bk, bv, ck, cv = new_caches
    return (y_prompt, y_sample,
            ak.reshape(BATCH, DEPTH, SEQ, 2 * HA, DA), av.reshape(BATCH, DEPTH, SEQ, HA, 2 * DA),
            bk.reshape(BATCH, DEPTH, SEQ, KVB, DB), bv.reshape(BATCH, DEPTH, SEQ, KVB, DB),
            ck.reshape(BATCH, DEPTH, SEQ, HC, DC), cv.reshape(BATCH, DEPTH, SEQ, HC, DC))
```

```python
import functools
import math

import jax
import jax.numpy as jnp
from jax import lax
from jax.experimental import pallas as pl
from jax.experimental.pallas import tpu as pltpu

D_MODEL = 1024
BATCH = 16
SEQ = 256
DEPTH = 4
DEC_BATCH = 4
DEC_SEQ = 1024
PAST_LEN = 256
GRID_W = 64
HA, DA = 4, 64
HB, KVB, DB = 8, 2, 64
HC, DC = 8, 64
NA_ROWS, NA_COLS = 8, 16
D_FF = 4 * D_MODEL
ROPE_BASE = 10000.0
EPS = 1e-6
NEG = -1e30
HEAD_DIM = 64
LOG2E = math.log2(math.e)
Q_SCALE = HEAD_DIM ** -0.5 * LOG2E

F32 = jnp.float32
BF16 = jnp.bfloat16

LANES = 128
N_QKV_GROUPS = 30
N_GATE_GROUPS = 24
COL_TILE = 768
GROUPS_PER_TILE = COL_TILE // LANES
N_QKV_TILES = N_QKV_GROUPS // GROUPS_PER_TILE
N_COL_TILES = N_QKV_TILES + N_GATE_GROUPS // GROUPS_PER_TILE
ROW_TILE = 1024
ROW_CHUNK = 256
N_TOK = BATCH * SEQ
Q_GROUPS = 4
ROPE_HALF = HEAD_DIM // 4
HEAD_SHIFT = HEAD_DIM.bit_length() - 1
MOD_TILE = 1536
VMEM_LIMIT_V7X = 58 * 1024 * 1024

G_QA, G_KA, G_VA, G_QB, G_KB, G_VB, G_QC, G_KC, G_VC = 0, 4, 8, 12, 16, 17, 18, 22, 26
CACHE_WIDTHS = (2 * HA * DA, HA * 2 * DA, KVB * DB, KVB * DB, HC * DC, HC * DC)

_NT = (((1,), (1,)), ((), ()))


def _params(sem, vmem=VMEM_LIMIT_V7X):
    return pltpu.CompilerParams(dimension_semantics=sem, vmem_limit_bytes=vmem)


def _lane_head(shape):
    return lax.shift_right_logical(lax.broadcasted_iota(jnp.int32, shape, len(shape) - 1), HEAD_SHIFT)


def _mod_kernel(c_ref, w_ref, b_ref, o_ref):
    c = c_ref[...]
    s = (c * jax.nn.sigmoid(c)).astype(BF16)
    o_ref[...] = jnp.dot(s, w_ref[...].astype(BF16), preferred_element_type=F32) + b_ref[...]


def _modulation(cvec, w_mod, b_mod):
    tn = MOD_TILE
    n6 = 6 * D_MODEL
    return pl.pallas_call(
        _mod_kernel,
        grid=(DEPTH, n6 // tn),
        in_specs=[pl.BlockSpec((8, D_MODEL), lambda l, n: (0, 0)),
                  pl.BlockSpec((None, D_MODEL, tn), lambda l, n: (l, 0, n)),
                  pl.BlockSpec((None, 1, tn), lambda l, n: (l, 0, n))],
        out_specs=pl.BlockSpec((None, 8, tn), lambda l, n: (l, 0, n)),
        out_shape=jax.ShapeDtypeStruct((DEPTH, 8, n6), F32),
        compiler_params=_params(("arbitrary", "arbitrary")),
        name="modulation",
    )(cvec, w_mod, b_mod.reshape(DEPTH, 1, n6))


def _rope(v, cos, sin):
    first = (lax.broadcasted_iota(jnp.int32, v.shape, 1) & ROPE_HALF) == 0
    partner = jnp.where(first, pltpu.roll(v, LANES - ROPE_HALF, 1), pltpu.roll(v, ROPE_HALF, 1))
    return v * cos + partner * sin


def _head_rmsnorm(v, bd, g):
    msq = jnp.dot((v * v).astype(BF16), bd, preferred_element_type=F32)
    return v * lax.rsqrt(msq + EPS) * g


def _make_inproj_kernel(latent, n_aliased):
    def kern(*refs):
        if latent:
            (x_ref, mod_ref, g1_ref, w_ref, bg_ref, gq_ref, gk_ref, bd_ref, cos_ref, sin_ref,
             z_ref, gate_ref, h_scr, wbf_scr) = refs
        else:
            (x_ref, mod_ref, g1_ref, w_ref, bg_ref, gq_ref, gk_ref, bd_ref) = refs[:8]
            (z_ref, gate_ref, ak_ref, av_ref, bk_ref, bv_ref, ck_ref, cv_ref,
             h_scr, wbf_scr) = refs[8 + n_aliased:]
        j = pl.program_id(0)
        i = pl.program_id(1)

        @pl.when(i == 0)
        def _():
            wbf_scr[...] = w_ref[...].astype(BF16)

        def chunks(first=False):
            for rc in range(ROW_TILE // ROW_CHUNK):
                rows = pl.ds(rc * ROW_CHUNK, ROW_CHUNK)
                tok = pl.ds(pl.multiple_of(i * ROW_TILE + rc * ROW_CHUNK, ROW_CHUNK), ROW_CHUNK)
                if first:
                    x = x_ref[rows, :]
                    ms = jnp.mean(x * x, axis=-1, keepdims=True)
                    y = x * lax.rsqrt(ms + EPS) * g1_ref[...]
                    m = mod_ref[0]
                    h = (y * (1.0 + m[:, D_MODEL:2 * D_MODEL]) + m[:, 0:D_MODEL]).astype(BF16)
                    h_scr[tok, :] = h
                else:
                    h = h_scr[tok, :]
                yield rc, rows, jnp.dot(h, wbf_scr[...], preferred_element_type=F32)

        def grp(a, c, n=1):
            return a[:, c * LANES:(c + n) * LANES]

        def rot(v, rows):
            return _rope(v, cos_ref[rows, :], sin_ref[rows, :]) if latent else v

        @pl.when(j == 0)
        def _():
            for rc, rows, a in chunks(first=True):
                for c in range(GROUPS_PER_TILE):
                    v = rot(grp(a, c), rows)
                    z_ref[c, rows, :] = (v * Q_SCALE if c < Q_GROUPS else v).astype(BF16)
                if not latent:
                    ak_ref[rc] = grp(a, 4, 2)

        @pl.when(j == 1)
        def _():
            for rc, rows, a in chunks():
                for c in range(2):
                    z_ref[c, rows, :] = rot(grp(a, c), rows).astype(BF16)
                for c in range(2, GROUPS_PER_TILE):
                    z_ref[c, rows, :] = grp(a, c).astype(BF16)
                if not latent:
                    ak_ref[rc] = grp(a, 0, 2)
                    for hd in range(HA):
                        av_ref[rc, :, hd, :] = grp(a, 2 + hd)

        @pl.when(j == 2)
        def _():
            bd = bd_ref[...]
            for rc, rows, a in chunks():
                for half in range(2):
                    qn = _head_rmsnorm(grp(a, 2 * half, 2), bd, gq_ref[...])
                    for c in range(2):
                        z_ref[2 * half + c, rows, :] = (rot(grp(qn, c), rows) * Q_SCALE).astype(BF16)
                kn = _head_rmsnorm(grp(a, 4), bd[0:LANES, 0:LANES], gk_ref[...])
                z_ref[4, rows, :] = rot(kn, rows).astype(BF16)
                z_ref[5, rows, :] = grp(a, 5).astype(BF16)
                if not latent:
                    bk_ref[rc] = kn
                    bv_ref[rc] = grp(a, 5)

        @pl.when(j == 3)
        def _():
            for rc, rows, a in chunks():
                for c in range(GROUPS_PER_TILE):
                    v = grp(a, c)
                    z_ref[c, rows, :] = (v * Q_SCALE if c < Q_GROUPS else v).astype(BF16)
                if not latent:
                    ck_ref[rc] = grp(a, 4, 2)

        @pl.when(j == 4)
        def _():
            for rc, rows, a in chunks():
                for c in range(GROUPS_PER_TILE):
                    z_ref[c, rows, :] = grp(a, c).astype(BF16)
                if not latent:
                    ck_ref[rc] = grp(a, 0, 2)
                    cv_ref[rc] = grp(a, 2, 4)

        @pl.when(j >= N_QKV_TILES)
        def _():
            for rc, rows, a in chunks():
                a = a + bg_ref[...]
                for c in range(GROUPS_PER_TILE):
                    gate_ref[c, rows, :] = jax.nn.sigmoid(grp(a, c)).astype(BF16)

    return kern


def _inproj(latent, l, x, mod, g_norm1, w_in, b_gate, gq, gk, bd, rope_cos, rope_sin, caches):
    n_row = N_TOK // ROW_TILE
    last = n_row - 1

    def row_block(j, i, first_tile, last_tile):
        return jnp.where(j < first_tile, 0, jnp.where(j <= last_tile, i, last))

    mod_idx = (lambda j, i: (1 + i, 0, 0)) if latent else (lambda j, i: (0, 0, 0))
    in_specs = [
        pl.BlockSpec((ROW_TILE, D_MODEL), lambda j, i: (row_block(j, i, 0, 0), 0)),
        pl.BlockSpec((1, 1, 6 * D_MODEL), mod_idx),
        pl.BlockSpec((None, 1, D_MODEL), lambda j, i: (l, 0, 0)),
        pl.BlockSpec((None, D_MODEL, COL_TILE), lambda j, i: (l, 0, j)),
        pl.BlockSpec((None, 1, COL_TILE), lambda j, i: (l, 0, jnp.maximum(j - N_QKV_TILES, 0))),
        pl.BlockSpec((None, 1, 2 * LANES), lambda j, i: (l, 0, 0)),
        pl.BlockSpec((None, 1, LANES), lambda j, i: (l, 0, 0)),
        pl.BlockSpec((2 * LANES, 2 * LANES), lambda j, i: (0, 0)),
    ]
    args = [x, mod, g_norm1, w_in, b_gate, gq, gk, bd]
    out_specs = [
        pl.BlockSpec((GROUPS_PER_TILE, ROW_TILE, LANES),
                     lambda j, i: (jnp.minimum(j, N_QKV_TILES - 1), row_block(j, i, 0, N_QKV_TILES - 1), 0)),
        pl.BlockSpec((GROUPS_PER_TILE, ROW_TILE, LANES),
                     lambda j, i: (jnp.maximum(j - N_QKV_TILES, 0), row_block(j, i, N_QKV_TILES, N_COL_TILES), 0)),
    ]
    out_shape = [jax.ShapeDtypeStruct((N_QKV_GROUPS, N_TOK, LANES), BF16),
                 jax.ShapeDtypeStruct((N_GATE_GROUPS, N_TOK, LANES), BF16)]
    aliases = {}
    if latent:
        in_specs += [pl.BlockSpec((DEC_SEQ, LANES), lambda j, i: (0, 0))] * 2
        args += [rope_cos, rope_sin]
    else:
        nb = ROW_TILE // SEQ
        tiles = ((0, 1, 256), (1, 1, 512), (2, 2, 128), (2, 2, 128), (3, 4, 256), (4, 4, 512))
        for k, (w, (t0, t1, bw)) in enumerate(zip(CACHE_WIDTHS, tiles)):
            if caches is not None:
                in_specs.append(pl.BlockSpec(memory_space=pl.ANY))
                args.append(caches[k])
                aliases[8 + k] = 2 + k

            def cache_idx(j, i, t0=t0, t1=t1):
                return (row_block(j, i, t0, t1), l, 0, jnp.where(j <= t0, 0, (t1 - t0)))

            if k == 1:
                out_specs.append(pl.BlockSpec((nb, None, SEQ, HA, 2 * DA),
                                              lambda j, i, f=cache_idx: f(j, i) + (0,)))
                out_shape.append(jax.ShapeDtypeStruct((BATCH, DEPTH, SEQ, HA, 2 * DA), F32))
                continue
            out_specs.append(pl.BlockSpec((nb, None, SEQ, bw), cache_idx))
            out_shape.append(jax.ShapeDtypeStruct((BATCH, DEPTH, SEQ, w), F32))
    return pl.pallas_call(
        _make_inproj_kernel(latent, len(aliases)),
        grid=(N_COL_TILES, n_row),
        in_specs=in_specs,
        out_specs=out_specs,
        out_shape=out_shape,
        scratch_shapes=[pltpu.VMEM((N_TOK, D_MODEL), BF16), pltpu.VMEM((D_MODEL, COL_TILE), BF16)],
        input_output_aliases=aliases,
        compiler_params=_params(("arbitrary", "arbitrary")),
        name="inproj_lat" if latent else "inproj_ctx",
    )(*args)


def _mask_head(q, head):
    qf = q.astype(F32)
    keep = _lane_head(qf.shape) == head
    return jnp.where(keep, qf, 0.0).astype(BF16)


def _stack_heads(q):
    return jnp.concatenate([_mask_head(q, 0), _mask_head(q, 1)], axis=0)


def _unstack_heads(o, rows):
    return jnp.where(_lane_head((rows, LANES)) == 1, o[rows:2 * rows], o[0:rows])


def _dup_head(kv, head):
    f = kv.astype(F32)
    r = pltpu.roll(f, HEAD_DIM, 1)
    return jnp.where(_lane_head(f.shape) == head, f, r).astype(BF16)


def _attend(qm, ks, vs):
    nt = (False,) * len(ks)
    return _attend_multi([(qm, ks, vs, None, nt, nt)])[0]


def _attend_multi(jobs):
    sss = []
    for qm, ks, _, biases, k_t, _ in jobs:
        ss = [jnp.dot(qm, k, preferred_element_type=F32) if t else
              lax.dot_general(qm, k, _NT, preferred_element_type=F32) for k, t in zip(ks, k_t)]
        if biases is not None:
            ss = [s if b is None else s + b for s, b in zip(ss, biases)]
        sss.append(ss)
    pss, dens = [], []
    for ss in sss:
        m = functools.reduce(jnp.maximum, [jnp.max(s, axis=-1, keepdims=True) for s in ss])
        ps = [jnp.exp2(s - m) for s in ss]
        dens.append(functools.reduce(jnp.add, [jnp.sum(p, axis=-1, keepdims=True) for p in ps]))
        pss.append(ps)
    outs = []
    for (_, _, vs, _, _, v_t), ps, den in zip(jobs, pss, dens):
        o = functools.reduce(jnp.add, [lax.dot_general(p.astype(BF16), v, _NT, preferred_element_type=F32) if t else
                                        jnp.dot(p.astype(BF16), v, preferred_element_type=F32)
                                        for p, v, t in zip(ps, vs, v_t)])
        outs.append(o * (1.0 / den))
    return outs


_CTX_T = (False, True)


def _pair_t(ref):
    return ref[...].reshape(2 * HEAD_DIM, ref.shape[-1]).astype(BF16)


def _diff_lambda(lam_ref, lam_init):
    lq = lam_ref[...]
    a = jnp.sum(lq[0:1] * lq[1:2], axis=-1, keepdims=True)
    b = jnp.sum(lq[2:3] * lq[3:4], axis=-1, keepdims=True)
    return jnp.exp(a) - jnp.exp(b) + lam_init


def _diff_combine(o1, o2, lam, gsub, lam_init):
    o = o1 - lam * o2
    ms = jnp.mean(o * o, axis=-1, keepdims=True)
    return (o * lax.rsqrt(ms + EPS) * gsub) * (1.0 - lam_init)


CTX_BATCH_PER_STEP = 2


def _make_attn_ctx_kernel(lam_init):
    def kern(z_ref, lam_ref, gsub_ref, oa_ref, ob_ref, oc_ref):
        lam = _diff_lambda(lam_ref, lam_init)
        gsub = gsub_ref[...]
        for bb in range(CTX_BATCH_PER_STEP):
            rows = pl.ds(bb * SEQ, SEQ)

            def z(g):
                return z_ref[g, rows, :]

            for vh in range(HA):
                hi = vh % 2
                v = z(G_VA + vh)
                o1 = _attend(_mask_head(z(G_QA + vh // 2), hi), [z(G_KA + vh // 2)], [v])
                o2 = _attend(_mask_head(z(G_QA + 2 + vh // 2), hi), [z(G_KA + 2 + vh // 2)], [v])
                oa_ref[vh, rows, :] = _diff_combine(o1, o2, lam, gsub, lam_init).astype(BF16)
            for g in range(KVB):
                kd = _dup_head(z(G_KB), g)
                vd = _dup_head(z(G_VB), g)
                for c in range(2 * g, 2 * g + 2):
                    o = _attend(_stack_heads(z(G_QB + c)), [kd], [vd])
                    ob_ref[c, rows, :] = _unstack_heads(o, SEQ).astype(BF16)
            for c in range(HC // 2):
                o = _attend(_stack_heads(z(G_QC + c)), [z(G_KC + c)], [z(G_VC + c)])
                oc_ref[c, rows, :] = _unstack_heads(o, SEQ).astype(BF16)
    return kern


def _attn_ctx(l, z, lambda_qk, g_subln):
    lam_init = 0.8 - 0.6 * math.exp(-0.3 * l)
    rows = CTX_BATCH_PER_STEP * SEQ
    o_spec = pl.BlockSpec((4, rows, LANES), lambda b: (0, b, 0))
    o_shape = jax.ShapeDtypeStruct((4, N_TOK, LANES), BF16)
    return pl.pallas_call(
        _make_attn_ctx_kernel(lam_init),
        grid=(BATCH // CTX_BATCH_PER_STEP,),
        in_specs=[pl.BlockSpec((N_QKV_GROUPS, rows, LANES), lambda b: (0, b, 0)),
                  pl.BlockSpec((None, 4, DA), lambda b: (l, 0, 0)),
                  pl.BlockSpec((None, 1, 2 * DA), lambda b: (l, 0, 0))],
        out_specs=[o_spec, o_spec, o_spec],
        out_shape=[o_shape, o_shape, o_shape],
        compiler_params=_params(("arbitrary",)),
        name="attn_ctx",
    )(z, lambda_qk, g_subln)


Q_BLK_GQA = 512
Q_BLK_DIFF = 1024


def _make_attn_lat_a_kernel(lam_init):
    def kern(q1_ref, q2_ref, k1_ref, k2_ref, v_ref, ck1_ref, ck2_ref, cv_ref, lam_ref, gsub_ref, o_ref):
        vh = pl.program_id(1)
        hi = jnp.bitwise_and(vh, 1)
        lam = _diff_lambda(lam_ref, lam_init)
        gsub = gsub_ref[...]
        ck1 = _pair_t(ck1_ref)
        ck2 = _pair_t(ck2_ref)
        cv = cv_ref[:, vh, :].astype(BF16)
        k1, k2, v = k1_ref[0], k2_ref[0], v_ref[0]
        rows = [pl.ds(qb * Q_BLK_DIFF, Q_BLK_DIFF) for qb in range(DEC_SEQ // Q_BLK_DIFF)]
        jobs = []
        for r in rows:
            jobs.append((_mask_head(q1_ref[0, r, :], hi), [k1, ck1], [v, cv], None, _CTX_T, (False, False)))
            jobs.append((_mask_head(q2_ref[0, r, :], hi), [k2, ck2], [v, cv], None, _CTX_T, (False, False)))
        outs = _attend_multi(jobs)
        for t, r in enumerate(rows):
            o_ref[0, r, :] = _diff_combine(outs[2 * t], outs[2 * t + 1], lam, gsub, lam_init).astype(BF16)
    return kern


def _attn_lat_a(l, z, cache_k_t, cache_v, lambda_qk, g_subln):
    lam_init = 0.8 - 0.6 * math.exp(-0.3 * l)

    def zspec(fn):
        return pl.BlockSpec((1, DEC_SEQ, LANES), lambda b, h: (fn(h), b, 0))

    return pl.pallas_call(
        _make_attn_lat_a_kernel(lam_init),
        grid=(DEC_BATCH, HA),
        in_specs=[zspec(lambda h: G_QA + h // 2), zspec(lambda h: G_QA + 2 + h // 2),
                  zspec(lambda h: G_KA + h // 2), zspec(lambda h: G_KA + 2 + h // 2),
                  zspec(lambda h: G_VA + h),
                  pl.BlockSpec((None, None, 2, DA, PAST_LEN), lambda b, h: (b, l, h // 2, 0, 0)),
                  pl.BlockSpec((None, None, 2, DA, PAST_LEN), lambda b, h: (b, l, HA // 2 + h // 2, 0, 0)),
                  pl.BlockSpec((None, None, PAST_LEN, HA, 2 * DA), lambda b, h: (b, l, 0, 0, 0)),
                  pl.BlockSpec((None, 4, DA), lambda b, h: (l, 0, 0)),
                  pl.BlockSpec((None, 1, 2 * DA), lambda b, h: (l, 0, 0))],
        out_specs=pl.BlockSpec((1, DEC_SEQ, LANES), lambda b, h: (h, b, 0)),
        out_shape=jax.ShapeDtypeStruct((4, N_TOK, LANES), BF16),
        compiler_params=_params(("arbitrary", "arbitrary")),
        name="attn_lat_a",
    )(z, z, z, z, z, cache_k_t, cache_k_t, cache_v, lambda_qk, g_subln)


def _attn_lat_b_kernel(q_ref, k_ref, v_ref, ck_ref, cv_ref, o_ref):
    hi = lax.shift_right_logical(pl.program_id(1), 1)
    kd = _dup_head(k_ref[0], hi)
    vd = _dup_head(v_ref[0], hi)
    ck = ck_ref[hi].astype(BF16)
    cv = cv_ref[hi].astype(BF16)
    ckd = jnp.concatenate([ck, ck], axis=0)
    cvd = jnp.concatenate([cv, cv], axis=0)
    rows = [pl.ds(qb * Q_BLK_GQA, Q_BLK_GQA) for qb in range(DEC_SEQ // Q_BLK_GQA)]
    outs = _attend_multi([(_stack_heads(q_ref[0, r, :]), [kd, ckd], [vd, cvd], None, _CTX_T, _CTX_T)
                          for r in rows])
    for r, o in zip(rows, outs):
        o_ref[0, r, :] = _unstack_heads(o, Q_BLK_GQA).astype(BF16)


def _attn_lat_b(l, z, cache_k, cache_v):
    cspec = pl.BlockSpec((None, None, KVB, DB, PAST_LEN), lambda b, c: (b, l, 0, 0, 0))
    return pl.pallas_call(
        _attn_lat_b_kernel,
        grid=(DEC_BATCH, HB // 2),
        in_specs=[pl.BlockSpec((1, DEC_SEQ, LANES), lambda b, c: (G_QB + c, b, 0)),
                  pl.BlockSpec((1, DEC_SEQ, LANES), lambda b, c: (G_KB, b, 0)),
                  pl.BlockSpec((1, DEC_SEQ, LANES), lambda b, c: (G_VB, b, 0)),
                  cspec, cspec],
        out_specs=pl.BlockSpec((1, DEC_SEQ, LANES), lambda b, c: (c, b, 0)),
        out_shape=jax.ShapeDtypeStruct((4, N_TOK, LANES), BF16),
        compiler_params=_params(("arbitrary", "arbitrary")),
        name="attn_lat_b",
    )(z, z, z, cache_k, cache_v)


N_GRID_ROWS = DEC_SEQ // GRID_W
NA_GROUP = 2
BIAS_TABLE_ROWS = 16


def _window_start(r):
    return min(max(r - NA_ROWS // 2, 0), N_GRID_ROWS - NA_ROWS)


def _group_keys(g):
    starts = [_window_start(r) for r in range(g * NA_GROUP, (g + 1) * NA_GROUP)]
    n = max(starts) + NA_ROWS - min(starts)
    n += n % 2
    return min(min(starts), N_GRID_ROWS - n), n


NA_MAX_KEY_ROWS = max(_group_keys(g)[1] for g in range(N_GRID_ROWS // NA_GROUP))


def _build_window_bias(tab_ref, bias_scr):
    qcol = lax.broadcasted_iota(jnp.int32, (GRID_W, LANES), 0)
    lane = lax.broadcasted_iota(jnp.int32, (GRID_W, LANES), 1)
    kcol = lane & (GRID_W - 1)
    c0 = jnp.clip(qcol - NA_COLS // 2, 0, GRID_W - NA_COLS)
    in_win = (kcol >= c0) & (kcol < c0 + NA_COLS)
    masks = {(True, True): in_win, (True, False): in_win & (lane < GRID_W), (False, True): in_win & (lane >= GRID_W)}
    for head in range(2):
        tiles = {}

        def tile(drs):
            if drs not in tiles:
                if drs == (None, None):
                    tiles[drs] = jnp.full((GRID_W, LANES), NEG, F32)
                else:
                    vec = None
                    for half, dr in enumerate(drs):
                        if dr is not None:
                            part = tab_ref[head, pl.ds(half * BIAS_TABLE_ROWS + dr, 1), :]
                            vec = part if vec is None else vec + part
                    toep = pltpu.roll(jnp.broadcast_to(vec, (GRID_W, LANES)), 0, 1, stride=1, stride_axis=0)
                    valid = tuple(dr is not None for dr in drs)
                    tiles[drs] = jnp.where(masks[valid], toep * LOG2E, NEG)
            return tiles[drs]

        for r in range(N_GRID_ROWS):
            u0, n = _group_keys(r // NA_GROUP)
            rows = pl.ds(r * GRID_W, GRID_W)
            for m in range(n // 2):
                drs = tuple(kr - r + NA_ROWS - 1 if _window_start(r) <= kr < _window_start(r) + NA_ROWS else None
                            for kr in (u0 + 2 * m, u0 + 2 * m + 1))
                bias_scr[head, rows, pl.ds(m * LANES, LANES)] = tile(drs)


def _attn_lat_c_kernel(q_ref, k_ref, v_ref, ck_ref, cv_ref, tab_ref, o_ref, bias_scr):
    @pl.when(pl.program_id(1) == 0)
    def _():
        _build_window_bias(tab_ref, bias_scr)

    ck = _pair_t(ck_ref)
    cv = _pair_t(cv_ref)
    n_rows = NA_GROUP * GRID_W
    jobs, row_slices = [], []
    for g in range(N_GRID_ROWS // NA_GROUP):
        u0, n = _group_keys(g)
        rows = pl.ds(g * n_rows, n_rows)
        keys = pl.ds(u0 * GRID_W, n * GRID_W)
        bias = jnp.concatenate([bias_scr[0, rows, 0:n * GRID_W], bias_scr[1, rows, 0:n * GRID_W]], axis=0)
        jobs.append((_stack_heads(q_ref[0, rows, :]), [k_ref[0, keys, :], ck], [v_ref[0, keys, :], cv],
                     [bias, None], _CTX_T, _CTX_T))
        row_slices.append(rows)
    for rows, o in zip(row_slices, _attend_multi(jobs)):
        o_ref[0, rows, :] = _unstack_heads(o, n_rows).astype(BF16)


def _attn_lat_c(l, z, cache_k, cache_v, bias_rows):
    def zspec(g0):
        return pl.BlockSpec((1, DEC_SEQ, LANES), lambda c, b: (g0 + c, b, 0))

    cspec = pl.BlockSpec((None, None, 2, DC, PAST_LEN), lambda c, b: (b, l, c, 0, 0))
    return pl.pallas_call(
        _attn_lat_c_kernel,
        grid=(HC // 2, DEC_BATCH),
        in_specs=[zspec(G_QC), zspec(G_KC), zspec(G_VC), cspec, cspec,
                  pl.BlockSpec((None, 2, 2 * BIAS_TABLE_ROWS, LANES), lambda c, b: (l, c, 0, 0))],
        out_specs=pl.BlockSpec((1, DEC_SEQ, LANES), lambda c, b: (c, b, 0)),
        out_shape=jax.ShapeDtypeStruct((4, N_TOK, LANES), BF16),
        scratch_shapes=[pltpu.VMEM((2, DEC_SEQ, NA_MAX_KEY_ROWS * GRID_W), F32)],
        compiler_params=_params(("arbitrary", "arbitrary")),
        name="attn_lat_c",
    )(z, z, z, cache_k, cache_v, bias_rows)


MERGE_TILE = 512
BRANCH_GROUPS = 4
MODEL_GROUPS = D_MODEL // LANES


def _merge_kernel(x_ref, oa_ref, ob_ref, oc_ref, gate_ref, mod_ref, wa_ref, wb_ref, wc_ref, wo_ref,
                  xo_ref, wbr_scr, wo_scr):
    @pl.when(pl.program_id(0) == 0)
    def _():
        wbr_scr[0] = wa_ref[...].astype(BF16)
        wbr_scr[1] = wb_ref[...].astype(BF16)
        wbr_scr[2] = wc_ref[...].astype(BF16)
        wo_scr[...] = wo_ref[...].astype(BF16)

    m = mod_ref[0]
    for rc in range(MERGE_TILE // ROW_CHUNK):
        rows = pl.ds(rc * ROW_CHUNK, ROW_CHUNK)
        y = None
        for k, o_ref in enumerate((oa_ref, ob_ref, oc_ref)):
            o = jnp.concatenate([o_ref[c, rows, :] for c in range(BRANCH_GROUPS)], axis=-1)
            p = jnp.dot(o, wbr_scr[k], preferred_element_type=F32)
            g = jnp.concatenate([gate_ref[MODEL_GROUPS * k + c, rows, :] for c in range(MODEL_GROUPS)],
                                axis=-1).astype(F32)
            y = g * p if y is None else y + g * p
        out = jnp.dot(y.astype(BF16), wo_scr[...], preferred_element_type=F32)
        xo_ref[rows, :] = x_ref[rows, :] + m[:, 2 * D_MODEL:3 * D_MODEL] * out


def _merge(latent, l, x, oa, ob, oc, gates, mod, w_a, w_b, w_c, w_o):
    tm = MERGE_TILE
    if latent:
        mod_idx = lambda i: (1 + (i * tm) // DEC_SEQ, 0, 0)
    else:
        mod_idx = lambda i: (0, 0, 0)
    o_spec = pl.BlockSpec((4, tm, LANES), lambda i: (0, i, 0))
    wbr_spec = pl.BlockSpec((None, 4 * LANES, D_MODEL), lambda i: (l, 0, 0))
    return pl.pallas_call(
        _merge_kernel,
        grid=(N_TOK // tm,),
        in_specs=[pl.BlockSpec((tm, D_MODEL), lambda i: (i, 0)),
                  o_spec, o_spec, o_spec,
                  pl.BlockSpec((N_GATE_GROUPS, tm, LANES), lambda i: (0, i, 0)),
                  pl.BlockSpec((1, 1, 6 * D_MODEL), mod_idx),
                  wbr_spec, wbr_spec, wbr_spec,
                  pl.BlockSpec((None, D_MODEL, D_MODEL), lambda i: (l, 0, 0))],
        out_specs=pl.BlockSpec((tm, D_MODEL), lambda i: (i, 0)),
        out_shape=jax.ShapeDtypeStruct((N_TOK, D_MODEL), F32),
        scratch_shapes=[pltpu.VMEM((3, 4 * LANES, D_MODEL), BF16), pltpu.VMEM((D_MODEL, D_MODEL), BF16)],
        compiler_params=_params(("arbitrary",)),
        name="merge_lat" if latent else "merge_ctx",
    )(x, oa, ob, oc, gates, mod, w_a, w_b, w_c, w_o)


FF_TILE = 1024
FF_CHUNK = 1024


def _ffn_kernel(x_ref, mod_ref, g2_ref, w1_ref, w2_ref, gf_ref, xo_ref, acc_scr, h2_scr, *, final):
    f = pl.program_id(1)

    def run(first):
        w1 = w1_ref[...].astype(BF16)
        w2 = w2_ref[...].astype(BF16)
        m = mod_ref[0]
        for rc in range(ROW_TILE // FF_CHUNK):
            rows = pl.ds(rc * FF_CHUNK, FF_CHUNK)
            if first:
                x = x_ref[rows, :]
                ms = jnp.mean(x * x, axis=-1, keepdims=True)
                y = x * lax.rsqrt(ms + EPS) * g2_ref[...]
                h2 = (y * (1.0 + m[:, 4 * D_MODEL:5 * D_MODEL]) + m[:, 3 * D_MODEL:4 * D_MODEL]).astype(BF16)
                h2_scr[rows, :] = h2
            else:
                h2 = h2_scr[rows, :]
            u = jnp.dot(h2, w1, preferred_element_type=F32)
            u = jnp.square(jnp.maximum(u, 0.0)).astype(BF16)
            d = jnp.dot(u, w2, preferred_element_type=F32)
            if first:
                acc_scr[rows, :] = d
            else:
                acc_scr[rows, :] += d

    @pl.when(f == 0)
    def _():
        run(True)

    @pl.when(f > 0)
    def _():
        run(False)

    @pl.when(f == D_FF // FF_TILE - 1)
    def _():
        x = x_ref[...] + mod_ref[0][:, 5 * D_MODEL:6 * D_MODEL] * acc_scr[...]
        if final:
            ms = jnp.mean(x * x, axis=-1, keepdims=True)
            x = x * lax.rsqrt(ms + EPS) * gf_ref[...]
        xo_ref[...] = x


def _ffn(latent, l, x, mod, g_norm2, w1, w2, g_final):
    tm = ROW_TILE
    mod_idx = (lambda i, f: (1 + i, 0, 0)) if latent else (lambda i, f: (0, 0, 0))
    return pl.pallas_call(
        functools.partial(_ffn_kernel, final=(l == DEPTH - 1)),
        grid=(N_TOK // tm, D_FF // FF_TILE),
        in_specs=[pl.BlockSpec((tm, D_MODEL), lambda i, f: (i, 0)),
                  pl.BlockSpec((1, 1, 6 * D_MODEL), mod_idx),
                  pl.BlockSpec((None, 1, D_MODEL), lambda i, f: (l, 0, 0)),
                  pl.BlockSpec((None, D_MODEL, FF_TILE), lambda i, f: (l, 0, f)),
                  pl.BlockSpec((None, FF_TILE, D_MODEL), lambda i, f: (l, f, 0)),
                  pl.BlockSpec((1, D_MODEL), lambda i, f: (0, 0))],
        out_specs=pl.BlockSpec((tm, D_MODEL), lambda i, f: (i, 0)),
        out_shape=jax.ShapeDtypeStruct((N_TOK, D_MODEL), F32),
        scratch_shapes=[pltpu.VMEM((tm, D_MODEL), F32), pltpu.VMEM((tm, D_MODEL), BF16)],
        compiler_params=_params(("arbitrary", "arbitrary")),
        name="ffn_lat" if latent else "ffn_ctx",
    )(x, mod, g_norm2, w1, w2, g_final)


def _rope_tables():
    nf = ROPE_HALF
    t = jnp.arange(DEC_SEQ)
    row = (t // GRID_W).astype(F32)
    col = (t % GRID_W).astype(F32)
    inv = ROPE_BASE ** (-jnp.arange(nf, dtype=F32) / nf)
    ar = row[:, None] * inv[None, :]
    ac = col[:, None] * inv[None, :]
    cos = jnp.concatenate([jnp.cos(ar), jnp.cos(ar), jnp.cos(ac), jnp.cos(ac)], axis=-1)
    sin = jnp.concatenate([-jnp.sin(ar), jnp.sin(ar), -jnp.sin(ac), jnp.sin(ac)], axis=-1)
    return jnp.tile(cos, (1, 2)), jnp.tile(sin, (1, 2))


def _packed_bias_rows(rel_bias):
    n = 2 * NA_ROWS - 1
    first = jnp.concatenate([rel_bias[..., NA_COLS - 1:], jnp.zeros((DEPTH, HC, n, LANES - (2 * NA_COLS - 1)), F32),
                             rel_bias[..., :NA_COLS - 1]], axis=-1)
    lo = GRID_W - NA_COLS + 1
    second = jnp.pad(rel_bias, ((0, 0), (0, 0), (0, 0), (lo, LANES - lo - (2 * NA_COLS - 1))))
    pad_rows = ((0, 0), (0, 0), (0, BIAS_TABLE_ROWS - n), (0, 0))
    return jnp.concatenate([jnp.pad(first, pad_rows), jnp.pad(second, pad_rows)], axis=2)


def kernel(x_prompt, x_sample, c, cache_a_k, cache_a_v, cache_b_k, cache_b_v, cache_c_k, cache_c_v, c_ctx, w_mod, b_mod, g_norm1, g_norm2, w_in, b_gate, lambda_qk, g_subln, g_qnorm, g_knorm, rel_bias, w_branch_a, w_branch_b, w_branch_c, w_out, w_ff1, w_ff2, g_final):
    xp = x_prompt.reshape(N_TOK, D_MODEL)
    xs = x_sample.reshape(N_TOK, D_MODEL)

    cvec = jnp.concatenate([c_ctx[None, :], c, jnp.zeros((3, D_MODEL), F32)], axis=0)
    mods = _modulation(cvec, w_mod, b_mod)

    rope_cos, rope_sin = _rope_tables()
    bias_rows = _packed_bias_rows(rel_bias)
    bd = jnp.kron(jnp.eye(2 * LANES // HEAD_DIM, dtype=F32),
                  jnp.full((HEAD_DIM, HEAD_DIM), 1.0 / HEAD_DIM, F32)).astype(BF16)
    gq = jnp.tile(g_qnorm, (1, 2 * LANES // DB)).reshape(DEPTH, 1, 2 * LANES)
    gk = jnp.tile(g_knorm, (1, LANES // DB)).reshape(DEPTH, 1, LANES)
    g1 = g_norm1.reshape(DEPTH, 1, D_MODEL)
    g2 = g_norm2.reshape(DEPTH, 1, D_MODEL)
    bg = b_gate.reshape(DEPTH, 1, 3 * D_MODEL)
    gsub = g_subln.reshape(DEPTH, 1, 2 * DA)
    gf = g_final.reshape(1, D_MODEL)

    def heads_t(cache):
        return cache.transpose(0, 1, 3, 4, 2)

    ctx_ak, ctx_bk, ctx_bv = heads_t(cache_a_k), heads_t(cache_b_k), heads_t(cache_b_v)
    ctx_ck, ctx_cv = heads_t(cache_c_k), heads_t(cache_c_v)

    new_caches = None

    for l in range(DEPTH):
        mod = mods[l].reshape(8, 1, 6 * D_MODEL)

        outs = _inproj(False, l, xp, mod, g1, w_in, bg, gq, gk, bd, None, None, new_caches)
        z, gates, new_caches = outs[0], outs[1], list(outs[2:])
        oa, ob, oc = _attn_ctx(l, z, lambda_qk, gsub)
        xp = _merge(False, l, xp, oa, ob, oc, gates, mod, w_branch_a, w_branch_b, w_branch_c, w_out)
        xp = _ffn(False, l, xp, mod, g2, w_ff1, w_ff2, gf)

        z, gates = _inproj(True, l, xs, mod, g1, w_in, bg, gq, gk, bd, rope_cos, rope_sin, None)
        oa = _attn_lat_a(l, z, ctx_ak, cache_a_v, lambda_qk, gsub)
        ob = _attn_lat_b(l, z, ctx_bk, ctx_bv)
        oc = _attn_lat_c(l, z, ctx_ck, ctx_cv, bias_rows)
        xs = _merge(True, l, xs, oa, ob, oc, gates, mod, w_branch_a, w_branch_b, w_branch_c, w_out)
        xs = _ffn(True, l, xs, mod, g2, w_ff1, w_ff2, gf)

    y_prompt = xp.reshape(BATCH, SEQ, D_MODEL)
    y_sample = xs.reshape(DEC_BATCH, DEC_SEQ, D_MODEL)
    ak, av, bk, bv, ck, cv = new_caches
    return (y_prompt, y_sample,
            ak.reshape(BATCH, DEPTH, SEQ, 2 * HA, DA), av.reshape(BATCH, DEPTH, SEQ, HA, 2 * DA),
            bk.reshape(BATCH, DEPTH, SEQ, KVB, DB), bv.reshape(BATCH, DEPTH, SEQ, KVB, DB),
            ck.reshape(BATCH, DEPTH, SEQ, HC, DC), cv.reshape(BATCH, DEPTH, SEQ, HC, DC))
```

```python
import functools
import math

import jax
import jax.numpy as jnp
from jax import lax
from jax.experimental import pallas as pl
from jax.experimental.pallas import tpu as pltpu

D_MODEL = 1024
BATCH = 16
SEQ = 256
DEPTH = 4
DEC_BATCH = 4
DEC_SEQ = 1024
PAST_LEN = 256
GRID_W = 64
HA, DA = 4, 64
HB, KVB, DB = 8, 2, 64
HC, DC = 8, 64
NA_ROWS, NA_COLS = 8, 16
D_FF = 4 * D_MODEL
ROPE_BASE = 10000.0
EPS = 1e-6
NEG = -1e30
HEAD_DIM = 64
LOG2E = math.log2(math.e)
Q_SCALE = HEAD_DIM ** -0.5 * LOG2E

F32 = jnp.float32
BF16 = jnp.bfloat16

LANES = 128
N_QKV_GROUPS = 30
N_GATE_GROUPS = 24
COL_TILE = 768
GROUPS_PER_TILE = COL_TILE // LANES
N_QKV_TILES = N_QKV_GROUPS // GROUPS_PER_TILE
N_COL_TILES = N_QKV_TILES + N_GATE_GROUPS // GROUPS_PER_TILE
ROW_TILE = 1024
ROW_CHUNK = 256
N_TOK = BATCH * SEQ
Q_GROUPS = 4
ROPE_HALF = HEAD_DIM // 4
HEAD_SHIFT = HEAD_DIM.bit_length() - 1
MOD_TILE = 3072
VMEM_LIMIT_V7X = 58 * 1024 * 1024

G_QA, G_KA, G_VA, G_QB, G_KB, G_VB, G_QC, G_KC, G_VC = 0, 4, 8, 12, 16, 17, 18, 22, 26
CACHE_WIDTHS = (2 * HA * DA, HA * 2 * DA, KVB * DB, KVB * DB, HC * DC, HC * DC)

_NT = (((1,), (1,)), ((), ()))


def _params(sem, vmem=VMEM_LIMIT_V7X):
    return pltpu.CompilerParams(dimension_semantics=sem, vmem_limit_bytes=vmem)


def _lane_head(shape):
    return lax.shift_right_logical(lax.broadcasted_iota(jnp.int32, shape, len(shape) - 1), HEAD_SHIFT)


def _mod_kernel(c_ref, w_ref, b_ref, o_ref):
    c = c_ref[...]
    s = (c * jax.nn.sigmoid(c)).astype(BF16)
    o_ref[...] = jnp.dot(s, w_ref[...].astype(BF16), preferred_element_type=F32) + b_ref[...]


def _modulation(cvec, w_mod, b_mod):
    tn = MOD_TILE
    n6 = 6 * D_MODEL
    return pl.pallas_call(
        _mod_kernel,
        grid=(DEPTH, n6 // tn),
        in_specs=[pl.BlockSpec((8, D_MODEL), lambda l, n: (0, 0)),
                  pl.BlockSpec((None, D_MODEL, tn), lambda l, n: (l, 0, n)),
                  pl.BlockSpec((None, 1, tn), lambda l, n: (l, 0, n))],
        out_specs=pl.BlockSpec((None, 8, tn), lambda l, n: (l, 0, n)),
        out_shape=jax.ShapeDtypeStruct((DEPTH, 8, n6), F32),
        compiler_params=_params(("arbitrary", "arbitrary")),
        name="modulation",
    )(cvec, w_mod, b_mod.reshape(DEPTH, 1, n6))


def _rope(v, cos, sin):
    first = (lax.broadcasted_iota(jnp.int32, v.shape, 1) & ROPE_HALF) == 0
    partner = jnp.where(first, pltpu.roll(v, LANES - ROPE_HALF, 1), pltpu.roll(v, ROPE_HALF, 1))
    return v * cos + partner * sin


def _head_rmsnorm(v, bd, g):
    msq = jnp.dot((v * v).astype(BF16), bd, preferred_element_type=F32)
    return v * lax.rsqrt(msq + EPS) * g


def _make_inproj_kernel(latent, n_aliased):
    def kern(*refs):
        if latent:
            (x_ref, mod_ref, g1_ref, w_ref, bg_ref, gq_ref, gk_ref, bd_ref, cos_ref, sin_ref,
             z_ref, gate_ref, h_scr, wbf_scr) = refs
        else:
            (x_ref, mod_ref, g1_ref, w_ref, bg_ref, gq_ref, gk_ref, bd_ref) = refs[:8]
            (z_ref, gate_ref, ak_ref, av_ref, bk_ref, bv_ref, ck_ref, cv_ref,
             h_scr, wbf_scr) = refs[8 + n_aliased:]
        j = pl.program_id(0)
        i = pl.program_id(1)

        @pl.when(i == 0)
        def _():
            wbf_scr[...] = w_ref[...].astype(BF16)

        def chunks(first=False):
            for rc in range(ROW_TILE // ROW_CHUNK):
                rows = pl.ds(rc * ROW_CHUNK, ROW_CHUNK)
                tok = pl.ds(pl.multiple_of(i * ROW_TILE + rc * ROW_CHUNK, ROW_CHUNK), ROW_CHUNK)
                if first:
                    x = x_ref[rows, :]
                    ms = jnp.mean(x * x, axis=-1, keepdims=True)
                    y = x * lax.rsqrt(ms + EPS) * g1_ref[...]
                    m = mod_ref[0]
                    h = (y * (1.0 + m[:, D_MODEL:2 * D_MODEL]) + m[:, 0:D_MODEL]).astype(BF16)
                    h_scr[tok, :] = h
                else:
                    h = h_scr[tok, :]
                yield rc, rows, jnp.dot(h, wbf_scr[...], preferred_element_type=F32)

        def grp(a, c, n=1):
            return a[:, c * LANES:(c + n) * LANES]

        def rot(v, rows):
            return _rope(v, cos_ref[rows, :], sin_ref[rows, :]) if latent else v

        @pl.when(j == 0)
        def _():
            for rc, rows, a in chunks(first=True):
                for c in range(GROUPS_PER_TILE):
                    v = rot(grp(a, c), rows)
                    z_ref[c, rows, :] = (v * Q_SCALE if c < Q_GROUPS else v).astype(BF16)
                if not latent:
                    ak_ref[rc] = grp(a, 4, 2)

        @pl.when(j == 1)
        def _():
            for rc, rows, a in chunks():
                for c in range(2):
                    z_ref[c, rows, :] = rot(grp(a, c), rows).astype(BF16)
                for c in range(2, GROUPS_PER_TILE):
                    z_ref[c, rows, :] = grp(a, c).astype(BF16)
                if not latent:
                    ak_ref[rc] = grp(a, 0, 2)
                    av_ref[rc] = grp(a, 2, 4).reshape(ROW_CHUNK, HA, 2 * DA)

        @pl.when(j == 2)
        def _():
            bd = bd_ref[...]
            for rc, rows, a in chunks():
                for half in range(2):
                    qn = _head_rmsnorm(grp(a, 2 * half, 2), bd, gq_ref[...])
                    for c in range(2):
                        z_ref[2 * half + c, rows, :] = (rot(grp(qn, c), rows) * Q_SCALE).astype(BF16)
                kn = _head_rmsnorm(grp(a, 4), bd[0:LANES, 0:LANES], gk_ref[...])
                z_ref[4, rows, :] = rot(kn, rows).astype(BF16)
                z_ref[5, rows, :] = grp(a, 5).astype(BF16)
                if not latent:
                    bk_ref[rc] = kn
                    bv_ref[rc] = grp(a, 5)

        @pl.when(j == 3)
        def _():
            for rc, rows, a in chunks():
                for c in range(GROUPS_PER_TILE):
                    v = grp(a, c)
                    z_ref[c, rows, :] = (v * Q_SCALE if c < Q_GROUPS else v).astype(BF16)
                if not latent:
                    ck_ref[rc] = grp(a, 4, 2)

        @pl.when(j == 4)
        def _():
            for rc, rows, a in chunks():
                for c in range(GROUPS_PER_TILE):
                    z_ref[c, rows, :] = grp(a, c).astype(BF16)
                if not latent:
                    ck_ref[rc] = grp(a, 0, 2)
                    cv_ref[rc] = grp(a, 2, 4)

        @pl.when(j >= N_QKV_TILES)
        def _():
            for rc, rows, a in chunks():
                a = a + bg_ref[...]
                for c in range(GROUPS_PER_TILE):
                    gate_ref[c, rows, :] = jax.nn.sigmoid(grp(a, c)).astype(BF16)

    return kern


def _inproj(latent, l, x, mod, g_norm1, w_in, b_gate, gq, gk, bd, rope_cos, rope_sin, caches):
    n_row = N_TOK // ROW_TILE
    last = n_row - 1

    def row_block(j, i, first_tile, last_tile):
        return jnp.where(j < first_tile, 0, jnp.where(j <= last_tile, i, last))

    mod_idx = (lambda j, i: (1 + i, 0, 0)) if latent else (lambda j, i: (0, 0, 0))
    in_specs = [
        pl.BlockSpec((ROW_TILE, D_MODEL), lambda j, i: (row_block(j, i, 0, 0), 0)),
        pl.BlockSpec((1, 1, 6 * D_MODEL), mod_idx),
        pl.BlockSpec((None, 1, D_MODEL), lambda j, i: (l, 0, 0)),
        pl.BlockSpec((None, D_MODEL, COL_TILE), lambda j, i: (l, 0, j)),
        pl.BlockSpec((None, 1, COL_TILE), lambda j, i: (l, 0, jnp.maximum(j - N_QKV_TILES, 0))),
        pl.BlockSpec((None, 1, 2 * LANES), lambda j, i: (l, 0, 0)),
        pl.BlockSpec((None, 1, LANES), lambda j, i: (l, 0, 0)),
        pl.BlockSpec((2 * LANES, 2 * LANES), lambda j, i: (0, 0)),
    ]
    args = [x, mod, g_norm1, w_in, b_gate, gq, gk, bd]
    out_specs = [
        pl.BlockSpec((GROUPS_PER_TILE, ROW_TILE, LANES),
                     lambda j, i: (jnp.minimum(j, N_QKV_TILES - 1), row_block(j, i, 0, N_QKV_TILES - 1), 0)),
        pl.BlockSpec((GROUPS_PER_TILE, ROW_TILE, LANES),
                     lambda j, i: (jnp.maximum(j - N_QKV_TILES, 0), row_block(j, i, N_QKV_TILES, N_COL_TILES), 0)),
    ]
    out_shape = [jax.ShapeDtypeStruct((N_QKV_GROUPS, N_TOK, LANES), BF16),
                 jax.ShapeDtypeStruct((N_GATE_GROUPS, N_TOK, LANES), BF16)]
    aliases = {}
    if latent:
        in_specs += [pl.BlockSpec((DEC_SEQ, LANES), lambda j, i: (0, 0))] * 2
        args += [rope_cos, rope_sin]
    else:
        nb = ROW_TILE // SEQ
        tiles = ((0, 1, 256), (1, 1, 512), (2, 2, 128), (2, 2, 128), (3, 4, 256), (4, 4, 512))
        for k, (w, (t0, t1, bw)) in enumerate(zip(CACHE_WIDTHS, tiles)):
            if caches is not None:
                in_specs.append(pl.BlockSpec(memory_space=pl.ANY))
                args.append(caches[k])
                aliases[8 + k] = 2 + k

            def cache_idx(j, i, t0=t0, t1=t1):
                return (row_block(j, i, t0, t1), l, 0, jnp.where(j <= t0, 0, (t1 - t0)))

            if k == 1:
                out_specs.append(pl.BlockSpec((nb, None, SEQ, HA, 2 * DA),
                                              lambda j, i, f=cache_idx: f(j, i) + (0,)))
                out_shape.append(jax.ShapeDtypeStruct((BATCH, DEPTH, SEQ, HA, 2 * DA), F32))
                continue
            out_specs.append(pl.BlockSpec((nb, None, SEQ, bw), cache_idx))
            out_shape.append(jax.ShapeDtypeStruct((BATCH, DEPTH, SEQ, w), F32))
    return pl.pallas_call(
        _make_inproj_kernel(latent, len(aliases)),
        grid=(N_COL_TILES, n_row),
        in_specs=in_specs,
        out_specs=out_specs,
        out_shape=out_shape,
        scratch_shapes=[pltpu.VMEM((N_TOK, D_MODEL), BF16), pltpu.VMEM((D_MODEL, COL_TILE), BF16)],
        input_output_aliases=aliases,
        compiler_params=_params(("arbitrary", "arbitrary")),
        name="inproj_lat" if latent else "inproj_ctx",
    )(*args)


def _mask_head(q, head):
    qf = q.astype(F32)
    keep = _lane_head(qf.shape) == head
    return jnp.where(keep, qf, 0.0).astype(BF16)


def _stack_heads(q):
    return jnp.concatenate([_mask_head(q, 0), _mask_head(q, 1)], axis=0)


def _unstack_heads(o, rows):
    return jnp.where(_lane_head((rows, LANES)) == 1, o[rows:2 * rows], o[0:rows])


def _dup_head(kv, head):
    f = kv.astype(F32)
    r = pltpu.roll(f, HEAD_DIM, 1)
    return jnp.where(_lane_head(f.shape) == head, f, r).astype(BF16)


def _attend(qm, ks, vs):
    nt = (False,) * len(ks)
    return _attend_multi([(qm, ks, vs, None, nt, nt)])[0]


def _attend_multi(jobs):
    sss = []
    for qm, ks, _, biases, k_t, _ in jobs:
        ss = [jnp.dot(qm, k, preferred_element_type=F32) if t else
              lax.dot_general(qm, k, _NT, preferred_element_type=F32) for k, t in zip(ks, k_t)]
        if biases is not None:
            ss = [s if b is None else s + b for s, b in zip(ss, biases)]
        sss.append(ss)
    pss, dens = [], []
    for ss in sss:
        m = functools.reduce(jnp.maximum, [jnp.max(s, axis=-1, keepdims=True) for s in ss])
        ps = [jnp.exp2(s - m) for s in ss]
        dens.append(functools.reduce(jnp.add, [jnp.sum(p, axis=-1, keepdims=True) for p in ps]))
        pss.append(ps)
    outs = []
    for (_, _, vs, _, _, v_t), ps, den in zip(jobs, pss, dens):
        o = functools.reduce(jnp.add, [lax.dot_general(p.astype(BF16), v, _NT, preferred_element_type=F32) if t else
                                        jnp.dot(p.astype(BF16), v, preferred_element_type=F32)
                                        for p, v, t in zip(ps, vs, v_t)])
        outs.append(o * (1.0 / den))
    return outs


_CTX_T = (False, True)


def _pair_t(ref):
    return ref[...].reshape(2 * HEAD_DIM, ref.shape[-1]).astype(BF16)


def _diff_lambda(lam_ref, lam_init):
    lq = lam_ref[...]
    a = jnp.sum(lq[0:1] * lq[1:2], axis=-1, keepdims=True)
    b = jnp.sum(lq[2:3] * lq[3:4], axis=-1, keepdims=True)
    return jnp.exp(a) - jnp.exp(b) + lam_init


def _diff_combine(o1, o2, lam, gsub, lam_init):
    o = o1 - lam * o2
    ms = jnp.mean(o * o, axis=-1, keepdims=True)
    return (o * lax.rsqrt(ms + EPS) * gsub) * (1.0 - lam_init)


CTX_BATCH_PER_STEP = 2


def _make_attn_ctx_kernel(lam_init):
    def kern(z_ref, lam_ref, gsub_ref, oa_ref, ob_ref, oc_ref):
        lam = _diff_lambda(lam_ref, lam_init)
        gsub = gsub_ref[...]
        for bb in range(CTX_BATCH_PER_STEP):
            rows = pl.ds(bb * SEQ, SEQ)

            def z(g):
                return z_ref[g, rows, :]

            for vh in range(HA):
                hi = vh % 2
                v = z(G_VA + vh)
                o1 = _attend(_mask_head(z(G_QA + vh // 2), hi), [z(G_KA + vh // 2)], [v])
                o2 = _attend(_mask_head(z(G_QA + 2 + vh // 2), hi), [z(G_KA + 2 + vh // 2)], [v])
                oa_ref[vh, rows, :] = _diff_combine(o1, o2, lam, gsub, lam_init).astype(BF16)
            for g in range(KVB):
                kd = _dup_head(z(G_KB), g)
                vd = _dup_head(z(G_VB), g)
                for c in range(2 * g, 2 * g + 2):
                    o = _attend(_stack_heads(z(G_QB + c)), [kd], [vd])
                    ob_ref[c, rows, :] = _unstack_heads(o, SEQ).astype(BF16)
            for c in range(HC // 2):
                o = _attend(_stack_heads(z(G_QC + c)), [z(G_KC + c)], [z(G_VC + c)])
                oc_ref[c, rows, :] = _unstack_heads(o, SEQ).astype(BF16)
    return kern


def _attn_ctx(l, z, lambda_qk, g_subln):
    lam_init = 0.8 - 0.6 * math.exp(-0.3 * l)
    rows = CTX_BATCH_PER_STEP * SEQ
    o_spec = pl.BlockSpec((4, rows, LANES), lambda b: (0, b, 0))
    o_shape = jax.ShapeDtypeStruct((4, N_TOK, LANES), BF16)
    return pl.pallas_call(
        _make_attn_ctx_kernel(lam_init),
        grid=(BATCH // CTX_BATCH_PER_STEP,),
        in_specs=[pl.BlockSpec((N_QKV_GROUPS, rows, LANES), lambda b: (0, b, 0)),
                  pl.BlockSpec((None, 4, DA), lambda b: (l, 0, 0)),
                  pl.BlockSpec((None, 1, 2 * DA), lambda b: (l, 0, 0))],
        out_specs=[o_spec, o_spec, o_spec],
        out_shape=[o_shape, o_shape, o_shape],
        compiler_params=_params(("arbitrary",)),
        name="attn_ctx",
    )(z, lambda_qk, g_subln)


Q_BLK_GQA = 512
Q_BLK_DIFF = 1024


def _make_attn_lat_a_kernel(lam_init):
    def kern(q1_ref, q2_ref, k1_ref, k2_ref, v_ref, ck1_ref, ck2_ref, cv_ref, lam_ref, gsub_ref, o_ref):
        vh = pl.program_id(1)
        hi = jnp.bitwise_and(vh, 1)
        lam = _diff_lambda(lam_ref, lam_init)
        gsub = gsub_ref[...]
        ck1 = _pair_t(ck1_ref)
        ck2 = _pair_t(ck2_ref)
        cv = cv_ref[:, vh, :].astype(BF16)
        k1, k2, v = k1_ref[0], k2_ref[0], v_ref[0]
        rows = [pl.ds(qb * Q_BLK_DIFF, Q_BLK_DIFF) for qb in range(DEC_SEQ // Q_BLK_DIFF)]
        jobs = []
        for r in rows:
            jobs.append((_mask_head(q1_ref[0, r, :], hi), [k1, ck1], [v, cv], None, _CTX_T, (False, False)))
            jobs.append((_mask_head(q2_ref[0, r, :], hi), [k2, ck2], [v, cv], None, _CTX_T, (False, False)))
        outs = _attend_multi(jobs)
        for t, r in enumerate(rows):
            o_ref[0, r, :] = _diff_combine(outs[2 * t], outs[2 * t + 1], lam, gsub, lam_init).astype(BF16)
    return kern


def _attn_lat_a(l, z, cache_k_t, cache_v, lambda_qk, g_subln):
    lam_init = 0.8 - 0.6 * math.exp(-0.3 * l)

    def zspec(fn):
        return pl.BlockSpec((1, DEC_SEQ, LANES), lambda b, h: (fn(h), b, 0))

    return pl.pallas_call(
        _make_attn_lat_a_kernel(lam_init),
        grid=(DEC_BATCH, HA),
        in_specs=[zspec(lambda h: G_QA + h // 2), zspec(lambda h: G_QA + 2 + h // 2),
                  zspec(lambda h: G_KA + h // 2), zspec(lambda h: G_KA + 2 + h // 2),
                  zspec(lambda h: G_VA + h),
                  pl.BlockSpec((None, None, 2, DA, PAST_LEN), lambda b, h: (b, l, h // 2, 0, 0)),
                  pl.BlockSpec((None, None, 2, DA, PAST_LEN), lambda b, h: (b, l, HA // 2 + h // 2, 0, 0)),
                  pl.BlockSpec((None, None, PAST_LEN, HA, 2 * DA), lambda b, h: (b, l, 0, 0, 0)),
                  pl.BlockSpec((None, 4, DA), lambda b, h: (l, 0, 0)),
                  pl.BlockSpec((None, 1, 2 * DA), lambda b, h: (l, 0, 0))],
        out_specs=pl.BlockSpec((1, DEC_SEQ, LANES), lambda b, h: (h, b, 0)),
        out_shape=jax.ShapeDtypeStruct((4, N_TOK, LANES), BF16),
        compiler_params=_params(("arbitrary", "arbitrary")),
        name="attn_lat_a",
    )(z, z, z, z, z, cache_k_t, cache_k_t, cache_v, lambda_qk, g_subln)


def _attn_lat_b_kernel(q_ref, k_ref, v_ref, ck_ref, cv_ref, o_ref):
    hi = lax.shift_right_logical(pl.program_id(1), 1)
    kd = _dup_head(k_ref[0], hi)
    vd = _dup_head(v_ref[0], hi)
    ck = ck_ref[hi].astype(BF16)
    cv = cv_ref[hi].astype(BF16)
    ckd = jnp.concatenate([ck, ck], axis=0)
    cvd = jnp.concatenate([cv, cv], axis=0)
    rows = [pl.ds(qb * Q_BLK_GQA, Q_BLK_GQA) for qb in range(DEC_SEQ // Q_BLK_GQA)]
    outs = _attend_multi([(_stack_heads(q_ref[0, r, :]), [kd, ckd], [vd, cvd], None, _CTX_T, _CTX_T)
                          for r in rows])
    for r, o in zip(rows, outs):
        o_ref[0, r, :] = _unstack_heads(o, Q_BLK_GQA).astype(BF16)


def _attn_lat_b(l, z, cache_k, cache_v):
    cspec = pl.BlockSpec((None, None, KVB, DB, PAST_LEN), lambda b, c: (b, l, 0, 0, 0))
    return pl.pallas_call(
        _attn_lat_b_kernel,
        grid=(DEC_BATCH, HB // 2),
        in_specs=[pl.BlockSpec((1, DEC_SEQ, LANES), lambda b, c: (G_QB + c, b, 0)),
                  pl.BlockSpec((1, DEC_SEQ, LANES), lambda b, c: (G_KB, b, 0)),
                  pl.BlockSpec((1, DEC_SEQ, LANES), lambda b, c: (G_VB, b, 0)),
                  cspec, cspec],
        out_specs=pl.BlockSpec((1, DEC_SEQ, LANES), lambda b, c: (c, b, 0)),
        out_shape=jax.ShapeDtypeStruct((4, N_TOK, LANES), BF16),
        compiler_params=_params(("arbitrary", "arbitrary")),
        name="attn_lat_b",
    )(z, z, z, cache_k, cache_v)


N_GRID_ROWS = DEC_SEQ // GRID_W
NA_GROUP = 2
BIAS_TABLE_ROWS = 16


def _window_start(r):
    return min(max(r - NA_ROWS // 2, 0), N_GRID_ROWS - NA_ROWS)


def _group_keys(g):
    starts = [_window_start(r) for r in range(g * NA_GROUP, (g + 1) * NA_GROUP)]
    n = max(starts) + NA_ROWS - min(starts)
    n += n % 2
    return min(min(starts), N_GRID_ROWS - n), n


NA_MAX_KEY_ROWS = max(_group_keys(g)[1] for g in range(N_GRID_ROWS // NA_GROUP))


def _build_window_bias(tab_ref, bias_scr):
    qcol = lax.broadcasted_iota(jnp.int32, (GRID_W, LANES), 0)
    lane = lax.broadcasted_iota(jnp.int32, (GRID_W, LANES), 1)
    kcol = lane & (GRID_W - 1)
    c0 = jnp.clip(qcol - NA_COLS // 2, 0, GRID_W - NA_COLS)
    in_win = (kcol >= c0) & (kcol < c0 + NA_COLS)
    masks = {(True, True): in_win, (True, False): in_win & (lane < GRID_W), (False, True): in_win & (lane >= GRID_W)}
    for head in range(2):
        tiles = {}

        def tile(drs):
            if drs not in tiles:
                if drs == (None, None):
                    tiles[drs] = jnp.full((GRID_W, LANES), NEG, F32)
                else:
                    vec = None
                    for half, dr in enumerate(drs):
                        if dr is not None:
                            part = tab_ref[head, pl.ds(half * BIAS_TABLE_ROWS + dr, 1), :]
                            vec = part if vec is None else vec + part
                    toep = pltpu.roll(jnp.broadcast_to(vec, (GRID_W, LANES)), 0, 1, stride=1, stride_axis=0)
                    valid = tuple(dr is not None for dr in drs)
                    tiles[drs] = jnp.where(masks[valid], toep * LOG2E, NEG)
            return tiles[drs]

        for r in range(N_GRID_ROWS):
            u0, n = _group_keys(r // NA_GROUP)
            rows = pl.ds(r * GRID_W, GRID_W)
            for m in range(n // 2):
                drs = tuple(kr - r + NA_ROWS - 1 if _window_start(r) <= kr < _window_start(r) + NA_ROWS else None
                            for kr in (u0 + 2 * m, u0 + 2 * m + 1))
                bias_scr[head, rows, pl.ds(m * LANES, LANES)] = tile(drs)


def _attn_lat_c_kernel(q_ref, k_ref, v_ref, ck_ref, cv_ref, tab_ref, o_ref, bias_scr):
    @pl.when(pl.program_id(1) == 0)
    def _():
        _build_window_bias(tab_ref, bias_scr)

    ck = _pair_t(ck_ref)
    cv = _pair_t(cv_ref)
    n_rows = NA_GROUP * GRID_W
    jobs, row_slices = [], []
    for g in range(N_GRID_ROWS // NA_GROUP):
        u0, n = _group_keys(g)
        rows = pl.ds(g * n_rows, n_rows)
        keys = pl.ds(u0 * GRID_W, n * GRID_W)
        bias = jnp.concatenate([bias_scr[0, rows, 0:n * GRID_W], bias_scr[1, rows, 0:n * GRID_W]], axis=0)
        jobs.append((_stack_heads(q_ref[0, rows, :]), [k_ref[0, keys, :], ck], [v_ref[0, keys, :], cv],
                     [bias, None], _CTX_T, _CTX_T))
        row_slices.append(rows)
    for rows, o in zip(row_slices, _attend_multi(jobs)):
        o_ref[0, rows, :] = _unstack_heads(o, n_rows).astype(BF16)


def _attn_lat_c(l, z, cache_k, cache_v, bias_rows):
    def zspec(g0):
        return pl.BlockSpec((1, DEC_SEQ, LANES), lambda c, b: (g0 + c, b, 0))

    cspec = pl.BlockSpec((None, None, 2, DC, PAST_LEN), lambda c, b: (b, l, c, 0, 0))
    return pl.pallas_call(
        _attn_lat_c_kernel,
        grid=(HC // 2, DEC_BATCH),
        in_specs=[zspec(G_QC), zspec(G_KC), zspec(G_VC), cspec, cspec,
                  pl.BlockSpec((None, 2, 2 * BIAS_TABLE_ROWS, LANES), lambda c, b: (l, c, 0, 0))],
        out_specs=pl.BlockSpec((1, DEC_SEQ, LANES), lambda c, b: (c, b, 0)),
        out_shape=jax.ShapeDtypeStruct((4, N_TOK, LANES), BF16),
        scratch_shapes=[pltpu.VMEM((2, DEC_SEQ, NA_MAX_KEY_ROWS * GRID_W), F32)],
        compiler_params=_params(("arbitrary", "arbitrary")),
        name="attn_lat_c",
    )(z, z, z, cache_k, cache_v, bias_rows)


MERGE_TILE = 512
BRANCH_GROUPS = 4
MODEL_GROUPS = D_MODEL // LANES


def _merge_kernel(x_ref, oa_ref, ob_ref, oc_ref, gate_ref, mod_ref, wa_ref, wb_ref, wc_ref, wo_ref,
                  xo_ref, wbr_scr, wo_scr):
    @pl.when(pl.program_id(0) == 0)
    def _():
        wbr_scr[0] = wa_ref[...].astype(BF16)
        wbr_scr[1] = wb_ref[...].astype(BF16)
        wbr_scr[2] = wc_ref[...].astype(BF16)
        wo_scr[...] = wo_ref[...].astype(BF16)

    m = mod_ref[0]
    for rc in range(MERGE_TILE // ROW_CHUNK):
        rows = pl.ds(rc * ROW_CHUNK, ROW_CHUNK)
        y = None
        for k, o_ref in enumerate((oa_ref, ob_ref, oc_ref)):
            o = jnp.concatenate([o_ref[c, rows, :] for c in range(BRANCH_GROUPS)], axis=-1)
            p = jnp.dot(o, wbr_scr[k], preferred_element_type=F32)
            g = jnp.concatenate([gate_ref[MODEL_GROUPS * k + c, rows, :] for c in range(MODEL_GROUPS)],
                                axis=-1).astype(F32)
            y = g * p if y is None else y + g * p
        out = jnp.dot(y.astype(BF16), wo_scr[...], preferred_element_type=F32)
        xo_ref[rows, :] = x_ref[rows, :] + m[:, 2 * D_MODEL:3 * D_MODEL] * out


def _merge(latent, l, x, oa, ob, oc, gates, mod, w_a, w_b, w_c, w_o):
    tm = MERGE_TILE
    if latent:
        mod_idx = lambda i: (1 + (i * tm) // DEC_SEQ, 0, 0)
    else:
        mod_idx = lambda i: (0, 0, 0)
    o_spec = pl.BlockSpec((4, tm, LANES), lambda i: (0, i, 0))
    wbr_spec = pl.BlockSpec((None, 4 * LANES, D_MODEL), lambda i: (l, 0, 0))
    return pl.pallas_call(
        _merge_kernel,
        grid=(N_TOK // tm,),
        in_specs=[pl.BlockSpec((tm, D_MODEL), lambda i: (i, 0)),
                  o_spec, o_spec, o_spec,
                  pl.BlockSpec((N_GATE_GROUPS, tm, LANES), lambda i: (0, i, 0)),
                  pl.BlockSpec((1, 1, 6 * D_MODEL), mod_idx),
                  wbr_spec, wbr_spec, wbr_spec,
                  pl.BlockSpec((None, D_MODEL, D_MODEL), lambda i: (l, 0, 0))],
        out_specs=pl.BlockSpec((tm, D_MODEL), lambda i: (i, 0)),
        out_shape=jax.ShapeDtypeStruct((N_TOK, D_MODEL), F32),
        scratch_shapes=[pltpu.VMEM((3, 4 * LANES, D_MODEL), BF16), pltpu.VMEM((D_MODEL, D_MODEL), BF16)],
        compiler_params=_params(("arbitrary",)),
        name="merge_lat" if latent else "merge_ctx",
    )(x, oa, ob, oc, gates, mod, w_a, w_b, w_c, w_o)


FF_TILE = 1024
FF_CHUNK = 1024


def _ffn_kernel(x_ref, mod_ref, g2_ref, w1_ref, w2_ref, gf_ref, xo_ref, acc_scr, h2_scr, *, final):
    f = pl.program_id(1)

    def run(first):
        w1 = w1_ref[...].astype(BF16)
        w2 = w2_ref[...].astype(BF16)
        m = mod_ref[0]
        for rc in range(ROW_TILE // FF_CHUNK):
            rows = pl.ds(rc * FF_CHUNK, FF_CHUNK)
            if first:
                x = x_ref[rows, :]
                ms = jnp.mean(x * x, axis=-1, keepdims=True)
                y = x * lax.rsqrt(ms + EPS) * g2_ref[...]
                h2 = (y * (1.0 + m[:, 4 * D_MODEL:5 * D_MODEL]) + m[:, 3 * D_MODEL:4 * D_MODEL]).astype(BF16)
                h2_scr[rows, :] = h2
            else:
                h2 = h2_scr[rows, :]
            u = jnp.dot(h2, w1, preferred_element_type=F32)
            u = jnp.square(jnp.maximum(u, 0.0)).astype(BF16)
            d = jnp.dot(u, w2, preferred_element_type=F32)
            if first:
                acc_scr[rows, :] = d
            else:
                acc_scr[rows, :] += d

    @pl.when(f == 0)
    def _():
        run(True)

    @pl.when(f > 0)
    def _():
        run(False)

    @pl.when(f == D_FF // FF_TILE - 1)
    def _():
        x = x_ref[...] + mod_ref[0][:, 5 * D_MODEL:6 * D_MODEL] * acc_scr[...]
        if final:
            ms = jnp.mean(x * x, axis=-1, keepdims=True)
            x = x * lax.rsqrt(ms + EPS) * gf_ref[...]
        xo_ref[...] = x


def _ffn(latent, l, x, mod, g_norm2, w1, w2, g_final):
    tm = ROW_TILE
    mod_idx = (lambda i, f: (1 + i, 0, 0)) if latent else (lambda i, f: (0, 0, 0))
    return pl.pallas_call(
        functools.partial(_ffn_kernel, final=(l == DEPTH - 1)),
        grid=(N_TOK // tm, D_FF // FF_TILE),
        in_specs=[pl.BlockSpec((tm, D_MODEL), lambda i, f: (i, 0)),
                  pl.BlockSpec((1, 1, 6 * D_MODEL), mod_idx),
                  pl.BlockSpec((None, 1, D_MODEL), lambda i, f: (l, 0, 0)),
                  pl.BlockSpec((None, D_MODEL, FF_TILE), lambda i, f: (l, 0, f)),
                  pl.BlockSpec((None, FF_TILE, D_MODEL), lambda i, f: (l, f, 0)),
                  pl.BlockSpec((1, D_MODEL), lambda i, f: (0, 0))],
        out_specs=pl.BlockSpec((tm, D_MODEL), lambda i, f: (i, 0)),
        out_shape=jax.ShapeDtypeStruct((N_TOK, D_MODEL), F32),
        scratch_shapes=[pltpu.VMEM((tm, D_MODEL), F32), pltpu.VMEM((tm, D_MODEL), BF16)],
        compiler_params=_params(("arbitrary", "arbitrary")),
        name="ffn_lat" if latent else "ffn_ctx",
    )(x, mod, g_norm2, w1, w2, g_final)


def _rope_tables():
    nf = ROPE_HALF
    t = jnp.arange(DEC_SEQ)
    row = (t // GRID_W).astype(F32)
    col = (t % GRID_W).astype(F32)
    inv = ROPE_BASE ** (-jnp.arange(nf, dtype=F32) / nf)
    ar = row[:, None] * inv[None, :]
    ac = col[:, None] * inv[None, :]
    cos = jnp.concatenate([jnp.cos(ar), jnp.cos(ar), jnp.cos(ac), jnp.cos(ac)], axis=-1)
    sin = jnp.concatenate([-jnp.sin(ar), jnp.sin(ar), -jnp.sin(ac), jnp.sin(ac)], axis=-1)
    return jnp.tile(cos, (1, 2)), jnp.tile(sin, (1, 2))


def _packed_bias_rows(rel_bias):
    n = 2 * NA_ROWS - 1
    first = jnp.concatenate([rel_bias[..., NA_COLS - 1:], jnp.zeros((DEPTH, HC, n, LANES - (2 * NA_COLS - 1)), F32),
                             rel_bias[..., :NA_COLS - 1]], axis=-1)
    lo = GRID_W - NA_COLS + 1
    second = jnp.pad(rel_bias, ((0, 0), (0, 0), (0, 0), (lo, LANES - lo - (2 * NA_COLS - 1))))
    pad_rows = ((0, 0), (0, 0), (0, BIAS_TABLE_ROWS - n), (0, 0))
    return jnp.concatenate([jnp.pad(first, pad_rows), jnp.pad(second, pad_rows)], axis=2)


def kernel(x_prompt, x_sample, c, cache_a_k, cache_a_v, cache_b_k, cache_b_v, cache_c_k, cache_c_v, c_ctx, w_mod, b_mod, g_norm1, g_norm2, w_in, b_gate, lambda_qk, g_subln, g_qnorm, g_knorm, rel_bias, w_branch_a, w_branch_b, w_branch_c, w_out, w_ff1, w_ff2, g_final):
    xp = x_prompt.reshape(N_TOK, D_MODEL)
    xs = x_sample.reshape(N_TOK, D_MODEL)

    cvec = jnp.concatenate([c_ctx[None, :], c, jnp.zeros((3, D_MODEL), F32)], axis=0)
    mods = _modulation(cvec, w_mod, b_mod)

    rope_cos, rope_sin = _rope_tables()
    bias_rows = _packed_bias_rows(rel_bias)
    bd = jnp.kron(jnp.eye(2 * LANES // HEAD_DIM, dtype=F32),
                  jnp.full((HEAD_DIM, HEAD_DIM), 1.0 / HEAD_DIM, F32)).astype(BF16)
    gq = jnp.tile(g_qnorm, (1, 2 * LANES // DB)).reshape(DEPTH, 1, 2 * LANES)
    gk = jnp.tile(g_knorm, (1, LANES // DB)).reshape(DEPTH, 1, LANES)
    g1 = g_norm1.reshape(DEPTH, 1, D_MODEL)
    g2 = g_norm2.reshape(DEPTH, 1, D_MODEL)
    bg = b_gate.reshape(DEPTH, 1, 3 * D_MODEL)
    gsub = g_subln.reshape(DEPTH, 1, 2 * DA)
    gf = g_final.reshape(1, D_MODEL)

    def heads_t(cache):
        return cache.transpose(0, 1, 3, 4, 2)

    ctx_ak, ctx_bk, ctx_bv = heads_t(cache_a_k), heads_t(cache_b_k), heads_t(cache_b_v)
    ctx_ck, ctx_cv = heads_t(cache_c_k), heads_t(cache_c_v)

    new_caches = None

    for l in range(DEPTH):
        mod = mods[l].reshape(8, 1, 6 * D_MODEL)

        outs = _inproj(False, l, xp, mod, g1, w_in, bg, gq, gk, bd, None, None, new_caches)
        z, gates, new_caches = outs[0], outs[1], list(outs[2:])
        oa, ob, oc = _attn_ctx(l, z, lambda_qk, gsub)
        xp = _merge(False, l, xp, oa, ob, oc, gates, mod, w_branch_a, w_branch_b, w_branch_c, w_out)
        xp = _ffn(False, l, xp, mod, g2, w_ff1, w_ff2, gf)

        z, gates = _inproj(True, l, xs, mod, g1, w_in, bg, gq, gk, bd, rope_cos, rope_sin, None)
        oa = _attn_lat_a(l, z, ctx_ak, cache_a_v, lambda_qk, gsub)
        ob = _attn_lat_b(l, z, ctx_bk, ctx_bv)
        oc = _attn_lat_c(l, z, ctx_ck, ctx_cv, bias_rows)
        xs = _merge(True, l, xs, oa, ob, oc, gates, mod, w_branch_a, w_branch_b, w_branch_c, w_out)
        xs = _ffn(True, l, xs, mod, g2, w_ff1, w_ff2, gf)

    y_prompt = xp.reshape(BATCH, SEQ, D_MODEL)
    y_sample = xs.reshape(DEC_BATCH, DEC_SEQ, D_MODEL)
    ak, av, bk, bv, ck, cv = new_caches
    return (y_prompt, y_sample,
            ak.reshape(BATCH, DEPTH, SEQ, 2 * HA, DA), av.reshape(BATCH, DEPTH, SEQ, HA, 2 * DA),
            bk.reshape(BATCH, DEPTH, SEQ, KVB, DB), bv.reshape(BATCH, DEPTH, SEQ, KVB, DB),
            ck.reshape(BATCH, DEPTH, SEQ, HC, DC), cv.reshape(BATCH, DEPTH, SEQ, HC, DC))
```

```python
import functools
import math

import jax
import jax.numpy as jnp
from jax import lax
from jax.experimental import pallas as pl
from jax.experimental.pallas import tpu as pltpu

D_MODEL = 1024
BATCH = 16
SEQ = 256
DEPTH = 4
DEC_BATCH = 4
DEC_SEQ = 1024
PAST_LEN = 256
GRID_W = 64
HA, DA = 4, 64
HB, KVB, DB = 8, 2, 64
HC, DC = 8, 64
NA_ROWS, NA_COLS = 8, 16
D_FF = 4 * D_MODEL
ROPE_BASE = 10000.0
EPS = 1e-6
NEG = -1e30
HEAD_DIM = 64
LOG2E = math.log2(math.e)
Q_SCALE = HEAD_DIM ** -0.5 * LOG2E

F32 = jnp.float32
BF16 = jnp.bfloat16

LANES = 128
N_QKV_GROUPS = 30
N_GATE_GROUPS = 24
COL_TILE = 768
GROUPS_PER_TILE = COL_TILE // LANES
N_QKV_TILES = N_QKV_GROUPS // GROUPS_PER_TILE
N_COL_TILES = N_QKV_TILES + N_GATE_GROUPS // GROUPS_PER_TILE
ROW_TILE = 1024
ROW_CHUNK = 256
N_TOK = BATCH * SEQ
Q_GROUPS = 4
ROPE_HALF = HEAD_DIM // 4
HEAD_SHIFT = HEAD_DIM.bit_length() - 1
MOD_TILE = 3072
VMEM_LIMIT_V7X = 58 * 1024 * 1024

G_QA, G_KA, G_VA, G_QB, G_KB, G_VB, G_QC, G_KC, G_VC = 0, 4, 8, 12, 16, 17, 18, 22, 26
CACHE_WIDTHS = (2 * HA * DA, HA * 2 * DA, KVB * DB, KVB * DB, HC * DC, HC * DC)

_NT = (((1,), (1,)), ((), ()))


def _params(sem, vmem=VMEM_LIMIT_V7X):
    return pltpu.CompilerParams(dimension_semantics=sem, vmem_limit_bytes=vmem)


def _lane_head(shape):
    return lax.shift_right_logical(lax.broadcasted_iota(jnp.int32, shape, len(shape) - 1), HEAD_SHIFT)


def _mod_kernel(c_ref, w_ref, b_ref, o_ref):
    c = c_ref[...]
    s = (c * jax.nn.sigmoid(c)).astype(BF16)
    o_ref[...] = jnp.dot(s, w_ref[...].astype(BF16), preferred_element_type=F32) + b_ref[...]


def _modulation(cvec, w_mod, b_mod):
    tn = MOD_TILE
    n6 = 6 * D_MODEL
    return pl.pallas_call(
        _mod_kernel,
        grid=(DEPTH, n6 // tn),
        in_specs=[pl.BlockSpec((8, D_MODEL), lambda l, n: (0, 0)),
                  pl.BlockSpec((None, D_MODEL, tn), lambda l, n: (l, 0, n)),
                  pl.BlockSpec((None, 1, tn), lambda l, n: (l, 0, n))],
        out_specs=pl.BlockSpec((None, 8, tn), lambda l, n: (l, 0, n)),
        out_shape=jax.ShapeDtypeStruct((DEPTH, 8, n6), F32),
        compiler_params=_params(("arbitrary", "arbitrary")),
        name="modulation",
    )(cvec, w_mod, b_mod.reshape(DEPTH, 1, n6))


def _rope(v, cos, sin):
    first = (lax.broadcasted_iota(jnp.int32, v.shape, 1) & ROPE_HALF) == 0
    partner = jnp.where(first, pltpu.roll(v, LANES - ROPE_HALF, 1), pltpu.roll(v, ROPE_HALF, 1))
    return v * cos + partner * sin


def _head_rmsnorm(v, bd, g):
    msq = jnp.dot((v * v).astype(BF16), bd, preferred_element_type=F32)
    return v * lax.rsqrt(msq + EPS) * g


def _make_inproj_kernel(latent, n_aliased):
    def kern(*refs):
        if latent:
            (x_ref, mod_ref, g1_ref, w_ref, bg_ref, gq_ref, gk_ref, bd_ref, cos_ref, sin_ref,
             z_ref, gate_ref, h_scr, wbf_scr) = refs
        else:
            (x_ref, mod_ref, g1_ref, w_ref, bg_ref, gq_ref, gk_ref, bd_ref) = refs[:8]
            (z_ref, gate_ref, ak_ref, av_ref, bk_ref, bv_ref, ck_ref, cv_ref,
             h_scr, wbf_scr) = refs[8 + n_aliased:]
        j = pl.program_id(0)
        i = pl.program_id(1)

        @pl.when(i == 0)
        def _():
            wbf_scr[...] = w_ref[...].astype(BF16)

        def chunks(first=False):
            for rc in range(ROW_TILE // ROW_CHUNK):
                rows = pl.ds(rc * ROW_CHUNK, ROW_CHUNK)
                tok = pl.ds(pl.multiple_of(i * ROW_TILE + rc * ROW_CHUNK, ROW_CHUNK), ROW_CHUNK)
                if first:
                    x = x_ref[rows, :]
                    ms = jnp.mean(x * x, axis=-1, keepdims=True)
                    y = x * lax.rsqrt(ms + EPS) * g1_ref[...]
                    m = mod_ref[0]
                    h = (y * (1.0 + m[:, D_MODEL:2 * D_MODEL]) + m[:, 0:D_MODEL]).astype(BF16)
                    h_scr[tok, :] = h
                else:
                    h = h_scr[tok, :]
                yield rc, rows, jnp.dot(h, wbf_scr[...], preferred_element_type=F32)

        def grp(a, c, n=1):
            return a[:, c * LANES:(c + n) * LANES]

        def rot(v, rows):
            return _rope(v, cos_ref[rows, :], sin_ref[rows, :]) if latent else v

        @pl.when(j == 0)
        def _():
            for rc, rows, a in chunks(first=True):
                for c in range(GROUPS_PER_TILE):
                    v = rot(grp(a, c), rows)
                    z_ref[c, rows, :] = (v * Q_SCALE if c < Q_GROUPS else v).astype(BF16)
                if not latent:
                    ak_ref[rc] = grp(a, 4, 2)

        @pl.when(j == 1)
        def _():
            for rc, rows, a in chunks():
                for c in range(2):
                    z_ref[c, rows, :] = rot(grp(a, c), rows).astype(BF16)
                for c in range(2, GROUPS_PER_TILE):
                    z_ref[c, rows, :] = grp(a, c).astype(BF16)
                if not latent:
                    ak_ref[rc] = grp(a, 0, 2)
                    av_ref[rc] = grp(a, 2, 4).reshape(ROW_CHUNK, HA, 2 * DA)

        @pl.when(j == 2)
        def _():
            bd = bd_ref[...]
            for rc, rows, a in chunks():
                for half in range(2):
                    qn = _head_rmsnorm(grp(a, 2 * half, 2), bd, gq_ref[...])
                    for c in range(2):
                        z_ref[2 * half + c, rows, :] = (rot(grp(qn, c), rows) * Q_SCALE).astype(BF16)
                kn = _head_rmsnorm(grp(a, 4), bd[0:LANES, 0:LANES], gk_ref[...])
                z_ref[4, rows, :] = rot(kn, rows).astype(BF16)
                z_ref[5, rows, :] = grp(a, 5).astype(BF16)
                if not latent:
                    bk_ref[rc] = kn
                    bv_ref[rc] = grp(a, 5)

        @pl.when(j == 3)
        def _():
            for rc, rows, a in chunks():
                for c in range(GROUPS_PER_TILE):
                    v = grp(a, c)
                    z_ref[c, rows, :] = (v * Q_SCALE if c < Q_GROUPS else v).astype(BF16)
                if not latent:
                    ck_ref[rc] = grp(a, 4, 2)

        @pl.when(j == 4)
        def _():
            for rc, rows, a in chunks():
                for c in range(GROUPS_PER_TILE):
                    z_ref[c, rows, :] = grp(a, c).astype(BF16)
                if not latent:
                    ck_ref[rc] = grp(a, 0, 2)
                    cv_ref[rc] = grp(a, 2, 4)

        @pl.when(j >= N_QKV_TILES)
        def _():
            for rc, rows, a in chunks():
                a = a + bg_ref[...]
                for c in range(GROUPS_PER_TILE):
                    gate_ref[c, rows, :] = jax.nn.sigmoid(grp(a, c)).astype(BF16)

    return kern


def _inproj(latent, l, x, mod, g_norm1, w_in, b_gate, gq, gk, bd, rope_cos, rope_sin, caches):
    n_row = N_TOK // ROW_TILE
    last = n_row - 1

    def row_block(j, i, first_tile, last_tile):
        return jnp.where(j < first_tile, 0, jnp.where(j <= last_tile, i, last))

    mod_idx = (lambda j, i: (1 + i, 0, 0)) if latent else (lambda j, i: (0, 0, 0))
    in_specs = [
        pl.BlockSpec((ROW_TILE, D_MODEL), lambda j, i: (row_block(j, i, 0, 0), 0)),
        pl.BlockSpec((1, 1, 6 * D_MODEL), mod_idx),
        pl.BlockSpec((None, 1, D_MODEL), lambda j, i: (l, 0, 0)),
        pl.BlockSpec((None, D_MODEL, COL_TILE), lambda j, i: (l, 0, j)),
        pl.BlockSpec((None, 1, COL_TILE), lambda j, i: (l, 0, jnp.maximum(j - N_QKV_TILES, 0))),
        pl.BlockSpec((None, 1, 2 * LANES), lambda j, i: (l, 0, 0)),
        pl.BlockSpec((None, 1, LANES), lambda j, i: (l, 0, 0)),
        pl.BlockSpec((2 * LANES, 2 * LANES), lambda j, i: (0, 0)),
    ]
    args = [x, mod, g_norm1, w_in, b_gate, gq, gk, bd]
    out_specs = [
        pl.BlockSpec((GROUPS_PER_TILE, ROW_TILE, LANES),
                     lambda j, i: (jnp.minimum(j, N_QKV_TILES - 1), row_block(j, i, 0, N_QKV_TILES - 1), 0)),
        pl.BlockSpec((GROUPS_PER_TILE, ROW_TILE, LANES),
                     lambda j, i: (jnp.maximum(j - N_QKV_TILES, 0), row_block(j, i, N_QKV_TILES, N_COL_TILES), 0)),
    ]
    out_shape = [jax.ShapeDtypeStruct((N_QKV_GROUPS, N_TOK, LANES), BF16),
                 jax.ShapeDtypeStruct((N_GATE_GROUPS, N_TOK, LANES), BF16)]
    aliases = {}
    if latent:
        in_specs += [pl.BlockSpec((DEC_SEQ, LANES), lambda j, i: (0, 0))] * 2
        args += [rope_cos, rope_sin]
    else:
        nb = ROW_TILE // SEQ
        tiles = ((0, 1, 256), (1, 1, 512), (2, 2, 128), (2, 2, 128), (3, 4, 256), (4, 4, 512))
        for k, (w, (t0, t1, bw)) in enumerate(zip(CACHE_WIDTHS, tiles)):
            if caches is not None:
                in_specs.append(pl.BlockSpec(memory_space=pl.ANY))
                args.append(caches[k])
                aliases[8 + k] = 2 + k

            def cache_idx(j, i, t0=t0, t1=t1):
                return (row_block(j, i, t0, t1), l, 0, jnp.where(j <= t0, 0, (t1 - t0)))

            if k == 1:
                out_specs.append(pl.BlockSpec((nb, None, SEQ, HA, 2 * DA),
                                              lambda j, i, f=cache_idx: f(j, i) + (0,)))
                out_shape.append(jax.ShapeDtypeStruct((BATCH, DEPTH, SEQ, HA, 2 * DA), F32))
                continue
            out_specs.append(pl.BlockSpec((nb, None, SEQ, bw), cache_idx))
            out_shape.append(jax.ShapeDtypeStruct((BATCH, DEPTH, SEQ, w), F32))
    return pl.pallas_call(
        _make_inproj_kernel(latent, len(aliases)),
        grid=(N_COL_TILES, n_row),
        in_specs=in_specs,
        out_specs=out_specs,
        out_shape=out_shape,
        scratch_shapes=[pltpu.VMEM((N_TOK, D_MODEL), BF16), pltpu.VMEM((D_MODEL, COL_TILE), BF16)],
        input_output_aliases=aliases,
        compiler_params=_params(("arbitrary", "arbitrary")),
        name="inproj_lat" if latent else "inproj_ctx",
    )(*args)


def _mask_head(q, head):
    qf = q.astype(F32)
    keep = _lane_head(qf.shape) == head
    return jnp.where(keep, qf, 0.0).astype(BF16)


def _stack_heads(q):
    return jnp.concatenate([_mask_head(q, 0), _mask_head(q, 1)], axis=0)


def _unstack_heads(o, rows):
    return jnp.where(_lane_head((rows, LANES)) == 1, o[rows:2 * rows], o[0:rows])


def _dup_head(kv, head):
    f = kv.astype(F32)
    r = pltpu.roll(f, HEAD_DIM, 1)
    return jnp.where(_lane_head(f.shape) == head, f, r).astype(BF16)


def _attend(qm, ks, vs):
    nt = (False,) * len(ks)
    return _attend_multi([(qm, ks, vs, None, nt, nt)])[0]


def _attend_multi(jobs):
    sss = []
    for qm, ks, _, biases, k_t, _ in jobs:
        ss = [jnp.dot(qm, k, preferred_element_type=F32) if t else
              lax.dot_general(qm, k, _NT, preferred_element_type=F32) for k, t in zip(ks, k_t)]
        if biases is not None:
            ss = [s if b is None else s + b for s, b in zip(ss, biases)]
        sss.append(ss)
    pss, dens = [], []
    for ss in sss:
        m = functools.reduce(jnp.maximum, [jnp.max(s, axis=-1, keepdims=True) for s in ss])
        ps = [jnp.exp2(s - m) for s in ss]
        dens.append(functools.reduce(jnp.add, [jnp.sum(p, axis=-1, keepdims=True) for p in ps]))
        pss.append(ps)
    outs = []
    for (_, _, vs, _, _, v_t), ps, den in zip(jobs, pss, dens):
        o = functools.reduce(jnp.add, [lax.dot_general(p.astype(BF16), v, _NT, preferred_element_type=F32) if t else
                                        jnp.dot(p.astype(BF16), v, preferred_element_type=F32)
                                        for p, v, t in zip(ps, vs, v_t)])
        outs.append(o * (1.0 / den))
    return outs


_CTX_T = (False, True)


def _pair_t(ref):
    return ref[...].reshape(2 * HEAD_DIM, ref.shape[-1]).astype(BF16)


def _diff_lambda(lam_ref, lam_init):
    lq = lam_ref[...]
    a = jnp.sum(lq[0:1] * lq[1:2], axis=-1, keepdims=True)
    b = jnp.sum(lq[2:3] * lq[3:4], axis=-1, keepdims=True)
    return jnp.exp(a) - jnp.exp(b) + lam_init


def _diff_combine(o1, o2, lam, gsub, lam_init):
    o = o1 - lam * o2
    ms = jnp.mean(o * o, axis=-1, keepdims=True)
    return (o * lax.rsqrt(ms + EPS) * gsub) * (1.0 - lam_init)


CTX_BATCH_PER_STEP = 2


def _make_attn_ctx_kernel(lam_init):
    def kern(z_ref, lam_ref, gsub_ref, oa_ref, ob_ref, oc_ref):
        lam = _diff_lambda(lam_ref, lam_init)
        gsub = gsub_ref[...]
        for bb in range(CTX_BATCH_PER_STEP):
            rows = pl.ds(bb * SEQ, SEQ)

            def z(g):
                return z_ref[g, rows, :]

            for vh in range(HA):
                hi = vh % 2
                v = z(G_VA + vh)
                o1 = _attend(_mask_head(z(G_QA + vh // 2), hi), [z(G_KA + vh // 2)], [v])
                o2 = _attend(_mask_head(z(G_QA + 2 + vh // 2), hi), [z(G_KA + 2 + vh // 2)], [v])
                oa_ref[vh, rows, :] = _diff_combine(o1, o2, lam, gsub, lam_init).astype(BF16)
            for g in range(KVB):
                kd = _dup_head(z(G_KB), g)
                vd = _dup_head(z(G_VB), g)
                for c in range(2 * g, 2 * g + 2):
                    o = _attend(_stack_heads(z(G_QB + c)), [kd], [vd])
                    ob_ref[c, rows, :] = _unstack_heads(o, SEQ).astype(BF16)
            for c in range(HC // 2):
                o = _attend(_stack_heads(z(G_QC + c)), [z(G_KC + c)], [z(G_VC + c)])
                oc_ref[c, rows, :] = _unstack_heads(o, SEQ).astype(BF16)
    return kern


def _attn_ctx(l, z, lambda_qk, g_subln):
    lam_init = 0.8 - 0.6 * math.exp(-0.3 * l)
    rows = CTX_BATCH_PER_STEP * SEQ
    o_spec = pl.BlockSpec((4, rows, LANES), lambda b: (0, b, 0))
    o_shape = jax.ShapeDtypeStruct((4, N_TOK, LANES), BF16)
    return pl.pallas_call(
        _make_attn_ctx_kernel(lam_init),
        grid=(BATCH // CTX_BATCH_PER_STEP,),
        in_specs=[pl.BlockSpec((N_QKV_GROUPS, rows, LANES), lambda b: (0, b, 0)),
                  pl.BlockSpec((None, 4, DA), lambda b: (l, 0, 0)),
                  pl.BlockSpec((None, 1, 2 * DA), lambda b: (l, 0, 0))],
        out_specs=[o_spec, o_spec, o_spec],
        out_shape=[o_shape, o_shape, o_shape],
        compiler_params=_params(("arbitrary",)),
        name="attn_ctx",
    )(z, lambda_qk, g_subln)


Q_BLK_GQA = 512
GQA_PAIRS = HB // KVB // 2
Q_BLK_DIFF = 1024


def _make_attn_lat_a_kernel(lam_init):
    def kern(q1_ref, q2_ref, k1_ref, k2_ref, v_ref, ck1_ref, ck2_ref, cv_ref, lam_ref, gsub_ref, o_ref):
        vh = pl.program_id(1)
        hi = jnp.bitwise_and(vh, 1)
        lam = _diff_lambda(lam_ref, lam_init)
        gsub = gsub_ref[...]
        ck1 = _pair_t(ck1_ref)
        ck2 = _pair_t(ck2_ref)
        cv = cv_ref[:, vh, :].astype(BF16)
        k1, k2, v = k1_ref[0], k2_ref[0], v_ref[0]
        rows = [pl.ds(qb * Q_BLK_DIFF, Q_BLK_DIFF) for qb in range(DEC_SEQ // Q_BLK_DIFF)]
        jobs = []
        for r in rows:
            jobs.append((_mask_head(q1_ref[0, r, :], hi), [k1, ck1], [v, cv], None, _CTX_T, (False, False)))
            jobs.append((_mask_head(q2_ref[0, r, :], hi), [k2, ck2], [v, cv], None, _CTX_T, (False, False)))
        outs = _attend_multi(jobs)
        for t, r in enumerate(rows):
            o_ref[0, r, :] = _diff_combine(outs[2 * t], outs[2 * t + 1], lam, gsub, lam_init).astype(BF16)
    return kern


def _attn_lat_a(l, z, cache_k_t, cache_v, lambda_qk, g_subln):
    lam_init = 0.8 - 0.6 * math.exp(-0.3 * l)

    def zspec(fn):
        return pl.BlockSpec((1, DEC_SEQ, LANES), lambda b, h: (fn(h), b, 0))

    return pl.pallas_call(
        _make_attn_lat_a_kernel(lam_init),
        grid=(DEC_BATCH, HA),
        in_specs=[zspec(lambda h: G_QA + h // 2), zspec(lambda h: G_QA + 2 + h // 2),
                  zspec(lambda h: G_KA + h // 2), zspec(lambda h: G_KA + 2 + h // 2),
                  zspec(lambda h: G_VA + h),
                  pl.BlockSpec((None, None, 2, DA, PAST_LEN), lambda b, h: (b, l, h // 2, 0, 0)),
                  pl.BlockSpec((None, None, 2, DA, PAST_LEN), lambda b, h: (b, l, HA // 2 + h // 2, 0, 0)),
                  pl.BlockSpec((None, None, PAST_LEN, HA, 2 * DA), lambda b, h: (b, l, 0, 0, 0)),
                  pl.BlockSpec((None, 4, DA), lambda b, h: (l, 0, 0)),
                  pl.BlockSpec((None, 1, 2 * DA), lambda b, h: (l, 0, 0))],
        out_specs=pl.BlockSpec((1, DEC_SEQ, LANES), lambda b, h: (h, b, 0)),
        out_shape=jax.ShapeDtypeStruct((4, N_TOK, LANES), BF16),
        compiler_params=_params(("arbitrary", "arbitrary")),
        name="attn_lat_a",
    )(z, z, z, z, z, cache_k_t, cache_k_t, cache_v, lambda_qk, g_subln)


def _attn_lat_b_kernel(q_ref, k_ref, v_ref, ck_ref, cv_ref, o_ref):
    hi = pl.program_id(1)
    kd = _dup_head(k_ref[0], hi)
    vd = _dup_head(v_ref[0], hi)
    ck = ck_ref[hi].astype(BF16)
    cv = cv_ref[hi].astype(BF16)
    ckd = jnp.concatenate([ck, ck], axis=0)
    cvd = jnp.concatenate([cv, cv], axis=0)
    rows = [pl.ds(qb * Q_BLK_GQA, Q_BLK_GQA) for qb in range(DEC_SEQ // Q_BLK_GQA)]
    work = [(c, r) for c in range(GQA_PAIRS) for r in rows]
    outs = _attend_multi([(_stack_heads(q_ref[c, r, :]), [kd, ckd], [vd, cvd], None, _CTX_T, _CTX_T)
                          for c, r in work])
    for (c, r), o in zip(work, outs):
        o_ref[c, r, :] = _unstack_heads(o, Q_BLK_GQA).astype(BF16)


def _attn_lat_b(l, z, cache_k, cache_v):
    cspec = pl.BlockSpec((None, None, KVB, DB, PAST_LEN), lambda b, g: (b, l, 0, 0, 0))
    return pl.pallas_call(
        _attn_lat_b_kernel,
        grid=(DEC_BATCH, KVB),
        in_specs=[pl.BlockSpec((GQA_PAIRS, DEC_SEQ, LANES), lambda b, g: (G_QB // GQA_PAIRS + g, b, 0)),
                  pl.BlockSpec((1, DEC_SEQ, LANES), lambda b, g: (G_KB, b, 0)),
                  pl.BlockSpec((1, DEC_SEQ, LANES), lambda b, g: (G_VB, b, 0)),
                  cspec, cspec],
        out_specs=pl.BlockSpec((GQA_PAIRS, DEC_SEQ, LANES), lambda b, g: (g, b, 0)),
        out_shape=jax.ShapeDtypeStruct((4, N_TOK, LANES), BF16),
        compiler_params=_params(("arbitrary", "arbitrary")),
        name="attn_lat_b",
    )(z, z, z, cache_k, cache_v)


N_GRID_ROWS = DEC_SEQ // GRID_W
NA_GROUP = 2
BIAS_TABLE_ROWS = 16


def _window_start(r):
    return min(max(r - NA_ROWS // 2, 0), N_GRID_ROWS - NA_ROWS)


def _group_keys(g):
    starts = [_window_start(r) for r in range(g * NA_GROUP, (g + 1) * NA_GROUP)]
    n = max(starts) + NA_ROWS - min(starts)
    n += n % 2
    return min(min(starts), N_GRID_ROWS - n), n


NA_MAX_KEY_ROWS = max(_group_keys(g)[1] for g in range(N_GRID_ROWS // NA_GROUP))


def _build_window_bias(tab_ref, bias_scr):
    qcol = lax.broadcasted_iota(jnp.int32, (GRID_W, LANES), 0)
    lane = lax.broadcasted_iota(jnp.int32, (GRID_W, LANES), 1)
    kcol = lane & (GRID_W - 1)
    c0 = jnp.clip(qcol - NA_COLS // 2, 0, GRID_W - NA_COLS)
    in_win = (kcol >= c0) & (kcol < c0 + NA_COLS)
    masks = {(True, True): in_win, (True, False): in_win & (lane < GRID_W), (False, True): in_win & (lane >= GRID_W)}
    for head in range(2):
        tiles = {}

        def tile(drs):
            if drs not in tiles:
                if drs == (None, None):
                    tiles[drs] = jnp.full((GRID_W, LANES), NEG, F32)
                else:
                    vec = None
                    for half, dr in enumerate(drs):
                        if dr is not None:
                            part = tab_ref[head, pl.ds(half * BIAS_TABLE_ROWS + dr, 1), :]
                            vec = part if vec is None else vec + part
                    toep = pltpu.roll(jnp.broadcast_to(vec, (GRID_W, LANES)), 0, 1, stride=1, stride_axis=0)
                    valid = tuple(dr is not None for dr in drs)
                    tiles[drs] = jnp.where(masks[valid], toep * LOG2E, NEG)
            return tiles[drs]

        for r in range(N_GRID_ROWS):
            u0, n = _group_keys(r // NA_GROUP)
            rows = pl.ds(r * GRID_W, GRID_W)
            for m in range(n // 2):
                drs = tuple(kr - r + NA_ROWS - 1 if _window_start(r) <= kr < _window_start(r) + NA_ROWS else None
                            for kr in (u0 + 2 * m, u0 + 2 * m + 1))
                bias_scr[head, rows, pl.ds(m * LANES, LANES)] = tile(drs)


def _attn_lat_c_kernel(q_ref, k_ref, v_ref, ck_ref, cv_ref, tab_ref, o_ref, bias_scr):
    @pl.when(pl.program_id(1) == 0)
    def _():
        _build_window_bias(tab_ref, bias_scr)

    ck = _pair_t(ck_ref)
    cv = _pair_t(cv_ref)
    n_rows = NA_GROUP * GRID_W
    jobs, row_slices = [], []
    for g in range(N_GRID_ROWS // NA_GROUP):
        u0, n = _group_keys(g)
        rows = pl.ds(g * n_rows, n_rows)
        keys = pl.ds(u0 * GRID_W, n * GRID_W)
        bias = jnp.concatenate([bias_scr[0, rows, 0:n * GRID_W], bias_scr[1, rows, 0:n * GRID_W]], axis=0)
        jobs.append((_stack_heads(q_ref[0, rows, :]), [k_ref[0, keys, :], ck], [v_ref[0, keys, :], cv],
                     [bias, None], _CTX_T, _CTX_T))
        row_slices.append(rows)
    for rows, o in zip(row_slices, _attend_multi(jobs)):
        o_ref[0, rows, :] = _unstack_heads(o, n_rows).astype(BF16)


def _attn_lat_c(l, z, cache_k, cache_v, bias_rows):
    def zspec(g0):
        return pl.BlockSpec((1, DEC_SEQ, LANES), lambda c, b: (g0 + c, b, 0))

    cspec = pl.BlockSpec((None, None, 2, DC, PAST_LEN), lambda c, b: (b, l, c, 0, 0))
    return pl.pallas_call(
        _attn_lat_c_kernel,
        grid=(HC // 2, DEC_BATCH),
        in_specs=[zspec(G_QC), zspec(G_KC), zspec(G_VC), cspec, cspec,
                  pl.BlockSpec((None, 2, 2 * BIAS_TABLE_ROWS, LANES), lambda c, b: (l, c, 0, 0))],
        out_specs=pl.BlockSpec((1, DEC_SEQ, LANES), lambda c, b: (c, b, 0)),
        out_shape=jax.ShapeDtypeStruct((4, N_TOK, LANES), BF16),
        scratch_shapes=[pltpu.VMEM((2, DEC_SEQ, NA_MAX_KEY_ROWS * GRID_W), F32)],
        compiler_params=_params(("arbitrary", "arbitrary")),
        name="attn_lat_c",
    )(z, z, z, cache_k, cache_v, bias_rows)


MERGE_TILE = 512
BRANCH_GROUPS = 4
MODEL_GROUPS = D_MODEL // LANES


def _merge_kernel(x_ref, oa_ref, ob_ref, oc_ref, gate_ref, mod_ref, wa_ref, wb_ref, wc_ref, wo_ref,
                  xo_ref, wbr_scr, wo_scr):
    @pl.when(pl.program_id(0) == 0)
    def _():
        wbr_scr[0] = wa_ref[...].astype(BF16)
        wbr_scr[1] = wb_ref[...].astype(BF16)
        wbr_scr[2] = wc_ref[...].astype(BF16)
        wo_scr[...] = wo_ref[...].astype(BF16)

    m = mod_ref[0]
    for rc in range(MERGE_TILE // ROW_CHUNK):
        rows = pl.ds(rc * ROW_CHUNK, ROW_CHUNK)
        y = None
        for k, o_ref in enumerate((oa_ref, ob_ref, oc_ref)):
            o = jnp.concatenate([o_ref[c, rows, :] for c in range(BRANCH_GROUPS)], axis=-1)
            p = jnp.dot(o, wbr_scr[k], preferred_element_type=F32)
            g = jnp.concatenate([gate_ref[MODEL_GROUPS * k + c, rows, :] for c in range(MODEL_GROUPS)],
                                axis=-1).astype(F32)
            y = g * p if y is None else y + g * p
        out = jnp.dot(y.astype(BF16), wo_scr[...], preferred_element_type=F32)
        xo_ref[rows, :] = x_ref[rows, :] + m[:, 2 * D_MODEL:3 * D_MODEL] * out


def _merge(latent, l, x, oa, ob, oc, gates, mod, w_a, w_b, w_c, w_o):
    tm = MERGE_TILE
    if latent:
        mod_idx = lambda i: (1 + (i * tm) // DEC_SEQ, 0, 0)
    else:
        mod_idx = lambda i: (0, 0, 0)
    o_spec = pl.BlockSpec((4, tm, LANES), lambda i: (0, i, 0))
    wbr_spec = pl.BlockSpec((None, 4 * LANES, D_MODEL), lambda i: (l, 0, 0))
    return pl.pallas_call(
        _merge_kernel,
        grid=(N_TOK // tm,),
        in_specs=[pl.BlockSpec((tm, D_MODEL), lambda i: (i, 0)),
                  o_spec, o_spec, o_spec,
                  pl.BlockSpec((N_GATE_GROUPS, tm, LANES), lambda i: (0, i, 0)),
                  pl.BlockSpec((1, 1, 6 * D_MODEL), mod_idx),
                  wbr_spec, wbr_spec, wbr_spec,
                  pl.BlockSpec((None, D_MODEL, D_MODEL), lambda i: (l, 0, 0))],
        out_specs=pl.BlockSpec((tm, D_MODEL), lambda i: (i, 0)),
        out_shape=jax.ShapeDtypeStruct((N_TOK, D_MODEL), F32),
        scratch_shapes=[pltpu.VMEM((3, 4 * LANES, D_MODEL), BF16), pltpu.VMEM((D_MODEL, D_MODEL), BF16)],
        compiler_params=_params(("arbitrary",)),
        name="merge_lat" if latent else "merge_ctx",
    )(x, oa, ob, oc, gates, mod, w_a, w_b, w_c, w_o)


FF_TILE = 1024
FF_CHUNK = 1024


def _ffn_kernel(x_ref, mod_ref, g2_ref, w1_ref, w2_ref, gf_ref, xo_ref, acc_scr, h2_scr, *, final):
    f = pl.program_id(1)

    def run(first):
        w1 = w1_ref[...].astype(BF16)
        w2 = w2_ref[...].astype(BF16)
        m = mod_ref[0]
        for rc in range(ROW_TILE // FF_CHUNK):
            rows = pl.ds(rc * FF_CHUNK, FF_CHUNK)
            if first:
                x = x_ref[rows, :]
                ms = jnp.mean(x * x, axis=-1, keepdims=True)
                y = x * lax.rsqrt(ms + EPS) * g2_ref[...]
                h2 = (y * (1.0 + m[:, 4 * D_MODEL:5 * D_MODEL]) + m[:, 3 * D_MODEL:4 * D_MODEL]).astype(BF16)
                h2_scr[rows, :] = h2
            else:
                h2 = h2_scr[rows, :]
            u = jnp.dot(h2, w1, preferred_element_type=F32)
            u = jnp.square(jnp.maximum(u, 0.0)).astype(BF16)
            d = jnp.dot(u, w2, preferred_element_type=F32)
            if first:
                acc_scr[rows, :] = d
            else:
                acc_scr[rows, :] += d

    @pl.when(f == 0)
    def _():
        run(True)

    @pl.when(f > 0)
    def _():
        run(False)

    @pl.when(f == D_FF // FF_TILE - 1)
    def _():
        x = x_ref[...] + mod_ref[0][:, 5 * D_MODEL:6 * D_MODEL] * acc_scr[...]
        if final:
            ms = jnp.mean(x * x, axis=-1, keepdims=True)
            x = x * lax.rsqrt(ms + EPS) * gf_ref[...]
        xo_ref[...] = x


def _ffn(latent, l, x, mod, g_norm2, w1, w2, g_final):
    tm = ROW_TILE
    mod_idx = (lambda i, f: (1 + i, 0, 0)) if latent else (lambda i, f: (0, 0, 0))
    return pl.pallas_call(
        functools.partial(_ffn_kernel, final=(l == DEPTH - 1)),
        grid=(N_TOK // tm, D_FF // FF_TILE),
        in_specs=[pl.BlockSpec((tm, D_MODEL), lambda i, f: (i, 0)),
                  pl.BlockSpec((1, 1, 6 * D_MODEL), mod_idx),
                  pl.BlockSpec((None, 1, D_MODEL), lambda i, f: (l, 0, 0)),
                  pl.BlockSpec((None, D_MODEL, FF_TILE), lambda i, f: (l, 0, f)),
                  pl.BlockSpec((None, FF_TILE, D_MODEL), lambda i, f: (l, f, 0)),
                  pl.BlockSpec((1, D_MODEL), lambda i, f: (0, 0))],
        out_specs=pl.BlockSpec((tm, D_MODEL), lambda i, f: (i, 0)),
        out_shape=jax.ShapeDtypeStruct((N_TOK, D_MODEL), F32),
        scratch_shapes=[pltpu.VMEM((tm, D_MODEL), F32), pltpu.VMEM((tm, D_MODEL), BF16)],
        compiler_params=_params(("arbitrary", "arbitrary")),
        name="ffn_lat" if latent else "ffn_ctx",
    )(x, mod, g_norm2, w1, w2, g_final)


def _rope_tables():
    nf = ROPE_HALF
    t = jnp.arange(DEC_SEQ)
    row = (t // GRID_W).astype(F32)
    col = (t % GRID_W).astype(F32)
    inv = ROPE_BASE ** (-jnp.arange(nf, dtype=F32) / nf)
    ar = row[:, None] * inv[None, :]
    ac = col[:, None] * inv[None, :]
    cos = jnp.concatenate([jnp.cos(ar), jnp.cos(ar), jnp.cos(ac), jnp.cos(ac)], axis=-1)
    sin = jnp.concatenate([-jnp.sin(ar), jnp.sin(ar), -jnp.sin(ac), jnp.sin(ac)], axis=-1)
    return jnp.tile(cos, (1, 2)), jnp.tile(sin, (1, 2))


def _packed_bias_rows(rel_bias):
    n = 2 * NA_ROWS - 1
    first = jnp.concatenate([rel_bias[..., NA_COLS - 1:], jnp.zeros((DEPTH, HC, n, LANES - (2 * NA_COLS - 1)), F32),
                             rel_bias[..., :NA_COLS - 1]], axis=-1)
    lo = GRID_W - NA_COLS + 1
    second = jnp.pad(rel_bias, ((0, 0), (0, 0), (0, 0), (lo, LANES - lo - (2 * NA_COLS - 1))))
    pad_rows = ((0, 0), (0, 0), (0, BIAS_TABLE_ROWS - n), (0, 0))
    return jnp.concatenate([jnp.pad(first, pad_rows), jnp.pad(second, pad_rows)], axis=2)


def kernel(x_prompt, x_sample, c, cache_a_k, cache_a_v, cache_b_k, cache_b_v, cache_c_k, cache_c_v, c_ctx, w_mod, b_mod, g_norm1, g_norm2, w_in, b_gate, lambda_qk, g_subln, g_qnorm, g_knorm, rel_bias, w_branch_a, w_branch_b, w_branch_c, w_out, w_ff1, w_ff2, g_final):
    xp = x_prompt.reshape(N_TOK, D_MODEL)
    xs = x_sample.reshape(N_TOK, D_MODEL)

    cvec = jnp.concatenate([c_ctx[None, :], c, jnp.zeros((3, D_MODEL), F32)], axis=0)
    mods = _modulation(cvec, w_mod, b_mod)

    rope_cos, rope_sin = _rope_tables()
    bias_rows = _packed_bias_rows(rel_bias)
    bd = jnp.kron(jnp.eye(2 * LANES // HEAD_DIM, dtype=F32),
                  jnp.full((HEAD_DIM, HEAD_DIM), 1.0 / HEAD_DIM, F32)).astype(BF16)
    gq = jnp.tile(g_qnorm, (1, 2 * LANES // DB)).reshape(DEPTH, 1, 2 * LANES)
    gk = jnp.tile(g_knorm, (1, LANES // DB)).reshape(DEPTH, 1, LANES)
    g1 = g_norm1.reshape(DEPTH, 1, D_MODEL)
    g2 = g_norm2.reshape(DEPTH, 1, D_MODEL)
    bg = b_gate.reshape(DEPTH, 1, 3 * D_MODEL)
    gsub = g_subln.reshape(DEPTH, 1, 2 * DA)
    gf = g_final.reshape(1, D_MODEL)

    def heads_t(cache):
        return cache.transpose(0, 1, 3, 4, 2)

    ctx_ak, ctx_bk, ctx_bv = heads_t(cache_a_k), heads_t(cache_b_k), heads_t(cache_b_v)
    ctx_ck, ctx_cv = heads_t(cache_c_k), heads_t(cache_c_v)

    new_caches = None

    for l in range(DEPTH):
        mod = mods[l].reshape(8, 1, 6 * D_MODEL)

        outs = _inproj(False, l, xp, mod, g1, w_in, bg, gq, gk, bd, None, None, new_caches)
        z, gates, new_caches = outs[0], outs[1], list(outs[2:])
        oa, ob, oc = _attn_ctx(l, z, lambda_qk, gsub)
        xp = _merge(False, l, xp, oa, ob, oc, gates, mod, w_branch_a, w_branch_b, w_branch_c, w_out)
        xp = _ffn(False, l, xp, mod, g2, w_ff1, w_ff2, gf)

        z, gates = _inproj(True, l, xs, mod, g1, w_in, bg, gq, gk, bd, rope_cos, rope_sin, None)
        oa = _attn_lat_a(l, z, ctx_ak, cache_a_v, lambda_qk, gsub)
        ob = _attn_lat_b(l, z, ctx_bk, ctx_bv)
        oc = _attn_lat_c(l, z, ctx_ck, ctx_cv, bias_rows)
        xs = _merge(True, l, xs, oa, ob, oc, gates, mod, w_branch_a, w_branch_b, w_branch_c, w_out)
        xs = _ffn(True, l, xs, mod, g2, w_ff1, w_ff2, gf)

    y_prompt = xp.reshape(BATCH, SEQ, D_MODEL)
    y_sample = xs.reshape(DEC_BATCH, DEC_SEQ, D_MODEL)
    ak, av, bk, bv, ck, cv = new_caches
    return (y_prompt, y_sample,
            ak.reshape(BATCH, DEPTH, SEQ, 2 * HA, DA), av.reshape(BATCH, DEPTH, SEQ, HA, 2 * DA),
            bk.reshape(BATCH, DEPTH, SEQ, KVB, DB), bv.reshape(BATCH, DEPTH, SEQ, KVB, DB),
            ck.reshape(BATCH, DEPTH, SEQ, HC, DC), cv.reshape(BATCH, DEPTH, SEQ, HC, DC))
```

```python
import functools
import math

import jax
import jax.numpy as jnp
from jax import lax
from jax.experimental import pallas as pl
from jax.experimental.pallas import tpu as pltpu

D_MODEL = 1024
BATCH = 16
SEQ = 256
DEPTH = 4
DEC_BATCH = 4
DEC_SEQ = 1024
PAST_LEN = 256
GRID_W = 64
HA, DA = 4, 64
HB, KVB, DB = 8, 2, 64
HC, DC = 8, 64
NA_ROWS, NA_COLS = 8, 16
D_FF = 4 * D_MODEL
ROPE_BASE = 10000.0
EPS = 1e-6
NEG = -1e30
HEAD_DIM = 64
LOG2E = math.log2(math.e)
Q_SCALE = HEAD_DIM ** -0.5 * LOG2E

F32 = jnp.float32
BF16 = jnp.bfloat16

LANES = 128
N_QKV_GROUPS = 30
N_GATE_GROUPS = 24
COL_TILE = 768
GROUPS_PER_TILE = COL_TILE // LANES
N_QKV_TILES = N_QKV_GROUPS // GROUPS_PER_TILE
N_COL_TILES = N_QKV_TILES + N_GATE_GROUPS // GROUPS_PER_TILE
ROW_TILE = 1024
ROW_CHUNK = 256
N_TOK = BATCH * SEQ
Q_GROUPS = 4
ROPE_HALF = HEAD_DIM // 4
HEAD_SHIFT = HEAD_DIM.bit_length() - 1
MOD_TILE = 3072
VMEM_LIMIT_V7X = 58 * 1024 * 1024

G_QA, G_KA, G_VA, G_QB, G_KB, G_VB, G_QC, G_KC, G_VC = 0, 4, 8, 12, 16, 17, 18, 22, 26
CACHE_WIDTHS = (2 * HA * DA, HA * 2 * DA, KVB * DB, KVB * DB, HC * DC, HC * DC)

_NT = (((1,), (1,)), ((), ()))


def _params(sem, vmem=VMEM_LIMIT_V7X):
    return pltpu.CompilerParams(dimension_semantics=sem, vmem_limit_bytes=vmem)


def _lane_head(shape):
    return lax.shift_right_logical(lax.broadcasted_iota(jnp.int32, shape, len(shape) - 1), HEAD_SHIFT)


def _mod_kernel(c_ref, w_ref, b_ref, o_ref):
    c = c_ref[...]
    s = (c * jax.nn.sigmoid(c)).astype(BF16)
    o_ref[...] = jnp.dot(s, w_ref[...].astype(BF16), preferred_element_type=F32) + b_ref[...]


def _modulation(cvec, w_mod, b_mod):
    tn = MOD_TILE
    n6 = 6 * D_MODEL
    return pl.pallas_call(
        _mod_kernel,
        grid=(DEPTH, n6 // tn),
        in_specs=[pl.BlockSpec((8, D_MODEL), lambda l, n: (0, 0)),
                  pl.BlockSpec((None, D_MODEL, tn), lambda l, n: (l, 0, n)),
                  pl.BlockSpec((None, 1, tn), lambda l, n: (l, 0, n))],
        out_specs=pl.BlockSpec((None, 8, tn), lambda l, n: (l, 0, n)),
        out_shape=jax.ShapeDtypeStruct((DEPTH, 8, n6), F32),
        compiler_params=_params(("arbitrary", "arbitrary")),
        name="modulation",
    )(cvec, w_mod, b_mod.reshape(DEPTH, 1, n6))


def _rope(v, cos, sin):
    first = (lax.broadcasted_iota(jnp.int32, v.shape, 1) & ROPE_HALF) == 0
    partner = jnp.where(first, pltpu.roll(v, LANES - ROPE_HALF, 1), pltpu.roll(v, ROPE_HALF, 1))
    return v * cos + partner * sin


def _head_rmsnorm(v, bd, g):
    msq = jnp.dot((v * v).astype(BF16), bd, preferred_element_type=F32)
    return v * lax.rsqrt(msq + EPS) * g


def _make_inproj_kernel(latent, n_aliased):
    def kern(*refs):
        if latent:
            (x_ref, mod_ref, g1_ref, w_ref, bg_ref, gq_ref, gk_ref, bd_ref, cos_ref, sin_ref,
             z_ref, gate_ref, h_scr, wbf_scr) = refs
        else:
            (x_ref, mod_ref, g1_ref, w_ref, bg_ref, gq_ref, gk_ref, bd_ref) = refs[:8]
            (z_ref, gate_ref, ak_ref, av_ref, bk_ref, bv_ref, ck_ref, cv_ref,
             h_scr, wbf_scr) = refs[8 + n_aliased:]
        j = pl.program_id(0)
        i = pl.program_id(1)

        @pl.when(i == 0)
        def _():
            wbf_scr[...] = w_ref[...].astype(BF16)

        def chunks(first=False):
            for rc in range(ROW_TILE // ROW_CHUNK):
                rows = pl.ds(rc * ROW_CHUNK, ROW_CHUNK)
                tok = pl.ds(pl.multiple_of(i * ROW_TILE + rc * ROW_CHUNK, ROW_CHUNK), ROW_CHUNK)
                if first:
                    x = x_ref[rows, :]
                    ms = jnp.mean(x * x, axis=-1, keepdims=True)
                    y = x * lax.rsqrt(ms + EPS) * g1_ref[...]
                    m = mod_ref[0]
                    h = (y * (1.0 + m[:, D_MODEL:2 * D_MODEL]) + m[:, 0:D_MODEL]).astype(BF16)
                    h_scr[tok, :] = h
                else:
                    h = h_scr[tok, :]
                yield rc, rows, jnp.dot(h, wbf_scr[...], preferred_element_type=F32)

        def grp(a, c, n=1):
            return a[:, c * LANES:(c + n) * LANES]

        def rot(v, rows):
            return _rope(v, cos_ref[rows, :], sin_ref[rows, :]) if latent else v

        @pl.when(j == 0)
        def _():
            for rc, rows, a in chunks(first=True):
                for c in range(GROUPS_PER_TILE):
                    v = rot(grp(a, c), rows)
                    z_ref[c, rows, :] = (v * Q_SCALE if c < Q_GROUPS else v).astype(BF16)
                if not latent:
                    ak_ref[rc] = grp(a, 4, 2)

        @pl.when(j == 1)
        def _():
            for rc, rows, a in chunks():
                for c in range(2):
                    z_ref[c, rows, :] = rot(grp(a, c), rows).astype(BF16)
                for c in range(2, GROUPS_PER_TILE):
                    z_ref[c, rows, :] = grp(a, c).astype(BF16)
                if not latent:
                    ak_ref[rc] = grp(a, 0, 2)
                    av_ref[rc] = grp(a, 2, 4).reshape(ROW_CHUNK, HA, 2 * DA)

        @pl.when(j == 2)
        def _():
            bd = bd_ref[...]
            for rc, rows, a in chunks():
                for half in range(2):
                    qn = _head_rmsnorm(grp(a, 2 * half, 2), bd, gq_ref[...])
                    for c in range(2):
                        z_ref[2 * half + c, rows, :] = (rot(grp(qn, c), rows) * Q_SCALE).astype(BF16)
                kn = _head_rmsnorm(grp(a, 4), bd[0:LANES, 0:LANES], gk_ref[...])
                z_ref[4, rows, :] = rot(kn, rows).astype(BF16)
                z_ref[5, rows, :] = grp(a, 5).astype(BF16)
                if not latent:
                    bk_ref[rc] = kn
                    bv_ref[rc] = grp(a, 5)

        @pl.when(j == 3)
        def _():
            for rc, rows, a in chunks():
                for c in range(GROUPS_PER_TILE):
                    v = grp(a, c)
                    z_ref[c, rows, :] = (v * Q_SCALE if c < Q_GROUPS else v).astype(BF16)
                if not latent:
                    ck_ref[rc] = grp(a, 4, 2)

        @pl.when(j == 4)
        def _():
            for rc, rows, a in chunks():
                for c in range(GROUPS_PER_TILE):
                    z_ref[c, rows, :] = grp(a, c).astype(BF16)
                if not latent:
                    ck_ref[rc] = grp(a, 0, 2)
                    cv_ref[rc] = grp(a, 2, 4)

        @pl.when(j >= N_QKV_TILES)
        def _():
            for rc, rows, a in chunks():
                a = a + bg_ref[...]
                for c in range(GROUPS_PER_TILE):
                    gate_ref[c, rows, :] = jax.nn.sigmoid(grp(a, c)).astype(BF16)

    return kern


def _inproj(latent, l, x, mod, g_norm1, w_in, b_gate, gq, gk, bd, rope_cos, rope_sin, caches):
    n_row = N_TOK // ROW_TILE
    last = n_row - 1

    def row_block(j, i, first_tile, last_tile):
        return jnp.where(j < first_tile, 0, jnp.where(j <= last_tile, i, last))

    mod_idx = (lambda j, i: (1 + i, 0, 0)) if latent else (lambda j, i: (0, 0, 0))
    in_specs = [
        pl.BlockSpec((ROW_TILE, D_MODEL), lambda j, i: (row_block(j, i, 0, 0), 0)),
        pl.BlockSpec((1, 1, 6 * D_MODEL), mod_idx),
        pl.BlockSpec((None, 1, D_MODEL), lambda j, i: (l, 0, 0)),
        pl.BlockSpec((None, D_MODEL, COL_TILE), lambda j, i: (l, 0, j)),
        pl.BlockSpec((None, 1, COL_TILE), lambda j, i: (l, 0, jnp.maximum(j - N_QKV_TILES, 0))),
        pl.BlockSpec((None, 1, 2 * LANES), lambda j, i: (l, 0, 0)),
        pl.BlockSpec((None, 1, LANES), lambda j, i: (l, 0, 0)),
        pl.BlockSpec((2 * LANES, 2 * LANES), lambda j, i: (0, 0)),
    ]
    args = [x, mod, g_norm1, w_in, b_gate, gq, gk, bd]
    out_specs = [
        pl.BlockSpec((GROUPS_PER_TILE, ROW_TILE, LANES),
                     lambda j, i: (jnp.minimum(j, N_QKV_TILES - 1), row_block(j, i, 0, N_QKV_TILES - 1), 0)),
        pl.BlockSpec((GROUPS_PER_TILE, ROW_TILE, LANES),
                     lambda j, i: (jnp.maximum(j - N_QKV_TILES, 0), row_block(j, i, N_QKV_TILES, N_COL_TILES), 0)),
    ]
    out_shape = [jax.ShapeDtypeStruct((N_QKV_GROUPS, N_TOK, LANES), BF16),
                 jax.ShapeDtypeStruct((N_GATE_GROUPS, N_TOK, LANES), BF16)]
    aliases = {}
    if latent:
        in_specs += [pl.BlockSpec((DEC_SEQ, LANES), lambda j, i: (0, 0))] * 2
        args += [rope_cos, rope_sin]
    else:
        nb = ROW_TILE // SEQ
        tiles = ((0, 1, 256), (1, 1, 512), (2, 2, 128), (2, 2, 128), (3, 4, 256), (4, 4, 512))
        for k, (w, (t0, t1, bw)) in enumerate(zip(CACHE_WIDTHS, tiles)):
            if caches is not None:
                in_specs.append(pl.BlockSpec(memory_space=pl.ANY))
                args.append(caches[k])
                aliases[8 + k] = 2 + k

            def cache_idx(j, i, t0=t0, t1=t1):
                return (row_block(j, i, t0, t1), l, 0, jnp.where(j <= t0, 0, (t1 - t0)))

            if k == 1:
                out_specs.append(pl.BlockSpec((nb, None, SEQ, HA, 2 * DA),
                                              lambda j, i, f=cache_idx: f(j, i) + (0,)))
                out_shape.append(jax.ShapeDtypeStruct((BATCH, DEPTH, SEQ, HA, 2 * DA), F32))
                continue
            out_specs.append(pl.BlockSpec((nb, None, SEQ, bw), cache_idx))
            out_shape.append(jax.ShapeDtypeStruct((BATCH, DEPTH, SEQ, w), F32))
    return pl.pallas_call(
        _make_inproj_kernel(latent, len(aliases)),
        grid=(N_COL_TILES, n_row),
        in_specs=in_specs,
        out_specs=out_specs,
        out_shape=out_shape,
        scratch_shapes=[pltpu.VMEM((N_TOK, D_MODEL), BF16), pltpu.VMEM((D_MODEL, COL_TILE), BF16)],
        input_output_aliases=aliases,
        compiler_params=_params(("arbitrary", "arbitrary")),
        name="inproj_lat" if latent else "inproj_ctx",
    )(*args)


def _mask_head(q, head):
    qf = q.astype(F32)
    keep = _lane_head(qf.shape) == head
    return jnp.where(keep, qf, 0.0).astype(BF16)


def _stack_heads(q):
    return jnp.concatenate([_mask_head(q, 0), _mask_head(q, 1)], axis=0)


def _unstack_heads(o, rows):
    return jnp.where(_lane_head((rows, LANES)) == 1, o[rows:2 * rows], o[0:rows])


def _dup_head(kv, head):
    f = kv.astype(F32)
    r = pltpu.roll(f, HEAD_DIM, 1)
    return jnp.where(_lane_head(f.shape) == head, f, r).astype(BF16)


def _attend(qm, ks, vs):
    nt = (False,) * len(ks)
    return _attend_multi([(qm, ks, vs, None, nt, nt)])[0]


def _attend_multi(jobs, softmax_rows=None):
    sss = []
    for qm, ks, _, biases, k_t, _ in jobs:
        ss = [jnp.dot(qm, k, preferred_element_type=F32) if t else
              lax.dot_general(qm, k, _NT, preferred_element_type=F32) for k, t in zip(ks, k_t)]
        if biases is not None:
            ss = [s if b is None else s + b for s, b in zip(ss, biases)]
        sss.append(ss)
    pss, dens = [], []
    for ss in sss:
        n_rows = ss[0].shape[0]
        chunk = softmax_rows or n_rows
        p_chunks, den_chunks = [[] for _ in ss], []
        for r0 in range(0, n_rows, chunk):
            sl = [s[r0:r0 + chunk] for s in ss]
            m = functools.reduce(jnp.maximum, [jnp.max(s, axis=-1, keepdims=True) for s in sl])
            ps = [jnp.exp2(s - m) for s in sl]
            den_chunks.append(functools.reduce(jnp.add, [jnp.sum(p, axis=-1, keepdims=True) for p in ps]))
            for lst, p in zip(p_chunks, ps):
                lst.append(p.astype(BF16))
        dens.append(jnp.concatenate(den_chunks, axis=0))
        pss.append([jnp.concatenate(lst, axis=0) for lst in p_chunks])
    outs = []
    for (_, _, vs, _, _, v_t), ps, den in zip(jobs, pss, dens):
        o = functools.reduce(jnp.add, [lax.dot_general(p.astype(BF16), v, _NT, preferred_element_type=F32) if t else
                                        jnp.dot(p.astype(BF16), v, preferred_element_type=F32)
                                        for p, v, t in zip(ps, vs, v_t)])
        outs.append(o * (1.0 / den))
    return outs


SOFTMAX_ROWS = 128
_CTX_T = (False, True)


def _pair_t(ref):
    return ref[...].reshape(2 * HEAD_DIM, ref.shape[-1]).astype(BF16)


def _diff_lambda(lam_ref, lam_init):
    lq = lam_ref[...]
    a = jnp.sum(lq[0:1] * lq[1:2], axis=-1, keepdims=True)
    b = jnp.sum(lq[2:3] * lq[3:4], axis=-1, keepdims=True)
    return jnp.exp(a) - jnp.exp(b) + lam_init


def _diff_combine(o1, o2, lam, gsub, lam_init):
    o = o1 - lam * o2
    ms = jnp.mean(o * o, axis=-1, keepdims=True)
    return (o * lax.rsqrt(ms + EPS) * gsub) * (1.0 - lam_init)


CTX_BATCH_PER_STEP = 2


def _make_attn_ctx_kernel(lam_init):
    def kern(z_ref, lam_ref, gsub_ref, oa_ref, ob_ref, oc_ref):
        lam = _diff_lambda(lam_ref, lam_init)
        gsub = gsub_ref[...]
        for bb in range(CTX_BATCH_PER_STEP):
            rows = pl.ds(bb * SEQ, SEQ)

            def z(g):
                return z_ref[g, rows, :]

            for vh in range(HA):
                hi = vh % 2
                v = z(G_VA + vh)
                o1 = _attend(_mask_head(z(G_QA + vh // 2), hi), [z(G_KA + vh // 2)], [v])
                o2 = _attend(_mask_head(z(G_QA + 2 + vh // 2), hi), [z(G_KA + 2 + vh // 2)], [v])
                oa_ref[vh, rows, :] = _diff_combine(o1, o2, lam, gsub, lam_init).astype(BF16)
            for g in range(KVB):
                kd = _dup_head(z(G_KB), g)
                vd = _dup_head(z(G_VB), g)
                for c in range(2 * g, 2 * g + 2):
                    o = _attend(_stack_heads(z(G_QB + c)), [kd], [vd])
                    ob_ref[c, rows, :] = _unstack_heads(o, SEQ).astype(BF16)
            for c in range(HC // 2):
                o = _attend(_stack_heads(z(G_QC + c)), [z(G_KC + c)], [z(G_VC + c)])
                oc_ref[c, rows, :] = _unstack_heads(o, SEQ).astype(BF16)
    return kern


def _attn_ctx(l, z, lambda_qk, g_subln):
    lam_init = 0.8 - 0.6 * math.exp(-0.3 * l)
    rows = CTX_BATCH_PER_STEP * SEQ
    o_spec = pl.BlockSpec((4, rows, LANES), lambda b: (0, b, 0))
    o_shape = jax.ShapeDtypeStruct((4, N_TOK, LANES), BF16)
    return pl.pallas_call(
        _make_attn_ctx_kernel(lam_init),
        grid=(BATCH // CTX_BATCH_PER_STEP,),
        in_specs=[pl.BlockSpec((N_QKV_GROUPS, rows, LANES), lambda b: (0, b, 0)),
                  pl.BlockSpec((None, 4, DA), lambda b: (l, 0, 0)),
                  pl.BlockSpec((None, 1, 2 * DA), lambda b: (l, 0, 0))],
        out_specs=[o_spec, o_spec, o_spec],
        out_shape=[o_shape, o_shape, o_shape],
        compiler_params=_params(("arbitrary",)),
        name="attn_ctx",
    )(z, lambda_qk, g_subln)


Q_BLK_GQA = 512
GQA_PAIRS = HB // KVB // 2
Q_BLK_DIFF = 1024


def _make_attn_lat_a_kernel(lam_init):
    def kern(q1_ref, q2_ref, k1_ref, k2_ref, v_ref, ck1_ref, ck2_ref, cv_ref, lam_ref, gsub_ref, o_ref):
        vh = pl.program_id(1)
        hi = jnp.bitwise_and(vh, 1)
        lam = _diff_lambda(lam_ref, lam_init)
        gsub = gsub_ref[...]
        ck1 = _pair_t(ck1_ref)
        ck2 = _pair_t(ck2_ref)
        cv = cv_ref[:, vh, :].astype(BF16)
        k1, k2, v = k1_ref[0], k2_ref[0], v_ref[0]
        rows = [pl.ds(qb * Q_BLK_DIFF, Q_BLK_DIFF) for qb in range(DEC_SEQ // Q_BLK_DIFF)]
        jobs = []
        for r in rows:
            jobs.append((_mask_head(q1_ref[0, r, :], hi), [k1, ck1], [v, cv], None, _CTX_T, (False, False)))
            jobs.append((_mask_head(q2_ref[0, r, :], hi), [k2, ck2], [v, cv], None, _CTX_T, (False, False)))
        outs = _attend_multi(jobs)
        for t, r in enumerate(rows):
            o_ref[0, r, :] = _diff_combine(outs[2 * t], outs[2 * t + 1], lam, gsub, lam_init).astype(BF16)
    return kern


def _attn_lat_a(l, z, cache_k_t, cache_v, lambda_qk, g_subln):
    lam_init = 0.8 - 0.6 * math.exp(-0.3 * l)

    def zspec(fn):
        return pl.BlockSpec((1, DEC_SEQ, LANES), lambda b, h: (fn(h), b, 0))

    return pl.pallas_call(
        _make_attn_lat_a_kernel(lam_init),
        grid=(DEC_BATCH, HA),
        in_specs=[zspec(lambda h: G_QA + h // 2), zspec(lambda h: G_QA + 2 + h // 2),
                  zspec(lambda h: G_KA + h // 2), zspec(lambda h: G_KA + 2 + h // 2),
                  zspec(lambda h: G_VA + h),
                  pl.BlockSpec((None, None, 2, DA, PAST_LEN), lambda b, h: (b, l, h // 2, 0, 0)),
                  pl.BlockSpec((None, None, 2, DA, PAST_LEN), lambda b, h: (b, l, HA // 2 + h // 2, 0, 0)),
                  pl.BlockSpec((None, None, PAST_LEN, HA, 2 * DA), lambda b, h: (b, l, 0, 0, 0)),
                  pl.BlockSpec((None, 4, DA), lambda b, h: (l, 0, 0)),
                  pl.BlockSpec((None, 1, 2 * DA), lambda b, h: (l, 0, 0))],
        out_specs=pl.BlockSpec((1, DEC_SEQ, LANES), lambda b, h: (h, b, 0)),
        out_shape=jax.ShapeDtypeStruct((4, N_TOK, LANES), BF16),
        compiler_params=_params(("arbitrary", "arbitrary")),
        name="attn_lat_a",
    )(z, z, z, z, z, cache_k_t, cache_k_t, cache_v, lambda_qk, g_subln)


def _attn_lat_b_kernel(q_ref, k_ref, v_ref, ck_ref, cv_ref, o_ref):
    hi = pl.program_id(1)
    kd = _dup_head(k_ref[0], hi)
    vd = _dup_head(v_ref[0], hi)
    ck = ck_ref[hi].astype(BF16)
    cv = cv_ref[hi].astype(BF16)
    ckd = jnp.concatenate([ck, ck], axis=0)
    cvd = jnp.concatenate([cv, cv], axis=0)
    rows = [pl.ds(qb * Q_BLK_GQA, Q_BLK_GQA) for qb in range(DEC_SEQ // Q_BLK_GQA)]
    work = [(c, r) for c in range(GQA_PAIRS) for r in rows]
    outs = _attend_multi([(_stack_heads(q_ref[c, r, :]), [kd, ckd], [vd, cvd], None, _CTX_T, _CTX_T)
                          for c, r in work], softmax_rows=SOFTMAX_ROWS)
    for (c, r), o in zip(work, outs):
        o_ref[c, r, :] = _unstack_heads(o, Q_BLK_GQA).astype(BF16)


def _attn_lat_b(l, z, cache_k, cache_v):
    cspec = pl.BlockSpec((None, None, KVB, DB, PAST_LEN), lambda b, g: (b, l, 0, 0, 0))
    return pl.pallas_call(
        _attn_lat_b_kernel,
        grid=(DEC_BATCH, KVB),
        in_specs=[pl.BlockSpec((GQA_PAIRS, DEC_SEQ, LANES), lambda b, g: (G_QB // GQA_PAIRS + g, b, 0)),
                  pl.BlockSpec((1, DEC_SEQ, LANES), lambda b, g: (G_KB, b, 0)),
                  pl.BlockSpec((1, DEC_SEQ, LANES), lambda b, g: (G_VB, b, 0)),
                  cspec, cspec],
        out_specs=pl.BlockSpec((GQA_PAIRS, DEC_SEQ, LANES), lambda b, g: (g, b, 0)),
        out_shape=jax.ShapeDtypeStruct((4, N_TOK, LANES), BF16),
        compiler_params=_params(("arbitrary", "arbitrary")),
        name="attn_lat_b",
    )(z, z, z, cache_k, cache_v)


N_GRID_ROWS = DEC_SEQ // GRID_W
NA_GROUP = 2
BIAS_TABLE_ROWS = 16


def _window_start(r):
    return min(max(r - NA_ROWS // 2, 0), N_GRID_ROWS - NA_ROWS)


def _group_keys(g):
    starts = [_window_start(r) for r in range(g * NA_GROUP, (g + 1) * NA_GROUP)]
    n = max(starts) + NA_ROWS - min(starts)
    n += n % 2
    return min(min(starts), N_GRID_ROWS - n), n


NA_MAX_KEY_ROWS = max(_group_keys(g)[1] for g in range(N_GRID_ROWS // NA_GROUP))


def _build_window_bias(tab_ref, bias_scr):
    qcol = lax.broadcasted_iota(jnp.int32, (GRID_W, LANES), 0)
    lane = lax.broadcasted_iota(jnp.int32, (GRID_W, LANES), 1)
    kcol = lane & (GRID_W - 1)
    c0 = jnp.clip(qcol - NA_COLS // 2, 0, GRID_W - NA_COLS)
    in_win = (kcol >= c0) & (kcol < c0 + NA_COLS)
    masks = {(True, True): in_win, (True, False): in_win & (lane < GRID_W), (False, True): in_win & (lane >= GRID_W)}
    for head in range(2):
        tiles = {}

        def tile(drs):
            if drs not in tiles:
                if drs == (None, None):
                    tiles[drs] = jnp.full((GRID_W, LANES), NEG, F32)
                else:
                    vec = None
                    for half, dr in enumerate(drs):
                        if dr is not None:
                            part = tab_ref[head, pl.ds(half * BIAS_TABLE_ROWS + dr, 1), :]
                            vec = part if vec is None else vec + part
                    toep = pltpu.roll(jnp.broadcast_to(vec, (GRID_W, LANES)), 0, 1, stride=1, stride_axis=0)
                    valid = tuple(dr is not None for dr in drs)
                    tiles[drs] = jnp.where(masks[valid], toep * LOG2E, NEG)
            return tiles[drs]

        for r in range(N_GRID_ROWS):
            u0, n = _group_keys(r // NA_GROUP)
            rows = pl.ds(r * GRID_W, GRID_W)
            for m in range(n // 2):
                drs = tuple(kr - r + NA_ROWS - 1 if _window_start(r) <= kr < _window_start(r) + NA_ROWS else None
                            for kr in (u0 + 2 * m, u0 + 2 * m + 1))
                bias_scr[head, rows, pl.ds(m * LANES, LANES)] = tile(drs)


def _attn_lat_c_kernel(q_ref, k_ref, v_ref, ck_ref, cv_ref, tab_ref, o_ref, bias_scr):
    @pl.when(pl.program_id(1) == 0)
    def _():
        _build_window_bias(tab_ref, bias_scr)

    ck = _pair_t(ck_ref)
    cv = _pair_t(cv_ref)
    n_rows = NA_GROUP * GRID_W
    jobs, row_slices = [], []
    for g in range(N_GRID_ROWS // NA_GROUP):
        u0, n = _group_keys(g)
        rows = pl.ds(g * n_rows, n_rows)
        keys = pl.ds(u0 * GRID_W, n * GRID_W)
        bias = jnp.concatenate([bias_scr[0, rows, 0:n * GRID_W], bias_scr[1, rows, 0:n * GRID_W]], axis=0)
        jobs.append((_stack_heads(q_ref[0, rows, :]), [k_ref[0, keys, :], ck], [v_ref[0, keys, :], cv],
                     [bias, None], _CTX_T, _CTX_T))
        row_slices.append(rows)
    for rows, o in zip(row_slices, _attend_multi(jobs, softmax_rows=SOFTMAX_ROWS)):
        o_ref[0, rows, :] = _unstack_heads(o, n_rows).astype(BF16)


def _attn_lat_c(l, z, cache_k, cache_v, bias_rows):
    def zspec(g0):
        return pl.BlockSpec((1, DEC_SEQ, LANES), lambda c, b: (g0 + c, b, 0))

    cspec = pl.BlockSpec((None, None, 2, DC, PAST_LEN), lambda c, b: (b, l, c, 0, 0))
    return pl.pallas_call(
        _attn_lat_c_kernel,
        grid=(HC // 2, DEC_BATCH),
        in_specs=[zspec(G_QC), zspec(G_KC), zspec(G_VC), cspec, cspec,
                  pl.BlockSpec((None, 2, 2 * BIAS_TABLE_ROWS, LANES), lambda c, b: (l, c, 0, 0))],
        out_specs=pl.BlockSpec((1, DEC_SEQ, LANES), lambda c, b: (c, b, 0)),
        out_shape=jax.ShapeDtypeStruct((4, N_TOK, LANES), BF16),
        scratch_shapes=[pltpu.VMEM((2, DEC_SEQ, NA_MAX_KEY_ROWS * GRID_W), F32)],
        compiler_params=_params(("arbitrary", "arbitrary")),
        name="attn_lat_c",
    )(z, z, z, cache_k, cache_v, bias_rows)


MERGE_TILE = 512
BRANCH_GROUPS = 4
MODEL_GROUPS = D_MODEL // LANES


def _merge_kernel(x_ref, oa_ref, ob_ref, oc_ref, gate_ref, mod_ref, wa_ref, wb_ref, wc_ref, wo_ref,
                  xo_ref, wbr_scr, wo_scr):
    @pl.when(pl.program_id(0) == 0)
    def _():
        wbr_scr[0] = wa_ref[...].astype(BF16)
        wbr_scr[1] = wb_ref[...].astype(BF16)
        wbr_scr[2] = wc_ref[...].astype(BF16)
        wo_scr[...] = wo_ref[...].astype(BF16)

    m = mod_ref[0]
    for rc in range(MERGE_TILE // ROW_CHUNK):
        rows = pl.ds(rc * ROW_CHUNK, ROW_CHUNK)
        y = None
        for k, o_ref in enumerate((oa_ref, ob_ref, oc_ref)):
            o = jnp.concatenate([o_ref[c, rows, :] for c in range(BRANCH_GROUPS)], axis=-1)
            p = jnp.dot(o, wbr_scr[k], preferred_element_type=F32)
            g = jnp.concatenate([gate_ref[MODEL_GROUPS * k + c, rows, :] for c in range(MODEL_GROUPS)],
                                axis=-1).astype(F32)
            y = g * p if y is None else y + g * p
        out = jnp.dot(y.astype(BF16), wo_scr[...], preferred_element_type=F32)
        xo_ref[rows, :] = x_ref[rows, :] + m[:, 2 * D_MODEL:3 * D_MODEL] * out


def _merge(latent, l, x, oa, ob, oc, gates, mod, w_a, w_b, w_c, w_o):
    tm = MERGE_TILE
    if latent:
        mod_idx = lambda i: (1 + (i * tm) // DEC_SEQ, 0, 0)
    else:
        mod_idx = lambda i: (0, 0, 0)
    o_spec = pl.BlockSpec((4, tm, LANES), lambda i: (0, i, 0))
    wbr_spec = pl.BlockSpec((None, 4 * LANES, D_MODEL), lambda i: (l, 0, 0))
    return pl.pallas_call(
        _merge_kernel,
        grid=(N_TOK // tm,),
        in_specs=[pl.BlockSpec((tm, D_MODEL), lambda i: (i, 0)),
                  o_spec, o_spec, o_spec,
                  pl.BlockSpec((N_GATE_GROUPS, tm, LANES), lambda i: (0, i, 0)),
                  pl.BlockSpec((1, 1, 6 * D_MODEL), mod_idx),
                  wbr_spec, wbr_spec, wbr_spec,
                  pl.BlockSpec((None, D_MODEL, D_MODEL), lambda i: (l, 0, 0))],
        out_specs=pl.BlockSpec((tm, D_MODEL), lambda i: (i, 0)),
        out_shape=jax.ShapeDtypeStruct((N_TOK, D_MODEL), F32),
        scratch_shapes=[pltpu.VMEM((3, 4 * LANES, D_MODEL), BF16), pltpu.VMEM((D_MODEL, D_MODEL), BF16)],
        compiler_params=_params(("arbitrary",)),
        name="merge_lat" if latent else "merge_ctx",
    )(x, oa, ob, oc, gates, mod, w_a, w_b, w_c, w_o)


FF_TILE = 1024
FF_CHUNK = 1024


def _ffn_kernel(x_ref, mod_ref, g2_ref, w1_ref, w2_ref, gf_ref, xo_ref, acc_scr, h2_scr, *, final):
    f = pl.program_id(1)

    def run(first):
        w1 = w1_ref[...].astype(BF16)
        w2 = w2_ref[...].astype(BF16)
        m = mod_ref[0]
        for rc in range(ROW_TILE // FF_CHUNK):
            rows = pl.ds(rc * FF_CHUNK, FF_CHUNK)
            if first:
                x = x_ref[rows, :]
                ms = jnp.mean(x * x, axis=-1, keepdims=True)
                y = x * lax.rsqrt(ms + EPS) * g2_ref[...]
                h2 = (y * (1.0 + m[:, 4 * D_MODEL:5 * D_MODEL]) + m[:, 3 * D_MODEL:4 * D_MODEL]).astype(BF16)
                h2_scr[rows, :] = h2
            else:
                h2 = h2_scr[rows, :]
            u = jnp.dot(h2, w1, preferred_element_type=F32)
            u = jnp.square(jnp.maximum(u, 0.0)).astype(BF16)
            d = jnp.dot(u, w2, preferred_element_type=F32)
            if first:
                acc_scr[rows, :] = d
            else:
                acc_scr[rows, :] += d

    @pl.when(f == 0)
    def _():
        run(True)

    @pl.when(f > 0)
    def _():
        run(False)

    @pl.when(f == D_FF // FF_TILE - 1)
    def _():
        x = x_ref[...] + mod_ref[0][:, 5 * D_MODEL:6 * D_MODEL] * acc_scr[...]
        if final:
            ms = jnp.mean(x * x, axis=-1, keepdims=True)
            x = x * lax.rsqrt(ms + EPS) * gf_ref[...]
        xo_ref[...] = x


def _ffn(latent, l, x, mod, g_norm2, w1, w2, g_final):
    tm = ROW_TILE
    mod_idx = (lambda i, f: (1 + i, 0, 0)) if latent else (lambda i, f: (0, 0, 0))
    return pl.pallas_call(
        functools.partial(_ffn_kernel, final=(l == DEPTH - 1)),
        grid=(N_TOK // tm, D_FF // FF_TILE),
        in_specs=[pl.BlockSpec((tm, D_MODEL), lambda i, f: (i, 0)),
                  pl.BlockSpec((1, 1, 6 * D_MODEL), mod_idx),
                  pl.BlockSpec((None, 1, D_MODEL), lambda i, f: (l, 0, 0)),
                  pl.BlockSpec((None, D_MODEL, FF_TILE), lambda i, f: (l, 0, f)),
                  pl.BlockSpec((None, FF_TILE, D_MODEL), lambda i, f: (l, f, 0)),
                  pl.BlockSpec((1, D_MODEL), lambda i, f: (0, 0))],
        out_specs=pl.BlockSpec((tm, D_MODEL), lambda i, f: (i, 0)),
        out_shape=jax.ShapeDtypeStruct((N_TOK, D_MODEL), F32),
        scratch_shapes=[pltpu.VMEM((tm, D_MODEL), F32), pltpu.VMEM((tm, D_MODEL), BF16)],
        compiler_params=_params(("arbitrary", "arbitrary")),
        name="ffn_lat" if latent else "ffn_ctx",
    )(x, mod, g_norm2, w1, w2, g_final)


def _rope_tables():
    nf = ROPE_HALF
    t = jnp.arange(DEC_SEQ)
    row = (t // GRID_W).astype(F32)
    col = (t % GRID_W).astype(F32)
    inv = ROPE_BASE ** (-jnp.arange(nf, dtype=F32) / nf)
    ar = row[:, None] * inv[None, :]
    ac = col[:, None] * inv[None, :]
    cos = jnp.concatenate([jnp.cos(ar), jnp.cos(ar), jnp.cos(ac), jnp.cos(ac)], axis=-1)
    sin = jnp.concatenate([-jnp.sin(ar), jnp.sin(ar), -jnp.sin(ac), jnp.sin(ac)], axis=-1)
    return jnp.tile(cos, (1, 2)), jnp.tile(sin, (1, 2))


def _packed_bias_rows(rel_bias):
    n = 2 * NA_ROWS - 1
    first = jnp.concatenate([rel_bias[..., NA_COLS - 1:], jnp.zeros((DEPTH, HC, n, LANES - (2 * NA_COLS - 1)), F32),
                             rel_bias[..., :NA_COLS - 1]], axis=-1)
    lo = GRID_W - NA_COLS + 1
    second = jnp.pad(rel_bias, ((0, 0), (0, 0), (0, 0), (lo, LANES - lo - (2 * NA_COLS - 1))))
    pad_rows = ((0, 0), (0, 0), (0, BIAS_TABLE_ROWS - n), (0, 0))
    return jnp.concatenate([jnp.pad(first, pad_rows), jnp.pad(second, pad_rows)], axis=2)


def kernel(x_prompt, x_sample, c, cache_a_k, cache_a_v, cache_b_k, cache_b_v, cache_c_k, cache_c_v, c_ctx, w_mod, b_mod, g_norm1, g_norm2, w_in, b_gate, lambda_qk, g_subln, g_qnorm, g_knorm, rel_bias, w_branch_a, w_branch_b, w_branch_c, w_out, w_ff1, w_ff2, g_final):
    xp = x_prompt.reshape(N_TOK, D_MODEL)
    xs = x_sample.reshape(N_TOK, D_MODEL)

    cvec = jnp.concatenate([c_ctx[None, :], c, jnp.zeros((3, D_MODEL), F32)], axis=0)
    mods = _modulation(cvec, w_mod, b_mod)

    rope_cos, rope_sin = _rope_tables()
    bias_rows = _packed_bias_rows(rel_bias)
    bd = jnp.kron(jnp.eye(2 * LANES // HEAD_DIM, dtype=F32),
                  jnp.full((HEAD_DIM, HEAD_DIM), 1.0 / HEAD_DIM, F32)).astype(BF16)
    gq = jnp.tile(g_qnorm, (1, 2 * LANES // DB)).reshape(DEPTH, 1, 2 * LANES)
    gk = jnp.tile(g_knorm, (1, LANES // DB)).reshape(DEPTH, 1, LANES)
    g1 = g_norm1.reshape(DEPTH, 1, D_MODEL)
    g2 = g_norm2.reshape(DEPTH, 1, D_MODEL)
    bg = b_gate.reshape(DEPTH, 1, 3 * D_MODEL)
    gsub = g_subln.reshape(DEPTH, 1, 2 * DA)
    gf = g_final.reshape(1, D_MODEL)

    def heads_t(cache):
        return cache.transpose(0, 1, 3, 4, 2)

    ctx_ak, ctx_bk, ctx_bv = heads_t(cache_a_k), heads_t(cache_b_k), heads_t(cache_b_v)
    ctx_ck, ctx_cv = heads_t(cache_c_k), heads_t(cache_c_v)

    new_caches = None

    for l in range(DEPTH):
        mod = mods[l].reshape(8, 1, 6 * D_MODEL)

        outs = _inproj(False, l, xp, mod, g1, w_in, bg, gq, gk, bd, None, None, new_caches)
        z, gates, new_caches = outs[0], outs[1], list(outs[2:])
        oa, ob, oc = _attn_ctx(l, z, lambda_qk, gsub)
        xp = _merge(False, l, xp, oa, ob, oc, gates, mod, w_branch_a, w_branch_b, w_branch_c, w_out)
        xp = _ffn(False, l, xp, mod, g2, w_ff1, w_ff2, gf)

        z, gates = _inproj(True, l, xs, mod, g1, w_in, bg, gq, gk, bd, rope_cos, rope_sin, None)
        oa = _attn_lat_a(l, z, ctx_ak, cache_a_v, lambda_qk, gsub)
        ob = _attn_lat_b(l, z, ctx_bk, ctx_bv)
        oc = _attn_lat_c(l, z, ctx_ck, ctx_cv, bias_rows)
        xs = _merge(True, l, xs, oa, ob, oc, gates, mod, w_branch_a, w_branch_b, w_branch_c, w_out)
        xs = _ffn(True, l, xs, mod, g2, w_ff1, w_ff2, gf)

    y_prompt = xp.reshape(BATCH, SEQ, D_MODEL)
    y_sample = xs.reshape(DEC_BATCH, DEC_SEQ, D_MODEL)
    ak, av, bk, bv, ck, cv = new_caches
    return (y_prompt, y_sample,
            ak.reshape(BATCH, DEPTH, SEQ, 2 * HA, DA), av.reshape(BATCH, DEPTH, SEQ, HA, 2 * DA),
            bk.reshape(BATCH, DEPTH, SEQ, KVB, DB), bv.reshape(BATCH, DEPTH, SEQ, KVB, DB),
            ck.reshape(BATCH, DEPTH, SEQ, HC, DC), cv.reshape(BATCH, DEPTH, SEQ, HC, DC))
```
